```python
import jax, jax.numpy as jnp
from jax import lax
import numpy as np

D_MODEL = 2048
BATCH = 8
SEQ = 8192
DEPTH = 4

HEAD_DIM = 128
ROPE_THETA = 10000.0
GRID_W = 64
BLOCK = 128
EPS = 1e-6
NEG_INF = -1e30
A_HEADS = 4
A_KV = 2
A_WINDOW = 128
B_HEADS = 4
B_KV = 4
B_PATTERNS = ((128, 1), (512, 4), (2048, 16))
C_HEADS = 4
C_KV = 2
D_HEADS = 4
D_KV = 4
NA_ROWS = 8
NA_COLS = 16

MIXER_HEADS = ((A_HEADS, A_KV), (B_HEADS, B_KV), (C_HEADS, C_KV), (D_HEADS, D_KV))
N_BRANCH = 4
BRANCH_W = A_HEADS * HEAD_DIM
IN_COLS = sum((2 * h + 2 * kv) * HEAD_DIM for h, kv in MIXER_HEADS)

kernel_name = "hybrid_gated_parallel_mixers_encoder"


def rms_norm(x, g):
    xf = x.astype(jnp.float32)
    y = xf * lax.rsqrt(jnp.mean(xf * xf, axis=-1, keepdims=True) + EPS)
    return (y * g.astype(jnp.float32)).astype(x.dtype)


def rope_tables(pos, dim):
    inv = ROPE_THETA ** (-jnp.arange(0, dim, 2, dtype=jnp.float32) / dim)
    ang = pos.astype(jnp.float32)[:, None] * inv[None, :]
    ang = jnp.concatenate([ang, ang], axis=-1)
    return (jnp.cos(ang), jnp.sin(ang))


def apply_rope(x, cos, sin):
    xf = x.astype(jnp.float32)
    x1, x2 = jnp.split(xf, 2, axis=-1)
    return (xf * cos + jnp.concatenate([-x2, x1], axis=-1) * sin).astype(x.dtype)


def apply_axial_rope(x, cos_r, sin_r, cos_c, sin_c):
    half = x.shape[-1] // 2
    return jnp.concatenate([apply_rope(x[..., :half], cos_r, sin_r),
                            apply_rope(x[..., half:], cos_c, sin_c)], axis=-1)


def column_split_points():
    points, total = [], 0
    for heads, kv in MIXER_HEADS:
        for width in (heads * HEAD_DIM, kv * HEAD_DIM, kv * HEAD_DIM, heads * HEAD_DIM):
            total += width
            points.append(total)
    return points[:-1]


def to_q_heads(t, n_kv):
    b, s, _ = t.shape
    return t.reshape(b, s, n_kv, -1, HEAD_DIM).transpose(0, 2, 3, 1, 4)


def to_kv_heads(t):
    b, s, _ = t.shape
    return t.reshape(b, s, -1, HEAD_DIM).transpose(0, 2, 1, 3)


def merge_heads(y):
    return jnp.moveaxis(y, -2, 1).reshape(y.shape[0], y.shape[-2], -1)


def banded_attention(q, k, v, reach, sink=None):
    n, hkv, g, length, hd = q.shape
    blk = min(BLOCK, length)
    nb = -(-length // blk)
    lp = nb * blk
    qb = jnp.pad(q, ((0, 0), (0, 0), (0, 0), (0, lp - length), (0, 0))).reshape(n, hkv, g, nb, blk, hd)
    pad_kv = ((0, 0), (0, 0), (reach, lp - length + reach), (0, 0))
    kp = jnp.pad(k, pad_kv)
    vp = jnp.pad(v, pad_kv)
    span = blk + 2 * reach
    idx = jnp.arange(nb)[:, None] * blk + jnp.arange(span)[None, :]
    kband = kp[:, :, idx]
    vband = vp[:, :, idx]
    kpos = (idx - reach)[:, None, :]
    qpos = (jnp.arange(nb)[:, None] * blk + jnp.arange(blk)[None, :])[:, :, None]
    mask = (jnp.abs(qpos - kpos) <= reach) & (kpos >= 0) & (kpos < length)
    s = jnp.einsum('nhgbqd,nhbkd->nhgbqk', qb, kband).astype(jnp.float32) * (hd ** -0.5)
    s = jnp.where(mask, s, NEG_INF)
    m = jnp.max(s, axis=-1)
    if sink is not None:
        sink_b = sink.astype(jnp.float32)[None, :, :, None, None]
        m = jnp.maximum(m, sink_b)
    p = jnp.exp(s - m[..., None])
    den = jnp.sum(p, axis=-1)
    if sink is not None:
        den = den + jnp.exp(sink_b - m)
    o = jnp.einsum('nhgbqk,nhbkd->nhgbqd', p.astype(v.dtype), vband).astype(jnp.float32) / den[..., None]
    o = o.reshape(n, hkv, g, lp, hd)[:, :, :, :length].astype(q.dtype)
    lse = (m + jnp.log(den)).reshape(n, hkv, g, lp)[..., :length]
    return o, lse


def dilated_attention(q, k, v):
    bsz, hkv, g, seq, hd = q.shape
    outs, lses = [], []
    for window, dil in B_PATTERNS:
        reach = (window // 2) // dil
        length = seq // dil
        qd = q.reshape(bsz, hkv, g, length, dil, hd).transpose(0, 4, 1, 2, 3, 5).reshape(bsz * dil, hkv, g, length, hd)
        kd = k.reshape(bsz, hkv, length, dil, hd).transpose(0, 3, 1, 2, 4).reshape(bsz * dil, hkv, length, hd)
        vd = v.reshape(bsz, hkv, length, dil, hd).transpose(0, 3, 1, 2, 4).reshape(bsz * dil, hkv, length, hd)
        o, lse = banded_attention(qd, kd, vd, reach)
        outs.append(o.reshape(bsz, dil, hkv, g, length, hd).transpose(0, 2, 3, 4, 1, 5).reshape(bsz, hkv, g, seq, hd))
        lses.append(lse.reshape(bsz, dil, hkv, g, length).transpose(0, 2, 3, 4, 1).reshape(bsz, hkv, g, seq))
    w = jax.nn.softmax(jnp.stack(lses, axis=0), axis=0)
    return jnp.sum(w[..., None] * jnp.stack(outs, axis=0).astype(jnp.float32), axis=0).astype(q.dtype)


def dense_block_attention(q, k, v):
    bsz, hkv, g, seq, hd = q.shape
    nb = seq // BLOCK
    qb = jnp.moveaxis(q.reshape(bsz, hkv, g, nb, BLOCK, hd), 3, 0)

    def attend(qi):
        s = jnp.einsum('nhgqd,nhkd->nhgqk', qi, k).astype(jnp.float32) * (hd ** -0.5)
        p = jax.nn.softmax(s, axis=-1)
        return jnp.einsum('nhgqk,nhkd->nhgqd', p.astype(v.dtype), v)

    o = lax.map(attend, qb)
    return jnp.moveaxis(o, 0, 3).reshape(bsz, hkv, g, seq, hd)


def neighbourhood_attention(q, k, v, rel_bias, rows):
    bsz, heads, seq, hd = q.shape
    kr = min(NA_ROWS, rows)
    kc = min(NA_COLS, GRID_W)
    qg = q.reshape(bsz, heads, rows, GRID_W, hd)
    kg = k.reshape(bsz, heads, rows, GRID_W, hd)
    vg = v.reshape(bsz, heads, rows, GRID_W, hd)
    r = jnp.arange(rows)
    col = jnp.arange(GRID_W)
    row_start = jnp.clip(r - kr // 2, 0, rows - kr)
    row_idx = row_start[:, None] + jnp.arange(kr)[None, :]
    col_start = jnp.clip(col - kc // 2, 0, GRID_W - kc)
    col_mask = (col[None, :] >= col_start[:, None]) & (col[None, :] < col_start[:, None] + kc)
    kblk = kg[:, :, row_idx].reshape(bsz, heads, rows, kr * GRID_W, hd)
    vblk = vg[:, :, row_idx].reshape(bsz, heads, rows, kr * GRID_W, hd)
    dr = row_idx - r[:, None]
    dc = jnp.clip(col[None, :] - col[:, None], -(NA_COLS - 1), NA_COLS - 1)
    bias = rel_bias[:, (dr + NA_ROWS - 1)[:, None, :, None], (dc + NA_COLS - 1)[None, :, None, :]]
    bias = bias.reshape(heads, rows, GRID_W, kr * GRID_W).astype(jnp.float32)
    mask = jnp.broadcast_to(col_mask[:, None, :], (GRID_W, kr, GRID_W)).reshape(GRID_W, kr * GRID_W)
    s = jnp.einsum('bhrqd,bhrkd->bhrqk', qg, kblk).astype(jnp.float32) * (hd ** -0.5) + bias
    s = jnp.where(mask, s, NEG_INF)
    p = jax.nn.softmax(s, axis=-1)
    o = jnp.einsum('bhrqk,bhrkd->bhrqd', p.astype(v.dtype), vblk)
    return o.reshape(bsz, heads, seq, hd)


def _fwd_setup_inputs(seed: int = 0) -> dict:
    key = jax.random.key(seed)
    ks = jax.random.split(key, 14)
    nrm = jax.random.normal
    f32 = jnp.float32
    return {
        "x": nrm(ks[0], (BATCH, SEQ, D_MODEL), f32),
        "c": nrm(ks[1], (BATCH, D_MODEL), f32),
        "norm_g": 1.0 + 0.02 * nrm(ks[2], (DEPTH, D_MODEL), f32),
        "w_ada": nrm(ks[3], (DEPTH, D_MODEL, 3 * D_MODEL), f32) * D_MODEL ** -0.5,
        "b_ada": 0.01 * nrm(ks[4], (DEPTH, 3 * D_MODEL), f32),
        "w_in": nrm(ks[5], (DEPTH, D_MODEL, IN_COLS), f32) * D_MODEL ** -0.5,
        "a_sink": nrm(ks[6], (DEPTH, A_HEADS), f32),
        "c_q_norm": 1.0 + 0.02 * nrm(ks[7], (DEPTH, HEAD_DIM), f32),
        "c_k_norm": 1.0 + 0.02 * nrm(ks[8], (DEPTH, HEAD_DIM), f32),
        "d_rel_bias": 0.1 * nrm(ks[9], (DEPTH, D_HEADS, 2 * NA_ROWS - 1, 2 * NA_COLS - 1), f32),
        "w_gate_merge": nrm(ks[10], (DEPTH, D_MODEL, N_BRANCH * D_MODEL), f32) * D_MODEL ** -0.5,
        "w_branch": nrm(ks[11], (DEPTH, N_BRANCH, BRANCH_W, D_MODEL), f32) * BRANCH_W ** -0.5,
        "w_out": nrm(ks[12], (DEPTH, D_MODEL, D_MODEL), f32) * D_MODEL ** -0.5,
        "final_g": 1.0 + 0.02 * nrm(ks[13], (D_MODEL,), f32),
    }


def _fwd_reference(x, c, norm_g, w_ada, b_ada, w_in, a_sink, c_q_norm, c_k_norm, d_rel_bias, w_gate_merge, w_branch, w_out, final_g):
    bsz, seq, dm = x.shape
    rows = seq // GRID_W
    pos = jnp.arange(seq, dtype=jnp.int32)
    cos1, sin1 = rope_tables(pos, HEAD_DIM)
    axial = rope_tables(pos // GRID_W, HEAD_DIM // 2) + rope_tables(pos % GRID_W, HEAD_DIM // 2)
    split_points = column_split_points()
    cond = jax.nn.silu(c)
    for layer in range(DEPTH):
        shift, scale, gate = jnp.split(cond @ w_ada[layer] + b_ada[layer], 3, axis=-1)
        h = rms_norm(x, norm_g[layer]) * (1.0 + scale[:, None, :]) + shift[:, None, :]
        (qa, ka, va, ga, qb, kb, vb, gb, qc, kc, vc, gc, qd, kd, vd, gd) = jnp.split(h @ w_in[layer], split_points, axis=-1)
        ya, _ = banded_attention(apply_rope(to_q_heads(qa, A_KV), cos1, sin1),
                                 apply_rope(to_kv_heads(ka), cos1, sin1), to_kv_heads(va),
                                 A_WINDOW, a_sink[layer].reshape(A_KV, A_HEADS // A_KV))
        yb = dilated_attention(apply_rope(to_q_heads(qb, B_KV), cos1, sin1),
                               apply_rope(to_kv_heads(kb), cos1, sin1), to_kv_heads(vb))
        yc = dense_block_attention(apply_axial_rope(rms_norm(to_q_heads(qc, C_KV), c_q_norm[layer]), *axial),
                                   apply_axial_rope(rms_norm(to_kv_heads(kc), c_k_norm[layer]), *axial),
                                   to_kv_heads(vc))
        yd = neighbourhood_attention(to_kv_heads(qd), to_kv_heads(kd), to_kv_heads(vd), d_rel_bias[layer], rows)
        branches = jnp.stack([merge_heads(ya) * jax.nn.silu(ga), merge_heads(yb) * jax.nn.silu(gb),
                              merge_heads(yc) * jax.nn.silu(gc), merge_heads(yd) * jax.nn.silu(gd)], axis=2)
        proj = jnp.einsum('bsnw,nwd->bsnd', branches, w_branch[layer])
        merge_gate = jax.nn.sigmoid((h @ w_gate_merge[layer]).reshape(bsz, seq, N_BRANCH, dm))
        x = x + gate[:, None, :] * (jnp.sum(merge_gate * proj, axis=2) @ w_out[layer])
    return rms_norm(x, final_g)


import jax as _jax
import jax.numpy as _jnp

TWIN_FORMAT = 'train_step'
FWD_PARAMS = ['x', 'c', 'norm_g', 'w_ada', 'b_ada', 'w_in', 'a_sink', 'c_q_norm', 'c_k_norm', 'd_rel_bias', 'w_gate_merge', 'w_branch', 'w_out', 'final_g']
TWIN_WEIGHTS = ['norm_g', 'w_ada', 'b_ada', 'w_in', 'a_sink', 'c_q_norm', 'c_k_norm', 'd_rel_bias', 'w_gate_merge', 'w_branch', 'w_out', 'final_g']
TWIN_DIFF_INPUT = 'x'
TWIN_INPUTS = ['x', 'c', 'norm_g', 'w_ada', 'b_ada', 'w_in', 'a_sink', 'c_q_norm', 'c_k_norm', 'd_rel_bias', 'w_gate_merge', 'w_branch', 'w_out', 'final_g', 'loss_target', 'm_norm_g', 'm_w_ada', 'm_b_ada', 'm_w_in', 'm_a_sink', 'm_c_q_norm', 'm_c_k_norm', 'm_d_rel_bias', 'm_w_gate_merge', 'm_w_branch', 'm_w_out', 'm_final_g', 'v_norm_g', 'v_w_ada', 'v_b_ada', 'v_w_in', 'v_a_sink', 'v_c_q_norm', 'v_c_k_norm', 'v_d_rel_bias', 'v_w_gate_merge', 'v_w_branch', 'v_w_out', 'v_final_g']
TWIN_OUTPUTS = ['loss', 'grad_x', 'grad_norm_g', 'grad_w_ada', 'grad_b_ada', 'grad_w_in', 'grad_a_sink', 'grad_c_q_norm', 'grad_c_k_norm', 'grad_d_rel_bias', 'grad_w_gate_merge', 'grad_w_branch', 'grad_w_out', 'grad_final_g', 'delta_norm_g', 'delta_w_ada', 'delta_b_ada', 'delta_w_in', 'delta_a_sink', 'delta_c_q_norm', 'delta_c_k_norm', 'delta_d_rel_bias', 'delta_w_gate_merge', 'delta_w_branch', 'delta_w_out', 'delta_final_g', 'new_m_norm_g', 'new_m_w_ada', 'new_m_b_ada', 'new_m_w_in', 'new_m_a_sink', 'new_m_c_q_norm', 'new_m_c_k_norm', 'new_m_d_rel_bias', 'new_m_w_gate_merge', 'new_m_w_branch', 'new_m_w_out', 'new_m_final_g', 'new_v_norm_g', 'new_v_w_ada', 'new_v_b_ada', 'new_v_w_in', 'new_v_a_sink', 'new_v_c_q_norm', 'new_v_c_k_norm', 'new_v_d_rel_bias', 'new_v_w_gate_merge', 'new_v_w_branch', 'new_v_w_out', 'new_v_final_g']
TWIN_LEAF_KINDS = {'loss': 'loss', 'grad_x': 'grad_x', 'grad_norm_g': 'grad_w', 'grad_w_ada': 'grad_w', 'grad_b_ada': 'grad_w', 'grad_w_in': 'grad_w', 'grad_a_sink': 'grad_w', 'grad_c_q_norm': 'grad_w', 'grad_c_k_norm': 'grad_w', 'grad_d_rel_bias': 'grad_w', 'grad_w_gate_merge': 'grad_w', 'grad_w_branch': 'grad_w', 'grad_w_out': 'grad_w', 'grad_final_g': 'grad_w', 'delta_norm_g': 'delta_w', 'delta_w_ada': 'delta_w', 'delta_b_ada': 'delta_w', 'delta_w_in': 'delta_w', 'delta_a_sink': 'delta_w', 'delta_c_q_norm': 'delta_w', 'delta_c_k_norm': 'delta_w', 'delta_d_rel_bias': 'delta_w', 'delta_w_gate_merge': 'delta_w', 'delta_w_branch': 'delta_w', 'delta_w_out': 'delta_w', 'delta_final_g': 'delta_w', 'new_m_norm_g': 'new_m', 'new_m_w_ada': 'new_m', 'new_m_b_ada': 'new_m', 'new_m_w_in': 'new_m', 'new_m_a_sink': 'new_m', 'new_m_c_q_norm': 'new_m', 'new_m_c_k_norm': 'new_m', 'new_m_d_rel_bias': 'new_m', 'new_m_w_gate_merge': 'new_m', 'new_m_w_branch': 'new_m', 'new_m_w_out': 'new_m', 'new_m_final_g': 'new_m', 'new_v_norm_g': 'new_v', 'new_v_w_ada': 'new_v', 'new_v_b_ada': 'new_v', 'new_v_w_in': 'new_v', 'new_v_a_sink': 'new_v', 'new_v_c_q_norm': 'new_v', 'new_v_c_k_norm': 'new_v', 'new_v_d_rel_bias': 'new_v', 'new_v_w_gate_merge': 'new_v', 'new_v_w_branch': 'new_v', 'new_v_w_out': 'new_v', 'new_v_final_g': 'new_v'}


def _forward(args):
    return _fwd_reference(*[args[k] for k in FWD_PARAMS])


def _output_shape():
    def fwd():
        inp = _fwd_setup_inputs(0)
        return _fwd_reference(*[inp[k] for k in FWD_PARAMS])
    out = _jax.eval_shape(fwd)
    return out.shape, out.dtype

N_MICROBATCH = 1
ADAM_LR = 0.001
ADAM_B1 = 0.9
ADAM_B2 = 0.999
ADAM_EPS = 1e-08
ADAM_WD = 0.01
ADAM_STEP = 10
PER_EXAMPLE_BATCH_AXIS = {'x': 0, 'c': 0, 'loss_target': 0}
SHARED_INPUTS = []
_WEIGHT_DTYPES = {'norm_g': _jnp.float32, 'w_ada': _jnp.float32, 'b_ada': _jnp.float32, 'w_in': _jnp.float32, 'a_sink': _jnp.float32, 'c_q_norm': _jnp.float32, 'c_k_norm': _jnp.float32, 'd_rel_bias': _jnp.float32, 'w_gate_merge': _jnp.float32, 'w_branch': _jnp.float32, 'w_out': _jnp.float32, 'final_g': _jnp.float32}
MOMENT_SCALE = {'norm_g': 5.466832e-02, 'w_ada': 4.201310e-02, 'b_ada': 7.450612e-02, 'w_in': 3.707272e-02, 'a_sink': 2.075109e-03, 'c_q_norm': 1.241258e-02, 'c_k_norm': 1.276407e-02, 'd_rel_bias': 9.097258e-03, 'w_gate_merge': 9.886915e-03, 'w_branch': 2.208292e-02, 'w_out': 4.413326e-02, 'final_g': 3.217085e+01}


def _to_microbatches(a, axis):
    t = _jnp.moveaxis(a, axis, 0)
    t = t.reshape((N_MICROBATCH, t.shape[0] // N_MICROBATCH) + t.shape[1:])
    return _jnp.moveaxis(t, 1, axis + 1)


def setup_inputs(seed: int = 0) -> dict:
    inp = _fwd_setup_inputs(seed)
    key = _jax.random.fold_in(_jax.random.key(seed), 7919)
    shape, _ = _output_shape()
    out = dict(inp)
    out["loss_target"] = _jax.random.normal(_jax.random.fold_in(key, 0), shape, _jnp.float32)
    for i, name in enumerate(TWIN_WEIGHTS):
        w = inp[name].astype(_jnp.float32)
        if MOMENT_SCALE is None:
            s = _jnp.sqrt(_jnp.mean(_jnp.square(w)) + 1e-30)
        else:
            s = MOMENT_SCALE[name]
        km, kv = _jax.random.split(_jax.random.fold_in(key, i + 1))
        out[name] = w
        out["m_" + name] = s * _jax.random.normal(km, w.shape, _jnp.float32)
        out["v_" + name] = (s * s) * _jax.random.uniform(kv, w.shape, _jnp.float32, 0.5, 1.5)
    if N_MICROBATCH > 1:
        for name, axis in PER_EXAMPLE_BATCH_AXIS.items():
            out[name] = _to_microbatches(out[name], axis)
    return {'x': out['x'], 'c': out['c'], 'norm_g': out['norm_g'], 'w_ada': out['w_ada'], 'b_ada': out['b_ada'], 'w_in': out['w_in'], 'a_sink': out['a_sink'], 'c_q_norm': out['c_q_norm'], 'c_k_norm': out['c_k_norm'], 'd_rel_bias': out['d_rel_bias'], 'w_gate_merge': out['w_gate_merge'], 'w_branch': out['w_branch'], 'w_out': out['w_out'], 'final_g': out['final_g'], 'loss_target': out['loss_target'], 'm_norm_g': out['m_norm_g'], 'm_w_ada': out['m_w_ada'], 'm_b_ada': out['m_b_ada'], 'm_w_in': out['m_w_in'], 'm_a_sink': out['m_a_sink'], 'm_c_q_norm': out['m_c_q_norm'], 'm_c_k_norm': out['m_c_k_norm'], 'm_d_rel_bias': out['m_d_rel_bias'], 'm_w_gate_merge': out['m_w_gate_merge'], 'm_w_branch': out['m_w_branch'], 'm_w_out': out['m_w_out'], 'm_final_g': out['m_final_g'], 'v_norm_g': out['v_norm_g'], 'v_w_ada': out['v_w_ada'], 'v_b_ada': out['v_b_ada'], 'v_w_in': out['v_w_in'], 'v_a_sink': out['v_a_sink'], 'v_c_q_norm': out['v_c_q_norm'], 'v_c_k_norm': out['v_c_k_norm'], 'v_d_rel_bias': out['v_d_rel_bias'], 'v_w_gate_merge': out['v_w_gate_merge'], 'v_w_branch': out['v_w_branch'], 'v_w_out': out['v_w_out'], 'v_final_g': out['v_final_g']}


def _loss(weights, diff, rest, loss_target):
    with _jax.named_scope("forward"):
        args = {**rest, TWIN_DIFF_INPUT: diff, **{k: w.astype(_WEIGHT_DTYPES[k]) for k, w in weights.items()}}
        y = _forward(args)
    with _jax.named_scope("loss_head"):
        err = _jnp.square(y.astype(_jnp.float32) - loss_target)
        return 0.5 * _jnp.sum(_jnp.mean(err, axis=-1)) if err.ndim else 0.5 * err


def _adamw(w, g, m, v):
    m = ADAM_B1 * m + (1.0 - ADAM_B1) * g
    v = ADAM_B2 * v + (1.0 - ADAM_B2) * _jnp.square(g)
    m_hat = m / (1.0 - ADAM_B1 ** ADAM_STEP)
    v_hat = v / (1.0 - ADAM_B2 ** ADAM_STEP)
    delta = -ADAM_LR * (m_hat / (_jnp.sqrt(v_hat) + ADAM_EPS) + ADAM_WD * w)
    return delta, m, v


def reference(x, c, norm_g, w_ada, b_ada, w_in, a_sink, c_q_norm, c_k_norm, d_rel_bias, w_gate_merge, w_branch, w_out, final_g, loss_target, m_norm_g, m_w_ada, m_b_ada, m_w_in, m_a_sink, m_c_q_norm, m_c_k_norm, m_d_rel_bias, m_w_gate_merge, m_w_branch, m_w_out, m_final_g, v_norm_g, v_w_ada, v_b_ada, v_w_in, v_a_sink, v_c_q_norm, v_c_k_norm, v_d_rel_bias, v_w_gate_merge, v_w_branch, v_w_out, v_final_g):
    given = dict(x=x, c=c, norm_g=norm_g, w_ada=w_ada, b_ada=b_ada, w_in=w_in, a_sink=a_sink, c_q_norm=c_q_norm, c_k_norm=c_k_norm, d_rel_bias=d_rel_bias, w_gate_merge=w_gate_merge, w_branch=w_branch, w_out=w_out, final_g=final_g, loss_target=loss_target, m_norm_g=m_norm_g, m_w_ada=m_w_ada, m_b_ada=m_b_ada, m_w_in=m_w_in, m_a_sink=m_a_sink, m_c_q_norm=m_c_q_norm, m_c_k_norm=m_c_k_norm, m_d_rel_bias=m_d_rel_bias, m_w_gate_merge=m_w_gate_merge, m_w_branch=m_w_branch, m_w_out=m_w_out, m_final_g=m_final_g, v_norm_g=v_norm_g, v_w_ada=v_w_ada, v_b_ada=v_b_ada, v_w_in=v_w_in, v_a_sink=v_a_sink, v_c_q_norm=v_c_q_norm, v_c_k_norm=v_c_k_norm, v_d_rel_bias=v_d_rel_bias, v_w_gate_merge=v_w_gate_merge, v_w_branch=v_w_branch, v_w_out=v_w_out, v_final_g=v_final_g)
    weights = {n: given[n] for n in TWIN_WEIGHTS}
    shared = {n: given[n] for n in SHARED_INPUTS}
    per_example = {n: given[n] for n in ['x', 'c']}
    grad_fn = _jax.value_and_grad(_loss, argnums=(0, 1))

    def one_microbatch(ex, loss_target):
        ex = dict(ex)
        diff = ex.pop(TWIN_DIFF_INPUT)
        return grad_fn(weights, diff, {**shared, **ex}, loss_target)

    if N_MICROBATCH == 1:
        loss, (grad_w, grad_x) = one_microbatch(per_example, given["loss_target"])
    else:
        def body(carry, xs):
            loss_sum, grad_sum = carry
            l_k, (gw_k, gx_k) = one_microbatch(xs[0], xs[1])
            with _jax.named_scope("update"):
                return (loss_sum + l_k, _jax.tree.map(_jnp.add, grad_sum, gw_k)), gx_k

        init = (_jnp.zeros((), _jnp.float32), _jax.tree.map(_jnp.zeros_like, weights))
        (loss, grad_w), grad_x = _jax.lax.scan(body, init, (per_example, given["loss_target"]))
    with _jax.named_scope("update"):
        delta_w, new_m, new_v = {}, {}, {}
        for n in TWIN_WEIGHTS:
            delta_w[n], new_m[n], new_v[n] = _adamw(weights[n], grad_w[n], given["m_" + n], given["v_" + n])
    return (loss, grad_x, *[grad_w[n] for n in TWIN_WEIGHTS], *[delta_w[n] for n in TWIN_WEIGHTS],
            *[new_m[n] for n in TWIN_WEIGHTS], *[new_v[n] for n in TWIN_WEIGHTS])
```

```python
import numpy as np
import jax
import jax.numpy as jnp
from jax import lax
from jax.experimental import pallas as pl
from jax.experimental.pallas import tpu as pltpu

F32 = jnp.float32
BF16 = jnp.bfloat16
I32 = jnp.int32
SDS = jax.ShapeDtypeStruct
MESH = pl.DeviceIdType.MESH

HEAD_DIM = 128
GRID_W = 64
EPS = 1e-6
NEG_INF = -1e30
ROPE_THETA = 10000.0
SCALE = HEAD_DIM ** -0.5
N_SHARD = 4
BRANCH_W = 512
IN_COLS = 7168
IN_SHARD = IN_COLS // N_SHARD
QCOL = (0, 12, 28, 40)
KCOL = (4, 16, 32, 44)
VCOL = (6, 20, 34, 48)
GCOL = (8, 24, 36, 52)
KV_HEADS = (2, 4, 2, 4)

ADAM_LR = 0.001
ADAM_B1 = 0.9
ADAM_B2 = 0.999
ADAM_EPS = 1e-08
ADAM_WD = 0.01
ADAM_STEP = 10

V7X_VMEM_BYTES = 64 * 1024 * 1024
VMEM_LIMIT = V7X_VMEM_BYTES * 7 // 8

ATT_TILE = {"a": 256, "b": 512, "c": 512, "d": 256}
ROW_TILE = 512
EW_ROWS = 256


def _band(reach, tile):
    return -(-reach // tile)


def _call(body, *, name, grid, in_specs, out_specs, out_shape, scratch=(), sem=None, nsp=0):
    params = pltpu.CompilerParams(dimension_semantics=sem, vmem_limit_bytes=VMEM_LIMIT)
    if nsp:
        gs = pltpu.PrefetchScalarGridSpec(num_scalar_prefetch=nsp, grid=grid, in_specs=in_specs,
                                          out_specs=out_specs, scratch_shapes=list(scratch))
        return pl.pallas_call(body, grid_spec=gs, out_shape=out_shape, name=name, compiler_params=params)
    return pl.pallas_call(body, grid=grid, in_specs=in_specs, out_specs=out_specs, out_shape=out_shape,
                          scratch_shapes=list(scratch), name=name, compiler_params=params)


def _sigmoid(x):
    return 1.0 / (1.0 + jnp.exp(-x))


def _matmul(name, grid, ins, in_specs, out_shape, out_specs, *, ta=False, tb=False, k_axis=None,
            acc_shape=None, epilogue=None, a_fn=None):
    n_in = len(ins)
    n_out = len(out_shape)
    nk = grid[k_axis] if k_axis is not None else 1
    dn = (((0 if ta else 1,), (1 if tb else 0,)), ((), ()))

    def body(*refs):
        a = refs[0][...]
        if a_fn is not None:
            a = a_fn(a)
        p = lax.dot_general(a.astype(BF16), refs[1][...].astype(BF16), dn, preferred_element_type=F32)
        extra = refs[2:n_in]
        outs = refs[n_in:n_in + n_out]

        def fin(acc):
            vals = epilogue(acc, *extra) if epilogue is not None else (acc,)
            for o_ref, v in zip(outs, vals):
                o_ref[...] = v.astype(o_ref.dtype)

        if k_axis is None:
            fin(p)
        else:
            acc_ref = refs[-1]
            k = pl.program_id(k_axis)

            @pl.when(k == 0)
            def _():
                acc_ref[...] = p

            @pl.when(k > 0)
            def _():
                acc_ref[...] += p

            @pl.when(k == nk - 1)
            def _():
                fin(acc_ref[...])

    sem = tuple("arbitrary" if ax == k_axis else "parallel" for ax in range(len(grid)))
    scratch = [pltpu.VMEM(acc_shape, F32)] if k_axis is not None else []
    return _call(body, name=name, grid=grid, in_specs=in_specs, out_specs=out_specs, out_shape=out_shape,
                 scratch=scratch, sem=sem)(*ins)


def _norm_mod(name, x, g, scale, shift):
    S, D = x.shape
    ts = min(S, EW_ROWS)

    def body(x_ref, g_ref, sc_ref, sh_ref, h_ref):
        xv = x_ref[...]
        r = lax.rsqrt(jnp.mean(xv * xv, axis=-1, keepdims=True) + EPS)
        h_ref[...] = (((xv * r) * g_ref[...]) * (1.0 + sc_ref[...]) + sh_ref[...]).astype(BF16)

    row = pl.BlockSpec((1, D), lambda i: (0, 0))
    blk = pl.BlockSpec((ts, D), lambda i: (i, 0))
    return _call(body, name=name, grid=(S // ts,), in_specs=[blk, row, row, row], out_specs=blk,
                 out_shape=SDS((S, D), BF16), sem=("parallel",))(x, g, scale, shift)


def _norm_mod_bwd(name, x, dh1, dh2, dxo, g, scale):
    S, D = x.shape
    ts = min(S, EW_ROWS)

    def body(x_ref, a_ref, b_ref, dxo_ref, g_ref, sc_ref, dx_ref, dsh_ref, dsc_ref, dg_ref):
        @pl.when(pl.program_id(0) == 0)
        def _():
            dsh_ref[...] = jnp.zeros_like(dsh_ref)
            dsc_ref[...] = jnp.zeros_like(dsc_ref)
            dg_ref[...] = jnp.zeros_like(dg_ref)

        xv = x_ref[...]
        r = lax.rsqrt(jnp.mean(xv * xv, axis=-1, keepdims=True) + EPS)
        xh = xv * r
        dh = a_ref[...] + b_ref[...]
        gv = g_ref[...]
        one_sc = 1.0 + sc_ref[...]
        dsh_ref[...] += jnp.sum(dh, axis=0, keepdims=True)
        dsc_ref[...] += jnp.sum(dh * xh * gv, axis=0, keepdims=True)
        dg_ref[...] += jnp.sum(dh * xh * one_sc, axis=0, keepdims=True)
        dxh = dh * gv * one_sc
        dx = r * (dxh - xh * jnp.mean(dxh * xh, axis=-1, keepdims=True))
        dx_ref[...] = dxo_ref[...] + dx

    row = pl.BlockSpec((1, D), lambda i: (0, 0))
    blk = pl.BlockSpec((ts, D), lambda i: (i, 0))
    return _call(body, name=name, grid=(S // ts,), in_specs=[blk, blk, blk, blk, row, row],
                 out_specs=[blk, row, row, row],
                 out_shape=[SDS((S, D), F32), SDS((1, D), F32), SDS((1, D), F32), SDS((1, D), F32)],
                 sem=("arbitrary",))(x, dh1, dh2, dxo, g, scale)


def _out_bwd_ew(name, dxo, o2, gate):
    S, D = dxo.shape
    ts = min(S, EW_ROWS)

    def body(dxo_ref, o2_ref, gt_ref, do2_ref, dgt_ref):
        @pl.when(pl.program_id(0) == 0)
        def _():
            dgt_ref[...] = jnp.zeros_like(dgt_ref)

        d = dxo_ref[...]
        do2_ref[...] = (d * gt_ref[...]).astype(BF16)
        dgt_ref[...] += jnp.sum(d * o2_ref[...].astype(F32), axis=0, keepdims=True)

    row = pl.BlockSpec((1, D), lambda i: (0, 0))
    blk = pl.BlockSpec((ts, D), lambda i: (i, 0))
    return _call(body, name=name, grid=(S // ts,), in_specs=[blk, blk, row], out_specs=[blk, row],
                 out_shape=[SDS((S, D), BF16), SDS((1, D), F32)], sem=("arbitrary",))(dxo, o2, gate)


def _merge_bwd_ew(name, dz, mg, proj):
    S, D = dz.shape
    ts = min(S, ROW_TILE)
    td = min(D, 512)
    nd = D // td

    def body(dz_ref, mg_ref, pj_ref, dmg_ref, dpj_ref):
        d = dz_ref[...].astype(F32)
        m = mg_ref[...].astype(F32)
        dmg_ref[...] = (d * pj_ref[...].astype(F32) * m * (1.0 - m)).astype(BF16)
        dpj_ref[...] = (d * m).astype(BF16)

    wide = pl.BlockSpec((ts, td), lambda i, j, n: (i, n * nd + j))
    return _call(body, name=name, grid=(S // ts, nd, 4),
                 in_specs=[pl.BlockSpec((ts, td), lambda i, j, n: (i, j)), wide, wide],
                 out_specs=[wide, wide], out_shape=[SDS((S, 4 * D), BF16), SDS((S, 4 * D), BF16)],
                 sem=("parallel", "parallel", "arbitrary"))(dz, mg, proj)


def _final_loss(name, x, tgt, g):
    S, D = x.shape
    ts = min(S, EW_ROWS)

    def body(x_ref, t_ref, g_ref, dx_ref, dg_ref, loss_ref):
        @pl.when(pl.program_id(0) == 0)
        def _():
            dg_ref[...] = jnp.zeros_like(dg_ref)
            loss_ref[...] = jnp.zeros_like(loss_ref)

        xv = x_ref[...]
        r = lax.rsqrt(jnp.mean(xv * xv, axis=-1, keepdims=True) + EPS)
        xh = xv * r
        gv = g_ref[...]
        err = xh * gv - t_ref[...]
        row_loss = jnp.mean(err * err, axis=-1, keepdims=True)
        loss_ref[...] += 0.5 * jnp.sum(row_loss, axis=0, keepdims=True)
        dy = err * (1.0 / D)
        dg_ref[...] += jnp.sum(dy * xh, axis=0, keepdims=True)
        dxh = dy * gv
        dx_ref[...] = r * (dxh - xh * jnp.mean(dxh * xh, axis=-1, keepdims=True))

    row = pl.BlockSpec((1, D), lambda i: (0, 0))
    blk = pl.BlockSpec((ts, D), lambda i: (i, 0))
    return _call(body, name=name, grid=(S // ts,), in_specs=[blk, blk, row],
                 out_specs=[blk, row, pl.BlockSpec((1, 128), lambda i: (0, 0))],
                 out_shape=[SDS((S, D), F32), SDS((1, D), F32), SDS((1, 128), F32)],
                 sem=("arbitrary",))(x, tgt, g)


def _rope_tables(S):
    def tables(pos, dim):
        inv = ROPE_THETA ** (-jnp.arange(0, dim, 2, dtype=F32) / dim)
        ang = pos.astype(F32)[:, None] * inv[None, :]
        ang = jnp.concatenate([ang, ang], axis=-1)
        return jnp.cos(ang), jnp.sin(ang)

    pos = jnp.arange(S, dtype=I32)
    lane = np.arange(HEAD_DIM)
    cos1, sin1 = tables(pos, HEAD_DIM)
    up1 = jnp.asarray((lane >= 64).astype(np.float32))[None, :]
    one_d = (cos1, sin1 * up1, -sin1 * (1.0 - up1))
    cr, sr = tables(pos // GRID_W, HEAD_DIM // 2)
    cc, sc = tables(pos % GRID_W, HEAD_DIM // 2)
    cos2 = jnp.concatenate([cr, cc], axis=-1)
    sin2 = jnp.concatenate([sr, sc], axis=-1)
    up2 = jnp.asarray(((lane % 64) >= 32).astype(np.float32))[None, :]
    axial = (cos2, sin2 * up2, -sin2 * (1.0 - up2))
    return one_d, axial


def _rope_fwd(name, src, c0, nb, tabs, sh, gain=None):
    S = src.shape[0]
    ts = min(S, ROW_TILE)
    has_gain = gain is not None

    def body(*refs):
        x_ref, c_ref, sa_ref, sb_ref = refs[:4]
        o_ref = refs[-1]
        xv = x_ref[...].astype(F32)
        if has_gain:
            r = lax.rsqrt(jnp.mean(xv * xv, axis=-1, keepdims=True) + EPS)
            xv = (xv * r) * refs[4][...]
        out = xv * c_ref[...] + pltpu.roll(xv, sh, 1) * sa_ref[...] + pltpu.roll(xv, HEAD_DIM - sh, 1) * sb_ref[...]
        o_ref[...] = out.astype(BF16)

    tab = pl.BlockSpec((ts, HEAD_DIM), lambda i, j: (i, 0))
    in_specs = [pl.BlockSpec((ts, HEAD_DIM), lambda i, j: (i, c0 + j)), tab, tab, tab]
    ins = [src, *tabs]
    if has_gain:
        in_specs.append(pl.BlockSpec((1, HEAD_DIM), lambda i, j: (0, 0)))
        ins.append(gain)
    return _call(body, name=name, grid=(S // ts, nb), in_specs=in_specs,
                 out_specs=pl.BlockSpec((ts, HEAD_DIM), lambda i, j: (i, j)),
                 out_shape=SDS((S, nb * HEAD_DIM), BF16), sem=("parallel", "parallel"))(*ins)


def _rope_bwd(name, dout, src, c0, nb, tabs, sh, gain=None):
    S = src.shape[0]
    ts = min(S, ROW_TILE)
    has_gain = gain is not None

    def body(*refs):
        d_ref, x_ref, c_ref, sa_ref, sb_ref = refs[:5]
        d = d_ref[...].astype(F32)
        dxn = (d * c_ref[...] + pltpu.roll(d * sa_ref[...], HEAD_DIM - sh, 1) + pltpu.roll(d * sb_ref[...], sh, 1))
        if has_gain:
            gn_ref, dx_ref, dgn_ref = refs[5:]

            @pl.when((pl.program_id(0) == 0) & (pl.program_id(1) == 0))
            def _():
                dgn_ref[...] = jnp.zeros_like(dgn_ref)

            xv = x_ref[...].astype(F32)
            r = lax.rsqrt(jnp.mean(xv * xv, axis=-1, keepdims=True) + EPS)
            xh = xv * r
            dgn_ref[...] += jnp.sum(dxn * xh, axis=0, keepdims=True)
            dxh = dxn * gn_ref[...]
            dx_ref[...] = (r * (dxh - xh * jnp.mean(dxh * xh, axis=-1, keepdims=True))).astype(BF16)
        else:
            refs[5][...] = dxn.astype(BF16)

    tab = pl.BlockSpec((ts, HEAD_DIM), lambda i, j: (i, 0))
    own = pl.BlockSpec((ts, HEAD_DIM), lambda i, j: (i, j))
    in_specs = [own, pl.BlockSpec((ts, HEAD_DIM), lambda i, j: (i, c0 + j)), tab, tab, tab]
    ins = [dout, src, *tabs]
    out_specs = [own]
    out_shape = [SDS((S, nb * HEAD_DIM), BF16)]
    if has_gain:
        row = pl.BlockSpec((1, HEAD_DIM), lambda i, j: (0, 0))
        in_specs.append(row)
        ins.append(gain)
        out_specs.append(row)
        out_shape.append(SDS((1, HEAD_DIM), F32))
    res = _call(body, name=name, grid=(S // ts, nb), in_specs=in_specs, out_specs=out_specs,
                out_shape=out_shape, sem=("arbitrary", "arbitrary"))(*ins)
    return res if has_gain else (res[0], None)


def _mask_a(qp, kp):
    return jnp.abs(qp - kp) <= 128, None


def _mask_b(qp, kp):
    d = qp - kp
    ad = jnp.abs(d)
    mult = ((ad <= 64).astype(F32) + ((ad <= 256) & ((d & 3) == 0)).astype(F32)
            + ((ad <= 1024) & ((d & 15) == 0)).astype(F32))
    return mult > 0.0, mult


def _mask_d(rows):
    def mask(qp, kp):
        qr, qc, kr, kc = qp >> 6, qp & 63, kp >> 6, kp & 63
        rs = jnp.clip(qr - 4, 0, rows - 8)
        cs = jnp.clip(qc - 8, 0, GRID_W - 16)
        return (kr >= rs) & (kr < rs + 8) & (kc >= cs) & (kc < cs + 16), None
    return mask


def _positions(i, j, T):
    qp = i * T + lax.broadcasted_iota(I32, (T, T), 0)
    kp = j * T + lax.broadcasted_iota(I32, (T, T), 1)
    return qp, kp


_NT = (((1,), (1,)), ((), ()))
_TN = (((0,), (0,)), ((), ()))
_NN = (((1,), (0,)), ((), ()))


def _attn_fwd(name, T, W, G, mask, q_arr, qc0, k_arr, kc0, v_arr, vc0, u, gc0, sink, bias=None):
    S = q_arr.shape[0]
    nq = S // T
    nd = nq if W is None else 2 * W + 1
    has_bias = bias is not None

    def jmap(i, d):
        return d if W is None else jnp.clip(i + d - W, 0, nq - 1)

    def body(*refs):
        sink_ref, q_ref, k_ref, v_ref, g_ref = refs[:5]
        bias_ref = refs[5] if has_bias else None
        br_ref, o_ref, lse_ref, m_s, l_s, acc_s = refs[-6:]
        h, i, d = pl.program_id(0), pl.program_id(1), pl.program_id(2)
        j = d if W is None else i + d - W

        @pl.when(d == 0)
        def _():
            m_s[...] = jnp.full_like(m_s, NEG_INF)
            l_s[...] = jnp.zeros_like(l_s)
            acc_s[...] = jnp.zeros_like(acc_s)

        def step():
            s = lax.dot_general(q_ref[...], k_ref[...], _NT, preferred_element_type=F32) * SCALE
            if has_bias:
                s = s + bias_ref[0, d]
            valid = mult = None
            if mask is not None:
                valid, mult = mask(*_positions(i, j, T))
                s = jnp.where(valid, s, NEG_INF)
            m_prev = m_s[...]
            m_new = jnp.maximum(m_prev, jnp.max(s, axis=-1, keepdims=True))
            alpha = jnp.exp(m_prev - m_new)
            p = jnp.exp(s - m_new)
            if valid is not None:
                p = jnp.where(valid, p, 0.0)
            if mult is not None:
                p = p * mult
            l_s[...] = alpha * l_s[...] + jnp.sum(p, axis=-1, keepdims=True)
            acc_s[...] = alpha * acc_s[...] + lax.dot_general(p.astype(BF16), v_ref[...], _NN,
                                                               preferred_element_type=F32)
            m_s[...] = m_new

        if W is None:
            step()
        else:
            pl.when((j >= 0) & (j < nq))(step)

        @pl.when(d == nd - 1)
        def _():
            sk = sink_ref[h]
            m = m_s[...]
            mf = jnp.maximum(m, sk)
            a = jnp.exp(m - mf)
            lf = l_s[...] * a + jnp.exp(sk - mf)
            o = (acc_s[...] * a) / lf
            gv = g_ref[...].astype(F32)
            o_ref[...] = o.astype(BF16)
            br_ref[...] = (o * (gv * _sigmoid(gv))).astype(BF16)
            lse_ref[0] = mf + jnp.log(lf)

    hd = HEAD_DIM
    in_specs = [pl.BlockSpec(memory_space=pltpu.SMEM),
                pl.BlockSpec((T, hd), lambda h, i, d: (i, qc0 + h)),
                pl.BlockSpec((T, hd), lambda h, i, d: (jmap(i, d), kc0 + h // G)),
                pl.BlockSpec((T, hd), lambda h, i, d: (jmap(i, d), vc0 + h // G)),
                pl.BlockSpec((T, hd), lambda h, i, d: (i, gc0 + h))]
    ins = [sink, q_arr, k_arr, v_arr, u]
    if has_bias:
        in_specs.append(pl.BlockSpec((1, nd, T, T), lambda h, i, d: (h, 0, 0, 0)))
        ins.append(bias)
    own = pl.BlockSpec((T, hd), lambda h, i, d: (i, h))
    return _call(body, name=name, grid=(4, nq, nd), in_specs=in_specs,
                 out_specs=[own, own, pl.BlockSpec((1, T, 1), lambda h, i, d: (h, i, 0))],
                 out_shape=[SDS((S, 4 * hd), BF16), SDS((S, 4 * hd), BF16), SDS((4, S, 1), F32)],
                 scratch=[pltpu.VMEM((T, 1), F32), pltpu.VMEM((T, 1), F32), pltpu.VMEM((T, hd), F32)],
                 sem=("parallel", "parallel", "arbitrary"))(*ins)


def _attn_bwd(name, T, W, G, mask, q_arr, qc0, k_arr, kc0, v_arr, vc0, do_all, hb0, lse, delta, bias=None):
    S = q_arr.shape[0]
    nq = S // T
    nd = nq if W is None else 2 * W + 1
    n_kv = 4 // G
    hd = HEAD_DIM
    has_bias = bias is not None
    assert qc0 % G == 0 and hb0 % G == 0 and (not has_bias or G == 1)

    def imap(j, d):
        return d if W is None else jnp.clip(j + d - W, 0, nq - 1)

    def body(*refs):
        q_ref, k_ref, v_ref, do_ref, lse_ref, dl_ref = refs[:6]
        bias_ref = refs[6] if has_bias else None
        n_o = 4 if has_bias else 3
        outs = refs[-(n_o + 2):-2]
        dq_ref, dk_ref, dv_ref = outs[:3]
        dk_s, dv_s = refs[-2:]
        j, d = pl.program_id(1), pl.program_id(2)
        i = d if W is None else j + d - W

        @pl.when((j == 0) & (d == 0))
        def _():
            dq_ref[...] = jnp.zeros_like(dq_ref)
            if has_bias:
                outs[3][...] = jnp.zeros_like(outs[3])

        @pl.when(d == 0)
        def _():
            dk_s[...] = jnp.zeros_like(dk_s)
            dv_s[...] = jnp.zeros_like(dv_s)

        def step():
            k = k_ref[...]
            v = v_ref[...]
            valid = mult = None
            if mask is not None:
                valid, mult = mask(*_positions(i, j, T))
            row0 = pl.multiple_of(i * T, T)
            for g in range(G):
                lanes = slice(g * hd, (g + 1) * hd)
                q = q_ref[:, lanes]
                do = do_ref[:, lanes]
                s = lax.dot_general(q, k, _NT, preferred_element_type=F32) * SCALE
                if has_bias:
                    s = s + bias_ref[0, 2 * W - d]
                p = jnp.exp(s - lse_ref[g])
                if valid is not None:
                    p = jnp.where(valid, p, 0.0)
                if mult is not None:
                    p = p * mult
                dv_s[...] += lax.dot_general(p.astype(BF16), do, _TN, preferred_element_type=F32)
                dp = lax.dot_general(do, v, _NT, preferred_element_type=F32)
                ds = p * (dp - dl_ref[g])
                if has_bias:
                    outs[3][0, 2 * W - d] += ds
                dsb = ds.astype(BF16)
                dk_s[...] += lax.dot_general(dsb, q, _TN, preferred_element_type=F32)
                dq_ref[pl.ds(row0, T), lanes] += lax.dot_general(dsb, k, _NN, preferred_element_type=F32) * SCALE

        if W is None:
            step()
        else:
            pl.when((i >= 0) & (i < nq))(step)

        @pl.when(d == nd - 1)
        def _():
            dk_ref[...] = dk_s[...] * SCALE
            dv_ref[...] = dv_s[...]

    in_specs = [pl.BlockSpec((T, G * hd), lambda kv, j, d: (imap(j, d), qc0 // G + kv)),
                pl.BlockSpec((T, hd), lambda kv, j, d: (j, kc0 + kv)),
                pl.BlockSpec((T, hd), lambda kv, j, d: (j, vc0 + kv)),
                pl.BlockSpec((T, G * hd), lambda kv, j, d: (imap(j, d), hb0 // G + kv)),
                pl.BlockSpec((G, T, 1), lambda kv, j, d: (kv, imap(j, d), 0)),
                pl.BlockSpec((G, T, 1), lambda kv, j, d: (hb0 // G + kv, imap(j, d), 0))]
    ins = [q_arr, k_arr, v_arr, do_all, lse, delta]
    kv_blk = pl.BlockSpec((T, hd), lambda kv, j, d: (j, kv))
    out_specs = [pl.BlockSpec((S, G * hd), lambda kv, j, d: (0, kv)), kv_blk, kv_blk]
    out_shape = [SDS((S, 4 * hd), F32), SDS((S, n_kv * hd), F32), SDS((S, n_kv * hd), F32)]
    if has_bias:
        bspec = pl.BlockSpec((1, nd, T, T), lambda kv, j, d: (kv, 0, 0, 0))
        in_specs.append(bspec)
        ins.append(bias)
        out_specs.append(bspec)
        out_shape.append(SDS((4, nd, T, T), F32))
    res = _call(body, name=name, grid=(n_kv, nq, nd), in_specs=in_specs, out_specs=out_specs,
                out_shape=out_shape, scratch=[pltpu.VMEM((T, hd), F32), pltpu.VMEM((T, hd), F32)],
                sem=("parallel", "arbitrary", "arbitrary"))(*ins)
    return res if has_bias else (*res, None)


def _attn_bwd_pre(name, dbr, o_all, u):
    S = dbr.shape[0]
    ts = min(S, ROW_TILE)
    hd = HEAD_DIM

    def gcol(hb):
        n = hb // 4
        return GCOL[0] + n * 16 - jnp.where(n >= 2, 4, 0) + hb % 4

    def body(dbr_ref, o_ref, g_ref, do_ref, dg_ref, dl_ref):
        db = dbr_ref[...].astype(F32)
        o = o_ref[...].astype(F32)
        gv = g_ref[...].astype(F32)
        sg = _sigmoid(gv)
        do = db * (gv * sg)
        do_ref[...] = do.astype(BF16)
        dg_ref[...] = (db * o * (sg * (1.0 + gv * (1.0 - sg)))).astype(BF16)
        dl_ref[0] = jnp.sum(do * o, axis=-1, keepdims=True)

    own = pl.BlockSpec((ts, hd), lambda i, hb: (i, hb))
    return _call(body, name=name, grid=(S // ts, 16),
                 in_specs=[own, own, pl.BlockSpec((ts, hd), lambda i, hb: (i, gcol(hb)))],
                 out_specs=[own, own, pl.BlockSpec((1, ts, 1), lambda i, hb: (hb, i, 0))],
                 out_shape=[SDS((S, 16 * hd), BF16), SDS((S, 16 * hd), BF16), SDS((16, S, 1), F32)],
                 sem=("parallel", "parallel"))(dbr, o_all, u)


def _sink_grad(name, sink, lse, delta):
    S = lse.shape[1]
    ts = min(S, 2048)

    def body(sink_ref, lse_ref, dl_ref, out_ref):
        @pl.when(pl.program_id(1) == 0)
        def _():
            out_ref[...] = jnp.zeros_like(out_ref)

        sk = sink_ref[pl.program_id(0)]
        part = jnp.sum(jnp.exp(sk - lse_ref[0]) * dl_ref[0], axis=0, keepdims=True)
        out_ref[0] += -jnp.broadcast_to(part, (1, 128))

    col = pl.BlockSpec((1, ts, 1), lambda h, i: (h, i, 0))
    return _call(body, name=name, grid=(4, S // ts),
                 in_specs=[pl.BlockSpec(memory_space=pltpu.SMEM), col, col],
                 out_specs=pl.BlockSpec((1, 1, 128), lambda h, i: (h, 0, 0)),
                 out_shape=SDS((4, 1, 128), F32), sem=("parallel", "arbitrary"))(sink, lse, delta)


def _bias_maps(T, W):
    rpb = T // GRID_W
    nd = 2 * W + 1
    rmap = np.zeros((nd, rpb, rpb, 15), np.float32)
    for df in range(nd):
        for a in range(rpb):
            for b in range(rpb):
                r = (df - W) * rpb + b - a + 7
                if 0 <= r < 15:
                    rmap[df, a, b, r] = 1.0
    cmap = np.zeros((GRID_W, GRID_W, 31), np.float32)
    for q in range(GRID_W):
        for k in range(GRID_W):
            cmap[q, k, int(np.clip(k - q, -15, 15)) + 15] = 1.0
    return jnp.asarray(rmap), jnp.asarray(cmap)


def _bias_tiles(rel_bias, T, W):
    rmap, cmap = _bias_maps(T, W)
    t = jnp.einsum("dabr,hrc,qkc->hdaqbk", rmap, rel_bias, cmap, precision=lax.Precision.HIGHEST)
    return t.reshape(4, 2 * W + 1, T, T)


def _bias_tiles_t(dtiles, T, W):
    rmap, cmap = _bias_maps(T, W)
    rpb = T // GRID_W
    t = dtiles.reshape(4, 2 * W + 1, rpb, GRID_W, rpb, GRID_W)
    return jnp.einsum("dabr,hdaqbk,qkc->hrc", rmap, t, cmap, precision=lax.Precision.HIGHEST)


def _mixer_cfg(S):
    ta, tb, tc, td = (min(S, ATT_TILE[k]) for k in "abcd")
    rpb = td // GRID_W
    return {"a": (ta, _band(128, ta), 2, _mask_a),
            "b": (tb, _band(1024, tb), 1, _mask_b),
            "c": (tc, None, 2, None),
            "d": (td, -(-7 // rpb), 1, _mask_d(S // GRID_W))}


def _layer_fwd(l, x, c8, lw, p, tabs):
    S, D = x.shape
    dq = D // N_SHARD
    ada_sh, win_sh, wgm_sh, wb_sh, wout_sh = lw
    one_d, axial = tabs
    cfg = _mixer_cfg(S)
    tm = min(S, ROW_TILE)

    ada = _matmul(f"ada_l{l}", (N_SHARD,), [c8, ada_sh, p["b_ada"][l][None, :]],
                  [pl.BlockSpec((8, D), lambda j: (0, 0)), pl.BlockSpec((None, D, 3 * dq), lambda j: (j, 0, 0)),
                   pl.BlockSpec((1, 3 * dq), lambda j: (0, j))],
                  [SDS((8, 3 * D), F32)], [pl.BlockSpec((8, 3 * dq), lambda j: (0, j))],
                  epilogue=lambda acc, b_ref: (acc + b_ref[...],), a_fn=lambda a: a * _sigmoid(a))[0][0:1]
    shift, scale, gate = ada[:, :D], ada[:, D:2 * D], ada[:, 2 * D:]
    g_row = p["norm_g"][l][None, :]
    h = _norm_mod(f"norm_mod_l{l}", x, g_row, scale, shift)

    u = _matmul(f"mm_in_l{l}", (S // tm, N_SHARD), [h, win_sh],
                [pl.BlockSpec((tm, D), lambda i, j: (i, 0)), pl.BlockSpec((None, D, IN_SHARD), lambda i, j: (j, 0, 0))],
                [SDS((S, IN_COLS), BF16)], [pl.BlockSpec((tm, IN_SHARD), lambda i, j: (i, j))])[0]

    qa = _rope_fwd(f"rope_qa_l{l}", u, QCOL[0], 4, one_d, 64)
    ka = _rope_fwd(f"rope_ka_l{l}", u, KCOL[0], 2, one_d, 64)
    qb = _rope_fwd(f"rope_qb_l{l}", u, QCOL[1], 4, one_d, 64)
    kb = _rope_fwd(f"rope_kb_l{l}", u, KCOL[1], 4, one_d, 64)
    qc = _rope_fwd(f"rope_qc_l{l}", u, QCOL[2], 4, axial, 32, p["c_q_norm"][l][None, :])
    kc = _rope_fwd(f"rope_kc_l{l}", u, KCOL[2], 2, axial, 32, p["c_k_norm"][l][None, :])

    no_sink = jnp.full((4,), NEG_INF, F32)
    bias = _bias_tiles(p["d_rel_bias"][l], cfg["d"][0], cfg["d"][1])
    br_a, o_a, lse_a = _attn_fwd(f"attn_a_l{l}", *cfg["a"], qa, 0, ka, 0, u, VCOL[0], u, GCOL[0], p["a_sink"][l])
    br_b, o_b, lse_b = _attn_fwd(f"attn_b_l{l}", *cfg["b"], qb, 0, kb, 0, u, VCOL[1], u, GCOL[1], no_sink)
    br_c, o_c, lse_c = _attn_fwd(f"attn_c_l{l}", *cfg["c"], qc, 0, kc, 0, u, VCOL[2], u, GCOL[2], no_sink)
    br_d, o_d, lse_d = _attn_fwd(f"attn_d_l{l}", *cfg["d"], u, QCOL[3], u, KCOL[3], u, VCOL[3], u, GCOL[3], no_sink, bias)
    br = jnp.concatenate([br_a, br_b, br_c, br_d], axis=1)
    o_all = jnp.concatenate([o_a, o_b, o_c, o_d], axis=1)

    def merge_body(h_ref, wg_ref, br_ref, wb_ref, mg_ref, pj_ref, z_ref, acc_ref):
        n = pl.program_id(2)
        mgv = _sigmoid(lax.dot_general(h_ref[...], wg_ref[...], _NN, preferred_element_type=F32))
        pj = lax.dot_general(br_ref[...], wb_ref[...], _NN, preferred_element_type=F32)
        mg_ref[...] = mgv.astype(BF16)
        pj_ref[...] = pj.astype(BF16)

        @pl.when(n == 0)
        def _():
            acc_ref[...] = mgv * pj

        @pl.when(n > 0)
        def _():
            acc_ref[...] += mgv * pj

        @pl.when(n == 3)
        def _():
            z_ref[...] = acc_ref[...].astype(BF16)

    wide = pl.BlockSpec((tm, dq), lambda i, j, n: (i, n * N_SHARD + j))
    mg, proj, z = _call(
        merge_body, name=f"merge_l{l}", grid=(S // tm, N_SHARD, 4),
        in_specs=[pl.BlockSpec((tm, D), lambda i, j, n: (i, 0)),
                  pl.BlockSpec((None, D, dq), lambda i, j, n: (n, 0, j)),
                  pl.BlockSpec((tm, BRANCH_W), lambda i, j, n: (i, n)),
                  pl.BlockSpec((None, None, BRANCH_W, dq), lambda i, j, n: (j, n, 0, 0))],
        out_specs=[wide, wide, pl.BlockSpec((tm, dq), lambda i, j, n: (i, j))],
        out_shape=[SDS((S, 4 * D), BF16), SDS((S, 4 * D), BF16), SDS((S, D), BF16)],
        scratch=[pltpu.VMEM((tm, dq), F32)], sem=("parallel", "parallel", "arbitrary"))(h, wgm_sh, br, wb_sh)

    tn = min(D, 1024)
    x_new, o2 = _matmul(
        f"mm_out_l{l}", (S // tm, D // tn, N_SHARD), [z, wout_sh, x, gate],
        [pl.BlockSpec((tm, dq), lambda i, j, k: (i, k)), pl.BlockSpec((None, dq, tn), lambda i, j, k: (k, 0, j)),
         pl.BlockSpec((tm, tn), lambda i, j, k: (i, j)), pl.BlockSpec((1, tn), lambda i, j, k: (0, j))],
        [SDS((S, D), F32), SDS((S, D), BF16)],
        [pl.BlockSpec((tm, tn), lambda i, j, k: (i, j)), pl.BlockSpec((tm, tn), lambda i, j, k: (i, j))],
        k_axis=2, acc_shape=(tm, tn), epilogue=lambda acc, x_ref, g_ref: (x_ref[...] + g_ref[...] * acc, acc))
    res = dict(x=x, h=h, u=u, qa=qa, ka=ka, qb=qb, kb=kb, qc=qc, kc=kc, br=br, o_all=o_all,
               lse=(lse_a, lse_b, lse_c, lse_d), bias=bias, mg=mg, proj=proj, z=z, o2=o2,
               g_row=g_row, scale=scale, gate=gate)
    return x_new, res


def _layer_bwd(l, dxo, r, lw, p, tabs):
    x, h, u = r["x"], r["h"], r["u"]
    S, D = x.shape
    dq = D // N_SHARD
    ada_sh, win_sh, wgm_sh, wb_sh, wout_sh = lw
    one_d, axial = tabs
    cfg = _mixer_cfg(S)
    tm = min(S, ROW_TILE)
    tk = min(S, 512)
    tn = min(D, 1024)

    do2, dgate = _out_bwd_ew(f"out_bwd_l{l}", dxo, r["o2"], r["gate"])
    dz = _matmul(f"mm_dz_l{l}", (S // tm, N_SHARD), [do2, wout_sh],
                 [pl.BlockSpec((tm, D), lambda i, n: (i, 0)), pl.BlockSpec((None, dq, D), lambda i, n: (n, 0, 0))],
                 [SDS((S, D), BF16)], [pl.BlockSpec((tm, dq), lambda i, n: (i, n))], tb=True)[0]
    g_out = _matmul(f"mm_gwout_l{l}", (N_SHARD, D // tn, S // tk), [r["z"], do2],
                    [pl.BlockSpec((tk, dq), lambda n, j, k: (k, n)), pl.BlockSpec((tk, tn), lambda n, j, k: (k, j))],
                    [SDS((N_SHARD, dq, D), F32)], [pl.BlockSpec((None, dq, tn), lambda n, j, k: (n, 0, j))],
                    ta=True, k_axis=2, acc_shape=(dq, tn))[0]

    dmg, dproj = _merge_bwd_ew(f"merge_bwd_l{l}", dz, r["mg"], r["proj"])
    nj = D // tn
    g_gm = _matmul(f"mm_gwgm_l{l}", (4, D // tn, nj, S // tk), [h, dmg],
                   [pl.BlockSpec((tk, tn), lambda n, i, j, k: (k, i)),
                    pl.BlockSpec((tk, tn), lambda n, i, j, k: (k, n * nj + j))],
                   [SDS((4, D, D), F32)], [pl.BlockSpec((None, tn, tn), lambda n, i, j, k: (n, i, j))],
                   ta=True, k_axis=3, acc_shape=(tn, tn))[0]
    dh1 = _matmul(f"mm_dh1_l{l}", (S // tm, nj, 4 * nj), [dmg, wgm_sh],
                  [pl.BlockSpec((tm, tn), lambda i, j, kk: (i, kk)),
                   pl.BlockSpec((None, tn, tn), lambda i, j, kk: (kk // nj, j, kk % nj))],
                  [SDS((S, D), F32)], [pl.BlockSpec((tm, tn), lambda i, j, kk: (i, j))],
                  tb=True, k_axis=2, acc_shape=(tm, tn))[0]
    dbr = _matmul(f"mm_dbr_l{l}", (S // tm, 4, N_SHARD), [dproj, wb_sh],
                  [pl.BlockSpec((tm, dq), lambda i, n, j: (i, n * N_SHARD + j)),
                   pl.BlockSpec((None, None, BRANCH_W, dq), lambda i, n, j: (j, n, 0, 0))],
                  [SDS((S, 4 * BRANCH_W), BF16)], [pl.BlockSpec((tm, BRANCH_W), lambda i, n, j: (i, n))],
                  tb=True, k_axis=2, acc_shape=(tm, BRANCH_W))[0]
    g_wb = _matmul(f"mm_gwb_l{l}", (N_SHARD, 4, S // tk), [r["br"], dproj],
                   [pl.BlockSpec((tk, BRANCH_W), lambda j, n, k: (k, n)),
                    pl.BlockSpec((tk, dq), lambda j, n, k: (k, n * N_SHARD + j))],
                   [SDS((N_SHARD, 4, BRANCH_W, dq), F32)],
                   [pl.BlockSpec((None, None, BRANCH_W, dq), lambda j, n, k: (j, n, 0, 0))],
                   ta=True, k_axis=2, acc_shape=(BRANCH_W, dq))[0]

    do_all, dg_all, delta = _attn_bwd_pre(f"attn_pre_l{l}", dbr, r["o_all"], u)
    lse_a, lse_b, lse_c, lse_d = r["lse"]
    dsink = _sink_grad(f"sink_grad_l{l}", p["a_sink"][l], lse_a, delta)[:, 0, 0]
    dqa, dka, dva, _ = _attn_bwd(f"attn_a_bwd_l{l}", *cfg["a"], r["qa"], 0, r["ka"], 0, u, VCOL[0], do_all, 0, lse_a, delta)
    dqb, dkb, dvb, _ = _attn_bwd(f"attn_b_bwd_l{l}", *cfg["b"], r["qb"], 0, r["kb"], 0, u, VCOL[1], do_all, 4, lse_b, delta)
    dqc, dkc, dvc, _ = _attn_bwd(f"attn_c_bwd_l{l}", *cfg["c"], r["qc"], 0, r["kc"], 0, u, VCOL[2], do_all, 8, lse_c, delta)
    dqd, dkd, dvd, dbias = _attn_bwd(f"attn_d_bwd_l{l}", *cfg["d"], u, QCOL[3], u, KCOL[3], u, VCOL[3], do_all, 12,
                                     lse_d, delta, r["bias"])
    d_rel = _bias_tiles_t(dbias, cfg["d"][0], cfg["d"][1])

    duqa, _ = _rope_bwd(f"rope_qa_bwd_l{l}", dqa, u, QCOL[0], 4, one_d, 64)
    duka, _ = _rope_bwd(f"rope_ka_bwd_l{l}", dka, u, KCOL[0], 2, one_d, 64)
    duqb, _ = _rope_bwd(f"rope_qb_bwd_l{l}", dqb, u, QCOL[1], 4, one_d, 64)
    dukb, _ = _rope_bwd(f"rope_kb_bwd_l{l}", dkb, u, KCOL[1], 4, one_d, 64)
    duqc, dcq = _rope_bwd(f"rope_qc_bwd_l{l}", dqc, u, QCOL[2], 4, axial, 32, p["c_q_norm"][l][None, :])
    dukc, dck = _rope_bwd(f"rope_kc_bwd_l{l}", dkc, u, KCOL[2], 2, axial, 32, p["c_k_norm"][l][None, :])
    bw = BRANCH_W
    du = jnp.concatenate(
        [duqa, duka, dva.astype(BF16), dg_all[:, 0:bw],
         duqb, dukb, dvb.astype(BF16), dg_all[:, bw:2 * bw],
         duqc, dukc, dvc.astype(BF16), dg_all[:, 2 * bw:3 * bw],
         dqd.astype(BF16), dkd.astype(BF16), dvd.astype(BF16), dg_all[:, 3 * bw:]], axis=1)

    tmi = min(D, 1024)
    g_in = _matmul(f"mm_gwin_l{l}", (N_SHARD, D // tmi, S // tk), [h, du],
                   [pl.BlockSpec((tk, tmi), lambda j, i, k: (k, i)), pl.BlockSpec((tk, IN_SHARD), lambda j, i, k: (k, j))],
                   [SDS((N_SHARD, D, IN_SHARD), F32)], [pl.BlockSpec((None, tmi, IN_SHARD), lambda j, i, k: (j, i, 0))],
                   ta=True, k_axis=2, acc_shape=(tmi, IN_SHARD))[0]
    dh2 = _matmul(f"mm_dh2_l{l}", (S // tm, nj, N_SHARD), [du, win_sh],
                  [pl.BlockSpec((tm, IN_SHARD), lambda i, j, k: (i, k)),
                   pl.BlockSpec((None, tn, IN_SHARD), lambda i, j, k: (k, j, 0))],
                  [SDS((S, D), F32)], [pl.BlockSpec((tm, tn), lambda i, j, k: (i, j))],
                  tb=True, k_axis=2, acc_shape=(tm, tn))[0]

    dx_prev, dshift, dscale, dng = _norm_mod_bwd(f"norm_mod_bwd_l{l}", x, dh1, dh2, dxo, r["g_row"], r["scale"])
    d_ada = jnp.concatenate([dshift, dscale, dgate], axis=1)[0]
    big = (g_in, g_gm, g_wb, g_out)
    small = dict(norm_g=dng[0], b_ada=d_ada, a_sink=dsink, c_q_norm=dcq[0], c_k_norm=dck[0], d_rel_bias=d_rel)
    return dx_prev, big, small


def _place():
    return lax.axis_index("x"), lax.axis_index("y"), lax.axis_index("c")


def _gather_shards(name, shards):
    n = len(shards)

    def body(*refs):
        ins, outs = refs[:n], refs[n:2 * n]
        send_sems, recv_sems, local_sems = refs[2 * n:]
        x, y, c = _place()
        me = 2 * x + y
        peers = [(1 - x, y), (x, 1 - y), (1 - x, 1 - y)]
        local = []
        sends = []
        for a in range(n):
            cp = pltpu.make_async_copy(ins[a], outs[a].at[me], local_sems.at[a])
            cp.start()
            local.append(cp)
            for k, (px, py) in enumerate(peers):
                cp = pltpu.make_async_remote_copy(ins[a], outs[a].at[me], send_sems.at[a, k], recv_sems.at[a, k],
                                                  device_id=(px, py, c), device_id_type=MESH)
                cp.start()
                sends.append(cp)
        for a in range(n):
            for k, (px, py) in enumerate(peers):
                pltpu.make_async_remote_copy(ins[a], outs[a].at[2 * px + py], send_sems.at[a, k], recv_sems.at[a, k],
                                             device_id=(px, py, c), device_id_type=MESH).wait_recv()
        for cp in sends:
            cp.wait_send()
        for cp in local:
            cp.wait()

    any_spec = pl.BlockSpec(memory_space=pl.ANY)
    return pl.pallas_call(
        body, name=name, in_specs=[any_spec] * n, out_specs=[any_spec] * n,
        out_shape=[SDS((N_SHARD, *s.shape), s.dtype) for s in shards],
        scratch_shapes=[pltpu.SemaphoreType.DMA((n, 3)), pltpu.SemaphoreType.DMA((n, 3)),
                        pltpu.SemaphoreType.DMA((n,))])(*shards)


def _gather_small(name, v):
    m_per, n = v.shape

    def body(x_ref, out_ref, send_sems, recv_sems, local_sem):
        x, y, c = _place()
        me, sibling = (x, y, c), (x, y, 1 - c)
        chips = [(1 - x, y), (x, 1 - y), (1 - x, 1 - y)]

        def rows(px, py, pc):
            return out_ref.at[pl.ds((4 * px + 2 * py + pc) * m_per, m_per), :]

        def copy(k, block, to, src=None):
            return pltpu.make_async_remote_copy(
                src_ref=rows(*block) if src is None else src, dst_ref=rows(*block),
                send_sem=send_sems.at[k], recv_sem=recv_sems.at[k], device_id=to, device_id_type=MESH)

        mine = pltpu.make_async_copy(x_ref, rows(*me), local_sem)
        mine.start()
        first = [copy(0, me, sibling, src=x_ref)]
        first += [copy(1 + j, me, (*chip, c), src=x_ref) for j, chip in enumerate(chips)]
        for cp in first:
            cp.start()
        passed = [copy(4 + j, (*chip, c), sibling) for j, chip in enumerate(chips)]
        for j, chip in enumerate(chips):
            copy(1 + j, (*chip, c), me).wait_recv()
            passed[j].start()
        copy(0, sibling, me).wait_recv()
        for j, chip in enumerate(chips):
            copy(4 + j, (*chip, 1 - c), me).wait_recv()
        for cp in first + passed:
            cp.wait_send()
        mine.wait()

    return pl.pallas_call(
        body, name=name, out_shape=SDS((8 * m_per, n), v.dtype),
        in_specs=[pl.BlockSpec(memory_space=pltpu.VMEM)], out_specs=pl.BlockSpec(memory_space=pltpu.VMEM),
        scratch_shapes=[pltpu.SemaphoreType.DMA((7,)), pltpu.SemaphoreType.DMA((7,)), pltpu.SemaphoreType.DMA])(v)


def _pair_send_half(name, grads):
    n = len(grads)

    def body(*refs):
        ins, outs = refs[:n], refs[n:2 * n]
        send_sems, recv_sems = refs[2 * n:]
        x, y, c = _place()
        cps = []
        for a in range(n):
            cp = pltpu.make_async_remote_copy(ins[a].at[:, 1 - c], outs[a], send_sems.at[a], recv_sems.at[a],
                                              device_id=(x, y, 1 - c), device_id_type=MESH)
            cp.start()
            cps.append(cp)
        for cp in cps:
            cp.wait_recv()
        for cp in cps:
            cp.wait_send()

    any_spec = pl.BlockSpec(memory_space=pl.ANY)
    return pl.pallas_call(
        body, name=name, in_specs=[any_spec] * n, out_specs=[any_spec] * n,
        out_shape=[SDS((g.shape[0], *g.shape[2:]), g.dtype) for g in grads],
        scratch_shapes=[pltpu.SemaphoreType.DMA((n,)), pltpu.SemaphoreType.DMA((n,))])(*grads)


def _shard_exchange(name, parts):
    n = len(parts)

    def body(*refs):
        ins, outs = refs[:n], refs[n:2 * n]
        send_sems, recv_sems = refs[2 * n:]
        x, y, c = _place()
        peers = [(1 - x, y), (x, 1 - y), (1 - x, 1 - y)]
        cps = []
        for a in range(n):
            for k, (px, py) in enumerate(peers):
                cp = pltpu.make_async_remote_copy(ins[a].at[2 * px + py], outs[a].at[k], send_sems.at[a, k],
                                                  recv_sems.at[a, k], device_id=(px, py, c), device_id_type=MESH)
                cp.start()
                cps.append(cp)
        for cp in cps:
            cp.wait_recv()
        for cp in cps:
            cp.wait_send()

    any_spec = pl.BlockSpec(memory_space=pl.ANY)
    return pl.pallas_call(
        body, name=name, in_specs=[any_spec] * n, out_specs=[any_spec] * n,
        out_shape=[SDS((3, *g.shape[1:]), g.dtype) for g in parts],
        scratch_shapes=[pltpu.SemaphoreType.DMA((n, 3)), pltpu.SemaphoreType.DMA((n, 3))])(*parts)


def _pair_gather(name, halves):
    n = len(halves)

    def body(*refs):
        ins, outs = refs[:n], refs[n:2 * n]
        send_sems, recv_sems, local_sems = refs[2 * n:]
        x, y, c = _place()
        cps = []
        for a in range(n):
            loc = pltpu.make_async_copy(ins[a], outs[a].at[c], local_sems.at[a])
            loc.start()
            cp = pltpu.make_async_remote_copy(ins[a], outs[a].at[c], send_sems.at[a], recv_sems.at[a],
                                              device_id=(x, y, 1 - c), device_id_type=MESH)
            cp.start()
            cps.append((loc, cp))
        for a in range(n):
            pltpu.make_async_remote_copy(ins[a], outs[a].at[1 - c], send_sems.at[a], recv_sems.at[a],
                                         device_id=(x, y, 1 - c), device_id_type=MESH).wait_recv()
        for loc, cp in cps:
            cp.wait_send()
            loc.wait()

    any_spec = pl.BlockSpec(memory_space=pl.ANY)
    return pl.pallas_call(
        body, name=name, in_specs=[any_spec] * n, out_specs=[any_spec] * n,
        out_shape=[SDS((2, *g.shape), g.dtype) for g in halves],
        scratch_shapes=[pltpu.SemaphoreType.DMA((n,)), pltpu.SemaphoreType.DMA((n,)),
                        pltpu.SemaphoreType.DMA((n,))])(*halves)


def _add_half(name, g, recv, c_idx):
    _, _, R, C = g.shape
    tr = min(R, 256)

    def body(c_ref, g_ref, r_ref, o_ref):
        o_ref[...] = g_ref[...] + r_ref[...]

    return _call(body, name=name, grid=(4, R // tr), nsp=1,
                 in_specs=[pl.BlockSpec((None, None, tr, C), lambda j, r, c_ref: (j, c_ref[0], r, 0)),
                           pl.BlockSpec((None, tr, C), lambda j, r, c_ref: (j, r, 0))],
                 out_specs=pl.BlockSpec((None, tr, C), lambda j, r, c_ref: (j, r, 0)),
                 out_shape=SDS((4, R, C), F32), sem=("parallel", "parallel"))(c_idx, g, recv)


def _add_shards(name, part, recv, me_idx):
    _, R, C = part.shape
    tr = min(R, 256)

    def body(me_ref, p_ref, r_ref, o_ref):
        o_ref[...] = ((p_ref[...] + r_ref[0]) + r_ref[1]) + r_ref[2]

    return _call(body, name=name, grid=(R // tr,), nsp=1,
                 in_specs=[pl.BlockSpec((None, tr, C), lambda r, me_ref: (me_ref[0], r, 0)),
                           pl.BlockSpec((3, tr, C), lambda r, me_ref: (0, r, 0))],
                 out_specs=pl.BlockSpec((tr, C), lambda r, me_ref: (r, 0)),
                 out_shape=SDS((R, C), F32), sem=("parallel",))(me_idx, part, recv)


def _reduce_scatter_layer(l, big, c_idx, me_idx):
    views = []
    for g in big:
        rows = g.shape[-2] if g.ndim == 3 else g.shape[1] * g.shape[2]
        views.append(g.reshape(N_SHARD, 2, rows // 2, g.shape[-1]))
    recv1 = _pair_send_half(f"rs_pair_send_l{l}", views)
    parts = [_add_half(f"rs_add_half{a}_l{l}", v, r1, c_idx) for a, (v, r1) in enumerate(zip(views, recv1))]
    recv2 = _shard_exchange(f"rs_shard_exchange_l{l}", parts)
    halves = [_add_shards(f"rs_add_shards{a}_l{l}", pt, r2, me_idx) for a, (pt, r2) in enumerate(zip(parts, recv2))]
    full = _pair_gather(f"rs_pair_gather_l{l}", halves)
    return [f.reshape(g.shape[1:]) for f, g in zip(full, big)]


def _adamw_math(w, g, m, v):
    m = ADAM_B1 * m + (1.0 - ADAM_B1) * g
    v = ADAM_B2 * v + (1.0 - ADAM_B2) * (g * g)
    m_hat = m / (1.0 - ADAM_B1 ** ADAM_STEP)
    v_hat = v / (1.0 - ADAM_B2 ** ADAM_STEP)
    delta = -ADAM_LR * (m_hat / (jnp.sqrt(v_hat) + ADAM_EPS) + ADAM_WD * w)
    return delta, m, v


def _adamw(name, w, g, m, v):
    shape = w.shape
    C = shape[-1]
    R = int(np.prod(shape[:-1]))
    tr = min(R, 256)

    def body(w_ref, g_ref, m_ref, v_ref, d_ref, nm_ref, nv_ref):
        d, nm, nv = _adamw_math(w_ref[...], g_ref[...], m_ref[...], v_ref[...])
        d_ref[...] = d
        nm_ref[...] = nm
        nv_ref[...] = nv

    blk = pl.BlockSpec((tr, C), lambda i: (i, 0))
    outs = _call(body, name=name, grid=(R // tr,), in_specs=[blk] * 4, out_specs=[blk] * 3,
                 out_shape=[SDS((R, C), F32)] * 3, sem=("parallel",))(*(a.reshape(R, C) for a in (w, g, m, v)))
    return [o.reshape(shape) for o in outs]


def _adamw_small(name, w, g8, m, v):
    R = w.shape[0]

    def body(w_ref, g_ref, m_ref, v_ref, go_ref, d_ref, nm_ref, nv_ref):
        g = g_ref[0]
        for b in range(1, 8):
            g = g + g_ref[b]
        d, nm, nv = _adamw_math(w_ref[...], g, m_ref[...], v_ref[...])
        go_ref[...] = g
        d_ref[...] = d
        nm_ref[...] = nm
        nv_ref[...] = nv

    blk = pl.BlockSpec((R, 128), lambda i: (0, 0))
    return _call(body, name=name, grid=(1,), in_specs=[blk, pl.BlockSpec((8, R, 128), lambda i: (0, 0, 0)), blk, blk],
                 out_specs=[blk] * 4, out_shape=[SDS((R, 128), F32)] * 4, sem=("arbitrary",))(w, g8, m, v)


SMALL_NAMES = ("norm_g", "b_ada", "a_sink", "c_q_norm", "c_k_norm", "d_rel_bias", "final_g")


def _pack(parts, extra_rows=0):
    flat = jnp.concatenate([a.reshape(-1) for a in parts])
    rows = -(-flat.shape[0] // 128)
    rows = -(-rows // 8) * 8 + extra_rows
    return jnp.pad(flat, (0, rows * 128 - flat.shape[0])).reshape(rows, 128)


def _unpack(packed, like):
    flat = packed.reshape(-1)
    out, off = [], 0
    for a in like:
        out.append(flat[off:off + a.size].reshape(a.shape))
        off += a.size
    return out


def _local_step(x, c8, tgt, gathered, p):
    S = x.shape[0]
    L = len(gathered)
    tabs = _rope_tables(S)
    res = []
    for l in range(L):
        x, r = _layer_fwd(l, x, c8, gathered[l], p, tabs)
        res.append(r)
    dx, dfg, loss = _final_loss("final_loss", x, tgt, p["final_g"][None, :])
    bigs, smalls = [None] * L, [None] * L
    for l in reversed(range(L)):
        dx, bigs[l], smalls[l] = _layer_bwd(l, dx, res[l], gathered[l], p, tabs)
    return loss, dx, bigs, smalls, dfg[0]


def kernel(x, c, norm_g, w_ada, b_ada, w_in, a_sink, c_q_norm, c_k_norm, d_rel_bias, w_gate_merge, w_branch, w_out, final_g, loss_target, m_norm_g, m_w_ada, m_b_ada, m_w_in, m_a_sink, m_c_q_norm, m_c_k_norm, m_d_rel_bias, m_w_gate_merge, m_w_branch, m_w_out, m_final_g, v_norm_g, v_w_ada, v_b_ada, v_w_in, v_a_sink, v_c_q_norm, v_c_k_norm, v_d_rel_bias, v_w_gate_merge, v_w_branch, v_w_out, v_final_g):
    L, D = norm_g.shape
    dq = D // N_SHARD
    p = dict(norm_g=norm_g, b_ada=b_ada, a_sink=a_sink, c_q_norm=c_q_norm, c_k_norm=c_k_norm,
             d_rel_bias=d_rel_bias, final_g=final_g)
    xi, yi, ci = _place()
    c_idx = jnp.reshape(ci, (1,)).astype(I32)
    me_idx = jnp.reshape(2 * xi + yi, (1,)).astype(I32)

    gathered = []
    for l in range(L):
        shards = [w_ada[l].astype(BF16), w_in[l].astype(BF16), w_gate_merge[l].astype(BF16),
                  w_branch[l].astype(BF16), w_out[l].astype(BF16)]
        gathered.append(_gather_shards(f"gather_w_l{l}", shards))

    c8 = jnp.broadcast_to(c, (8, D))
    loss, grad_x, bigs, smalls, dfg = _local_step(x[0], c8, loss_target[0], gathered, p)
    loss = lax.psum(loss[0, 0], ("x", "y", "c"))

    small_parts = [jnp.stack([s[n] for s in smalls]) for n in SMALL_NAMES[:-1]] + [dfg]
    packed = _pack(small_parts + [c[0]])
    rows = packed.shape[0]
    g8 = _gather_small("gather_small", packed).reshape(8, rows, 128)
    small_w = [p[n] for n in SMALL_NAMES]
    small_m = [m_norm_g, m_b_ada, m_a_sink, m_c_q_norm, m_c_k_norm, m_d_rel_bias, m_final_g]
    small_v = [v_norm_g, v_b_ada, v_a_sink, v_c_q_norm, v_c_k_norm, v_d_rel_bias, v_final_g]
    pad_c = [jnp.zeros((D,), F32)]
    sg, sd, sm, sv = _adamw_small("adamw_small", _pack(small_w + pad_c), g8, _pack(small_m + pad_c),
                                  _pack(small_v + pad_c))
    sg, sd, sm, sv = (_unpack(a, small_w) for a in (sg, sd, sm, sv))

    n_small = sum(a.size for a in small_parts)
    flat8 = g8.reshape(8, rows * 128)
    c_all = flat8[:, n_small:n_small + D]
    dada_all = flat8[:, L * D:L * D + L * 3 * D].reshape(8, L, 3 * D)
    dada_mine = lax.dynamic_slice_in_dim(dada_all, (2 * xi + yi) * (3 * dq), 3 * dq, axis=2)
    tma = min(D, 1024)
    g_ada = jnp.stack([
        _matmul(f"mm_gwada_l{l}", (D // tma,), [c_all, dada_mine[:, l]],
                [pl.BlockSpec((8, tma), lambda i: (0, i)), pl.BlockSpec((8, 3 * dq), lambda i: (0, 0))],
                [SDS((D, 3 * dq), F32)], [pl.BlockSpec((tma, 3 * dq), lambda i: (i, 0))],
                ta=True, a_fn=lambda a: a * _sigmoid(a))[0] for l in range(L)])

    reduced = [_reduce_scatter_layer(l, bigs[l], c_idx, me_idx) for l in range(L)]
    g_in, g_gm, g_wb, g_out = (jnp.stack([reduced[l][a] for l in range(L)]) for a in range(4))

    big = {}
    for nm, w, g, m, v in (("w_ada", w_ada, g_ada, m_w_ada, v_w_ada), ("w_in", w_in, g_in, m_w_in, v_w_in),
                           ("w_gate_merge", w_gate_merge, g_gm, m_w_gate_merge, v_w_gate_merge),
                           ("w_branch", w_branch, g_wb, m_w_branch, v_w_branch),
                           ("w_out", w_out, g_out, m_w_out, v_w_out)):
        big[nm] = (g, *_adamw(f"adamw_{nm}", w, g, m, v))

    order = ("norm_g", "w_ada", "b_ada", "w_in", "a_sink", "c_q_norm", "c_k_norm", "d_rel_bias",
             "w_gate_merge", "w_branch", "w_out", "final_g")
    cols = [[], [], [], []]
    for nm in order:
        if nm in big:
            vals = big[nm]
        else:
            k = SMALL_NAMES.index(nm)
            vals = (sg[k], sd[k], sm[k], sv[k])
        for col, val in zip(cols, vals):
            col.append(val)
    return (loss, grad_x[None], *cols[0], *cols[1], *cols[2], *cols[3])
```

```python
import numpy as np
import jax
import jax.numpy as jnp
from jax import lax
from jax.experimental import pallas as pl
from jax.experimental.pallas import tpu as pltpu

F32 = jnp.float32
BF16 = jnp.bfloat16
I32 = jnp.int32
SDS = jax.ShapeDtypeStruct
MESH = pl.DeviceIdType.MESH

HEAD_DIM = 128
GRID_W = 64
EPS = 1e-6
NEG_INF = -1e30
ROPE_THETA = 10000.0
SCALE = HEAD_DIM ** -0.5
N_SHARD = 4
BRANCH_W = 512
IN_COLS = 7168
IN_SHARD = IN_COLS // N_SHARD
QCOL = (0, 12, 28, 40)
KCOL = (4, 16, 32, 44)
VCOL = (6, 20, 34, 48)
GCOL = (8, 24, 36, 52)
KV_HEADS = (2, 4, 2, 4)

ADAM_LR = 0.001
ADAM_B1 = 0.9
ADAM_B2 = 0.999
ADAM_EPS = 1e-08
ADAM_WD = 0.01
ADAM_STEP = 10

V7X_VMEM_BYTES = 64 * 1024 * 1024
VMEM_LIMIT = V7X_VMEM_BYTES * 7 // 8

ATT_TILE = {"a": 256, "b": 512, "c": (512, 2048), "c_bwd": 512, "d": 256}
M_INIT = -1e20
ROW_TILE = 512
EW_ROWS = 256


def _band(reach, tile):
    return -(-reach // tile)


def _call(body, *, name, grid, in_specs, out_specs, out_shape, scratch=(), sem=None, nsp=0):
    params = pltpu.CompilerParams(dimension_semantics=sem, vmem_limit_bytes=VMEM_LIMIT)
    if nsp:
        gs = pltpu.PrefetchScalarGridSpec(num_scalar_prefetch=nsp, grid=grid, in_specs=in_specs,
                                          out_specs=out_specs, scratch_shapes=list(scratch))
        return pl.pallas_call(body, grid_spec=gs, out_shape=out_shape, name=name, compiler_params=params)
    return pl.pallas_call(body, grid=grid, in_specs=in_specs, out_specs=out_specs, out_shape=out_shape,
                          scratch_shapes=list(scratch), name=name, compiler_params=params)


def _sigmoid(x):
    return 1.0 / (1.0 + jnp.exp(-x))


def _matmul(name, grid, ins, in_specs, out_shape, out_specs, *, ta=False, tb=False, k_axis=None,
            acc_shape=None, epilogue=None, a_fn=None):
    n_in = len(ins)
    n_out = len(out_shape)
    nk = grid[k_axis] if k_axis is not None else 1
    dn = (((0 if ta else 1,), (1 if tb else 0,)), ((), ()))

    def body(*refs):
        a = refs[0][...]
        if a_fn is not None:
            a = a_fn(a)
        p = lax.dot_general(a.astype(BF16), refs[1][...].astype(BF16), dn, preferred_element_type=F32)
        extra = refs[2:n_in]
        outs = refs[n_in:n_in + n_out]

        def fin(acc):
            vals = epilogue(acc, *extra) if epilogue is not None else (acc,)
            for o_ref, v in zip(outs, vals):
                o_ref[...] = v.astype(o_ref.dtype)

        if k_axis is None:
            fin(p)
        else:
            acc_ref = refs[-1]
            k = pl.program_id(k_axis)

            @pl.when(k == 0)
            def _():
                acc_ref[...] = p

            @pl.when(k > 0)
            def _():
                acc_ref[...] += p

            @pl.when(k == nk - 1)
            def _():
                fin(acc_ref[...])

    sem = tuple("arbitrary" if ax == k_axis else "parallel" for ax in range(len(grid)))
    scratch = [pltpu.VMEM(acc_shape, F32)] if k_axis is not None else []
    return _call(body, name=name, grid=grid, in_specs=in_specs, out_specs=out_specs, out_shape=out_shape,
                 scratch=scratch, sem=sem)(*ins)


def _norm_mod(name, x, g, scale, shift):
    S, D = x.shape
    ts = min(S, EW_ROWS)

    def body(x_ref, g_ref, sc_ref, sh_ref, h_ref):
        xv = x_ref[...]
        r = lax.rsqrt(jnp.mean(xv * xv, axis=-1, keepdims=True) + EPS)
        h_ref[...] = (((xv * r) * g_ref[...]) * (1.0 + sc_ref[...]) + sh_ref[...]).astype(BF16)

    row = pl.BlockSpec((1, D), lambda i: (0, 0))
    blk = pl.BlockSpec((ts, D), lambda i: (i, 0))
    return _call(body, name=name, grid=(S // ts,), in_specs=[blk, row, row, row], out_specs=blk,
                 out_shape=SDS((S, D), BF16), sem=("parallel",))(x, g, scale, shift)


def _norm_mod_bwd(name, x, dh1, dh2, dxo, g, scale):
    S, D = x.shape
    ts = min(S, EW_ROWS)

    def body(x_ref, a_ref, b_ref, dxo_ref, g_ref, sc_ref, dx_ref, dsh_ref, dsc_ref, dg_ref):
        @pl.when(pl.program_id(0) == 0)
        def _():
            dsh_ref[...] = jnp.zeros_like(dsh_ref)
            dsc_ref[...] = jnp.zeros_like(dsc_ref)
            dg_ref[...] = jnp.zeros_like(dg_ref)

        xv = x_ref[...]
        r = lax.rsqrt(jnp.mean(xv * xv, axis=-1, keepdims=True) + EPS)
        xh = xv * r
        dh = a_ref[...] + b_ref[...]
        gv = g_ref[...]
        one_sc = 1.0 + sc_ref[...]
        dsh_ref[...] += jnp.sum(dh, axis=0, keepdims=True)
        dsc_ref[...] += jnp.sum(dh * xh * gv, axis=0, keepdims=True)
        dg_ref[...] += jnp.sum(dh * xh * one_sc, axis=0, keepdims=True)
        dxh = dh * gv * one_sc
        dx = r * (dxh - xh * jnp.mean(dxh * xh, axis=-1, keepdims=True))
        dx_ref[...] = dxo_ref[...] + dx

    row = pl.BlockSpec((1, D), lambda i: (0, 0))
    blk = pl.BlockSpec((ts, D), lambda i: (i, 0))
    return _call(body, name=name, grid=(S // ts,), in_specs=[blk, blk, blk, blk, row, row],
                 out_specs=[blk, row, row, row],
                 out_shape=[SDS((S, D), F32), SDS((1, D), F32), SDS((1, D), F32), SDS((1, D), F32)],
                 sem=("arbitrary",))(x, dh1, dh2, dxo, g, scale)


def _out_bwd_ew(name, dxo, o2, gate):
    S, D = dxo.shape
    ts = min(S, EW_ROWS)

    def body(dxo_ref, o2_ref, gt_ref, do2_ref, dgt_ref):
        @pl.when(pl.program_id(0) == 0)
        def _():
            dgt_ref[...] = jnp.zeros_like(dgt_ref)

        d = dxo_ref[...]
        do2_ref[...] = (d * gt_ref[...]).astype(BF16)
        dgt_ref[...] += jnp.sum(d * o2_ref[...].astype(F32), axis=0, keepdims=True)

    row = pl.BlockSpec((1, D), lambda i: (0, 0))
    blk = pl.BlockSpec((ts, D), lambda i: (i, 0))
    return _call(body, name=name, grid=(S // ts,), in_specs=[blk, blk, row], out_specs=[blk, row],
                 out_shape=[SDS((S, D), BF16), SDS((1, D), F32)], sem=("arbitrary",))(dxo, o2, gate)


def _merge_bwd_ew(name, dz, mg, proj):
    S, D = dz.shape
    ts = min(S, ROW_TILE)
    td = min(D, 512)
    nd = D // td

    def body(dz_ref, mg_ref, pj_ref, dmg_ref, dpj_ref):
        d = dz_ref[...].astype(F32)
        m = mg_ref[...].astype(F32)
        dmg_ref[...] = (d * pj_ref[...].astype(F32) * m * (1.0 - m)).astype(BF16)
        dpj_ref[...] = (d * m).astype(BF16)

    wide = pl.BlockSpec((ts, td), lambda i, j, n: (i, n * nd + j))
    return _call(body, name=name, grid=(S // ts, nd, 4),
                 in_specs=[pl.BlockSpec((ts, td), lambda i, j, n: (i, j)), wide, wide],
                 out_specs=[wide, wide], out_shape=[SDS((S, 4 * D), BF16), SDS((S, 4 * D), BF16)],
                 sem=("parallel", "parallel", "arbitrary"))(dz, mg, proj)


def _final_loss(name, x, tgt, g):
    S, D = x.shape
    ts = min(S, EW_ROWS)

    def body(x_ref, t_ref, g_ref, dx_ref, dg_ref, loss_ref):
        @pl.when(pl.program_id(0) == 0)
        def _():
            dg_ref[...] = jnp.zeros_like(dg_ref)
            loss_ref[...] = jnp.zeros_like(loss_ref)

        xv = x_ref[...]
        r = lax.rsqrt(jnp.mean(xv * xv, axis=-1, keepdims=True) + EPS)
        xh = xv * r
        gv = g_ref[...]
        err = xh * gv - t_ref[...]
        row_loss = jnp.mean(err * err, axis=-1, keepdims=True)
        loss_ref[...] += 0.5 * jnp.sum(row_loss, axis=0, keepdims=True)
        dy = err * (1.0 / D)
        dg_ref[...] += jnp.sum(dy * xh, axis=0, keepdims=True)
        dxh = dy * gv
        dx_ref[...] = r * (dxh - xh * jnp.mean(dxh * xh, axis=-1, keepdims=True))

    row = pl.BlockSpec((1, D), lambda i: (0, 0))
    blk = pl.BlockSpec((ts, D), lambda i: (i, 0))
    return _call(body, name=name, grid=(S // ts,), in_specs=[blk, blk, row],
                 out_specs=[blk, row, pl.BlockSpec((1, 128), lambda i: (0, 0))],
                 out_shape=[SDS((S, D), F32), SDS((1, D), F32), SDS((1, 128), F32)],
                 sem=("arbitrary",))(x, tgt, g)


def _rope_tables(S):
    def tables(pos, dim):
        inv = ROPE_THETA ** (-jnp.arange(0, dim, 2, dtype=F32) / dim)
        ang = pos.astype(F32)[:, None] * inv[None, :]
        ang = jnp.concatenate([ang, ang], axis=-1)
        return jnp.cos(ang), jnp.sin(ang)

    pos = jnp.arange(S, dtype=I32)
    lane = np.arange(HEAD_DIM)
    cos1, sin1 = tables(pos, HEAD_DIM)
    up1 = jnp.asarray((lane >= 64).astype(np.float32))[None, :]
    one_d = (cos1, sin1 * up1, -sin1 * (1.0 - up1))
    cr, sr = tables(pos // GRID_W, HEAD_DIM // 2)
    cc, sc = tables(pos % GRID_W, HEAD_DIM // 2)
    cos2 = jnp.concatenate([cr, cc], axis=-1)
    sin2 = jnp.concatenate([sr, sc], axis=-1)
    up2 = jnp.asarray(((lane % 64) >= 32).astype(np.float32))[None, :]
    axial = (cos2, sin2 * up2, -sin2 * (1.0 - up2))
    return one_d, axial


def _rope_fwd(name, src, c0, nb, tabs, sh, gain=None):
    S = src.shape[0]
    ts = min(S, ROW_TILE)
    has_gain = gain is not None

    def body(*refs):
        x_ref, c_ref, sa_ref, sb_ref = refs[:4]
        o_ref = refs[-1]
        xv = x_ref[...].astype(F32)
        if has_gain:
            r = lax.rsqrt(jnp.mean(xv * xv, axis=-1, keepdims=True) + EPS)
            xv = (xv * r) * refs[4][...]
        out = xv * c_ref[...] + pltpu.roll(xv, sh, 1) * sa_ref[...] + pltpu.roll(xv, HEAD_DIM - sh, 1) * sb_ref[...]
        o_ref[...] = out.astype(BF16)

    tab = pl.BlockSpec((ts, HEAD_DIM), lambda i, j: (i, 0))
    in_specs = [pl.BlockSpec((ts, HEAD_DIM), lambda i, j: (i, c0 + j)), tab, tab, tab]
    ins = [src, *tabs]
    if has_gain:
        in_specs.append(pl.BlockSpec((1, HEAD_DIM), lambda i, j: (0, 0)))
        ins.append(gain)
    return _call(body, name=name, grid=(S // ts, nb), in_specs=in_specs,
                 out_specs=pl.BlockSpec((ts, HEAD_DIM), lambda i, j: (i, j)),
                 out_shape=SDS((S, nb * HEAD_DIM), BF16), sem=("parallel", "parallel"))(*ins)


def _rope_bwd(name, dout, src, c0, nb, tabs, sh, gain=None):
    S = src.shape[0]
    ts = min(S, ROW_TILE)
    has_gain = gain is not None

    def body(*refs):
        d_ref, x_ref, c_ref, sa_ref, sb_ref = refs[:5]
        d = d_ref[...].astype(F32)
        dxn = (d * c_ref[...] + pltpu.roll(d * sa_ref[...], HEAD_DIM - sh, 1) + pltpu.roll(d * sb_ref[...], sh, 1))
        if has_gain:
            gn_ref, dx_ref, dgn_ref = refs[5:]

            @pl.when((pl.program_id(0) == 0) & (pl.program_id(1) == 0))
            def _():
                dgn_ref[...] = jnp.zeros_like(dgn_ref)

            xv = x_ref[...].astype(F32)
            r = lax.rsqrt(jnp.mean(xv * xv, axis=-1, keepdims=True) + EPS)
            xh = xv * r
            dgn_ref[...] += jnp.sum(dxn * xh, axis=0, keepdims=True)
            dxh = dxn * gn_ref[...]
            dx_ref[...] = (r * (dxh - xh * jnp.mean(dxh * xh, axis=-1, keepdims=True))).astype(BF16)
        else:
            refs[5][...] = dxn.astype(BF16)

    tab = pl.BlockSpec((ts, HEAD_DIM), lambda i, j: (i, 0))
    own = pl.BlockSpec((ts, HEAD_DIM), lambda i, j: (i, j))
    in_specs = [own, pl.BlockSpec((ts, HEAD_DIM), lambda i, j: (i, c0 + j)), tab, tab, tab]
    ins = [dout, src, *tabs]
    out_specs = [own]
    out_shape = [SDS((S, nb * HEAD_DIM), BF16)]
    if has_gain:
        row = pl.BlockSpec((1, HEAD_DIM), lambda i, j: (0, 0))
        in_specs.append(row)
        ins.append(gain)
        out_specs.append(row)
        out_shape.append(SDS((1, HEAD_DIM), F32))
    res = _call(body, name=name, grid=(S // ts, nb), in_specs=in_specs, out_specs=out_specs,
                out_shape=out_shape, sem=("arbitrary", "arbitrary"))(*ins)
    return res if has_gain else (res[0], None)


def _offset_grid(T, W):
    d = (np.arange(2 * W + 1) - W)[:, None, None] * T
    return d + np.arange(T)[None, :, None] - np.arange(T)[None, None, :]


def _mask_tiles_a(T):
    dk = _offset_grid(T, _band(128, T))
    return np.where(np.abs(dk) <= 128, 0.0, NEG_INF).astype(np.float32)[None, None]


def _mask_tiles_b(T):
    dk = _offset_grid(T, _band(1024, T))
    ad = np.abs(dk)
    mult = ((ad <= 64).astype(np.float32) + ((ad <= 256) & (dk % 4 == 0)) + ((ad <= 1024) & (dk % 16 == 0)))
    return np.where(mult > 0, np.log(np.maximum(mult, 1.0)), NEG_INF).astype(np.float32)[None, None]


def _edge_blocks_d(T):
    return -(-4 // (T // GRID_W))


def _mask_tiles_d(S, T):
    rows, nq, rpb = S // GRID_W, S // T, T // GRID_W
    W, E = -(-7 // rpb), _edge_blocks_d(T)
    assert nq >= 2 * E + 1
    out = []
    for i in [*range(E), nq // 2, *range(nq - E, nq)]:
        kp = ((i + np.arange(2 * W + 1) - W) * T)[:, None, None] + np.arange(T)[None, :, None]
        qp = i * T + np.arange(T)[None, None, :]
        qr, qc, kr, kc = qp >> 6, qp & 63, kp >> 6, kp & 63
        rs = np.clip(qr - 4, 0, rows - 8)
        cs = np.clip(qc - 8, 0, GRID_W - 16)
        valid = (kr >= rs) & (kr < rs + 8) & (kc >= cs) & (kc < cs + 16)
        out.append(np.where(valid, 0.0, NEG_INF).astype(np.float32))
    return np.stack(out)[:, None]


def _variant(i, nq, E):
    if E == 0:
        return 0
    return jnp.where(i < E, i, jnp.where(i >= nq - E, i - (nq - 2 * E - 1), E))


_NT = (((1,), (1,)), ((), ()))
_TN = (((0,), (0,)), ((), ()))
_NN = (((1,), (0,)), ((), ()))


class _Mixer:
    def __init__(self, tq, tk, W, G, E=0):
        self.tq, self.tk, self.W, self.G, self.E = tq, tk, W, G, E


def _attn_fwd(name, mx, q_arr, qc0, k_arr, kc0, v_arr, vc0, u, gc0, sink, bias=None, carry=None):
    S = q_arr.shape[0]
    tq, tk, W, G = mx.tq, mx.tk, mx.W, mx.G
    nq, nk = S // tq, S // tk
    nd = nk if W is None else 2 * W + 1
    has_bias = bias is not None
    nc = 0 if carry is None else carry.n
    hd = HEAD_DIM

    def jmap(i, d):
        return d if W is None else jnp.clip(i + d - W, 0, nk - 1)

    def body(*refs):
        sink_ref, q_ref, k_ref, v_ref, g_ref = refs[:5]
        bias_ref = refs[5] if has_bias else None
        n_in = 6 if has_bias else 5
        out0 = n_in + 2 * nc
        br_ref, o_ref, lse_ref, m_s, l_s, acc_s = refs[out0:out0 + 6]
        h, i, d = pl.program_id(0), pl.program_id(1), pl.program_id(2)
        j = d if W is None else i + d - W
        if carry is not None:
            carry_refs = (refs[n_in:n_in + nc], refs[n_in + nc:out0], refs[-2], refs[-1])
            pl.when((h == 0) & (i == 0) & (d == 0))(lambda: carry.start(*carry_refs))

        @pl.when(d == 0)
        def _():
            m_s[...] = jnp.full_like(m_s, M_INIT)
            l_s[...] = jnp.zeros_like(l_s)
            acc_s[...] = jnp.zeros_like(acc_s)

        def step():
            s = lax.dot_general(k_ref[...], q_ref[...], _NT, preferred_element_type=F32) * SCALE
            if has_bias:
                s = s + bias_ref[_variant(i, nq, mx.E), d]
            m_prev = m_s[...]
            m_new = jnp.maximum(m_prev, jnp.max(s, axis=0, keepdims=True))
            alpha = jnp.exp(m_prev - m_new)
            p = jnp.exp(s - m_new)
            l_s[...] = alpha * l_s[...] + jnp.sum(p, axis=0, keepdims=True)
            acc_s[...] = alpha * acc_s[...] + lax.dot_general(v_ref[...], p.astype(BF16), _TN,
                                                               preferred_element_type=F32)
            m_s[...] = m_new

        if W is None:
            step()
        else:
            pl.when((j >= 0) & (j < nk))(step)

        @pl.when(d == nd - 1)
        def _():
            sk = sink_ref[h]
            m = m_s[...]
            mf = jnp.maximum(m, sk)
            a = jnp.exp(m - mf)
            lf = l_s[...] * a + jnp.exp(sk - mf)
            o = ((acc_s[...] * a) / lf).T
            gv = g_ref[...].astype(F32)
            o_ref[...] = o.astype(BF16)
            br_ref[...] = (o * (gv * _sigmoid(gv))).astype(BF16)
            lse_ref[0] = mf + jnp.log(lf)

        if carry is not None:
            pl.when((h == 3) & (i == nq - 1) & (d == nd - 1))(lambda: carry.finish(*carry_refs))

    in_specs = [pl.BlockSpec(memory_space=pltpu.SMEM),
                pl.BlockSpec((tq, hd), lambda h, i, d: (i, qc0 + h)),
                pl.BlockSpec((tk, hd), lambda h, i, d: (jmap(i, d), kc0 + h // G)),
                pl.BlockSpec((tk, hd), lambda h, i, d: (jmap(i, d), vc0 + h // G)),
                pl.BlockSpec((tq, hd), lambda h, i, d: (i, gc0 + h))]
    ins = [sink, q_arr, k_arr, v_arr, u]
    if has_bias:
        per_head = bias.shape[1] == 4
        in_specs.append(pl.BlockSpec((bias.shape[0], None, nd, tk, tq),
                                     lambda h, i, d: (0, h if per_head else 0, 0, 0, 0)))
        ins.append(bias)
    own = pl.BlockSpec((tq, hd), lambda h, i, d: (i, h))
    out_specs = [own, own, pl.BlockSpec((1, 1, tq), lambda h, i, d: (h, 0, i))]
    out_shape = [SDS((S, 4 * hd), BF16), SDS((S, 4 * hd), BF16), SDS((4, 1, S), F32)]
    scratch = [pltpu.VMEM((1, tq), F32), pltpu.VMEM((1, tq), F32), pltpu.VMEM((hd, tq), F32)]
    sem = ("parallel", "parallel", "arbitrary")
    if carry is not None:
        ins += carry.ins
        in_specs += carry.in_specs
        out_specs = carry.out_specs + out_specs
        out_shape = carry.out_shape + out_shape
        scratch += carry.scratch
        sem = ("arbitrary",) * 3
    res = _call(body, name=name, grid=(4, nq, nd), in_specs=in_specs, out_specs=out_specs, out_shape=out_shape,
                scratch=scratch, sem=sem)(*ins)
    return (*res[nc:], list(res[:nc]))


def _attn_bwd(name, mx, q_arr, qc0, k_arr, kc0, v_arr, vc0, do_all, hb0, lse, delta, bias=None, want_dbias=False,
              carry=None):
    S = q_arr.shape[0]
    tq, tk, W, G = mx.tq, mx.tk, mx.W, mx.G
    nq, nk = S // tq, S // tk
    nd = nq if W is None else 2 * W + 1
    n_kv = 4 // G
    hd = HEAD_DIM
    has_bias = bias is not None
    nc = 0 if carry is None else carry.n
    assert qc0 % G == 0 and hb0 % G == 0 and (not want_dbias or (has_bias and G == 1)) and (W is None or tq == tk)

    def imap(j, d):
        return d if W is None else jnp.clip(j + d - W, 0, nq - 1)

    def body(*refs):
        q_ref, k_ref, v_ref, do_ref, lse_ref, dl_ref = refs[:6]
        bias_ref = refs[6] if has_bias else None
        n_in = 7 if has_bias else 6
        out0 = n_in + 2 * nc
        dq_ref, dk_ref, dv_ref = refs[out0:out0 + 3]
        n_o = 4 if want_dbias else 3
        db_ref = refs[out0 + 3] if want_dbias else None
        dk_s, dv_s = refs[out0 + n_o:out0 + n_o + 2]
        kv, j, d = pl.program_id(0), pl.program_id(1), pl.program_id(2)
        i = d if W is None else j + d - W
        if carry is not None:
            carry_refs = (refs[n_in:n_in + nc], refs[n_in + nc:out0], refs[-2], refs[-1])
            pl.when((kv == 0) & (j == 0) & (d == 0))(lambda: carry.start(*carry_refs))

        @pl.when((j == 0) & (d == 0))
        def _():
            dq_ref[...] = jnp.zeros_like(dq_ref)
            if want_dbias:
                db_ref[...] = jnp.zeros_like(db_ref)

        @pl.when(d == 0)
        def _():
            dk_s[...] = jnp.zeros_like(dk_s)
            dv_s[...] = jnp.zeros_like(dv_s)

        def step():
            k = k_ref[...]
            v = v_ref[...]
            row0 = pl.multiple_of(i * tq, tq)
            for g in range(G):
                lanes = slice(g * hd, (g + 1) * hd)
                q = q_ref[:, lanes]
                do = do_ref[:, lanes]
                s = lax.dot_general(k, q, _NT, preferred_element_type=F32) * SCALE
                if has_bias:
                    s = s + bias_ref[_variant(i, nq, mx.E), 2 * W - d]
                p = jnp.exp(s - lse_ref[g])
                dv_s[...] += lax.dot_general(p.astype(BF16), do, _NN, preferred_element_type=F32)
                dp = lax.dot_general(v, do, _NT, preferred_element_type=F32)
                ds = p * (dp - dl_ref[g])
                if want_dbias:
                    db_ref[2 * W - d] += ds
                dsb = ds.astype(BF16)
                dk_s[...] += lax.dot_general(dsb, q, _NN, preferred_element_type=F32)
                dq_ref[pl.ds(row0, tq), lanes] += lax.dot_general(dsb, k, _TN, preferred_element_type=F32) * SCALE

        if W is None:
            step()
        else:
            pl.when((i >= 0) & (i < nq))(step)

        @pl.when(d == nd - 1)
        def _():
            dk_ref[...] = dk_s[...] * SCALE
            dv_ref[...] = dv_s[...]

        if carry is not None:
            pl.when((kv == n_kv - 1) & (j == nk - 1) & (d == nd - 1))(lambda: carry.finish(*carry_refs))

    in_specs = [pl.BlockSpec((tq, G * hd), lambda kv, j, d: (imap(j, d), qc0 // G + kv)),
                pl.BlockSpec((tk, hd), lambda kv, j, d: (j, kc0 + kv)),
                pl.BlockSpec((tk, hd), lambda kv, j, d: (j, vc0 + kv)),
                pl.BlockSpec((tq, G * hd), lambda kv, j, d: (imap(j, d), hb0 // G + kv)),
                pl.BlockSpec((G, 1, tq), lambda kv, j, d: (kv, 0, imap(j, d))),
                pl.BlockSpec((G, 1, tq), lambda kv, j, d: (hb0 // G + kv, 0, imap(j, d)))]
    ins = [q_arr, k_arr, v_arr, do_all, lse, delta]
    if has_bias:
        per_head = bias.shape[1] == 4
        in_specs.append(pl.BlockSpec((bias.shape[0], None, nd, tk, tq),
                                     lambda kv, j, d: (0, kv if per_head else 0, 0, 0, 0)))
        ins.append(bias)
    kv_blk = pl.BlockSpec((tk, hd), lambda kv, j, d: (j, kv))
    out_specs = [pl.BlockSpec((S, G * hd), lambda kv, j, d: (0, kv)), kv_blk, kv_blk]
    out_shape = [SDS((S, 4 * hd), F32), SDS((S, n_kv * hd), F32), SDS((S, n_kv * hd), F32)]
    if want_dbias:
        out_specs.append(pl.BlockSpec((None, nd, tk, tq), lambda kv, j, d: (kv, 0, 0, 0)))
        out_shape.append(SDS((4, nd, tk, tq), F32))
    scratch = [pltpu.VMEM((tk, hd), F32), pltpu.VMEM((tk, hd), F32)]
    sem = ("parallel", "arbitrary", "arbitrary")
    if carry is not None:
        ins += carry.ins
        in_specs += carry.in_specs
        out_specs = carry.out_specs + out_specs
        out_shape = carry.out_shape + out_shape
        scratch += carry.scratch
        sem = ("arbitrary",) * 3
    res = _call(body, name=name, grid=(n_kv, nk, nd), in_specs=in_specs, out_specs=out_specs,
                out_shape=out_shape, scratch=scratch, sem=sem)(*ins)
    main = res[nc:]
    return (*main[:3], main[3] if want_dbias else None, list(res[:nc]))


def _attn_bwd_pre(name, dbr, o_all, u):
    S = dbr.shape[0]
    ts = min(S, ROW_TILE)
    hd = HEAD_DIM

    def gcol(hb):
        n = hb // 4
        return GCOL[0] + n * 16 - jnp.where(n >= 2, 4, 0) + hb % 4

    def body(dbr_ref, o_ref, g_ref, do_ref, dg_ref, dl_ref):
        db = dbr_ref[...].astype(F32)
        o = o_ref[...].astype(F32)
        gv = g_ref[...].astype(F32)
        sg = _sigmoid(gv)
        do = db * (gv * sg)
        do_ref[...] = do.astype(BF16)
        dg_ref[...] = (db * o * (sg * (1.0 + gv * (1.0 - sg)))).astype(BF16)
        dl_ref[0] = jnp.sum((do * o).T, axis=0, keepdims=True)

    own = pl.BlockSpec((ts, hd), lambda i, hb: (i, hb))
    return _call(body, name=name, grid=(S // ts, 16),
                 in_specs=[own, own, pl.BlockSpec((ts, hd), lambda i, hb: (i, gcol(hb)))],
                 out_specs=[own, own, pl.BlockSpec((1, 1, ts), lambda i, hb: (hb, 0, i))],
                 out_shape=[SDS((S, 16 * hd), BF16), SDS((S, 16 * hd), BF16), SDS((16, 1, S), F32)],
                 sem=("parallel", "parallel"))(dbr, o_all, u)


def _sink_grad(name, sink, lse, delta):
    S = lse.shape[2]
    ts = min(S, 2048)

    def body(sink_ref, lse_ref, dl_ref, out_ref):
        @pl.when(pl.program_id(1) == 0)
        def _():
            out_ref[...] = jnp.zeros_like(out_ref)

        sk = sink_ref[pl.program_id(0)]
        part = jnp.sum(jnp.exp(sk - lse_ref[0]) * dl_ref[0], axis=1, keepdims=True)
        out_ref[0] += -jnp.broadcast_to(part, (1, 128))

    col = pl.BlockSpec((1, 1, ts), lambda h, i: (h, 0, i))
    return _call(body, name=name, grid=(4, S // ts),
                 in_specs=[pl.BlockSpec(memory_space=pltpu.SMEM), col, col],
                 out_specs=pl.BlockSpec((1, 1, 128), lambda h, i: (h, 0, 0)),
                 out_shape=SDS((4, 1, 128), F32), sem=("parallel", "arbitrary"))(sink, lse, delta)


def _bias_maps(T, W):
    rpb = T // GRID_W
    nd = 2 * W + 1
    rmap = np.zeros((nd, rpb, rpb, 15), np.float32)
    for df in range(nd):
        for a in range(rpb):
            for b in range(rpb):
                r = (df - W) * rpb + b - a + 7
                if 0 <= r < 15:
                    rmap[df, a, b, r] = 1.0
    cmap = np.zeros((GRID_W, GRID_W, 31), np.float32)
    for q in range(GRID_W):
        for k in range(GRID_W):
            cmap[q, k, int(np.clip(k - q, -15, 15)) + 15] = 1.0
    return jnp.asarray(rmap), jnp.asarray(cmap)


def _bias_tiles(rel_bias, S, T):
    W = -(-7 // (T // GRID_W))
    rmap, cmap = _bias_maps(T, W)
    t = jnp.einsum("dabr,hrc,qkc->hdbkaq", rmap, rel_bias, cmap, precision=lax.Precision.HIGHEST)
    return t.reshape(1, 4, 2 * W + 1, T, T) + jnp.asarray(_mask_tiles_d(S, T))


def _bias_tiles_t(dtiles, T):
    rpb = T // GRID_W
    W = -(-7 // rpb)
    rmap, cmap = _bias_maps(T, W)
    t = dtiles.reshape(4, 2 * W + 1, rpb, GRID_W, rpb, GRID_W)
    return jnp.einsum("dabr,hdbkaq,qkc->hrc", rmap, t, cmap, precision=lax.Precision.HIGHEST)


def _mixer_cfg(S):
    ta, tb, td = (min(S, ATT_TILE[k]) for k in "abd")
    cq, ck = (min(S, t) for t in ATT_TILE["c"])
    cb = min(S, ATT_TILE["c_bwd"])
    a = _Mixer(ta, ta, _band(128, ta), 2)
    b = _Mixer(tb, tb, _band(1024, tb), 1)
    d = _Mixer(td, td, -(-7 // (td // GRID_W)), 1, _edge_blocks_d(td))
    return {"a": (a, a, jnp.asarray(_mask_tiles_a(ta))), "b": (b, b, jnp.asarray(_mask_tiles_b(tb))),
            "c": (_Mixer(cq, ck, None, 2), _Mixer(cb, cb, None, 2), None), "d": (d, d, None)}


def _layer_fwd(l, x, c8, lw, p, tabs, next_shards, me):
    S, D = x.shape
    dq = D // N_SHARD
    ada_sh, win_sh, wgm_sh, wb_sh, wout_sh = lw
    one_d, axial = tabs
    cfg = _mixer_cfg(S)
    tm = min(S, ROW_TILE)

    ada = _matmul(f"ada_l{l}", (N_SHARD,), [c8, ada_sh, p["b_ada"][l][None, :]],
                  [pl.BlockSpec((8, D), lambda j: (0, 0)), pl.BlockSpec((None, D, 3 * dq), lambda j: (j, 0, 0)),
                   pl.BlockSpec((1, 3 * dq), lambda j: (0, j))],
                  [SDS((8, 3 * D), F32)], [pl.BlockSpec((8, 3 * dq), lambda j: (0, j))],
                  epilogue=lambda acc, b_ref: (acc + b_ref[...],), a_fn=lambda a: a * _sigmoid(a))[0][0:1]
    shift, scale, gate = ada[:, :D], ada[:, D:2 * D], ada[:, 2 * D:]
    g_row = p["norm_g"][l][None, :]
    h = _norm_mod(f"norm_mod_l{l}", x, g_row, scale, shift)

    u = _matmul(f"mm_in_l{l}", (S // tm, N_SHARD), [h, win_sh],
                [pl.BlockSpec((tm, D), lambda i, j: (i, 0)), pl.BlockSpec((None, D, IN_SHARD), lambda i, j: (j, 0, 0))],
                [SDS((S, IN_COLS), BF16)], [pl.BlockSpec((tm, IN_SHARD), lambda i, j: (i, j))])[0]

    qa = _rope_fwd(f"rope_qa_l{l}", u, QCOL[0], 4, one_d, 64)
    ka = _rope_fwd(f"rope_ka_l{l}", u, KCOL[0], 2, one_d, 64)
    qb = _rope_fwd(f"rope_qb_l{l}", u, QCOL[1], 4, one_d, 64)
    kb = _rope_fwd(f"rope_kb_l{l}", u, KCOL[1], 4, one_d, 64)
    qc = _rope_fwd(f"rope_qc_l{l}", u, QCOL[2], 4, axial, 32, p["c_q_norm"][l][None, :])
    kc = _rope_fwd(f"rope_kc_l{l}", u, KCOL[2], 2, axial, 32, p["c_k_norm"][l][None, :])

    carries = [None] * 4
    if next_shards is not None:
        s_ada, s_in, s_gm, s_wb, s_out = next_shards
        carries = [_Carry("gather", [s_wb, s_out]), _Carry("gather", [s_in]), _Carry("gather", [s_gm]),
                   _Carry("gather", [s_ada])]
    no_sink = jnp.full((4,), NEG_INF, F32)
    bias = _bias_tiles(p["d_rel_bias"][l], S, cfg["d"][0].tq)
    br_a, o_a, lse_a, got_a = _attn_fwd(f"attn_a_l{l}", cfg["a"][0], qa, 0, ka, 0, u, VCOL[0], u, GCOL[0],
                                        p["a_sink"][l], cfg["a"][2], carries[0])
    br_b, o_b, lse_b, got_b = _attn_fwd(f"attn_b_l{l}", cfg["b"][0], qb, 0, kb, 0, u, VCOL[1], u, GCOL[1], no_sink,
                                        cfg["b"][2], carries[1])
    br_c, o_c, lse_c, got_c = _attn_fwd(f"attn_c_l{l}", cfg["c"][0], qc, 0, kc, 0, u, VCOL[2], u, GCOL[2], no_sink,
                                        None, carries[2])
    br_d, o_d, lse_d, got_d = _attn_fwd(f"attn_d_l{l}", cfg["d"][0], u, QCOL[3], u, KCOL[3], u, VCOL[3], u, GCOL[3],
                                        no_sink, bias, carries[3])
    next_lw = None
    if next_shards is not None:
        next_lw = _own_slot([got_d[0], got_b[0], got_c[0], got_a[0], got_a[1]], next_shards, me)
    br = jnp.concatenate([br_a, br_b, br_c, br_d], axis=1)
    o_all = jnp.concatenate([o_a, o_b, o_c, o_d], axis=1)

    def merge_body(h_ref, wg_ref, br_ref, wb_ref, mg_ref, pj_ref, z_ref, acc_ref):
        n = pl.program_id(2)
        mgv = _sigmoid(lax.dot_general(h_ref[...], wg_ref[...], _NN, preferred_element_type=F32))
        pj = lax.dot_general(br_ref[...], wb_ref[...], _NN, preferred_element_type=F32)
        mg_ref[...] = mgv.astype(BF16)
        pj_ref[...] = pj.astype(BF16)

        @pl.when(n == 0)
        def _():
            acc_ref[...] = mgv * pj

        @pl.when(n > 0)
        def _():
            acc_ref[...] += mgv * pj

        @pl.when(n == 3)
        def _():
            z_ref[...] = acc_ref[...].astype(BF16)

    wide = pl.BlockSpec((tm, dq), lambda i, j, n: (i, n * N_SHARD + j))
    mg, proj, z = _call(
        merge_body, name=f"merge_l{l}", grid=(S // tm, N_SHARD, 4),
        in_specs=[pl.BlockSpec((tm, D), lambda i, j, n: (i, 0)),
                  pl.BlockSpec((None, D, dq), lambda i, j, n: (n, 0, j)),
                  pl.BlockSpec((tm, BRANCH_W), lambda i, j, n: (i, n)),
                  pl.BlockSpec((None, None, BRANCH_W, dq), lambda i, j, n: (j, n, 0, 0))],
        out_specs=[wide, wide, pl.BlockSpec((tm, dq), lambda i, j, n: (i, j))],
        out_shape=[SDS((S, 4 * D), BF16), SDS((S, 4 * D), BF16), SDS((S, D), BF16)],
        scratch=[pltpu.VMEM((tm, dq), F32)], sem=("parallel", "parallel", "arbitrary"))(h, wgm_sh, br, wb_sh)

    tn = min(D, 1024)
    x_new, o2 = _matmul(
        f"mm_out_l{l}", (S // tm, D // tn, N_SHARD), [z, wout_sh, x, gate],
        [pl.BlockSpec((tm, dq), lambda i, j, k: (i, k)), pl.BlockSpec((None, dq, tn), lambda i, j, k: (k, 0, j)),
         pl.BlockSpec((tm, tn), lambda i, j, k: (i, j)), pl.BlockSpec((1, tn), lambda i, j, k: (0, j))],
        [SDS((S, D), F32), SDS((S, D), BF16)],
        [pl.BlockSpec((tm, tn), lambda i, j, k: (i, j)), pl.BlockSpec((tm, tn), lambda i, j, k: (i, j))],
        k_axis=2, acc_shape=(tm, tn), epilogue=lambda acc, x_ref, g_ref: (x_ref[...] + g_ref[...] * acc, acc))
    res = dict(x=x, h=h, u=u, qa=qa, ka=ka, qb=qb, kb=kb, qc=qc, kc=kc, br=br, o_all=o_all,
               lse=(lse_a, lse_b, lse_c, lse_d), bias=bias, mg=mg, proj=proj, z=z, o2=o2,
               g_row=g_row, scale=scale, gate=gate)
    return x_new, res, next_lw


def _layer_bwd(l, dxo, r, lw, p, tabs, pending):
    x, h, u = r["x"], r["h"], r["u"]
    S, D = x.shape
    dq = D // N_SHARD
    ada_sh, win_sh, wgm_sh, wb_sh, wout_sh = lw
    one_d, axial = tabs
    cfg = _mixer_cfg(S)
    tm = min(S, ROW_TILE)
    tk = min(S, 512)
    tn = min(D, 1024)

    do2, dgate = _out_bwd_ew(f"out_bwd_l{l}", dxo, r["o2"], r["gate"])
    dz = _matmul(f"mm_dz_l{l}", (S // tm, N_SHARD), [do2, wout_sh],
                 [pl.BlockSpec((tm, D), lambda i, n: (i, 0)), pl.BlockSpec((None, dq, D), lambda i, n: (n, 0, 0))],
                 [SDS((S, D), BF16)], [pl.BlockSpec((tm, dq), lambda i, n: (i, n))], tb=True)[0]
    g_out = _matmul(f"mm_gwout_l{l}", (N_SHARD, D // tn, S // tk), [r["z"], do2],
                    [pl.BlockSpec((tk, dq), lambda n, j, k: (k, n)), pl.BlockSpec((tk, tn), lambda n, j, k: (k, j))],
                    [SDS((N_SHARD, dq, D), F32)], [pl.BlockSpec((None, dq, tn), lambda n, j, k: (n, 0, j))],
                    ta=True, k_axis=2, acc_shape=(dq, tn))[0]

    dmg, dproj = _merge_bwd_ew(f"merge_bwd_l{l}", dz, r["mg"], r["proj"])
    nj = D // tn
    g_gm = _matmul(f"mm_gwgm_l{l}", (4, D // tn, nj, S // tk), [h, dmg],
                   [pl.BlockSpec((tk, tn), lambda n, i, j, k: (k, i)),
                    pl.BlockSpec((tk, tn), lambda n, i, j, k: (k, n * nj + j))],
                   [SDS((4, D, D), F32)], [pl.BlockSpec((None, tn, tn), lambda n, i, j, k: (n, i, j))],
                   ta=True, k_axis=3, acc_shape=(tn, tn))[0]
    dh1 = _matmul(f"mm_dh1_l{l}", (S // tm, nj, 4 * nj), [dmg, wgm_sh],
                  [pl.BlockSpec((tm, tn), lambda i, j, kk: (i, kk)),
                   pl.BlockSpec((None, tn, tn), lambda i, j, kk: (kk // nj, j, kk % nj))],
                  [SDS((S, D), F32)], [pl.BlockSpec((tm, tn), lambda i, j, kk: (i, j))],
                  tb=True, k_axis=2, acc_shape=(tm, tn))[0]
    dbr = _matmul(f"mm_dbr_l{l}", (S // tm, 4, N_SHARD), [dproj, wb_sh],
                  [pl.BlockSpec((tm, dq), lambda i, n, j: (i, n * N_SHARD + j)),
                   pl.BlockSpec((None, None, BRANCH_W, dq), lambda i, n, j: (j, n, 0, 0))],
                  [SDS((S, 4 * BRANCH_W), BF16)], [pl.BlockSpec((tm, BRANCH_W), lambda i, n, j: (i, n))],
                  tb=True, k_axis=2, acc_shape=(tm, BRANCH_W))[0]
    g_wb = _matmul(f"mm_gwb_l{l}", (N_SHARD, 4, S // tk), [r["br"], dproj],
                   [pl.BlockSpec((tk, BRANCH_W), lambda j, n, k: (k, n)),
                    pl.BlockSpec((tk, dq), lambda j, n, k: (k, n * N_SHARD + j))],
                   [SDS((N_SHARD, 4, BRANCH_W, dq), F32)],
                   [pl.BlockSpec((None, None, BRANCH_W, dq), lambda j, n, k: (j, n, 0, 0))],
                   ta=True, k_axis=2, acc_shape=(BRANCH_W, dq))[0]

    do_all, dg_all, delta = _attn_bwd_pre(f"attn_pre_l{l}", dbr, r["o_all"], u)
    lse_a, lse_b, lse_c, lse_d = r["lse"]
    dsink = _sink_grad(f"sink_grad_l{l}", p["a_sink"][l], lse_a, delta)[:, 0, 0]
    carries = [None] * 3
    if pending is not None:
        p_in, p_gm, p_wb, p_out = pending
        carries = [_Carry("exchange", [p_in]), _Carry("exchange", [p_gm]), _Carry("exchange", [p_wb, p_out])]
    dqa, dka, dva, _, _ = _attn_bwd(f"attn_a_bwd_l{l}", cfg["a"][1], r["qa"], 0, r["ka"], 0, u, VCOL[0], do_all, 0,
                                    lse_a, delta, cfg["a"][2])
    dqb, dkb, dvb, _, got_b = _attn_bwd(f"attn_b_bwd_l{l}", cfg["b"][1], r["qb"], 0, r["kb"], 0, u, VCOL[1], do_all, 4,
                                        lse_b, delta, cfg["b"][2], carry=carries[0])
    dqc, dkc, dvc, _, got_c = _attn_bwd(f"attn_c_bwd_l{l}", cfg["c"][1], r["qc"], 0, r["kc"], 0, u, VCOL[2], do_all, 8,
                                        lse_c, delta, carry=carries[1])
    dqd, dkd, dvd, dbias, got_d = _attn_bwd(f"attn_d_bwd_l{l}", cfg["d"][1], u, QCOL[3], u, KCOL[3], u, VCOL[3],
                                            do_all, 12, lse_d, delta, r["bias"], True, carries[2])
    arrived = None if pending is None else [got_b[0], got_c[0], got_d[0], got_d[1]]
    d_rel = _bias_tiles_t(dbias, cfg["d"][1].tq)

    duqa, _ = _rope_bwd(f"rope_qa_bwd_l{l}", dqa, u, QCOL[0], 4, one_d, 64)
    duka, _ = _rope_bwd(f"rope_ka_bwd_l{l}", dka, u, KCOL[0], 2, one_d, 64)
    duqb, _ = _rope_bwd(f"rope_qb_bwd_l{l}", dqb, u, QCOL[1], 4, one_d, 64)
    dukb, _ = _rope_bwd(f"rope_kb_bwd_l{l}", dkb, u, KCOL[1], 4, one_d, 64)
    duqc, dcq = _rope_bwd(f"rope_qc_bwd_l{l}", dqc, u, QCOL[2], 4, axial, 32, p["c_q_norm"][l][None, :])
    dukc, dck = _rope_bwd(f"rope_kc_bwd_l{l}", dkc, u, KCOL[2], 2, axial, 32, p["c_k_norm"][l][None, :])
    bw = BRANCH_W
    du = jnp.concatenate(
        [duqa, duka, dva.astype(BF16), dg_all[:, 0:bw],
         duqb, dukb, dvb.astype(BF16), dg_all[:, bw:2 * bw],
         duqc, dukc, dvc.astype(BF16), dg_all[:, 2 * bw:3 * bw],
         dqd.astype(BF16), dkd.astype(BF16), dvd.astype(BF16), dg_all[:, 3 * bw:]], axis=1)

    tmi = min(D, 1024)
    g_in = _matmul(f"mm_gwin_l{l}", (N_SHARD, D // tmi, S // tk), [h, du],
                   [pl.BlockSpec((tk, tmi), lambda j, i, k: (k, i)), pl.BlockSpec((tk, IN_SHARD), lambda j, i, k: (k, j))],
                   [SDS((N_SHARD, D, IN_SHARD), F32)], [pl.BlockSpec((None, tmi, IN_SHARD), lambda j, i, k: (j, i, 0))],
                   ta=True, k_axis=2, acc_shape=(tmi, IN_SHARD))[0]
    dh2 = _matmul(f"mm_dh2_l{l}", (S // tm, nj, N_SHARD), [du, win_sh],
                  [pl.BlockSpec((tm, IN_SHARD), lambda i, j, k: (i, k)),
                   pl.BlockSpec((None, tn, IN_SHARD), lambda i, j, k: (k, j, 0))],
                  [SDS((S, D), F32)], [pl.BlockSpec((tm, tn), lambda i, j, k: (i, j))],
                  tb=True, k_axis=2, acc_shape=(tm, tn))[0]

    dx_prev, dshift, dscale, dng = _norm_mod_bwd(f"norm_mod_bwd_l{l}", x, dh1, dh2, dxo, r["g_row"], r["scale"])
    d_ada = jnp.concatenate([dshift, dscale, dgate], axis=1)[0]
    big = (g_in, g_gm, g_wb, g_out)
    small = dict(norm_g=dng[0], b_ada=d_ada, a_sink=dsink, c_q_norm=dcq[0], c_k_norm=dck[0], d_rel_bias=d_rel)
    return dx_prev, big, small, arrived


def _place():
    return lax.axis_index("x"), lax.axis_index("y"), lax.axis_index("c")


class _Carry:
    def __init__(self, kind, arrays):
        self.kind, self.n, self.ins = kind, len(arrays), list(arrays)
        any_spec = pl.BlockSpec(memory_space=pl.ANY)
        self.in_specs = [any_spec] * self.n
        self.out_specs = [any_spec] * self.n
        if kind == "gather":
            self.out_shape = [SDS((N_SHARD, *a.shape), a.dtype) for a in arrays]
        else:
            self.out_shape = [SDS((3, *a.shape[1:]), a.dtype) for a in arrays]
        self.scratch = [pltpu.SemaphoreType.DMA((self.n, 3)), pltpu.SemaphoreType.DMA((self.n, 3))]

    def _copies(self, ins, outs, send_sems, recv_sems, arriving):
        x, y, c = _place()
        cps = []
        for a in range(self.n):
            for k, (px, py) in enumerate([(1 - x, y), (x, 1 - y), (1 - x, 1 - y)]):
                if self.kind == "gather":
                    src, dst = ins[a], outs[a].at[2 * px + py if arriving else 2 * x + y]
                else:
                    src, dst = ins[a].at[2 * px + py], outs[a].at[k]
                cps.append(pltpu.make_async_remote_copy(src, dst, send_sems.at[a, k], recv_sems.at[a, k],
                                                        device_id=(px, py, c), device_id_type=MESH))
        return cps

    def start(self, ins, outs, send_sems, recv_sems):
        for cp in self._copies(ins, outs, send_sems, recv_sems, False):
            cp.start()

    def finish(self, ins, outs, send_sems, recv_sems):
        for cp in self._copies(ins, outs, send_sems, recv_sems, True):
            cp.wait_recv()
        for cp in self._copies(ins, outs, send_sems, recv_sems, False):
            cp.wait_send()


def _run_carry(name, carry):
    n = carry.n

    def body(*refs):
        args = (refs[:n], refs[n:2 * n], refs[2 * n], refs[2 * n + 1])
        carry.start(*args)
        carry.finish(*args)

    return pl.pallas_call(body, name=name, in_specs=carry.in_specs, out_specs=carry.out_specs,
                          out_shape=carry.out_shape, scratch_shapes=carry.scratch)(*carry.ins)


def _own_slot(gathered, shards, me):
    return [lax.dynamic_update_index_in_dim(g, s, me, 0) for g, s in zip(gathered, shards)]


def _gather_small(name, v):
    m_per, n = v.shape

    def body(x_ref, out_ref, send_sems, recv_sems, local_sem):
        x, y, c = _place()
        me, sibling = (x, y, c), (x, y, 1 - c)
        chips = [(1 - x, y), (x, 1 - y), (1 - x, 1 - y)]

        def rows(px, py, pc):
            return out_ref.at[pl.ds((4 * px + 2 * py + pc) * m_per, m_per), :]

        def copy(k, block, to, src=None):
            return pltpu.make_async_remote_copy(
                src_ref=rows(*block) if src is None else src, dst_ref=rows(*block),
                send_sem=send_sems.at[k], recv_sem=recv_sems.at[k], device_id=to, device_id_type=MESH)

        mine = pltpu.make_async_copy(x_ref, rows(*me), local_sem)
        mine.start()
        first = [copy(0, me, sibling, src=x_ref)]
        first += [copy(1 + j, me, (*chip, c), src=x_ref) for j, chip in enumerate(chips)]
        for cp in first:
            cp.start()
        passed = [copy(4 + j, (*chip, c), sibling) for j, chip in enumerate(chips)]
        for j, chip in enumerate(chips):
            copy(1 + j, (*chip, c), me).wait_recv()
            passed[j].start()
        copy(0, sibling, me).wait_recv()
        for j, chip in enumerate(chips):
            copy(4 + j, (*chip, 1 - c), me).wait_recv()
        for cp in first + passed:
            cp.wait_send()
        mine.wait()

    return pl.pallas_call(
        body, name=name, out_shape=SDS((8 * m_per, n), v.dtype),
        in_specs=[pl.BlockSpec(memory_space=pltpu.VMEM)], out_specs=pl.BlockSpec(memory_space=pltpu.VMEM),
        scratch_shapes=[pltpu.SemaphoreType.DMA((7,)), pltpu.SemaphoreType.DMA((7,)), pltpu.SemaphoreType.DMA])(v)


def _pair_send_half(name, grads):
    n = len(grads)

    def body(*refs):
        ins, outs = refs[:n], refs[n:2 * n]
        send_sems, recv_sems = refs[2 * n:]
        x, y, c = _place()
        cps = []
        for a in range(n):
            cp = pltpu.make_async_remote_copy(ins[a].at[:, 1 - c], outs[a], send_sems.at[a], recv_sems.at[a],
                                              device_id=(x, y, 1 - c), device_id_type=MESH)
            cp.start()
            cps.append(cp)
        for cp in cps:
            cp.wait_recv()
        for cp in cps:
            cp.wait_send()

    any_spec = pl.BlockSpec(memory_space=pl.ANY)
    return pl.pallas_call(
        body, name=name, in_specs=[any_spec] * n, out_specs=[any_spec] * n,
        out_shape=[SDS((g.shape[0], *g.shape[2:]), g.dtype) for g in grads],
        scratch_shapes=[pltpu.SemaphoreType.DMA((n,)), pltpu.SemaphoreType.DMA((n,))])(*grads)


def _pair_gather(name, halves):
    n = len(halves)

    def body(*refs):
        outs = refs[n:2 * n]
        send_sems, recv_sems = refs[2 * n:]
        x, y, c = _place()
        cps = [pltpu.make_async_remote_copy(outs[a].at[c], outs[a].at[c], send_sems.at[a], recv_sems.at[a],
                                            device_id=(x, y, 1 - c), device_id_type=MESH) for a in range(n)]
        for cp in cps:
            cp.start()
        for a in range(n):
            pltpu.make_async_remote_copy(outs[a].at[c], outs[a].at[1 - c], send_sems.at[a], recv_sems.at[a],
                                         device_id=(x, y, 1 - c), device_id_type=MESH).wait_recv()
        for cp in cps:
            cp.wait_send()

    any_spec = pl.BlockSpec(memory_space=pl.ANY)
    return pl.pallas_call(
        body, name=name, in_specs=[any_spec] * n, out_specs=[any_spec] * n,
        out_shape=[SDS(g.shape, g.dtype) for g in halves], input_output_aliases={a: a for a in range(n)},
        scratch_shapes=[pltpu.SemaphoreType.DMA((n,)), pltpu.SemaphoreType.DMA((n,))])(*halves)


def _add_half(name, g, recv, c_idx):
    _, _, R, C = g.shape
    tr = min(R, 256)

    def body(c_ref, g_ref, r_ref, o_ref):
        o_ref[...] = g_ref[...] + r_ref[...]

    return _call(body, name=name, grid=(4, R // tr), nsp=1,
                 in_specs=[pl.BlockSpec((None, None, tr, C), lambda j, r, c_ref: (j, c_ref[0], r, 0)),
                           pl.BlockSpec((None, tr, C), lambda j, r, c_ref: (j, r, 0))],
                 out_specs=pl.BlockSpec((None, tr, C), lambda j, r, c_ref: (j, r, 0)),
                 out_shape=SDS((4, R, C), F32), sem=("parallel", "parallel"))(c_idx, g, recv)


def _add_shards(name, part, recv, idx):
    _, R, C = part.shape
    tr = min(R, 256)

    def body(idx_ref, p_ref, r_ref, o_ref):
        o_ref[...] = ((p_ref[...] + r_ref[0]) + r_ref[1]) + r_ref[2]

    return _call(body, name=name, grid=(R // tr,), nsp=1,
                 in_specs=[pl.BlockSpec((None, tr, C), lambda r, idx_ref: (idx_ref[0], r, 0)),
                           pl.BlockSpec((3, tr, C), lambda r, idx_ref: (0, r, 0))],
                 out_specs=pl.BlockSpec((None, tr, C), lambda r, idx_ref: (idx_ref[1], r, 0)),
                 out_shape=SDS((2, R, C), F32), sem=("parallel",))(idx, part, recv)


def _pair_sum_layer(l, big, c_idx):
    views = []
    for g in big:
        rows = g.shape[-2] if g.ndim == 3 else g.shape[1] * g.shape[2]
        views.append(g.reshape(N_SHARD, 2, rows // 2, g.shape[-1]))
    recv1 = _pair_send_half(f"rs_pair_send_l{l}", views)
    return [_add_half(f"rs_add_half{a}_l{l}", v, r1, c_idx) for a, (v, r1) in enumerate(zip(views, recv1))]


def _finish_reduce_layer(l, big, parts, recv2, idx):
    halves = [_add_shards(f"rs_add_shards{a}_l{l}", pt, r2, idx) for a, (pt, r2) in enumerate(zip(parts, recv2))]
    full = _pair_gather(f"rs_pair_gather_l{l}", halves)
    return [f.reshape(g.shape[1:]) for f, g in zip(full, big)]


def _adamw_math(w, g, m, v):
    m = ADAM_B1 * m + (1.0 - ADAM_B1) * g
    v = ADAM_B2 * v + (1.0 - ADAM_B2) * (g * g)
    m_hat = m / (1.0 - ADAM_B1 ** ADAM_STEP)
    v_hat = v / (1.0 - ADAM_B2 ** ADAM_STEP)
    delta = -ADAM_LR * (m_hat / (jnp.sqrt(v_hat) + ADAM_EPS) + ADAM_WD * w)
    return delta, m, v


def _adamw(name, w, g, m, v):
    shape = w.shape
    C = shape[-1]
    R = int(np.prod(shape[:-1]))
    tr = min(R, 256)

    def body(w_ref, g_ref, m_ref, v_ref, d_ref, nm_ref, nv_ref):
        d, nm, nv = _adamw_math(w_ref[...], g_ref[...], m_ref[...], v_ref[...])
        d_ref[...] = d
        nm_ref[...] = nm
        nv_ref[...] = nv

    blk = pl.BlockSpec((tr, C), lambda i: (i, 0))
    outs = _call(body, name=name, grid=(R // tr,), in_specs=[blk] * 4, out_specs=[blk] * 3,
                 out_shape=[SDS((R, C), F32)] * 3, sem=("parallel",))(*(a.reshape(R, C) for a in (w, g, m, v)))
    return [o.reshape(shape) for o in outs]


def _adamw_small(name, w, g8, m, v):
    R = w.shape[0]

    def body(w_ref, g_ref, m_ref, v_ref, go_ref, d_ref, nm_ref, nv_ref):
        g = g_ref[0]
        for b in range(1, 8):
            g = g + g_ref[b]
        d, nm, nv = _adamw_math(w_ref[...], g, m_ref[...], v_ref[...])
        go_ref[...] = g
        d_ref[...] = d
        nm_ref[...] = nm
        nv_ref[...] = nv

    blk = pl.BlockSpec((R, 128), lambda i: (0, 0))
    return _call(body, name=name, grid=(1,), in_specs=[blk, pl.BlockSpec((8, R, 128), lambda i: (0, 0, 0)), blk, blk],
                 out_specs=[blk] * 4, out_shape=[SDS((R, 128), F32)] * 4, sem=("arbitrary",))(w, g8, m, v)


SMALL_NAMES = ("norm_g", "b_ada", "a_sink", "c_q_norm", "c_k_norm", "d_rel_bias", "final_g")


def _pack(parts, extra_rows=0):
    flat = jnp.concatenate([a.reshape(-1) for a in parts])
    rows = -(-flat.shape[0] // 128)
    rows = -(-rows // 8) * 8 + extra_rows
    return jnp.pad(flat, (0, rows * 128 - flat.shape[0])).reshape(rows, 128)


def _unpack(packed, like):
    flat = packed.reshape(-1)
    out, off = [], 0
    for a in like:
        out.append(flat[off:off + a.size].reshape(a.shape))
        off += a.size
    return out


def _device_step(x, c8, tgt, shards, p, me, c_idx):
    S = x.shape[0]
    L = len(shards)
    tabs = _rope_tables(S)
    first = _Carry("gather", shards[0])
    lw = [_own_slot(_run_carry("gather_w_l0", first), shards[0], me)]
    res = []
    for l in range(L):
        x, r, nxt = _layer_fwd(l, x, c8, lw[l], p, tabs, shards[l + 1] if l + 1 < L else None, me)
        res.append(r)
        lw.append(nxt)
    dx, dfg, loss = _final_loss("final_loss", x, tgt, p["final_g"][None, :])
    bigs, smalls, parts, arrived = [None] * L, [None] * L, [None] * L, [None] * L
    for l in reversed(range(L)):
        pending = parts[l + 1] if l + 1 < L else None
        dx, bigs[l], smalls[l], arr = _layer_bwd(l, dx, res[l], lw[l], p, tabs, pending)
        if pending is not None:
            arrived[l + 1] = arr
        parts[l] = _pair_sum_layer(l, bigs[l], c_idx)
    arrived[0] = _run_carry("rs_shard_exchange_l0", _Carry("exchange", parts[0]))
    idx = jnp.concatenate([jnp.reshape(me, (1,)).astype(I32), c_idx])
    reduced = [_finish_reduce_layer(l, bigs[l], parts[l], arrived[l], idx) for l in range(L)]
    return loss, dx, reduced, smalls, dfg[0]


def kernel(x, c, norm_g, w_ada, b_ada, w_in, a_sink, c_q_norm, c_k_norm, d_rel_bias, w_gate_merge, w_branch, w_out, final_g, loss_target, m_norm_g, m_w_ada, m_b_ada, m_w_in, m_a_sink, m_c_q_norm, m_c_k_norm, m_d_rel_bias, m_w_gate_merge, m_w_branch, m_w_out, m_final_g, v_norm_g, v_w_ada, v_b_ada, v_w_in, v_a_sink, v_c_q_norm, v_c_k_norm, v_d_rel_bias, v_w_gate_merge, v_w_branch, v_w_out, v_final_g):
    L, D = norm_g.shape
    dq = D // N_SHARD
    p = dict(norm_g=norm_g, b_ada=b_ada, a_sink=a_sink, c_q_norm=c_q_norm, c_k_norm=c_k_norm,
             d_rel_bias=d_rel_bias, final_g=final_g)
    xi, yi, ci = _place()
    c_idx = jnp.reshape(ci, (1,)).astype(I32)
    me = 2 * xi + yi

    shards = [[w_ada[l].astype(BF16), w_in[l].astype(BF16), w_gate_merge[l].astype(BF16),
               w_branch[l].astype(BF16), w_out[l].astype(BF16)] for l in range(L)]
    c8 = jnp.broadcast_to(c, (8, D))
    loss, grad_x, reduced, smalls, dfg = _device_step(x[0], c8, loss_target[0], shards, p, me, c_idx)
    loss = lax.psum(loss[0, 0], ("x", "y", "c"))

    small_parts = [jnp.stack([s[n] for s in smalls]) for n in SMALL_NAMES[:-1]] + [dfg]
    packed = _pack(small_parts + [c[0]])
    rows = packed.shape[0]
    g8 = _gather_small("gather_small", packed).reshape(8, rows, 128)
    small_w = [p[n] for n in SMALL_NAMES]
    small_m = [m_norm_g, m_b_ada, m_a_sink, m_c_q_norm, m_c_k_norm, m_d_rel_bias, m_final_g]
    small_v = [v_norm_g, v_b_ada, v_a_sink, v_c_q_norm, v_c_k_norm, v_d_rel_bias, v_final_g]
    pad_c = [jnp.zeros((D,), F32)]
    sg, sd, sm, sv = _adamw_small("adamw_small", _pack(small_w + pad_c), g8, _pack(small_m + pad_c),
                                  _pack(small_v + pad_c))
    sg, sd, sm, sv = (_unpack(a, small_w) for a in (sg, sd, sm, sv))

    n_small = sum(a.size for a in small_parts)
    flat8 = g8.reshape(8, rows * 128)
    c_all = flat8[:, n_small:n_small + D]
    dada_all = flat8[:, L * D:L * D + L * 3 * D].reshape(8, L, 3 * D)
    dada_mine = lax.dynamic_slice_in_dim(dada_all, (2 * xi + yi) * (3 * dq), 3 * dq, axis=2)
    tma = min(D, 1024)
    g_ada = jnp.stack([
        _matmul(f"mm_gwada_l{l}", (D // tma,), [c_all, dada_mine[:, l]],
                [pl.BlockSpec((8, tma), lambda i: (0, i)), pl.BlockSpec((8, 3 * dq), lambda i: (0, 0))],
                [SDS((D, 3 * dq), F32)], [pl.BlockSpec((tma, 3 * dq), lambda i: (i, 0))],
                ta=True, a_fn=lambda a: a * _sigmoid(a))[0] for l in range(L)])

    g_in, g_gm, g_wb, g_out = (jnp.stack([reduced[l][a] for l in range(L)]) for a in range(4))

    big = {}
    for nm, w, g, m, v in (("w_ada", w_ada, g_ada, m_w_ada, v_w_ada), ("w_in", w_in, g_in, m_w_in, v_w_in),
                           ("w_gate_merge", w_gate_merge, g_gm, m_w_gate_merge, v_w_gate_merge),
                           ("w_branch", w_branch, g_wb, m_w_branch, v_w_branch),
                           ("w_out", w_out, g_out, m_w_out, v_w_out)):
        big[nm] = (g, *_adamw(f"adamw_{nm}", w, g, m, v))

    order = ("norm_g", "w_ada", "b_ada", "w_in", "a_sink", "c_q_norm", "c_k_norm", "d_rel_bias",
             "w_gate_merge", "w_branch", "w_out", "final_g")
    cols = [[], [], [], []]
    for nm in order:
        if nm in big:
            vals = big[nm]
        else:
            k = SMALL_NAMES.index(nm)
            vals = (sg[k], sd[k], sm[k], sv[k])
        for col, val in zip(cols, vals):
            col.append(val)
    return (loss, grad_x[None], *cols[0], *cols[1], *cols[2], *cols[3])
```

```python
import numpy as np
import jax
import jax.numpy as jnp
from jax import lax
from jax.experimental import pallas as pl
from jax.experimental.pallas import tpu as pltpu

F32 = jnp.float32
BF16 = jnp.bfloat16
I32 = jnp.int32
SDS = jax.ShapeDtypeStruct
MESH = pl.DeviceIdType.MESH

HEAD_DIM = 128
GRID_W = 64
EPS = 1e-6
NEG_INF = -1e30
ROPE_THETA = 10000.0
SCALE = HEAD_DIM ** -0.5
LOG2E = 1.4426950408889634
SCALE_LOG2E = SCALE * LOG2E
N_SHARD = 4
BRANCH_W = 512
IN_COLS = 7168
IN_SHARD = IN_COLS // N_SHARD
QCOL = (0, 12, 28, 40)
KCOL = (4, 16, 32, 44)
VCOL = (6, 20, 34, 48)
GCOL = (8, 24, 36, 52)
KV_HEADS = (2, 4, 2, 4)

ADAM_LR = 0.001
ADAM_B1 = 0.9
ADAM_B2 = 0.999
ADAM_EPS = 1e-08
ADAM_WD = 0.01
ADAM_STEP = 10

V7X_VMEM_BYTES = 64 * 1024 * 1024
VMEM_LIMIT = V7X_VMEM_BYTES * 7 // 8

ATT_TILE = {"a": 256, "b": 512, "c": (512, 1024), "c_bwd": 512, "d": 256}
M_INIT = -1e20
ROW_TILE = 512
EW_ROWS = 256


def _band(reach, tile):
    return -(-reach // tile)


def _call(body, *, name, grid, in_specs, out_specs, out_shape, scratch=(), sem=None, nsp=0):
    params = pltpu.CompilerParams(dimension_semantics=sem, vmem_limit_bytes=VMEM_LIMIT)
    if nsp:
        gs = pltpu.PrefetchScalarGridSpec(num_scalar_prefetch=nsp, grid=grid, in_specs=in_specs,
                                          out_specs=out_specs, scratch_shapes=list(scratch))
        return pl.pallas_call(body, grid_spec=gs, out_shape=out_shape, name=name, compiler_params=params)
    return pl.pallas_call(body, grid=grid, in_specs=in_specs, out_specs=out_specs, out_shape=out_shape,
                          scratch_shapes=list(scratch), name=name, compiler_params=params)


def _sigmoid(x):
    return 1.0 / (1.0 + jnp.exp(-x))


def _matmul(name, grid, ins, in_specs, out_shape, out_specs, *, ta=False, tb=False, k_axis=None,
            acc_shape=None, epilogue=None, a_fn=None, k_inner=None):
    n_in = len(ins)
    n_out = len(out_shape)
    nk = grid[k_axis] if k_axis is not None else 1
    dn = (((0 if ta else 1,), (1 if tb else 0,)), ((), ()))

    def body(*refs):
        a = refs[0][...]
        if a_fn is not None:
            a = a_fn(a)
        a = a.astype(BF16)
        if k_inner is None:
            p = lax.dot_general(a, refs[1][...].astype(BF16), dn, preferred_element_type=F32)
        else:
            ck = a.shape[1] // k_inner
            p = None
            for kk in range(k_inner):
                t = lax.dot_general(a[:, kk * ck:(kk + 1) * ck], refs[1][kk].astype(BF16), dn,
                                    preferred_element_type=F32)
                p = t if p is None else p + t
        extra = refs[2:n_in]
        outs = refs[n_in:n_in + n_out]

        def fin(acc):
            vals = epilogue(acc, *extra) if epilogue is not None else (acc,)
            for o_ref, v in zip(outs, vals):
                o_ref[...] = v.astype(o_ref.dtype)

        if k_axis is None:
            fin(p)
        else:
            acc_ref = refs[-1]
            k = pl.program_id(k_axis)

            @pl.when(k == 0)
            def _():
                acc_ref[...] = p

            @pl.when(k > 0)
            def _():
                acc_ref[...] += p

            @pl.when(k == nk - 1)
            def _():
                fin(acc_ref[...])

    sem = tuple("arbitrary" if ax == k_axis else "parallel" for ax in range(len(grid)))
    scratch = [pltpu.VMEM(acc_shape, F32)] if k_axis is not None else []
    return _call(body, name=name, grid=grid, in_specs=in_specs, out_specs=out_specs, out_shape=out_shape,
                 scratch=scratch, sem=sem)(*ins)


def _norm_mod(name, x, g, scale, shift):
    S, D = x.shape
    ts = min(S, EW_ROWS)

    def body(x_ref, g_ref, sc_ref, sh_ref, h_ref):
        xv = x_ref[...]
        r = lax.rsqrt(jnp.mean(xv * xv, axis=-1, keepdims=True) + EPS)
        h_ref[...] = (((xv * r) * g_ref[...]) * (1.0 + sc_ref[...]) + sh_ref[...]).astype(BF16)

    row = pl.BlockSpec((1, D), lambda i: (0, 0))
    blk = pl.BlockSpec((ts, D), lambda i: (i, 0))
    return _call(body, name=name, grid=(S // ts,), in_specs=[blk, row, row, row], out_specs=blk,
                 out_shape=SDS((S, D), BF16), sem=("parallel",))(x, g, scale, shift)


def _norm_mod_bwd(name, x, dh1, dh2, dxo, g, scale):
    S, D = x.shape
    ts = min(S, EW_ROWS)

    def body(x_ref, a_ref, b_ref, dxo_ref, g_ref, sc_ref, dx_ref, dsh_ref, dsc_ref, dg_ref):
        @pl.when(pl.program_id(0) == 0)
        def _():
            dsh_ref[...] = jnp.zeros_like(dsh_ref)
            dsc_ref[...] = jnp.zeros_like(dsc_ref)
            dg_ref[...] = jnp.zeros_like(dg_ref)

        xv = x_ref[...]
        r = lax.rsqrt(jnp.mean(xv * xv, axis=-1, keepdims=True) + EPS)
        xh = xv * r
        dh = a_ref[...] + b_ref[...]
        gv = g_ref[...]
        one_sc = 1.0 + sc_ref[...]
        dsh_ref[...] += jnp.sum(dh, axis=0, keepdims=True)
        dsc_ref[...] += jnp.sum(dh * xh * gv, axis=0, keepdims=True)
        dg_ref[...] += jnp.sum(dh * xh * one_sc, axis=0, keepdims=True)
        dxh = dh * gv * one_sc
        dx = r * (dxh - xh * jnp.mean(dxh * xh, axis=-1, keepdims=True))
        dx_ref[...] = dxo_ref[...] + dx

    row = pl.BlockSpec((1, D), lambda i: (0, 0))
    blk = pl.BlockSpec((ts, D), lambda i: (i, 0))
    return _call(body, name=name, grid=(S // ts,), in_specs=[blk, blk, blk, blk, row, row],
                 out_specs=[blk, row, row, row],
                 out_shape=[SDS((S, D), F32), SDS((1, D), F32), SDS((1, D), F32), SDS((1, D), F32)],
                 sem=("arbitrary",))(x, dh1, dh2, dxo, g, scale)


def _out_bwd_ew(name, dxo, o2, gate):
    S, D = dxo.shape
    ts = min(S, EW_ROWS)

    def body(dxo_ref, o2_ref, gt_ref, do2_ref, dgt_ref):
        @pl.when(pl.program_id(0) == 0)
        def _():
            dgt_ref[...] = jnp.zeros_like(dgt_ref)

        d = dxo_ref[...]
        do2_ref[...] = (d * gt_ref[...]).astype(BF16)
        dgt_ref[...] += jnp.sum(d * o2_ref[...].astype(F32), axis=0, keepdims=True)

    row = pl.BlockSpec((1, D), lambda i: (0, 0))
    blk = pl.BlockSpec((ts, D), lambda i: (i, 0))
    return _call(body, name=name, grid=(S // ts,), in_specs=[blk, blk, row], out_specs=[blk, row],
                 out_shape=[SDS((S, D), BF16), SDS((1, D), F32)], sem=("arbitrary",))(dxo, o2, gate)


def _merge_bwd_ew(name, dz, mg, proj):
    S, D = dz.shape
    ts = min(S, ROW_TILE)
    td = min(D, 512)
    nd = D // td

    def body(dz_ref, mg_ref, pj_ref, dmg_ref, dpj_ref):
        d = dz_ref[...].astype(F32)
        m = mg_ref[...].astype(F32)
        dmg_ref[...] = (d * pj_ref[...].astype(F32) * m * (1.0 - m)).astype(BF16)
        dpj_ref[...] = (d * m).astype(BF16)

    wide = pl.BlockSpec((ts, td), lambda i, j, n: (i, n * nd + j))
    return _call(body, name=name, grid=(S // ts, nd, 4),
                 in_specs=[pl.BlockSpec((ts, td), lambda i, j, n: (i, j)), wide, wide],
                 out_specs=[wide, wide], out_shape=[SDS((S, 4 * D), BF16), SDS((S, 4 * D), BF16)],
                 sem=("parallel", "parallel", "arbitrary"))(dz, mg, proj)


def _final_loss(name, x, tgt, g):
    S, D = x.shape
    ts = min(S, EW_ROWS)

    def body(x_ref, t_ref, g_ref, dx_ref, dg_ref, loss_ref):
        @pl.when(pl.program_id(0) == 0)
        def _():
            dg_ref[...] = jnp.zeros_like(dg_ref)
            loss_ref[...] = jnp.zeros_like(loss_ref)

        xv = x_ref[...]
        r = lax.rsqrt(jnp.mean(xv * xv, axis=-1, keepdims=True) + EPS)
        xh = xv * r
        gv = g_ref[...]
        err = xh * gv - t_ref[...]
        row_loss = jnp.mean(err * err, axis=-1, keepdims=True)
        loss_ref[...] += 0.5 * jnp.sum(row_loss, axis=0, keepdims=True)
        dy = err * (1.0 / D)
        dg_ref[...] += jnp.sum(dy * xh, axis=0, keepdims=True)
        dxh = dy * gv
        dx_ref[...] = r * (dxh - xh * jnp.mean(dxh * xh, axis=-1, keepdims=True))

    row = pl.BlockSpec((1, D), lambda i: (0, 0))
    blk = pl.BlockSpec((ts, D), lambda i: (i, 0))
    return _call(body, name=name, grid=(S // ts,), in_specs=[blk, blk, row],
                 out_specs=[blk, row, pl.BlockSpec((1, 128), lambda i: (0, 0))],
                 out_shape=[SDS((S, D), F32), SDS((1, D), F32), SDS((1, 128), F32)],
                 sem=("arbitrary",))(x, tgt, g)


def _rope_tables(S):
    def tables(pos, dim):
        inv = ROPE_THETA ** (-jnp.arange(0, dim, 2, dtype=F32) / dim)
        ang = pos.astype(F32)[:, None] * inv[None, :]
        ang = jnp.concatenate([ang, ang], axis=-1)
        return jnp.cos(ang), jnp.sin(ang)

    pos = jnp.arange(S, dtype=I32)
    lane = np.arange(HEAD_DIM)
    cos1, sin1 = tables(pos, HEAD_DIM)
    up1 = jnp.asarray((lane >= 64).astype(np.float32))[None, :]
    one_d = (cos1, sin1 * up1, -sin1 * (1.0 - up1))
    cr, sr = tables(pos // GRID_W, HEAD_DIM // 2)
    cc, sc = tables(pos % GRID_W, HEAD_DIM // 2)
    cos2 = jnp.concatenate([cr, cc], axis=-1)
    sin2 = jnp.concatenate([sr, sc], axis=-1)
    up2 = jnp.asarray(((lane % 64) >= 32).astype(np.float32))[None, :]
    axial = (cos2, sin2 * up2, -sin2 * (1.0 - up2))
    return one_d, axial


def _rope_fwd(name, src, c0, nb, tabs, sh, gain=None):
    S = src.shape[0]
    ts = min(S, ROW_TILE)
    has_gain = gain is not None

    assert c0 % nb == 0
    hd = HEAD_DIM

    def body(*refs):
        x_ref, c_ref, sa_ref, sb_ref = refs[:4]
        o_ref = refs[-1]
        cv, sa, sb = c_ref[...], sa_ref[...], sb_ref[...]
        for hh in range(nb):
            lanes = slice(hh * hd, (hh + 1) * hd)
            xv = x_ref[:, lanes].astype(F32)
            if has_gain:
                r = lax.rsqrt(jnp.mean(xv * xv, axis=-1, keepdims=True) + EPS)
                xv = (xv * r) * refs[4][...]
            out = xv * cv + pltpu.roll(xv, sh, 1) * sa + pltpu.roll(xv, hd - sh, 1) * sb
            o_ref[:, lanes] = out.astype(BF16)

    tab = pl.BlockSpec((ts, hd), lambda i: (i, 0))
    in_specs = [pl.BlockSpec((ts, nb * hd), lambda i: (i, c0 // nb)), tab, tab, tab]
    ins = [src, *tabs]
    if has_gain:
        in_specs.append(pl.BlockSpec((1, hd), lambda i: (0, 0)))
        ins.append(gain)
    return _call(body, name=name, grid=(S // ts,), in_specs=in_specs,
                 out_specs=pl.BlockSpec((ts, nb * hd), lambda i: (i, 0)),
                 out_shape=SDS((S, nb * hd), BF16), sem=("parallel",))(*ins)


def _rope_bwd(name, dout, src, c0, nb, tabs, sh, gain=None):
    S = src.shape[0]
    ts = min(S, ROW_TILE)
    has_gain = gain is not None

    assert c0 % nb == 0
    hd = HEAD_DIM

    def body(*refs):
        d_ref, x_ref, c_ref, sa_ref, sb_ref = refs[:5]
        cv, sa, sb = c_ref[...], sa_ref[...], sb_ref[...]
        if has_gain:
            gn_ref, dx_ref, dgn_ref = refs[5:]

            @pl.when(pl.program_id(0) == 0)
            def _():
                dgn_ref[...] = jnp.zeros_like(dgn_ref)
        else:
            dx_ref = refs[5]
        for hh in range(nb):
            lanes = slice(hh * hd, (hh + 1) * hd)
            d = d_ref[:, lanes].astype(F32)
            dxn = d * cv + pltpu.roll(d * sa, hd - sh, 1) + pltpu.roll(d * sb, sh, 1)
            if has_gain:
                xv = x_ref[:, lanes].astype(F32)
                r = lax.rsqrt(jnp.mean(xv * xv, axis=-1, keepdims=True) + EPS)
                xh = xv * r
                dgn_ref[...] += jnp.sum(dxn * xh, axis=0, keepdims=True)
                dxh = dxn * gn_ref[...]
                dx_ref[:, lanes] = (r * (dxh - xh * jnp.mean(dxh * xh, axis=-1, keepdims=True))).astype(BF16)
            else:
                dx_ref[:, lanes] = dxn.astype(BF16)

    tab = pl.BlockSpec((ts, hd), lambda i: (i, 0))
    own = pl.BlockSpec((ts, nb * hd), lambda i: (i, 0))
    in_specs = [own, pl.BlockSpec((ts, nb * hd), lambda i: (i, c0 // nb)), tab, tab, tab]
    ins = [dout, src, *tabs]
    out_specs = [own]
    out_shape = [SDS((S, nb * hd), BF16)]
    if has_gain:
        row = pl.BlockSpec((1, hd), lambda i: (0, 0))
        in_specs.append(row)
        ins.append(gain)
        out_specs.append(row)
        out_shape.append(SDS((1, hd), F32))
    res = _call(body, name=name, grid=(S // ts,), in_specs=in_specs, out_specs=out_specs,
                out_shape=out_shape, sem=("arbitrary",))(*ins)
    return res if has_gain else (res[0], None)


def _offset_grid(T, W):
    d = (np.arange(2 * W + 1) - W)[:, None, None] * T
    return d + np.arange(T)[None, :, None] - np.arange(T)[None, None, :]


def _mask_tiles_a(T):
    dk = _offset_grid(T, _band(128, T))
    return np.where(np.abs(dk) <= 128, 0.0, NEG_INF).astype(np.float32)[None, None]


def _mask_tiles_b(T):
    dk = _offset_grid(T, _band(1024, T))
    ad = np.abs(dk)
    mult = ((ad <= 64).astype(np.float32) + ((ad <= 256) & (dk % 4 == 0)) + ((ad <= 1024) & (dk % 16 == 0)))
    return np.where(mult > 0, np.log(np.maximum(mult, 1.0)) / SCALE, NEG_INF).astype(np.float32)[None, None]


def _edge_blocks_d(T):
    return -(-4 // (T // GRID_W))


def _mask_tiles_d(S, T):
    rows, nq, rpb = S // GRID_W, S // T, T // GRID_W
    W, E = -(-7 // rpb), _edge_blocks_d(T)
    assert nq >= 2 * E + 1
    out = []
    for i in [*range(E), nq // 2, *range(nq - E, nq)]:
        kp = ((i + np.arange(2 * W + 1) - W) * T)[:, None, None] + np.arange(T)[None, :, None]
        qp = i * T + np.arange(T)[None, None, :]
        qr, qc, kr, kc = qp >> 6, qp & 63, kp >> 6, kp & 63
        rs = np.clip(qr - 4, 0, rows - 8)
        cs = np.clip(qc - 8, 0, GRID_W - 16)
        valid = (kr >= rs) & (kr < rs + 8) & (kc >= cs) & (kc < cs + 16)
        out.append(np.where(valid, 0.0, NEG_INF).astype(np.float32))
    return np.stack(out)[:, None]


def _variant(i, nq, E):
    if E == 0:
        return 0
    return jnp.where(i < E, i, jnp.where(i >= nq - E, i - (nq - 2 * E - 1), E))


_NT = (((1,), (1,)), ((), ()))
_TN = (((0,), (0,)), ((), ()))
_NN = (((1,), (0,)), ((), ()))


class _Mixer:
    def __init__(self, tq, tk, W, G, E=0, hp=1):
        self.tq, self.tk, self.W, self.G, self.E, self.hp = tq, tk, W, G, E, hp


def _attn_fwd(name, mx, q_arr, qc0, k_arr, kc0, v_arr, vc0, u, gc0, sink, bias=None, carry=None):
    S = q_arr.shape[0]
    tq, tk, W, G = mx.tq, mx.tk, mx.W, mx.G
    nq, nk = S // tq, S // tk
    nd = nk if W is None else 2 * W + 1
    has_bias = bias is not None
    nc = 0 if carry is None else carry.n
    hd = HEAD_DIM

    def jmap(i, d):
        return d if W is None else jnp.clip(i + d - W, 0, nk - 1)

    def body(*refs):
        sink_ref, q_ref, k_ref, v_ref, g_ref = refs[:5]
        bias_ref = refs[5] if has_bias else None
        n_in = 6 if has_bias else 5
        out0 = n_in + 2 * nc
        br_ref, o_ref, lse_ref, m_s, l_s, acc_s = refs[out0:out0 + 6]
        i, d = pl.program_id(0), pl.program_id(1)
        j = d if W is None else i + d - W
        if carry is not None:
            carry_refs = (refs[n_in:n_in + nc], refs[n_in + nc:out0], refs[-2], refs[-1])
            pl.when((i == 0) & (d == 0))(lambda: carry.start(*carry_refs))

        @pl.when(d == 0)
        def _():
            m_s[...] = jnp.full_like(m_s, M_INIT)
            l_s[...] = jnp.zeros_like(l_s)
            acc_s[...] = jnp.zeros_like(acc_s)

        def step():
            for h in range(4):
                kv = slice((h // G) * hd, (h // G + 1) * hd)
                s = lax.dot_general(k_ref[:, kv], q_ref[:, h * hd:(h + 1) * hd], _NT,
                                    preferred_element_type=F32)
                if has_bias:
                    s = s + bias_ref[_variant(i, nq, mx.E), h if per_head else 0, d]
                m_prev = m_s[h]
                m_new = jnp.maximum(m_prev, jnp.max(s, axis=0, keepdims=True))
                alpha = jnp.exp2((m_prev - m_new) * SCALE_LOG2E)
                p = jnp.exp2((s - m_new) * SCALE_LOG2E)
                l_s[h] = alpha * l_s[h] + jnp.sum(p, axis=0, keepdims=True)
                acc_s[h] = alpha * acc_s[h] + lax.dot_general(v_ref[:, kv], p.astype(BF16), _TN,
                                                               preferred_element_type=F32)
                m_s[h] = m_new

        if W is None:
            step()
        else:
            pl.when((j >= 0) & (j < nk))(step)

        @pl.when(d == nd - 1)
        def _():
            for h in range(4):
                lanes = slice(h * hd, (h + 1) * hd)
                sk = sink_ref[h]
                m = m_s[h] * SCALE
                mf = jnp.maximum(m, sk)
                a = jnp.exp(m - mf)
                lf = l_s[h] * a + jnp.exp(sk - mf)
                o = ((acc_s[h] * a) / lf).T
                gv = g_ref[:, lanes].astype(F32)
                o_ref[:, lanes] = o.astype(BF16)
                br_ref[:, lanes] = (o * (gv * _sigmoid(gv))).astype(BF16)
                lse_ref[h] = mf + jnp.log(lf)

        if carry is not None:
            pl.when((i == nq - 1) & (d == nd - 1))(lambda: carry.finish(*carry_refs))

    n_kv = 4 // G
    assert qc0 % 4 == 0 and gc0 % 4 == 0 and kc0 % n_kv == 0 and vc0 % n_kv == 0
    per_head = has_bias and bias.shape[1] == 4
    in_specs = [pl.BlockSpec(memory_space=pltpu.SMEM),
                pl.BlockSpec((tq, 4 * hd), lambda i, d: (i, qc0 // 4)),
                pl.BlockSpec((tk, n_kv * hd), lambda i, d: (jmap(i, d), kc0 // n_kv)),
                pl.BlockSpec((tk, n_kv * hd), lambda i, d: (jmap(i, d), vc0 // n_kv)),
                pl.BlockSpec((tq, 4 * hd), lambda i, d: (i, gc0 // 4))]
    ins = [sink, q_arr, k_arr, v_arr, u]
    if has_bias:
        in_specs.append(pl.BlockSpec(bias.shape, lambda i, d: (0, 0, 0, 0, 0), pipeline_mode=pl.Buffered(1)))
        ins.append(bias)
    own = pl.BlockSpec((tq, 4 * hd), lambda i, d: (i, 0))
    out_specs = [own, own, pl.BlockSpec((4, 1, tq), lambda i, d: (0, 0, i))]
    out_shape = [SDS((S, 4 * hd), BF16), SDS((S, 4 * hd), BF16), SDS((4, 1, S), F32)]
    scratch = [pltpu.VMEM((4, 1, tq), F32), pltpu.VMEM((4, 1, tq), F32), pltpu.VMEM((4, hd, tq), F32)]
    sem = ("parallel", "arbitrary")
    if carry is not None:
        ins += carry.ins
        in_specs += carry.in_specs
        out_specs = carry.out_specs + out_specs
        out_shape = carry.out_shape + out_shape
        scratch += carry.scratch
        sem = ("arbitrary",) * 2
    res = _call(body, name=name, grid=(nq, nd), in_specs=in_specs, out_specs=out_specs, out_shape=out_shape,
                scratch=scratch, sem=sem)(*ins)
    return (*res[nc:], list(res[:nc]))


def _attn_bwd(name, mx, q_arr, qc0, k_arr, kc0, v_arr, vc0, do_all, hb0, lse, delta, bias=None, want_dbias=False,
              carry=None):
    S = q_arr.shape[0]
    tq, tk, W, G = mx.tq, mx.tk, mx.W, mx.G
    nq, nk = S // tq, S // tk
    nd = nq if W is None else 2 * W + 1
    n_kv = 4 // G
    hd = HEAD_DIM
    has_bias = bias is not None
    nc = 0 if carry is None else carry.n
    assert qc0 % G == 0 and hb0 % G == 0 and (not want_dbias or (has_bias and G == 1)) and (W is None or tq == tk)

    def imap(j, d):
        return d if W is None else jnp.clip(j + d - W, 0, nq - 1)

    def body(*refs):
        q_ref, k_ref, v_ref, do_ref, lse_ref, dl_ref = refs[:6]
        bias_ref = refs[6] if has_bias else None
        n_in = 7 if has_bias else 6
        out0 = n_in + 2 * nc
        dq_ref, dk_ref, dv_ref = refs[out0:out0 + 3]
        n_o = 4 if want_dbias else 3
        db_ref = refs[out0 + 3] if want_dbias else None
        dk_s, dv_s = refs[out0 + n_o:out0 + n_o + 2]
        kv, j, d = pl.program_id(0), pl.program_id(1), pl.program_id(2)
        i = d if W is None else j + d - W
        if carry is not None:
            carry_refs = (refs[n_in:n_in + nc], refs[n_in + nc:out0], refs[-2], refs[-1])
            pl.when((kv == 0) & (j == 0) & (d == 0))(lambda: carry.start(*carry_refs))

        @pl.when((j == 0) & (d == 0))
        def _():
            dq_ref[...] = jnp.zeros_like(dq_ref)
            if want_dbias:
                db_ref[...] = jnp.zeros_like(db_ref)

        @pl.when(d == 0)
        def _():
            dk_s[...] = jnp.zeros_like(dk_s)
            dv_s[...] = jnp.zeros_like(dv_s)

        def step():
            row0 = pl.multiple_of(i * tq, tq)
            for kh in range(hp):
                k = k_ref[:, kh * hd:(kh + 1) * hd]
                v = v_ref[:, kh * hd:(kh + 1) * hd]
                for g in range(G):
                    hh = kh * G + g
                    lanes = slice(hh * hd, (hh + 1) * hd)
                    q = q_ref[:, lanes]
                    do = do_ref[:, lanes]
                    s = lax.dot_general(k, q, _NT, preferred_element_type=F32)
                    if has_bias:
                        s = s + bias_ref[_variant(i, nq, mx.E), kh if per_head else 0, 2 * W - d]
                    p = jnp.exp2(s * SCALE_LOG2E - lse_ref[hh] * LOG2E)
                    dv_s[kh] += lax.dot_general(p.astype(BF16), do, _NN, preferred_element_type=F32)
                    dp = lax.dot_general(v, do, _NT, preferred_element_type=F32)
                    ds = p * (dp - dl_ref[hh])
                    if want_dbias:
                        db_ref[kh, 2 * W - d] += ds
                    dsb = ds.astype(BF16)
                    dk_s[kh] += lax.dot_general(dsb, q, _NN, preferred_element_type=F32)
                    dq_ref[pl.ds(row0, tq), lanes] += lax.dot_general(dsb, k, _TN,
                                                                      preferred_element_type=F32) * SCALE

        if W is None:
            step()
        else:
            pl.when((i >= 0) & (i < nq))(step)

        @pl.when(d == nd - 1)
        def _():
            for kh in range(hp):
                dk_ref[:, kh * hd:(kh + 1) * hd] = dk_s[kh] * SCALE
                dv_ref[:, kh * hd:(kh + 1) * hd] = dv_s[kh]

        if carry is not None:
            pl.when((kv == n_kv // hp - 1) & (j == nk - 1) & (d == nd - 1))(lambda: carry.finish(*carry_refs))

    hp = mx.hp
    hq = hp * G
    assert qc0 % hq == 0 and hb0 % hq == 0 and kc0 % hp == 0 and vc0 % hp == 0 and n_kv % hp == 0
    per_head = has_bias and bias.shape[1] == 4
    in_specs = [pl.BlockSpec((tq, hq * hd), lambda kv, j, d: (imap(j, d), qc0 // hq + kv)),
                pl.BlockSpec((tk, hp * hd), lambda kv, j, d: (j, kc0 // hp + kv)),
                pl.BlockSpec((tk, hp * hd), lambda kv, j, d: (j, vc0 // hp + kv)),
                pl.BlockSpec((tq, hq * hd), lambda kv, j, d: (imap(j, d), hb0 // hq + kv)),
                pl.BlockSpec((hq, 1, tq), lambda kv, j, d: (kv, 0, imap(j, d))),
                pl.BlockSpec((hq, 1, tq), lambda kv, j, d: (hb0 // hq + kv, 0, imap(j, d)))]
    ins = [q_arr, k_arr, v_arr, do_all, lse, delta]
    if has_bias:
        in_specs.append(pl.BlockSpec((bias.shape[0], hp if per_head else 1, nd, tk, tq),
                                     lambda kv, j, d: (0, kv if per_head else 0, 0, 0, 0)))
        ins.append(bias)
    kv_blk = pl.BlockSpec((tk, hp * hd), lambda kv, j, d: (j, kv))
    out_specs = [pl.BlockSpec((S, hq * hd), lambda kv, j, d: (0, kv)), kv_blk, kv_blk]
    out_shape = [SDS((S, 4 * hd), F32), SDS((S, n_kv * hd), F32), SDS((S, n_kv * hd), F32)]
    if want_dbias:
        out_specs.append(pl.BlockSpec((hp, nd, tk, tq), lambda kv, j, d: (kv, 0, 0, 0)))
        out_shape.append(SDS((4, nd, tk, tq), F32))
    scratch = [pltpu.VMEM((hp, tk, hd), F32), pltpu.VMEM((hp, tk, hd), F32)]
    sem = ("parallel", "arbitrary", "arbitrary")
    if carry is not None:
        ins += carry.ins
        in_specs += carry.in_specs
        out_specs = carry.out_specs + out_specs
        out_shape = carry.out_shape + out_shape
        scratch += carry.scratch
        sem = ("arbitrary",) * 3
    res = _call(body, name=name, grid=(n_kv // hp, nk, nd), in_specs=in_specs, out_specs=out_specs,
                out_shape=out_shape, scratch=scratch, sem=sem)(*ins)
    main = res[nc:]
    return (*main[:3], main[3] if want_dbias else None, list(res[:nc]))


def _attn_bwd_pre(name, dbr, o_all, u):
    S = dbr.shape[0]
    ts = min(S, ROW_TILE)
    hd = HEAD_DIM

    def gcol(hb):
        n = hb // 4
        return GCOL[0] + n * 16 - jnp.where(n >= 2, 4, 0) + hb % 4

    def body(dbr_ref, o_ref, g_ref, do_ref, dg_ref, dl_ref):
        db = dbr_ref[...].astype(F32)
        o = o_ref[...].astype(F32)
        gv = g_ref[...].astype(F32)
        sg = _sigmoid(gv)
        do = db * (gv * sg)
        do_ref[...] = do.astype(BF16)
        dg_ref[...] = (db * o * (sg * (1.0 + gv * (1.0 - sg)))).astype(BF16)
        dl_ref[0] = jnp.sum((do * o).T, axis=0, keepdims=True)

    own = pl.BlockSpec((ts, hd), lambda i, hb: (i, hb))
    return _call(body, name=name, grid=(S // ts, 16),
                 in_specs=[own, own, pl.BlockSpec((ts, hd), lambda i, hb: (i, gcol(hb)))],
                 out_specs=[own, own, pl.BlockSpec((1, 1, ts), lambda i, hb: (hb, 0, i))],
                 out_shape=[SDS((S, 16 * hd), BF16), SDS((S, 16 * hd), BF16), SDS((16, 1, S), F32)],
                 sem=("parallel", "parallel"))(dbr, o_all, u)


def _sink_grad(name, sink, lse, delta):
    S = lse.shape[2]
    ts = min(S, 2048)

    def body(sink_ref, lse_ref, dl_ref, out_ref):
        @pl.when(pl.program_id(1) == 0)
        def _():
            out_ref[...] = jnp.zeros_like(out_ref)

        sk = sink_ref[pl.program_id(0)]
        part = jnp.sum(jnp.exp(sk - lse_ref[0]) * dl_ref[0], axis=1, keepdims=True)
        out_ref[0] += -jnp.broadcast_to(part, (1, 128))

    col = pl.BlockSpec((1, 1, ts), lambda h, i: (h, 0, i))
    return _call(body, name=name, grid=(4, S // ts),
                 in_specs=[pl.BlockSpec(memory_space=pltpu.SMEM), col, col],
                 out_specs=pl.BlockSpec((1, 1, 128), lambda h, i: (h, 0, 0)),
                 out_shape=SDS((4, 1, 128), F32), sem=("parallel", "arbitrary"))(sink, lse, delta)


def _bias_maps(T, W):
    rpb = T // GRID_W
    nd = 2 * W + 1
    rmap = np.zeros((nd, rpb, rpb, 15), np.float32)
    for df in range(nd):
        for a in range(rpb):
            for b in range(rpb):
                r = (df - W) * rpb + b - a + 7
                if 0 <= r < 15:
                    rmap[df, a, b, r] = 1.0
    cmap = np.zeros((GRID_W, GRID_W, 31), np.float32)
    for q in range(GRID_W):
        for k in range(GRID_W):
            cmap[q, k, int(np.clip(k - q, -15, 15)) + 15] = 1.0
    return jnp.asarray(rmap), jnp.asarray(cmap)


def _bias_tiles(rel_bias, S, T):
    W = -(-7 // (T // GRID_W))
    rmap, cmap = _bias_maps(T, W)
    t = jnp.einsum("dabr,hrc,qkc->hdbkaq", rmap, rel_bias, cmap, precision=lax.Precision.HIGHEST)
    return t.reshape(1, 4, 2 * W + 1, T, T) * (1.0 / SCALE) + jnp.asarray(_mask_tiles_d(S, T))


def _bias_tiles_t(dtiles, T):
    rpb = T // GRID_W
    W = -(-7 // rpb)
    rmap, cmap = _bias_maps(T, W)
    t = dtiles.reshape(4, 2 * W + 1, rpb, GRID_W, rpb, GRID_W)
    return jnp.einsum("dabr,hdbkaq,qkc->hrc", rmap, t, cmap, precision=lax.Precision.HIGHEST)


def _mixer_cfg(S):
    ta, tb, td = (min(S, ATT_TILE[k]) for k in "abd")
    cq, ck = (min(S, t) for t in ATT_TILE["c"])
    cb = min(S, ATT_TILE["c_bwd"])
    a = _Mixer(ta, ta, _band(128, ta), 2)
    b = _Mixer(tb, tb, _band(1024, tb), 1, hp=2)
    d = _Mixer(td, td, -(-7 // (td // GRID_W)), 1, _edge_blocks_d(td), hp=2)
    return {"a": (a, a, jnp.asarray(_mask_tiles_a(ta))), "b": (b, b, jnp.asarray(_mask_tiles_b(tb))),
            "c": (_Mixer(cq, ck, None, 2), _Mixer(cb, cb, None, 2), None), "d": (d, d, None)}


def _layer_fwd(l, x, c8, lw, p, tabs, next_shards, me, late=None):
    S, D = x.shape
    dq = D // N_SHARD
    ada_sh, win_sh, wgm_sh, wb_sh, wout_sh = lw
    one_d, axial = tabs
    cfg = _mixer_cfg(S)
    tm = min(S, ROW_TILE)

    ada = _matmul(f"ada_l{l}", (N_SHARD,), [c8, ada_sh, p["b_ada"][l][None, :]],
                  [pl.BlockSpec((8, D), lambda j: (0, 0)), pl.BlockSpec((None, D, 3 * dq), lambda j: (j, 0, 0)),
                   pl.BlockSpec((1, 3 * dq), lambda j: (0, j))],
                  [SDS((8, 3 * D), F32)], [pl.BlockSpec((8, 3 * dq), lambda j: (0, j))],
                  epilogue=lambda acc, b_ref: (acc + b_ref[...],), a_fn=lambda a: a * _sigmoid(a))[0][0:1]
    shift, scale, gate = ada[:, :D], ada[:, D:2 * D], ada[:, 2 * D:]
    g_row = p["norm_g"][l][None, :]
    h = _norm_mod(f"norm_mod_l{l}", x, g_row, scale, shift)

    u = _matmul(f"mm_in_l{l}", (S // tm, N_SHARD), [h, win_sh],
                [pl.BlockSpec((tm, D), lambda i, j: (i, 0)), pl.BlockSpec((None, D, IN_SHARD), lambda i, j: (j, 0, 0))],
                [SDS((S, IN_COLS), BF16)], [pl.BlockSpec((tm, IN_SHARD), lambda i, j: (i, j))])[0]

    qa = _rope_fwd(f"rope_qa_l{l}", u, QCOL[0], 4, one_d, 64)
    ka = _rope_fwd(f"rope_ka_l{l}", u, KCOL[0], 2, one_d, 64)
    qb = _rope_fwd(f"rope_qb_l{l}", u, QCOL[1], 4, one_d, 64)
    kb = _rope_fwd(f"rope_kb_l{l}", u, KCOL[1], 4, one_d, 64)
    qc = _rope_fwd(f"rope_qc_l{l}", u, QCOL[2], 4, axial, 32, p["c_q_norm"][l][None, :])
    kc = _rope_fwd(f"rope_kc_l{l}", u, KCOL[2], 2, axial, 32, p["c_k_norm"][l][None, :])

    loads = [[], [], [], []]
    if next_shards is not None:
        s_ada, s_in, s_gm, s_wb, s_out = next_shards
        loads = [[s_wb, s_out], [s_in], [s_gm], [s_ada]]
    if late is not None:
        loads[0] += [late[1], late[2]]
        loads[2] += [late[0]]
    carries = [_Carry("gather", a) if a else None for a in loads]
    no_sink = jnp.full((4,), NEG_INF, F32)
    bias = _bias_tiles(p["d_rel_bias"][l], S, cfg["d"][0].tq)
    br_a, o_a, lse_a, got_a = _attn_fwd(f"attn_a_l{l}", cfg["a"][0], qa, 0, ka, 0, u, VCOL[0], u, GCOL[0],
                                        p["a_sink"][l], cfg["a"][2], carries[0])
    br_b, o_b, lse_b, got_b = _attn_fwd(f"attn_b_l{l}", cfg["b"][0], qb, 0, kb, 0, u, VCOL[1], u, GCOL[1], no_sink,
                                        cfg["b"][2], carries[1])
    br_c, o_c, lse_c, got_c = _attn_fwd(f"attn_c_l{l}", cfg["c"][0], qc, 0, kc, 0, u, VCOL[2], u, GCOL[2], no_sink,
                                        None, carries[2])
    br_d, o_d, lse_d, got_d = _attn_fwd(f"attn_d_l{l}", cfg["d"][0], u, QCOL[3], u, KCOL[3], u, VCOL[3], u, GCOL[3],
                                        no_sink, bias, carries[3])
    next_lw = None
    if next_shards is not None:
        next_lw = _own_slot([got_d[0], got_b[0], got_c[0], got_a[0], got_a[1]], next_shards, me)
    if late is not None:
        wgm_sh, wb_sh, wout_sh = _own_slot([got_c[-1], got_a[-2], got_a[-1]], late, me)
        lw = (ada_sh, win_sh, wgm_sh, wb_sh, wout_sh)
    br = jnp.concatenate([br_a, br_b, br_c, br_d], axis=1)
    o_all = jnp.concatenate([o_a, o_b, o_c, o_d], axis=1)

    def merge_body(h_ref, wg_ref, br_ref, wb_ref, mg_ref, pj_ref, z_ref, acc_ref):
        n = pl.program_id(2)
        mgv = _sigmoid(lax.dot_general(h_ref[...], wg_ref[...], _NN, preferred_element_type=F32))
        pj = lax.dot_general(br_ref[...], wb_ref[...], _NN, preferred_element_type=F32)
        mg_ref[...] = mgv.astype(BF16)
        pj_ref[...] = pj.astype(BF16)

        @pl.when(n == 0)
        def _():
            acc_ref[...] = mgv * pj

        @pl.when(n > 0)
        def _():
            acc_ref[...] += mgv * pj

        @pl.when(n == 3)
        def _():
            z_ref[...] = acc_ref[...].astype(BF16)

    wide = pl.BlockSpec((tm, dq), lambda i, j, n: (i, n * N_SHARD + j))
    mg, proj, z = _call(
        merge_body, name=f"merge_l{l}", grid=(S // tm, N_SHARD, 4),
        in_specs=[pl.BlockSpec((tm, D), lambda i, j, n: (i, 0)),
                  pl.BlockSpec((None, D, dq), lambda i, j, n: (n, 0, j)),
                  pl.BlockSpec((tm, BRANCH_W), lambda i, j, n: (i, n)),
                  pl.BlockSpec((None, None, BRANCH_W, dq), lambda i, j, n: (j, n, 0, 0))],
        out_specs=[wide, wide, pl.BlockSpec((tm, dq), lambda i, j, n: (i, j))],
        out_shape=[SDS((S, 4 * D), BF16), SDS((S, 4 * D), BF16), SDS((S, D), BF16)],
        scratch=[pltpu.VMEM((tm, dq), F32)], sem=("parallel", "parallel", "arbitrary"))(h, wgm_sh, br, wb_sh)

    tn = min(D, 1024)
    x_new, o2 = _matmul(
        f"mm_out_l{l}", (S // tm, D // tn), [z, wout_sh, x, gate],
        [pl.BlockSpec((tm, D), lambda i, j: (i, 0)), pl.BlockSpec((N_SHARD, dq, tn), lambda i, j: (0, 0, j)),
         pl.BlockSpec((tm, tn), lambda i, j: (i, j)), pl.BlockSpec((1, tn), lambda i, j: (0, j))],
        [SDS((S, D), F32), SDS((S, D), BF16)],
        [pl.BlockSpec((tm, tn), lambda i, j: (i, j)), pl.BlockSpec((tm, tn), lambda i, j: (i, j))],
        k_inner=N_SHARD, epilogue=lambda acc, x_ref, g_ref: (x_ref[...] + g_ref[...] * acc, acc))
    res = dict(x=x, h=h, u=u, qa=qa, ka=ka, qb=qb, kb=kb, qc=qc, kc=kc, br=br, o_all=o_all,
               lse=(lse_a, lse_b, lse_c, lse_d), bias=bias, mg=mg, proj=proj, z=z, o2=o2,
               g_row=g_row, scale=scale, gate=gate)
    return x_new, res, (ada_sh, win_sh, wgm_sh, wb_sh, wout_sh), next_lw


def _layer_bwd(l, dxo, r, lw, p, tabs, pending):
    x, h, u = r["x"], r["h"], r["u"]
    S, D = x.shape
    dq = D // N_SHARD
    ada_sh, win_sh, wgm_sh, wb_sh, wout_sh = lw
    one_d, axial = tabs
    cfg = _mixer_cfg(S)
    tm = min(S, ROW_TILE)
    tk = min(S, 512)
    tn = min(D, 1024)

    do2, dgate = _out_bwd_ew(f"out_bwd_l{l}", dxo, r["o2"], r["gate"])
    dz = _matmul(f"mm_dz_l{l}", (S // tm, N_SHARD), [do2, wout_sh],
                 [pl.BlockSpec((tm, D), lambda i, n: (i, 0)), pl.BlockSpec((None, dq, D), lambda i, n: (n, 0, 0))],
                 [SDS((S, D), BF16)], [pl.BlockSpec((tm, dq), lambda i, n: (i, n))], tb=True)[0]
    tkw = min(S, 2048)
    g_out = _matmul(f"mm_gwout_l{l}", (N_SHARD, D // tn, S // tkw), [r["z"], do2],
                    [pl.BlockSpec((tkw, dq), lambda n, j, k: (k, n)), pl.BlockSpec((tkw, tn), lambda n, j, k: (k, j))],
                    [SDS((N_SHARD, dq, D), F32)], [pl.BlockSpec((None, dq, tn), lambda n, j, k: (n, 0, j))],
                    ta=True, k_axis=2, acc_shape=(dq, tn))[0]

    dmg, dproj = _merge_bwd_ew(f"merge_bwd_l{l}", dz, r["mg"], r["proj"])
    nj = D // tn
    g_gm = _matmul(f"mm_gwgm_l{l}", (4, D // tn, nj, S // tk), [h, dmg],
                   [pl.BlockSpec((tk, tn), lambda n, i, j, k: (k, i)),
                    pl.BlockSpec((tk, tn), lambda n, i, j, k: (k, n * nj + j))],
                   [SDS((4, D, D), F32)], [pl.BlockSpec((None, tn, tn), lambda n, i, j, k: (n, i, j))],
                   ta=True, k_axis=3, acc_shape=(tn, tn))[0]
    dh1 = _matmul(f"mm_dh1_l{l}", (S // tm, nj, 4 * nj), [dmg, wgm_sh],
                  [pl.BlockSpec((tm, tn), lambda i, j, kk: (i, kk)),
                   pl.BlockSpec((None, tn, tn), lambda i, j, kk: (kk // nj, j, kk % nj))],
                  [SDS((S, D), F32)], [pl.BlockSpec((tm, tn), lambda i, j, kk: (i, j))],
                  tb=True, k_axis=2, acc_shape=(tm, tn))[0]
    dbr = _matmul(f"mm_dbr_l{l}", (S // tm, 4), [dproj, wb_sh],
                  [pl.BlockSpec((tm, D), lambda i, n: (i, n)),
                   pl.BlockSpec((N_SHARD, None, BRANCH_W, dq), lambda i, n: (0, n, 0, 0))],
                  [SDS((S, 4 * BRANCH_W), BF16)], [pl.BlockSpec((tm, BRANCH_W), lambda i, n: (i, n))],
                  tb=True, k_inner=N_SHARD)[0]
    tkl = min(S, 2048)
    g_wb = _matmul(f"mm_gwb_l{l}", (N_SHARD, 4, S // tkl), [r["br"], dproj],
                   [pl.BlockSpec((tkl, BRANCH_W), lambda j, n, k: (k, n)),
                    pl.BlockSpec((tkl, dq), lambda j, n, k: (k, n * N_SHARD + j))],
                   [SDS((N_SHARD, 4, BRANCH_W, dq), F32)],
                   [pl.BlockSpec((None, None, BRANCH_W, dq), lambda j, n, k: (j, n, 0, 0))],
                   ta=True, k_axis=2, acc_shape=(BRANCH_W, dq))[0]

    do_all, dg_all, delta = _attn_bwd_pre(f"attn_pre_l{l}", dbr, r["o_all"], u)
    lse_a, lse_b, lse_c, lse_d = r["lse"]
    dsink = _sink_grad(f"sink_grad_l{l}", p["a_sink"][l], lse_a, delta)[:, 0, 0]
    carries = [None] * 3
    if pending is not None:
        p_in, p_gm, p_wb, p_out = pending
        carries = [_Carry("exchange", [p_in]), _Carry("exchange", [p_gm]), _Carry("exchange", [p_wb, p_out])]
    dqa, dka, dva, _, _ = _attn_bwd(f"attn_a_bwd_l{l}", cfg["a"][1], r["qa"], 0, r["ka"], 0, u, VCOL[0], do_all, 0,
                                    lse_a, delta, cfg["a"][2])
    dqb, dkb, dvb, _, got_b = _attn_bwd(f"attn_b_bwd_l{l}", cfg["b"][1], r["qb"], 0, r["kb"], 0, u, VCOL[1], do_all, 4,
                                        lse_b, delta, cfg["b"][2], carry=carries[0])
    dqc, dkc, dvc, _, got_c = _attn_bwd(f"attn_c_bwd_l{l}", cfg["c"][1], r["qc"], 0, r["kc"], 0, u, VCOL[2], do_all, 8,
                                        lse_c, delta, carry=carries[1])
    dqd, dkd, dvd, dbias, got_d = _attn_bwd(f"attn_d_bwd_l{l}", cfg["d"][1], u, QCOL[3], u, KCOL[3], u, VCOL[3],
                                            do_all, 12, lse_d, delta, r["bias"], True, carries[2])
    arrived = None if pending is None else [got_b[0], got_c[0], got_d[0], got_d[1]]
    d_rel = _bias_tiles_t(dbias, cfg["d"][1].tq)

    duqa, _ = _rope_bwd(f"rope_qa_bwd_l{l}", dqa, u, QCOL[0], 4, one_d, 64)
    duka, _ = _rope_bwd(f"rope_ka_bwd_l{l}", dka, u, KCOL[0], 2, one_d, 64)
    duqb, _ = _rope_bwd(f"rope_qb_bwd_l{l}", dqb, u, QCOL[1], 4, one_d, 64)
    dukb, _ = _rope_bwd(f"rope_kb_bwd_l{l}", dkb, u, KCOL[1], 4, one_d, 64)
    duqc, dcq = _rope_bwd(f"rope_qc_bwd_l{l}", dqc, u, QCOL[2], 4, axial, 32, p["c_q_norm"][l][None, :])
    dukc, dck = _rope_bwd(f"rope_kc_bwd_l{l}", dkc, u, KCOL[2], 2, axial, 32, p["c_k_norm"][l][None, :])
    bw = BRANCH_W
    du = jnp.concatenate(
        [duqa, duka, dva.astype(BF16), dg_all[:, 0:bw],
         duqb, dukb, dvb.astype(BF16), dg_all[:, bw:2 * bw],
         duqc, dukc, dvc.astype(BF16), dg_all[:, 2 * bw:3 * bw],
         dqd.astype(BF16), dkd.astype(BF16), dvd.astype(BF16), dg_all[:, 3 * bw:]], axis=1)

    tmi = min(D, 1024)
    g_in = _matmul(f"mm_gwin_l{l}", (N_SHARD, D // tmi, S // tk), [h, du],
                   [pl.BlockSpec((tk, tmi), lambda j, i, k: (k, i)), pl.BlockSpec((tk, IN_SHARD), lambda j, i, k: (k, j))],
                   [SDS((N_SHARD, D, IN_SHARD), F32)], [pl.BlockSpec((None, tmi, IN_SHARD), lambda j, i, k: (j, i, 0))],
                   ta=True, k_axis=2, acc_shape=(tmi, IN_SHARD))[0]
    dh2 = _matmul(f"mm_dh2_l{l}", (S // tm, nj, N_SHARD), [du, win_sh],
                  [pl.BlockSpec((tm, IN_SHARD), lambda i, j, k: (i, k)),
                   pl.BlockSpec((None, tn, IN_SHARD), lambda i, j, k: (k, j, 0))],
                  [SDS((S, D), F32)], [pl.BlockSpec((tm, tn), lambda i, j, k: (i, j))],
                  tb=True, k_axis=2, acc_shape=(tm, tn))[0]

    dx_prev, dshift, dscale, dng = _norm_mod_bwd(f"norm_mod_bwd_l{l}", x, dh1, dh2, dxo, r["g_row"], r["scale"])
    d_ada = jnp.concatenate([dshift, dscale, dgate], axis=1)[0]
    big = (g_in, g_gm, g_wb, g_out)
    small = dict(norm_g=dng[0], b_ada=d_ada, a_sink=dsink, c_q_norm=dcq[0], c_k_norm=dck[0], d_rel_bias=d_rel)
    return dx_prev, big, small, arrived


def _place():
    return lax.axis_index("x"), lax.axis_index("y"), lax.axis_index("c")


class _Carry:
    def __init__(self, kind, arrays):
        self.kind, self.n, self.ins = kind, len(arrays), list(arrays)
        any_spec = pl.BlockSpec(memory_space=pl.ANY)
        self.in_specs = [any_spec] * self.n
        self.out_specs = [any_spec] * self.n
        if kind == "gather":
            self.out_shape = [SDS((N_SHARD, *a.shape), a.dtype) for a in arrays]
        else:
            self.out_shape = [SDS((3, *a.shape[1:]), a.dtype) for a in arrays]
        self.scratch = [pltpu.SemaphoreType.DMA((self.n, 3)), pltpu.SemaphoreType.DMA((self.n, 3))]

    def _copies(self, ins, outs, send_sems, recv_sems, arriving):
        x, y, c = _place()
        cps = []
        for a in range(self.n):
            for k, (px, py) in enumerate([(1 - x, y), (x, 1 - y), (1 - x, 1 - y)]):
                if self.kind == "gather":
                    src, dst = ins[a], outs[a].at[2 * px + py if arriving else 2 * x + y]
                else:
                    src, dst = ins[a].at[2 * px + py], outs[a].at[k]
                cps.append(pltpu.make_async_remote_copy(src, dst, send_sems.at[a, k], recv_sems.at[a, k],
                                                        device_id=(px, py, c), device_id_type=MESH))
        return cps

    def start(self, ins, outs, send_sems, recv_sems):
        for cp in self._copies(ins, outs, send_sems, recv_sems, False):
            cp.start()

    def finish(self, ins, outs, send_sems, recv_sems):
        for cp in self._copies(ins, outs, send_sems, recv_sems, True):
            cp.wait_recv()
        for cp in self._copies(ins, outs, send_sems, recv_sems, False):
            cp.wait_send()


def _run_carry(name, carry):
    n = carry.n

    def body(*refs):
        args = (refs[:n], refs[n:2 * n], refs[2 * n], refs[2 * n + 1])
        carry.start(*args)
        carry.finish(*args)

    return pl.pallas_call(body, name=name, in_specs=carry.in_specs, out_specs=carry.out_specs,
                          out_shape=carry.out_shape, scratch_shapes=carry.scratch)(*carry.ins)


def _own_slot(gathered, shards, me):
    return [lax.dynamic_update_index_in_dim(g, s, me, 0) for g, s in zip(gathered, shards)]


def _gather_small(name, v):
    m_per, n = v.shape

    def body(x_ref, out_ref, send_sems, recv_sems, local_sem):
        x, y, c = _place()
        me, sibling = (x, y, c), (x, y, 1 - c)
        chips = [(1 - x, y), (x, 1 - y), (1 - x, 1 - y)]

        def rows(px, py, pc):
            return out_ref.at[pl.ds((4 * px + 2 * py + pc) * m_per, m_per), :]

        def copy(k, block, to, src=None):
            return pltpu.make_async_remote_copy(
                src_ref=rows(*block) if src is None else src, dst_ref=rows(*block),
                send_sem=send_sems.at[k], recv_sem=recv_sems.at[k], device_id=to, device_id_type=MESH)

        mine = pltpu.make_async_copy(x_ref, rows(*me), local_sem)
        mine.start()
        first = [copy(0, me, sibling, src=x_ref)]
        first += [copy(1 + j, me, (*chip, c), src=x_ref) for j, chip in enumerate(chips)]
        for cp in first:
            cp.start()
        passed = [copy(4 + j, (*chip, c), sibling) for j, chip in enumerate(chips)]
        for j, chip in enumerate(chips):
            copy(1 + j, (*chip, c), me).wait_recv()
            passed[j].start()
        copy(0, sibling, me).wait_recv()
        for j, chip in enumerate(chips):
            copy(4 + j, (*chip, 1 - c), me).wait_recv()
        for cp in first + passed:
            cp.wait_send()
        mine.wait()

    return pl.pallas_call(
        body, name=name, out_shape=SDS((8 * m_per, n), v.dtype),
        in_specs=[pl.BlockSpec(memory_space=pltpu.VMEM)], out_specs=pl.BlockSpec(memory_space=pltpu.VMEM),
        scratch_shapes=[pltpu.SemaphoreType.DMA((7,)), pltpu.SemaphoreType.DMA((7,)), pltpu.SemaphoreType.DMA])(v)


def _pair_send_half(name, grads):
    n = len(grads)

    def body(*refs):
        ins, outs = refs[:n], refs[n:2 * n]
        send_sems, recv_sems = refs[2 * n:]
        x, y, c = _place()
        cps = []
        for a in range(n):
            cp = pltpu.make_async_remote_copy(ins[a].at[:, 1 - c], outs[a], send_sems.at[a], recv_sems.at[a],
                                              device_id=(x, y, 1 - c), device_id_type=MESH)
            cp.start()
            cps.append(cp)
        for cp in cps:
            cp.wait_recv()
        for cp in cps:
            cp.wait_send()

    any_spec = pl.BlockSpec(memory_space=pl.ANY)
    return pl.pallas_call(
        body, name=name, in_specs=[any_spec] * n, out_specs=[any_spec] * n,
        out_shape=[SDS((g.shape[0], *g.shape[2:]), g.dtype) for g in grads],
        scratch_shapes=[pltpu.SemaphoreType.DMA((n,)), pltpu.SemaphoreType.DMA((n,))])(*grads)


def _pair_gather(name, halves):
    n = len(halves)

    def body(*refs):
        outs = refs[n:2 * n]
        send_sems, recv_sems = refs[2 * n:]
        x, y, c = _place()
        cps = [pltpu.make_async_remote_copy(outs[a].at[c], outs[a].at[c], send_sems.at[a], recv_sems.at[a],
                                            device_id=(x, y, 1 - c), device_id_type=MESH) for a in range(n)]
        for cp in cps:
            cp.start()
        for a in range(n):
            pltpu.make_async_remote_copy(outs[a].at[c], outs[a].at[1 - c], send_sems.at[a], recv_sems.at[a],
                                         device_id=(x, y, 1 - c), device_id_type=MESH).wait_recv()
        for cp in cps:
            cp.wait_send()

    any_spec = pl.BlockSpec(memory_space=pl.ANY)
    return pl.pallas_call(
        body, name=name, in_specs=[any_spec] * n, out_specs=[any_spec] * n,
        out_shape=[SDS(g.shape, g.dtype) for g in halves], input_output_aliases={a: a for a in range(n)},
        scratch_shapes=[pltpu.SemaphoreType.DMA((n,)), pltpu.SemaphoreType.DMA((n,))])(*halves)


def _add_half(name, g, recv, c_idx):
    _, _, R, C = g.shape
    tr = min(R, 256)

    def body(c_ref, g_ref, r_ref, o_ref):
        o_ref[...] = g_ref[...] + r_ref[...]

    return _call(body, name=name, grid=(4, R // tr), nsp=1,
                 in_specs=[pl.BlockSpec((None, None, tr, C), lambda j, r, c_ref: (j, c_ref[0], r, 0)),
                           pl.BlockSpec((None, tr, C), lambda j, r, c_ref: (j, r, 0))],
                 out_specs=pl.BlockSpec((None, tr, C), lambda j, r, c_ref: (j, r, 0)),
                 out_shape=SDS((4, R, C), F32), sem=("parallel", "parallel"))(c_idx, g, recv)


def _add_shards(name, part, recv, idx):
    _, R, C = part.shape
    tr = min(R, 256)

    def body(idx_ref, p_ref, r_ref, o_ref):
        o_ref[...] = ((p_ref[...] + r_ref[0]) + r_ref[1]) + r_ref[2]

    return _call(body, name=name, grid=(R // tr,), nsp=1,
                 in_specs=[pl.BlockSpec((None, tr, C), lambda r, idx_ref: (idx_ref[0], r, 0)),
                           pl.BlockSpec((3, tr, C), lambda r, idx_ref: (0, r, 0))],
                 out_specs=pl.BlockSpec((None, tr, C), lambda r, idx_ref: (idx_ref[1], r, 0)),
                 out_shape=SDS((2, R, C), F32), sem=("parallel",))(idx, part, recv)


def _pair_sum_layer(l, big, c_idx):
    views = []
    for g in big:
        rows = g.shape[-2] if g.ndim == 3 else g.shape[1] * g.shape[2]
        views.append(g.reshape(N_SHARD, 2, rows // 2, g.shape[-1]))
    recv1 = _pair_send_half(f"rs_pair_send_l{l}", views)
    return [_add_half(f"rs_add_half{a}_l{l}", v, r1, c_idx) for a, (v, r1) in enumerate(zip(views, recv1))]


def _finish_reduce_layer(l, big, parts, recv2, idx):
    halves = [_add_shards(f"rs_add_shards{a}_l{l}", pt, r2, idx) for a, (pt, r2) in enumerate(zip(parts, recv2))]
    full = _pair_gather(f"rs_pair_gather_l{l}", halves)
    return [f.reshape(g.shape[1:]) for f, g in zip(full, big)]


def _adamw_math(w, g, m, v):
    m = ADAM_B1 * m + (1.0 - ADAM_B1) * g
    v = ADAM_B2 * v + (1.0 - ADAM_B2) * (g * g)
    m_hat = m / (1.0 - ADAM_B1 ** ADAM_STEP)
    v_hat = v / (1.0 - ADAM_B2 ** ADAM_STEP)
    delta = -ADAM_LR * (m_hat / (jnp.sqrt(v_hat) + ADAM_EPS) + ADAM_WD * w)
    return delta, m, v


def _adamw(name, w, g, m, v):
    shape = w.shape
    C = shape[-1]
    R = int(np.prod(shape[:-1]))
    tr = min(R, 256)

    def body(w_ref, g_ref, m_ref, v_ref, d_ref, nm_ref, nv_ref):
        d, nm, nv = _adamw_math(w_ref[...], g_ref[...], m_ref[...], v_ref[...])
        d_ref[...] = d
        nm_ref[...] = nm
        nv_ref[...] = nv

    blk = pl.BlockSpec((tr, C), lambda i: (i, 0))
    outs = _call(body, name=name, grid=(R // tr,), in_specs=[blk] * 4, out_specs=[blk] * 3,
                 out_shape=[SDS((R, C), F32)] * 3, sem=("parallel",))(*(a.reshape(R, C) for a in (w, g, m, v)))
    return [o.reshape(shape) for o in outs]


def _adamw_small(name, w, g8, m, v):
    R = w.shape[0]

    def body(w_ref, g_ref, m_ref, v_ref, go_ref, d_ref, nm_ref, nv_ref):
        g = g_ref[0]
        for b in range(1, 8):
            g = g + g_ref[b]
        d, nm, nv = _adamw_math(w_ref[...], g, m_ref[...], v_ref[...])
        go_ref[...] = g
        d_ref[...] = d
        nm_ref[...] = nm
        nv_ref[...] = nv

    blk = pl.BlockSpec((R, 128), lambda i: (0, 0))
    return _call(body, name=name, grid=(1,), in_specs=[blk, pl.BlockSpec((8, R, 128), lambda i: (0, 0, 0)), blk, blk],
                 out_specs=[blk] * 4, out_shape=[SDS((R, 128), F32)] * 4, sem=("arbitrary",))(w, g8, m, v)


SMALL_NAMES = ("norm_g", "b_ada", "a_sink", "c_q_norm", "c_k_norm", "d_rel_bias", "final_g")


def _pack(parts, extra_rows=0):
    flat = jnp.concatenate([a.reshape(-1) for a in parts])
    rows = -(-flat.shape[0] // 128)
    rows = -(-rows // 8) * 8 + extra_rows
    return jnp.pad(flat, (0, rows * 128 - flat.shape[0])).reshape(rows, 128)


def _unpack(packed, like):
    flat = packed.reshape(-1)
    out, off = [], 0
    for a in like:
        out.append(flat[off:off + a.size].reshape(a.shape))
        off += a.size
    return out


def _device_step(x, c8, tgt, shards, p, me, c_idx):
    S = x.shape[0]
    L = len(shards)
    tabs = _rope_tables(S)
    first = _Carry("gather", shards[0][:2])
    lw = [(*_own_slot(_run_carry("gather_w_l0", first), shards[0][:2], me), None, None, None)]
    res = []
    for l in range(L):
        x, r, lw[l], nxt = _layer_fwd(l, x, c8, lw[l], p, tabs, shards[l + 1] if l + 1 < L else None, me,
                                      shards[0][2:] if l == 0 else None)
        res.append(r)
        lw.append(nxt)
    dx, dfg, loss = _final_loss("final_loss", x, tgt, p["final_g"][None, :])
    bigs, smalls, parts, arrived = [None] * L, [None] * L, [None] * L, [None] * L
    for l in reversed(range(L)):
        pending = parts[l + 1] if l + 1 < L else None
        dx, bigs[l], smalls[l], arr = _layer_bwd(l, dx, res[l], lw[l], p, tabs, pending)
        if pending is not None:
            arrived[l + 1] = arr
        parts[l] = _pair_sum_layer(l, bigs[l], c_idx)
    arrived[0] = _run_carry("rs_shard_exchange_l0", _Carry("exchange", parts[0]))
    idx = jnp.concatenate([jnp.reshape(me, (1,)).astype(I32), c_idx])
    reduced = [_finish_reduce_layer(l, bigs[l], parts[l], arrived[l], idx) for l in range(L)]
    return loss, dx, reduced, smalls, dfg[0]


def kernel(x, c, norm_g, w_ada, b_ada, w_in, a_sink, c_q_norm, c_k_norm, d_rel_bias, w_gate_merge, w_branch, w_out, final_g, loss_target, m_norm_g, m_w_ada, m_b_ada, m_w_in, m_a_sink, m_c_q_norm, m_c_k_norm, m_d_rel_bias, m_w_gate_merge, m_w_branch, m_w_out, m_final_g, v_norm_g, v_w_ada, v_b_ada, v_w_in, v_a_sink, v_c_q_norm, v_c_k_norm, v_d_rel_bias, v_w_gate_merge, v_w_branch, v_w_out, v_final_g):
    L, D = norm_g.shape
    dq = D // N_SHARD
    p = dict(norm_g=norm_g, b_ada=b_ada, a_sink=a_sink, c_q_norm=c_q_norm, c_k_norm=c_k_norm,
             d_rel_bias=d_rel_bias, final_g=final_g)
    xi, yi, ci = _place()
    c_idx = jnp.reshape(ci, (1,)).astype(I32)
    me = 2 * xi + yi

    shards = [[w_ada[l].astype(BF16), w_in[l].astype(BF16), w_gate_merge[l].astype(BF16),
               w_branch[l].astype(BF16), w_out[l].astype(BF16)] for l in range(L)]
    c8 = jnp.broadcast_to(c, (8, D))
    loss, grad_x, reduced, smalls, dfg = _device_step(x[0], c8, loss_target[0], shards, p, me, c_idx)
    loss = lax.psum(loss[0, 0], ("x", "y", "c"))

    small_parts = [jnp.stack([s[n] for s in smalls]) for n in SMALL_NAMES[:-1]] + [dfg]
    packed = _pack(small_parts + [c[0]])
    rows = packed.shape[0]
    g8 = _gather_small("gather_small", packed).reshape(8, rows, 128)
    small_w = [p[n] for n in SMALL_NAMES]
    small_m = [m_norm_g, m_b_ada, m_a_sink, m_c_q_norm, m_c_k_norm, m_d_rel_bias, m_final_g]
    small_v = [v_norm_g, v_b_ada, v_a_sink, v_c_q_norm, v_c_k_norm, v_d_rel_bias, v_final_g]
    pad_c = [jnp.zeros((D,), F32)]
    sg, sd, sm, sv = _adamw_small("adamw_small", _pack(small_w + pad_c), g8, _pack(small_m + pad_c),
                                  _pack(small_v + pad_c))
    sg, sd, sm, sv = (_unpack(a, small_w) for a in (sg, sd, sm, sv))

    n_small = sum(a.size for a in small_parts)
    flat8 = g8.reshape(8, rows * 128)
    c_all = flat8[:, n_small:n_small + D]
    dada_all = flat8[:, L * D:L * D + L * 3 * D].reshape(8, L, 3 * D)
    dada_mine = lax.dynamic_slice_in_dim(dada_all, (2 * xi + yi) * (3 * dq), 3 * dq, axis=2)
    tma = min(D, 1024)
    g_ada = jnp.stack([
        _matmul(f"mm_gwada_l{l}", (D // tma,), [c_all, dada_mine[:, l]],
                [pl.BlockSpec((8, tma), lambda i: (0, i)), pl.BlockSpec((8, 3 * dq), lambda i: (0, 0))],
                [SDS((D, 3 * dq), F32)], [pl.BlockSpec((tma, 3 * dq), lambda i: (i, 0))],
                ta=True, a_fn=lambda a: a * _sigmoid(a))[0] for l in range(L)])

    g_in, g_gm, g_wb, g_out = (jnp.stack([reduced[l][a] for l in range(L)]) for a in range(4))

    big = {}
    for nm, w, g, m, v in (("w_ada", w_ada, g_ada, m_w_ada, v_w_ada), ("w_in", w_in, g_in, m_w_in, v_w_in),
                           ("w_gate_merge", w_gate_merge, g_gm, m_w_gate_merge, v_w_gate_merge),
                           ("w_branch", w_branch, g_wb, m_w_branch, v_w_branch),
                           ("w_out", w_out, g_out, m_w_out, v_w_out)):
        big[nm] = (g, *_adamw(f"adamw_{nm}", w, g, m, v))

    order = ("norm_g", "w_ada", "b_ada", "w_in", "a_sink", "c_q_norm", "c_k_norm", "d_rel_bias",
             "w_gate_merge", "w_branch", "w_out", "final_g")
    cols = [[], [], [], []]
    for nm in order:
        if nm in big:
            vals = big[nm]
        else:
            k = SMALL_NAMES.index(nm)
            vals = (sg[k], sd[k], sm[k], sv[k])
        for col, val in zip(cols, vals):
            col.append(val)
    return (loss, grad_x[None], *cols[0], *cols[1], *cols[2], *cols[3])
```

```python
import numpy as np
import jax
import jax.numpy as jnp
from jax import lax
from jax.experimental import pallas as pl
from jax.experimental.pallas import tpu as pltpu

F32 = jnp.float32
BF16 = jnp.bfloat16
I32 = jnp.int32
SDS = jax.ShapeDtypeStruct
MESH = pl.DeviceIdType.MESH

HEAD_DIM = 128
GRID_W = 64
EPS = 1e-6
NEG_INF = -1e30
ROPE_THETA = 10000.0
SCALE = HEAD_DIM ** -0.5
LOG2E = 1.4426950408889634
SCALE_LOG2E = SCALE * LOG2E
N_SHARD = 4
BRANCH_W = 512
IN_COLS = 7168
IN_SHARD = IN_COLS // N_SHARD
QCOL = (0, 12, 28, 40)
KCOL = (4, 16, 32, 44)
VCOL = (6, 20, 34, 48)
GCOL = (8, 24, 36, 52)
KV_HEADS = (2, 4, 2, 4)

ADAM_LR = 0.001
ADAM_B1 = 0.9
ADAM_B2 = 0.999
ADAM_EPS = 1e-08
ADAM_WD = 0.01
ADAM_STEP = 10

V7X_VMEM_BYTES = 64 * 1024 * 1024
VMEM_LIMIT = V7X_VMEM_BYTES * 7 // 8

ATT_TILE = {"a": 256, "b": 512, "c": (512, 1024), "c_bwd": 512, "d": 256}
M_INIT = -1e20
ROW_TILE = 512
EW_ROWS = 256


def _band(reach, tile):
    return -(-reach // tile)


def _call(body, *, name, grid, in_specs, out_specs, out_shape, scratch=(), sem=None, nsp=0):
    params = pltpu.CompilerParams(dimension_semantics=sem, vmem_limit_bytes=VMEM_LIMIT)
    if nsp:
        gs = pltpu.PrefetchScalarGridSpec(num_scalar_prefetch=nsp, grid=grid, in_specs=in_specs,
                                          out_specs=out_specs, scratch_shapes=list(scratch))
        return pl.pallas_call(body, grid_spec=gs, out_shape=out_shape, name=name, compiler_params=params)
    return pl.pallas_call(body, grid=grid, in_specs=in_specs, out_specs=out_specs, out_shape=out_shape,
                          scratch_shapes=list(scratch), name=name, compiler_params=params)


def _sigmoid(x):
    return 1.0 / (1.0 + jnp.exp(-x))


def _matmul(name, grid, ins, in_specs, out_shape, out_specs, *, ta=False, tb=False, k_axis=None,
            acc_shape=None, epilogue=None, a_fn=None, k_inner=None):
    n_in = len(ins)
    n_out = len(out_shape)
    nk = grid[k_axis] if k_axis is not None else 1
    dn = (((0 if ta else 1,), (1 if tb else 0,)), ((), ()))

    def body(*refs):
        a = refs[0][...]
        if a_fn is not None:
            a = a_fn(a)
        a = a.astype(BF16)
        if k_inner is None:
            p = lax.dot_general(a, refs[1][...].astype(BF16), dn, preferred_element_type=F32)
        else:
            ck = a.shape[1] // k_inner
            p = None
            for kk in range(k_inner):
                t = lax.dot_general(a[:, kk * ck:(kk + 1) * ck], refs[1][kk].astype(BF16), dn,
                                    preferred_element_type=F32)
                p = t if p is None else p + t
        extra = refs[2:n_in]
        outs = refs[n_in:n_in + n_out]

        def fin(acc):
            vals = epilogue(acc, *extra) if epilogue is not None else (acc,)
            for o_ref, v in zip(outs, vals):
                o_ref[...] = v.astype(o_ref.dtype)

        if k_axis is None:
            fin(p)
        else:
            acc_ref = refs[-1]
            k = pl.program_id(k_axis)

            @pl.when(k == 0)
            def _():
                acc_ref[...] = p

            @pl.when(k > 0)
            def _():
                acc_ref[...] += p

            @pl.when(k == nk - 1)
            def _():
                fin(acc_ref[...])

    sem = tuple("arbitrary" if ax == k_axis else "parallel" for ax in range(len(grid)))
    scratch = [pltpu.VMEM(acc_shape, F32)] if k_axis is not None else []
    return _call(body, name=name, grid=grid, in_specs=in_specs, out_specs=out_specs, out_shape=out_shape,
                 scratch=scratch, sem=sem)(*ins)


def _norm_mod(name, x, g, scale, shift):
    S, D = x.shape
    ts = min(S, EW_ROWS)

    def body(x_ref, g_ref, sc_ref, sh_ref, h_ref):
        xv = x_ref[...]
        r = lax.rsqrt(jnp.mean(xv * xv, axis=-1, keepdims=True) + EPS)
        h_ref[...] = (((xv * r) * g_ref[...]) * (1.0 + sc_ref[...]) + sh_ref[...]).astype(BF16)

    row = pl.BlockSpec((1, D), lambda i: (0, 0))
    blk = pl.BlockSpec((ts, D), lambda i: (i, 0))
    return _call(body, name=name, grid=(S // ts,), in_specs=[blk, row, row, row], out_specs=blk,
                 out_shape=SDS((S, D), BF16), sem=("parallel",))(x, g, scale, shift)


def _norm_mod_bwd(name, x, dh1, dh2, dxo, g, scale):
    S, D = x.shape
    ts = min(S, EW_ROWS)

    def body(x_ref, a_ref, b_ref, dxo_ref, g_ref, sc_ref, dx_ref, dsh_ref, dsc_ref, dg_ref):
        @pl.when(pl.program_id(0) == 0)
        def _():
            dsh_ref[...] = jnp.zeros_like(dsh_ref)
            dsc_ref[...] = jnp.zeros_like(dsc_ref)
            dg_ref[...] = jnp.zeros_like(dg_ref)

        xv = x_ref[...]
        r = lax.rsqrt(jnp.mean(xv * xv, axis=-1, keepdims=True) + EPS)
        xh = xv * r
        dh = a_ref[...] + b_ref[...]
        gv = g_ref[...]
        one_sc = 1.0 + sc_ref[...]
        dsh_ref[...] += jnp.sum(dh, axis=0, keepdims=True)
        dsc_ref[...] += jnp.sum(dh * xh * gv, axis=0, keepdims=True)
        dg_ref[...] += jnp.sum(dh * xh * one_sc, axis=0, keepdims=True)
        dxh = dh * gv * one_sc
        dx = r * (dxh - xh * jnp.mean(dxh * xh, axis=-1, keepdims=True))
        dx_ref[...] = dxo_ref[...] + dx

    row = pl.BlockSpec((1, D), lambda i: (0, 0))
    blk = pl.BlockSpec((ts, D), lambda i: (i, 0))
    return _call(body, name=name, grid=(S // ts,), in_specs=[blk, blk, blk, blk, row, row],
                 out_specs=[blk, row, row, row],
                 out_shape=[SDS((S, D), F32), SDS((1, D), F32), SDS((1, D), F32), SDS((1, D), F32)],
                 sem=("arbitrary",))(x, dh1, dh2, dxo, g, scale)


def _out_bwd_ew(name, dxo, o2, gate):
    S, D = dxo.shape
    ts = min(S, EW_ROWS)

    def body(dxo_ref, o2_ref, gt_ref, do2_ref, dgt_ref):
        @pl.when(pl.program_id(0) == 0)
        def _():
            dgt_ref[...] = jnp.zeros_like(dgt_ref)

        d = dxo_ref[...]
        do2_ref[...] = (d * gt_ref[...]).astype(BF16)
        dgt_ref[...] += jnp.sum(d * o2_ref[...].astype(F32), axis=0, keepdims=True)

    row = pl.BlockSpec((1, D), lambda i: (0, 0))
    blk = pl.BlockSpec((ts, D), lambda i: (i, 0))
    return _call(body, name=name, grid=(S // ts,), in_specs=[blk, blk, row], out_specs=[blk, row],
                 out_shape=[SDS((S, D), BF16), SDS((1, D), F32)], sem=("arbitrary",))(dxo, o2, gate)


def _merge_bwd_ew(name, dz, mg, proj):
    S, D = dz.shape
    ts = min(S, ROW_TILE)
    td = min(D, 512)
    nd = D // td

    def body(dz_ref, mg_ref, pj_ref, dmg_ref, dpj_ref):
        d = dz_ref[...].astype(F32)
        m = mg_ref[...].astype(F32)
        dmg_ref[...] = (d * pj_ref[...].astype(F32) * m * (1.0 - m)).astype(BF16)
        dpj_ref[...] = (d * m).astype(BF16)

    wide = pl.BlockSpec((ts, td), lambda i, j, n: (i, n * nd + j))
    return _call(body, name=name, grid=(S // ts, nd, 4),
                 in_specs=[pl.BlockSpec((ts, td), lambda i, j, n: (i, j)), wide, wide],
                 out_specs=[wide, wide], out_shape=[SDS((S, 4 * D), BF16), SDS((S, 4 * D), BF16)],
                 sem=("parallel", "parallel", "arbitrary"))(dz, mg, proj)


def _final_loss(name, x, tgt, g):
    S, D = x.shape
    ts = min(S, EW_ROWS)

    def body(x_ref, t_ref, g_ref, dx_ref, dg_ref, loss_ref):
        @pl.when(pl.program_id(0) == 0)
        def _():
            dg_ref[...] = jnp.zeros_like(dg_ref)
            loss_ref[...] = jnp.zeros_like(loss_ref)

        xv = x_ref[...]
        r = lax.rsqrt(jnp.mean(xv * xv, axis=-1, keepdims=True) + EPS)
        xh = xv * r
        gv = g_ref[...]
        err = xh * gv - t_ref[...]
        row_loss = jnp.mean(err * err, axis=-1, keepdims=True)
        loss_ref[...] += 0.5 * jnp.sum(row_loss, axis=0, keepdims=True)
        dy = err * (1.0 / D)
        dg_ref[...] += jnp.sum(dy * xh, axis=0, keepdims=True)
        dxh = dy * gv
        dx_ref[...] = r * (dxh - xh * jnp.mean(dxh * xh, axis=-1, keepdims=True))

    row = pl.BlockSpec((1, D), lambda i: (0, 0))
    blk = pl.BlockSpec((ts, D), lambda i: (i, 0))
    return _call(body, name=name, grid=(S // ts,), in_specs=[blk, blk, row],
                 out_specs=[blk, row, pl.BlockSpec((1, 128), lambda i: (0, 0))],
                 out_shape=[SDS((S, D), F32), SDS((1, D), F32), SDS((1, 128), F32)],
                 sem=("arbitrary",))(x, tgt, g)


def _rope_tables(S):
    def tables(pos, dim):
        inv = ROPE_THETA ** (-jnp.arange(0, dim, 2, dtype=F32) / dim)
        ang = pos.astype(F32)[:, None] * inv[None, :]
        ang = jnp.concatenate([ang, ang], axis=-1)
        return jnp.cos(ang), jnp.sin(ang)

    pos = jnp.arange(S, dtype=I32)
    lane = np.arange(HEAD_DIM)
    cos1, sin1 = tables(pos, HEAD_DIM)
    up1 = jnp.asarray((lane >= 64).astype(np.float32))[None, :]
    one_d = (cos1, sin1 * up1, -sin1 * (1.0 - up1))
    cr, sr = tables(pos // GRID_W, HEAD_DIM // 2)
    cc, sc = tables(pos % GRID_W, HEAD_DIM // 2)
    cos2 = jnp.concatenate([cr, cc], axis=-1)
    sin2 = jnp.concatenate([sr, sc], axis=-1)
    up2 = jnp.asarray(((lane % 64) >= 32).astype(np.float32))[None, :]
    axial = (cos2, sin2 * up2, -sin2 * (1.0 - up2))
    return one_d, axial


def _rope_fwd(name, src, c0, nb, tabs, sh, gain=None):
    S = src.shape[0]
    ts = min(S, ROW_TILE)
    has_gain = gain is not None

    assert c0 % nb == 0
    hd = HEAD_DIM

    def body(*refs):
        x_ref, c_ref, sa_ref, sb_ref = refs[:4]
        o_ref = refs[-1]
        cv, sa, sb = c_ref[...], sa_ref[...], sb_ref[...]
        for hh in range(nb):
            lanes = slice(hh * hd, (hh + 1) * hd)
            xv = x_ref[:, lanes].astype(F32)
            if has_gain:
                r = lax.rsqrt(jnp.mean(xv * xv, axis=-1, keepdims=True) + EPS)
                xv = (xv * r) * refs[4][...]
            out = xv * cv + pltpu.roll(xv, sh, 1) * sa + pltpu.roll(xv, hd - sh, 1) * sb
            o_ref[:, lanes] = out.astype(BF16)

    tab = pl.BlockSpec((ts, hd), lambda i: (i, 0))
    in_specs = [pl.BlockSpec((ts, nb * hd), lambda i: (i, c0 // nb)), tab, tab, tab]
    ins = [src, *tabs]
    if has_gain:
        in_specs.append(pl.BlockSpec((1, hd), lambda i: (0, 0)))
        ins.append(gain)
    return _call(body, name=name, grid=(S // ts,), in_specs=in_specs,
                 out_specs=pl.BlockSpec((ts, nb * hd), lambda i: (i, 0)),
                 out_shape=SDS((S, nb * hd), BF16), sem=("parallel",))(*ins)


def _rope_bwd(name, dout, src, c0, nb, tabs, sh, gain=None):
    S = src.shape[0]
    ts = min(S, ROW_TILE)
    has_gain = gain is not None

    assert c0 % nb == 0
    hd = HEAD_DIM

    def body(*refs):
        d_ref, x_ref, c_ref, sa_ref, sb_ref = refs[:5]
        cv, sa, sb = c_ref[...], sa_ref[...], sb_ref[...]
        if has_gain:
            gn_ref, dx_ref, dgn_ref = refs[5:]

            @pl.when(pl.program_id(0) == 0)
            def _():
                dgn_ref[...] = jnp.zeros_like(dgn_ref)
        else:
            dx_ref = refs[5]
        for hh in range(nb):
            lanes = slice(hh * hd, (hh + 1) * hd)
            d = d_ref[:, lanes].astype(F32)
            dxn = d * cv + pltpu.roll(d * sa, hd - sh, 1) + pltpu.roll(d * sb, sh, 1)
            if has_gain:
                xv = x_ref[:, lanes].astype(F32)
                r = lax.rsqrt(jnp.mean(xv * xv, axis=-1, keepdims=True) + EPS)
                xh = xv * r
                dgn_ref[...] += jnp.sum(dxn * xh, axis=0, keepdims=True)
                dxh = dxn * gn_ref[...]
                dx_ref[:, lanes] = (r * (dxh - xh * jnp.mean(dxh * xh, axis=-1, keepdims=True))).astype(BF16)
            else:
                dx_ref[:, lanes] = dxn.astype(BF16)

    tab = pl.BlockSpec((ts, hd), lambda i: (i, 0))
    own = pl.BlockSpec((ts, nb * hd), lambda i: (i, 0))
    in_specs = [own, pl.BlockSpec((ts, nb * hd), lambda i: (i, c0 // nb)), tab, tab, tab]
    ins = [dout, src, *tabs]
    out_specs = [own]
    out_shape = [SDS((S, nb * hd), BF16)]
    if has_gain:
        row = pl.BlockSpec((1, hd), lambda i: (0, 0))
        in_specs.append(row)
        ins.append(gain)
        out_specs.append(row)
        out_shape.append(SDS((1, hd), F32))
    res = _call(body, name=name, grid=(S // ts,), in_specs=in_specs, out_specs=out_specs,
                out_shape=out_shape, sem=("arbitrary",))(*ins)
    return res if has_gain else (res[0], None)


def _offset_grid(T, W):
    d = (np.arange(2 * W + 1) - W)[:, None, None] * T
    return d + np.arange(T)[None, :, None] - np.arange(T)[None, None, :]


def _mask_tiles_a(T):
    dk = _offset_grid(T, _band(128, T))
    return np.where(np.abs(dk) <= 128, 0.0, NEG_INF).astype(np.float32)[None, None]


def _mask_tiles_b(T):
    dk = _offset_grid(T, _band(1024, T))
    ad = np.abs(dk)
    mult = ((ad <= 64).astype(np.float32) + ((ad <= 256) & (dk % 4 == 0)) + ((ad <= 1024) & (dk % 16 == 0)))
    return np.where(mult > 0, np.log(np.maximum(mult, 1.0)) / SCALE, NEG_INF).astype(np.float32)[None, None]


def _edge_blocks_d(T):
    return -(-4 // (T // GRID_W))


def _mask_tiles_d(S, T):
    rows, nq, rpb = S // GRID_W, S // T, T // GRID_W
    W, E = -(-7 // rpb), _edge_blocks_d(T)
    assert nq >= 2 * E + 1
    out = []
    for i in [*range(E), nq // 2, *range(nq - E, nq)]:
        kp = ((i + np.arange(2 * W + 1) - W) * T)[:, None, None] + np.arange(T)[None, :, None]
        qp = i * T + np.arange(T)[None, None, :]
        qr, qc, kr, kc = qp >> 6, qp & 63, kp >> 6, kp & 63
        rs = np.clip(qr - 4, 0, rows - 8)
        cs = np.clip(qc - 8, 0, GRID_W - 16)
        valid = (kr >= rs) & (kr < rs + 8) & (kc >= cs) & (kc < cs + 16)
        out.append(np.where(valid, 0.0, NEG_INF).astype(np.float32))
    return np.stack(out)[:, None]


def _variant(i, nq, E):
    if E == 0:
        return 0
    return jnp.where(i < E, i, jnp.where(i >= nq - E, i - (nq - 2 * E - 1), E))


_NT = (((1,), (1,)), ((), ()))
_TN = (((0,), (0,)), ((), ()))
_NN = (((1,), (0,)), ((), ()))


class _Mixer:
    def __init__(self, tq, tk, W, G, E=0, hp=1):
        self.tq, self.tk, self.W, self.G, self.E, self.hp = tq, tk, W, G, E, hp


def _attn_fwd(name, mx, q_arr, qc0, k_arr, kc0, v_arr, vc0, u, gc0, sink, bias=None, carry=None):
    S = q_arr.shape[0]
    tq, tk, W, G = mx.tq, mx.tk, mx.W, mx.G
    nq, nk = S // tq, S // tk
    nd = nk if W is None else 2 * W + 1
    has_bias = bias is not None
    nc = 0 if carry is None else carry.n
    hd = HEAD_DIM

    def jmap(i, d):
        return d if W is None else jnp.clip(i + d - W, 0, nk - 1)

    nin = 1 if W is None else nd
    ngd = nd if W is None else 1

    def body(*refs):
        sink_ref, q_ref, g_ref = refs[:3]
        k_refs, v_refs = refs[3:3 + nin], refs[3 + nin:3 + 2 * nin]
        n_in = 3 + 2 * nin + (1 if has_bias else 0)
        bias_ref = refs[n_in - 1] if has_bias else None
        out0 = n_in + 2 * nc
        br_ref, o_ref, lse_ref, m_s, l_s, acc_s = refs[out0:out0 + 6]
        i, d = pl.program_id(0), pl.program_id(1)
        if carry is not None:
            carry_refs = (refs[n_in:n_in + nc], refs[n_in + nc:out0], refs[-2], refs[-1])
            pl.when((i == 0) & (d == 0))(lambda: carry.start(*carry_refs))

        @pl.when(d == 0)
        def _():
            m_s[...] = jnp.full_like(m_s, M_INIT)
            l_s[...] = jnp.zeros_like(l_s)
            acc_s[...] = jnp.zeros_like(acc_s)

        def step(dd):
            k_ref, v_ref = k_refs[dd], v_refs[dd]
            for h in range(4):
                kv = slice((h // G) * hd, (h // G + 1) * hd)
                s = lax.dot_general(k_ref[:, kv], q_ref[:, h * hd:(h + 1) * hd], _NT,
                                    preferred_element_type=F32)
                if has_bias:
                    s = s + bias_ref[_variant(i, nq, mx.E), h if per_head else 0, d if W is None else dd]
                m_prev = m_s[h]
                m_new = jnp.maximum(m_prev, jnp.max(s, axis=0, keepdims=True))
                alpha = jnp.exp2((m_prev - m_new) * SCALE_LOG2E)
                p = jnp.exp2((s - m_new) * SCALE_LOG2E)
                l_s[h] = alpha * l_s[h] + jnp.sum(p, axis=0, keepdims=True)
                acc_s[h] = alpha * acc_s[h] + lax.dot_general(v_ref[:, kv], p.astype(BF16), _TN,
                                                               preferred_element_type=F32)
                m_s[h] = m_new

        if W is None:
            step(0)
        else:
            for dd in range(nd):
                j = i + dd - W
                pl.when((j >= 0) & (j < nk))(lambda dd=dd: step(dd))

        @pl.when(d == ngd - 1)
        def _():
            for h in range(4):
                lanes = slice(h * hd, (h + 1) * hd)
                sk = sink_ref[h]
                m = m_s[h] * SCALE
                mf = jnp.maximum(m, sk)
                a = jnp.exp(m - mf)
                lf = l_s[h] * a + jnp.exp(sk - mf)
                o = ((acc_s[h] * a) / lf).T
                gv = g_ref[:, lanes].astype(F32)
                o_ref[:, lanes] = o.astype(BF16)
                br_ref[:, lanes] = (o * (gv * _sigmoid(gv))).astype(BF16)
                lse_ref[h] = mf + jnp.log(lf)

        if carry is not None:
            pl.when((i == nq - 1) & (d == ngd - 1))(lambda: carry.finish(*carry_refs))

    n_kv = 4 // G
    assert qc0 % 4 == 0 and gc0 % 4 == 0 and kc0 % n_kv == 0 and vc0 % n_kv == 0

    def kv_spec(c0, dd):
        if W is None:
            return pl.BlockSpec((tk, n_kv * hd), lambda i, d: (d, c0 // n_kv))
        return pl.BlockSpec((tk, n_kv * hd), lambda i, d: (jnp.clip(i + dd - W, 0, nk - 1), c0 // n_kv))

    per_head = has_bias and bias.shape[1] == 4
    in_specs = [pl.BlockSpec(memory_space=pltpu.SMEM),
                pl.BlockSpec((tq, 4 * hd), lambda i, d: (i, qc0 // 4)),
                pl.BlockSpec((tq, 4 * hd), lambda i, d: (i, gc0 // 4)),
                *[kv_spec(kc0, dd) for dd in range(nin)], *[kv_spec(vc0, dd) for dd in range(nin)]]
    ins = [sink, q_arr, u, *[k_arr] * nin, *[v_arr] * nin]
    if has_bias:
        in_specs.append(pl.BlockSpec(bias.shape, lambda i, d: (0, 0, 0, 0, 0), pipeline_mode=pl.Buffered(1)))
        ins.append(bias)
    own = pl.BlockSpec((tq, 4 * hd), lambda i, d: (i, 0))
    out_specs = [own, own, pl.BlockSpec((4, 1, tq), lambda i, d: (0, 0, i))]
    out_shape = [SDS((S, 4 * hd), BF16), SDS((S, 4 * hd), BF16), SDS((4, 1, S), F32)]
    scratch = [pltpu.VMEM((4, 1, tq), F32), pltpu.VMEM((4, 1, tq), F32), pltpu.VMEM((4, hd, tq), F32)]
    sem = ("parallel", "arbitrary")
    if carry is not None:
        ins += carry.ins
        in_specs += carry.in_specs
        out_specs = carry.out_specs + out_specs
        out_shape = carry.out_shape + out_shape
        scratch += carry.scratch
        sem = ("arbitrary",) * 2
    res = _call(body, name=name, grid=(nq, ngd), in_specs=in_specs, out_specs=out_specs, out_shape=out_shape,
                scratch=scratch, sem=sem)(*ins)
    return (*res[nc:], list(res[:nc]))


def _attn_bwd(name, mx, q_arr, qc0, k_arr, kc0, v_arr, vc0, do_all, hb0, lse, delta, bias=None, want_dbias=False,
              carry=None):
    S = q_arr.shape[0]
    tq, tk, W, G = mx.tq, mx.tk, mx.W, mx.G
    nq, nk = S // tq, S // tk
    nd = nq if W is None else 2 * W + 1
    n_kv = 4 // G
    hd = HEAD_DIM
    has_bias = bias is not None
    nc = 0 if carry is None else carry.n
    assert qc0 % G == 0 and hb0 % G == 0 and (not want_dbias or (has_bias and G == 1)) and (W is None or tq == tk)

    def imap(j, d):
        return d if W is None else jnp.clip(j + d - W, 0, nq - 1)

    nin = 1 if W is None else nd
    ngd = nd if W is None else 1

    def body(*refs):
        k_ref, v_ref = refs[:2]
        q_refs, do_refs = refs[2:2 + nin], refs[2 + nin:2 + 2 * nin]
        lse_refs, dl_refs = refs[2 + 2 * nin:2 + 3 * nin], refs[2 + 3 * nin:2 + 4 * nin]
        n_in = 2 + 4 * nin + (1 if has_bias else 0)
        bias_ref = refs[n_in - 1] if has_bias else None
        out0 = n_in + 2 * nc
        dq_ref, dk_ref, dv_ref = refs[out0:out0 + 3]
        n_o = 4 if want_dbias else 3
        db_ref = refs[out0 + 3] if want_dbias else None
        dk_s, dv_s = refs[out0 + n_o:out0 + n_o + 2]
        kv, j, d = pl.program_id(0), pl.program_id(1), pl.program_id(2)
        if carry is not None:
            carry_refs = (refs[n_in:n_in + nc], refs[n_in + nc:out0], refs[-2], refs[-1])
            pl.when((kv == 0) & (j == 0) & (d == 0))(lambda: carry.start(*carry_refs))

        @pl.when((j == 0) & (d == 0))
        def _():
            dq_ref[...] = jnp.zeros_like(dq_ref)
            if want_dbias:
                db_ref[...] = jnp.zeros_like(db_ref)

        @pl.when(d == 0)
        def _():
            dk_s[...] = jnp.zeros_like(dk_s)
            dv_s[...] = jnp.zeros_like(dv_s)

        def step(dd):
            i = d if W is None else j + dd - W
            dt = None if W is None else 2 * W - dd
            q_ref, do_ref, lse_ref, dl_ref = q_refs[dd], do_refs[dd], lse_refs[dd], dl_refs[dd]
            row0 = pl.multiple_of(i * tq, tq)
            for kh in range(hp):
                k = k_ref[:, kh * hd:(kh + 1) * hd]
                v = v_ref[:, kh * hd:(kh + 1) * hd]
                for g in range(G):
                    hh = kh * G + g
                    lanes = slice(hh * hd, (hh + 1) * hd)
                    q = q_ref[:, lanes]
                    do = do_ref[:, lanes]
                    s = lax.dot_general(k, q, _NT, preferred_element_type=F32)
                    if has_bias:
                        s = s + bias_ref[_variant(i, nq, mx.E), kh if per_head else 0, dt]
                    p = jnp.exp2(s * SCALE_LOG2E - lse_ref[hh] * LOG2E)
                    dv_s[kh] += lax.dot_general(p.astype(BF16), do, _NN, preferred_element_type=F32)
                    dp = lax.dot_general(v, do, _NT, preferred_element_type=F32)
                    ds = p * (dp - dl_ref[hh])
                    if want_dbias:
                        db_ref[kh, dt] += ds
                    dsb = ds.astype(BF16)
                    dk_s[kh] += lax.dot_general(dsb, q, _NN, preferred_element_type=F32)
                    dq_ref[pl.ds(row0, tq), lanes] += lax.dot_general(dsb, k, _TN,
                                                                      preferred_element_type=F32) * SCALE

        if W is None:
            step(0)
        else:
            for dd in range(nd):
                i_dd = j + dd - W
                pl.when((i_dd >= 0) & (i_dd < nq))(lambda dd=dd: step(dd))

        @pl.when(d == ngd - 1)
        def _():
            for kh in range(hp):
                dk_ref[:, kh * hd:(kh + 1) * hd] = dk_s[kh] * SCALE
                dv_ref[:, kh * hd:(kh + 1) * hd] = dv_s[kh]

        if carry is not None:
            pl.when((kv == n_kv // hp - 1) & (j == nk - 1) & (d == ngd - 1))(lambda: carry.finish(*carry_refs))

    hp = mx.hp
    hq = hp * G
    assert qc0 % hq == 0 and hb0 % hq == 0 and kc0 % hp == 0 and vc0 % hp == 0 and n_kv % hp == 0
    per_head = has_bias and bias.shape[1] == 4

    def q_spec(shape, col, dd, stat):
        def index(kv, j, d):
            blk = d if W is None else jnp.clip(j + dd - W, 0, nq - 1)
            return (col + kv, 0, blk) if stat else (blk, col + kv)
        return pl.BlockSpec(shape, index)

    in_specs = [pl.BlockSpec((tk, hp * hd), lambda kv, j, d: (j, kc0 // hp + kv)),
                pl.BlockSpec((tk, hp * hd), lambda kv, j, d: (j, vc0 // hp + kv)),
                *[q_spec((tq, hq * hd), qc0 // hq, dd, False) for dd in range(nin)],
                *[q_spec((tq, hq * hd), hb0 // hq, dd, False) for dd in range(nin)],
                *[q_spec((hq, 1, tq), 0, dd, True) for dd in range(nin)],
                *[q_spec((hq, 1, tq), hb0 // hq, dd, True) for dd in range(nin)]]
    ins = [k_arr, v_arr, *[q_arr] * nin, *[do_all] * nin, *[lse] * nin, *[delta] * nin]
    if has_bias:
        in_specs.append(pl.BlockSpec((bias.shape[0], hp if per_head else 1, nd, tk, tq),
                                     lambda kv, j, d: (0, kv if per_head else 0, 0, 0, 0)))
        ins.append(bias)
    kv_blk = pl.BlockSpec((tk, hp * hd), lambda kv, j, d: (j, kv))
    out_specs = [pl.BlockSpec((S, hq * hd), lambda kv, j, d: (0, kv)), kv_blk, kv_blk]
    out_shape = [SDS((S, 4 * hd), F32), SDS((S, n_kv * hd), F32), SDS((S, n_kv * hd), F32)]
    if want_dbias:
        out_specs.append(pl.BlockSpec((hp, nd, tk, tq), lambda kv, j, d: (kv, 0, 0, 0)))
        out_shape.append(SDS((4, nd, tk, tq), F32))
    scratch = [pltpu.VMEM((hp, tk, hd), F32), pltpu.VMEM((hp, tk, hd), F32)]
    sem = ("parallel", "arbitrary", "arbitrary")
    if carry is not None:
        ins += carry.ins
        in_specs += carry.in_specs
        out_specs = carry.out_specs + out_specs
        out_shape = carry.out_shape + out_shape
        scratch += carry.scratch
        sem = ("arbitrary",) * 3
    res = _call(body, name=name, grid=(n_kv // hp, nk, ngd), in_specs=in_specs, out_specs=out_specs,
                out_shape=out_shape, scratch=scratch, sem=sem)(*ins)
    main = res[nc:]
    return (*main[:3], main[3] if want_dbias else None, list(res[:nc]))


def _attn_bwd_pre(name, dbr, o_all, u):
    S = dbr.shape[0]
    ts = min(S, ROW_TILE)
    hd = HEAD_DIM

    assert all(g % 4 == 0 for g in GCOL)

    def gcol(n):
        return GCOL[0] // 4 + n * 4 - jnp.where(n >= 2, 1, 0)

    def body(dbr_ref, o_ref, g_ref, do_ref, dg_ref, dl_ref):
        for hh in range(4):
            lanes = slice(hh * hd, (hh + 1) * hd)
            db = dbr_ref[:, lanes].astype(F32)
            o = o_ref[:, lanes].astype(F32)
            gv = g_ref[:, lanes].astype(F32)
            sg = _sigmoid(gv)
            do = db * (gv * sg)
            do_ref[:, lanes] = do.astype(BF16)
            dg_ref[:, lanes] = (db * o * (sg * (1.0 + gv * (1.0 - sg)))).astype(BF16)
            dl_ref[hh] = jnp.sum((do * o).T, axis=0, keepdims=True)

    own = pl.BlockSpec((ts, 4 * hd), lambda i, n: (i, n))
    return _call(body, name=name, grid=(S // ts, 4),
                 in_specs=[own, own, pl.BlockSpec((ts, 4 * hd), lambda i, n: (i, gcol(n)))],
                 out_specs=[own, own, pl.BlockSpec((4, 1, ts), lambda i, n: (n, 0, i))],
                 out_shape=[SDS((S, 16 * hd), BF16), SDS((S, 16 * hd), BF16), SDS((16, 1, S), F32)],
                 sem=("parallel", "parallel"))(dbr, o_all, u)


def _sink_grad(name, sink, lse, delta):
    S = lse.shape[2]
    ts = min(S, 2048)

    def body(sink_ref, lse_ref, dl_ref, out_ref):
        @pl.when(pl.program_id(1) == 0)
        def _():
            out_ref[...] = jnp.zeros_like(out_ref)

        sk = sink_ref[pl.program_id(0)]
        part = jnp.sum(jnp.exp(sk - lse_ref[0]) * dl_ref[0], axis=1, keepdims=True)
        out_ref[0] += -jnp.broadcast_to(part, (1, 128))

    col = pl.BlockSpec((1, 1, ts), lambda h, i: (h, 0, i))
    return _call(body, name=name, grid=(4, S // ts),
                 in_specs=[pl.BlockSpec(memory_space=pltpu.SMEM), col, col],
                 out_specs=pl.BlockSpec((1, 1, 128), lambda h, i: (h, 0, 0)),
                 out_shape=SDS((4, 1, 128), F32), sem=("parallel", "arbitrary"))(sink, lse, delta)


def _bias_maps(T, W):
    rpb = T // GRID_W
    nd = 2 * W + 1
    rmap = np.zeros((nd, rpb, rpb, 15), np.float32)
    for df in range(nd):
        for a in range(rpb):
            for b in range(rpb):
                r = (df - W) * rpb + b - a + 7
                if 0 <= r < 15:
                    rmap[df, a, b, r] = 1.0
    cmap = np.zeros((GRID_W, GRID_W, 31), np.float32)
    for q in range(GRID_W):
        for k in range(GRID_W):
            cmap[q, k, int(np.clip(k - q, -15, 15)) + 15] = 1.0
    return jnp.asarray(rmap), jnp.asarray(cmap)


def _bias_tiles(rel_bias, S, T):
    W = -(-7 // (T // GRID_W))
    rmap, cmap = _bias_maps(T, W)
    t = jnp.einsum("dabr,hrc,qkc->hdbkaq", rmap, rel_bias, cmap, precision=lax.Precision.HIGHEST)
    return t.reshape(1, 4, 2 * W + 1, T, T) * (1.0 / SCALE) + jnp.asarray(_mask_tiles_d(S, T))


def _bias_tiles_t(dtiles, T):
    rpb = T // GRID_W
    W = -(-7 // rpb)
    rmap, cmap = _bias_maps(T, W)
    t = dtiles.reshape(4, 2 * W + 1, rpb, GRID_W, rpb, GRID_W)
    return jnp.einsum("dabr,hdbkaq,qkc->hrc", rmap, t, cmap, precision=lax.Precision.HIGHEST)


def _mixer_cfg(S):
    ta, tb, td = (min(S, ATT_TILE[k]) for k in "abd")
    cq, ck = (min(S, t) for t in ATT_TILE["c"])
    cb = min(S, ATT_TILE["c_bwd"])
    a = _Mixer(ta, ta, _band(128, ta), 2)
    b = _Mixer(tb, tb, _band(1024, tb), 1, hp=2)
    d = _Mixer(td, td, -(-7 // (td // GRID_W)), 1, _edge_blocks_d(td), hp=2)
    return {"a": (a, a, jnp.asarray(_mask_tiles_a(ta))), "b": (b, b, jnp.asarray(_mask_tiles_b(tb))),
            "c": (_Mixer(cq, ck, None, 2), _Mixer(cb, cb, None, 2), None), "d": (d, d, None)}


def _layer_fwd(l, x, c8, lw, p, tabs, next_shards, me, late=None):
    S, D = x.shape
    dq = D // N_SHARD
    ada_sh, win_sh, wgm_sh, wb_sh, wout_sh = lw
    one_d, axial = tabs
    cfg = _mixer_cfg(S)
    tm = min(S, ROW_TILE)

    ada = _matmul(f"ada_l{l}", (N_SHARD,), [c8, ada_sh, p["b_ada"][l][None, :]],
                  [pl.BlockSpec((8, D), lambda j: (0, 0)), pl.BlockSpec((None, D, 3 * dq), lambda j: (j, 0, 0)),
                   pl.BlockSpec((1, 3 * dq), lambda j: (0, j))],
                  [SDS((8, 3 * D), F32)], [pl.BlockSpec((8, 3 * dq), lambda j: (0, j))],
                  epilogue=lambda acc, b_ref: (acc + b_ref[...],), a_fn=lambda a: a * _sigmoid(a))[0][0:1]
    shift, scale, gate = ada[:, :D], ada[:, D:2 * D], ada[:, 2 * D:]
    g_row = p["norm_g"][l][None, :]
    h = _norm_mod(f"norm_mod_l{l}", x, g_row, scale, shift)

    u = _matmul(f"mm_in_l{l}", (S // tm, N_SHARD), [h, win_sh],
                [pl.BlockSpec((tm, D), lambda i, j: (i, 0)), pl.BlockSpec((None, D, IN_SHARD), lambda i, j: (j, 0, 0))],
                [SDS((S, IN_COLS), BF16)], [pl.BlockSpec((tm, IN_SHARD), lambda i, j: (i, j))])[0]

    qa = _rope_fwd(f"rope_qa_l{l}", u, QCOL[0], 4, one_d, 64)
    ka = _rope_fwd(f"rope_ka_l{l}", u, KCOL[0], 2, one_d, 64)
    qb = _rope_fwd(f"rope_qb_l{l}", u, QCOL[1], 4, one_d, 64)
    kb = _rope_fwd(f"rope_kb_l{l}", u, KCOL[1], 4, one_d, 64)
    qc = _rope_fwd(f"rope_qc_l{l}", u, QCOL[2], 4, axial, 32, p["c_q_norm"][l][None, :])
    kc = _rope_fwd(f"rope_kc_l{l}", u, KCOL[2], 2, axial, 32, p["c_k_norm"][l][None, :])

    loads = [[], [], [], []]
    if next_shards is not None:
        s_ada, s_in, s_gm, s_wb, s_out = next_shards
        loads = [[s_wb, s_out], [s_in], [s_gm], [s_ada]]
    if late is not None:
        loads[0] += [late[1], late[2]]
        loads[2] += [late[0]]
    carries = [_Carry("gather", a) if a else None for a in loads]
    no_sink = jnp.full((4,), NEG_INF, F32)
    bias = _bias_tiles(p["d_rel_bias"][l], S, cfg["d"][0].tq)
    br_a, o_a, lse_a, got_a = _attn_fwd(f"attn_a_l{l}", cfg["a"][0], qa, 0, ka, 0, u, VCOL[0], u, GCOL[0],
                                        p["a_sink"][l], cfg["a"][2], carries[0])
    br_b, o_b, lse_b, got_b = _attn_fwd(f"attn_b_l{l}", cfg["b"][0], qb, 0, kb, 0, u, VCOL[1], u, GCOL[1], no_sink,
                                        cfg["b"][2], carries[1])
    br_c, o_c, lse_c, got_c = _attn_fwd(f"attn_c_l{l}", cfg["c"][0], qc, 0, kc, 0, u, VCOL[2], u, GCOL[2], no_sink,
                                        None, carries[2])
    br_d, o_d, lse_d, got_d = _attn_fwd(f"attn_d_l{l}", cfg["d"][0], u, QCOL[3], u, KCOL[3], u, VCOL[3], u, GCOL[3],
                                        no_sink, bias, carries[3])
    next_lw = None
    if next_shards is not None:
        next_lw = _own_slot([got_d[0], got_b[0], got_c[0], got_a[0], got_a[1]], next_shards, me)
    if late is not None:
        wgm_sh, wb_sh, wout_sh = _own_slot([got_c[-1], got_a[-2], got_a[-1]], late, me)
        lw = (ada_sh, win_sh, wgm_sh, wb_sh, wout_sh)
    br = jnp.concatenate([br_a, br_b, br_c, br_d], axis=1)
    o_all = jnp.concatenate([o_a, o_b, o_c, o_d], axis=1)

    def merge_body(h_ref, wg_ref, br_ref, wb_ref, mg_ref, pj_ref, z_ref, acc_ref):
        n = pl.program_id(2)
        mgv = _sigmoid(lax.dot_general(h_ref[...], wg_ref[...], _NN, preferred_element_type=F32))
        pj = lax.dot_general(br_ref[...], wb_ref[...], _NN, preferred_element_type=F32)
        mg_ref[...] = mgv.astype(BF16)
        pj_ref[...] = pj.astype(BF16)

        @pl.when(n == 0)
        def _():
            acc_ref[...] = mgv * pj

        @pl.when(n > 0)
        def _():
            acc_ref[...] += mgv * pj

        @pl.when(n == 3)
        def _():
            z_ref[...] = acc_ref[...].astype(BF16)

    wide = pl.BlockSpec((tm, dq), lambda i, j, n: (i, n * N_SHARD + j))
    mg, proj, z = _call(
        merge_body, name=f"merge_l{l}", grid=(S // tm, N_SHARD, 4),
        in_specs=[pl.BlockSpec((tm, D), lambda i, j, n: (i, 0)),
                  pl.BlockSpec((None, D, dq), lambda i, j, n: (n, 0, j)),
                  pl.BlockSpec((tm, BRANCH_W), lambda i, j, n: (i, n)),
                  pl.BlockSpec((None, None, BRANCH_W, dq), lambda i, j, n: (j, n, 0, 0))],
        out_specs=[wide, wide, pl.BlockSpec((tm, dq), lambda i, j, n: (i, j))],
        out_shape=[SDS((S, 4 * D), BF16), SDS((S, 4 * D), BF16), SDS((S, D), BF16)],
        scratch=[pltpu.VMEM((tm, dq), F32)], sem=("parallel", "parallel", "arbitrary"))(h, wgm_sh, br, wb_sh)

    tn = min(D, 1024)
    x_new, o2 = _matmul(
        f"mm_out_l{l}", (S // tm, D // tn), [z, wout_sh, x, gate],
        [pl.BlockSpec((tm, D), lambda i, j: (i, 0)), pl.BlockSpec((N_SHARD, dq, tn), lambda i, j: (0, 0, j)),
         pl.BlockSpec((tm, tn), lambda i, j: (i, j)), pl.BlockSpec((1, tn), lambda i, j: (0, j))],
        [SDS((S, D), F32), SDS((S, D), BF16)],
        [pl.BlockSpec((tm, tn), lambda i, j: (i, j)), pl.BlockSpec((tm, tn), lambda i, j: (i, j))],
        k_inner=N_SHARD, epilogue=lambda acc, x_ref, g_ref: (x_ref[...] + g_ref[...] * acc, acc))
    res = dict(x=x, h=h, u=u, qa=qa, ka=ka, qb=qb, kb=kb, qc=qc, kc=kc, br=br, o_all=o_all,
               lse=(lse_a, lse_b, lse_c, lse_d), bias=bias, mg=mg, proj=proj, z=z, o2=o2,
               g_row=g_row, scale=scale, gate=gate)
    return x_new, res, (ada_sh, win_sh, wgm_sh, wb_sh, wout_sh), next_lw


def _layer_bwd(l, dxo, r, lw, p, tabs, pending):
    x, h, u = r["x"], r["h"], r["u"]
    S, D = x.shape
    dq = D // N_SHARD
    ada_sh, win_sh, wgm_sh, wb_sh, wout_sh = lw
    one_d, axial = tabs
    cfg = _mixer_cfg(S)
    tm = min(S, ROW_TILE)
    tk = min(S, 1024)
    tn = min(D, 1024)

    do2, dgate = _out_bwd_ew(f"out_bwd_l{l}", dxo, r["o2"], r["gate"])
    dz = _matmul(f"mm_dz_l{l}", (S // tm, N_SHARD), [do2, wout_sh],
                 [pl.BlockSpec((tm, D), lambda i, n: (i, 0)), pl.BlockSpec((None, dq, D), lambda i, n: (n, 0, 0))],
                 [SDS((S, D), BF16)], [pl.BlockSpec((tm, dq), lambda i, n: (i, n))], tb=True)[0]
    tkw = min(S, 2048)
    g_out = _matmul(f"mm_gwout_l{l}", (N_SHARD, D // tn, S // tkw), [r["z"], do2],
                    [pl.BlockSpec((tkw, dq), lambda n, j, k: (k, n)), pl.BlockSpec((tkw, tn), lambda n, j, k: (k, j))],
                    [SDS((N_SHARD, dq, D), F32)], [pl.BlockSpec((None, dq, tn), lambda n, j, k: (n, 0, j))],
                    ta=True, k_axis=2, acc_shape=(dq, tn))[0]

    dmg, dproj = _merge_bwd_ew(f"merge_bwd_l{l}", dz, r["mg"], r["proj"])
    nj = D // tn
    g_gm = _matmul(f"mm_gwgm_l{l}", (4, D // tn, nj, S // tk), [h, dmg],
                   [pl.BlockSpec((tk, tn), lambda n, i, j, k: (k, i)),
                    pl.BlockSpec((tk, tn), lambda n, i, j, k: (k, n * nj + j))],
                   [SDS((4, D, D), F32)], [pl.BlockSpec((None, tn, tn), lambda n, i, j, k: (n, i, j))],
                   ta=True, k_axis=3, acc_shape=(tn, tn))[0]
    dh1 = _matmul(f"mm_dh1_l{l}", (S // tm, nj, 4), [dmg, wgm_sh],
                  [pl.BlockSpec((tm, D), lambda i, j, n: (i, n)),
                   pl.BlockSpec((None, tn, D), lambda i, j, n: (n, j, 0))],
                  [SDS((S, D), F32)], [pl.BlockSpec((tm, tn), lambda i, j, kk: (i, j))],
                  tb=True, k_axis=2, acc_shape=(tm, tn))[0]
    dbr = _matmul(f"mm_dbr_l{l}", (S // tm, 4), [dproj, wb_sh],
                  [pl.BlockSpec((tm, D), lambda i, n: (i, n)),
                   pl.BlockSpec((N_SHARD, None, BRANCH_W, dq), lambda i, n: (0, n, 0, 0))],
                  [SDS((S, 4 * BRANCH_W), BF16)], [pl.BlockSpec((tm, BRANCH_W), lambda i, n: (i, n))],
                  tb=True, k_inner=N_SHARD)[0]
    tkl = min(S, 2048)
    g_wb = _matmul(f"mm_gwb_l{l}", (N_SHARD, 4, S // tkl), [r["br"], dproj],
                   [pl.BlockSpec((tkl, BRANCH_W), lambda j, n, k: (k, n)),
                    pl.BlockSpec((tkl, dq), lambda j, n, k: (k, n * N_SHARD + j))],
                   [SDS((N_SHARD, 4, BRANCH_W, dq), F32)],
                   [pl.BlockSpec((None, None, BRANCH_W, dq), lambda j, n, k: (j, n, 0, 0))],
                   ta=True, k_axis=2, acc_shape=(BRANCH_W, dq))[0]

    do_all, dg_all, delta = _attn_bwd_pre(f"attn_pre_l{l}", dbr, r["o_all"], u)
    lse_a, lse_b, lse_c, lse_d = r["lse"]
    dsink = _sink_grad(f"sink_grad_l{l}", p["a_sink"][l], lse_a, delta)[:, 0, 0]
    carries = [None] * 3
    if pending is not None:
        p_in, p_gm, p_wb, p_out = pending
        carries = [_Carry("exchange", [p_in]), _Carry("exchange", [p_gm]), _Carry("exchange", [p_wb, p_out])]
    dqa, dka, dva, _, _ = _attn_bwd(f"attn_a_bwd_l{l}", cfg["a"][1], r["qa"], 0, r["ka"], 0, u, VCOL[0], do_all, 0,
                                    lse_a, delta, cfg["a"][2])
    dqb, dkb, dvb, _, got_b = _attn_bwd(f"attn_b_bwd_l{l}", cfg["b"][1], r["qb"], 0, r["kb"], 0, u, VCOL[1], do_all, 4,
                                        lse_b, delta, cfg["b"][2], carry=carries[0])
    dqc, dkc, dvc, _, got_c = _attn_bwd(f"attn_c_bwd_l{l}", cfg["c"][1], r["qc"], 0, r["kc"], 0, u, VCOL[2], do_all, 8,
                                        lse_c, delta, carry=carries[1])
    dqd, dkd, dvd, dbias, got_d = _attn_bwd(f"attn_d_bwd_l{l}", cfg["d"][1], u, QCOL[3], u, KCOL[3], u, VCOL[3],
                                            do_all, 12, lse_d, delta, r["bias"], True, carries[2])
    arrived = None if pending is None else [got_b[0], got_c[0], got_d[0], got_d[1]]
    d_rel = _bias_tiles_t(dbias, cfg["d"][1].tq)

    duqa, _ = _rope_bwd(f"rope_qa_bwd_l{l}", dqa, u, QCOL[0], 4, one_d, 64)
    duka, _ = _rope_bwd(f"rope_ka_bwd_l{l}", dka, u, KCOL[0], 2, one_d, 64)
    duqb, _ = _rope_bwd(f"rope_qb_bwd_l{l}", dqb, u, QCOL[1], 4, one_d, 64)
    dukb, _ = _rope_bwd(f"rope_kb_bwd_l{l}", dkb, u, KCOL[1], 4, one_d, 64)
    duqc, dcq = _rope_bwd(f"rope_qc_bwd_l{l}", dqc, u, QCOL[2], 4, axial, 32, p["c_q_norm"][l][None, :])
    dukc, dck = _rope_bwd(f"rope_kc_bwd_l{l}", dkc, u, KCOL[2], 2, axial, 32, p["c_k_norm"][l][None, :])
    bw = BRANCH_W
    du = jnp.concatenate(
        [duqa, duka, dva.astype(BF16), dg_all[:, 0:bw],
         duqb, dukb, dvb.astype(BF16), dg_all[:, bw:2 * bw],
         duqc, dukc, dvc.astype(BF16), dg_all[:, 2 * bw:3 * bw],
         dqd.astype(BF16), dkd.astype(BF16), dvd.astype(BF16), dg_all[:, 3 * bw:]], axis=1)

    tmi = min(D, 1024)
    g_in = _matmul(f"mm_gwin_l{l}", (N_SHARD, D // tmi, S // tk), [h, du],
                   [pl.BlockSpec((tk, tmi), lambda j, i, k: (k, i)), pl.BlockSpec((tk, IN_SHARD), lambda j, i, k: (k, j))],
                   [SDS((N_SHARD, D, IN_SHARD), F32)], [pl.BlockSpec((None, tmi, IN_SHARD), lambda j, i, k: (j, i, 0))],
                   ta=True, k_axis=2, acc_shape=(tmi, IN_SHARD))[0]
    dh2 = _matmul(f"mm_dh2_l{l}", (S // tm, nj, N_SHARD), [du, win_sh],
                  [pl.BlockSpec((tm, IN_SHARD), lambda i, j, k: (i, k)),
                   pl.BlockSpec((None, tn, IN_SHARD), lambda i, j, k: (k, j, 0))],
                  [SDS((S, D), F32)], [pl.BlockSpec((tm, tn), lambda i, j, k: (i, j))],
                  tb=True, k_axis=2, acc_shape=(tm, tn))[0]

    dx_prev, dshift, dscale, dng = _norm_mod_bwd(f"norm_mod_bwd_l{l}", x, dh1, dh2, dxo, r["g_row"], r["scale"])
    d_ada = jnp.concatenate([dshift, dscale, dgate], axis=1)[0]
    big = (g_in, g_gm, g_wb, g_out)
    small = dict(norm_g=dng[0], b_ada=d_ada, a_sink=dsink, c_q_norm=dcq[0], c_k_norm=dck[0], d_rel_bias=d_rel)
    return dx_prev, big, small, arrived


def _place():
    return lax.axis_index("x"), lax.axis_index("y"), lax.axis_index("c")


class _Carry:
    def __init__(self, kind, arrays):
        self.kind, self.n, self.ins = kind, len(arrays), list(arrays)
        any_spec = pl.BlockSpec(memory_space=pl.ANY)
        self.in_specs = [any_spec] * self.n
        self.out_specs = [any_spec] * self.n
        if kind == "gather":
            self.out_shape = [SDS((N_SHARD, *a.shape), a.dtype) for a in arrays]
        else:
            self.out_shape = [SDS((3, *a.shape[1:]), a.dtype) for a in arrays]
        self.scratch = [pltpu.SemaphoreType.DMA((self.n, 3)), pltpu.SemaphoreType.DMA((self.n, 3))]

    def _copies(self, ins, outs, send_sems, recv_sems, arriving):
        x, y, c = _place()
        cps = []
        for a in range(self.n):
            for k, (px, py) in enumerate([(1 - x, y), (x, 1 - y), (1 - x, 1 - y)]):
                if self.kind == "gather":
                    src, dst = ins[a], outs[a].at[2 * px + py if arriving else 2 * x + y]
                else:
                    src, dst = ins[a].at[2 * px + py], outs[a].at[k]
                cps.append(pltpu.make_async_remote_copy(src, dst, send_sems.at[a, k], recv_sems.at[a, k],
                                                        device_id=(px, py, c), device_id_type=MESH))
        return cps

    def start(self, ins, outs, send_sems, recv_sems):
        for cp in self._copies(ins, outs, send_sems, recv_sems, False):
            cp.start()

    def finish(self, ins, outs, send_sems, recv_sems):
        for cp in self._copies(ins, outs, send_sems, recv_sems, True):
            cp.wait_recv()
        for cp in self._copies(ins, outs, send_sems, recv_sems, False):
            cp.wait_send()


def _run_carry(name, carry):
    n = carry.n

    def body(*refs):
        args = (refs[:n], refs[n:2 * n], refs[2 * n], refs[2 * n + 1])
        carry.start(*args)
        carry.finish(*args)

    return pl.pallas_call(body, name=name, in_specs=carry.in_specs, out_specs=carry.out_specs,
                          out_shape=carry.out_shape, scratch_shapes=carry.scratch)(*carry.ins)


def _own_slot(gathered, shards, me):
    return [lax.dynamic_update_index_in_dim(g, s, me, 0) for g, s in zip(gathered, shards)]


def _gather_small(name, v):
    m_per, n = v.shape

    def body(x_ref, out_ref, send_sems, recv_sems, local_sem):
        x, y, c = _place()
        me, sibling = (x, y, c), (x, y, 1 - c)
        chips = [(1 - x, y), (x, 1 - y), (1 - x, 1 - y)]

        def rows(px, py, pc):
            return out_ref.at[pl.ds((4 * px + 2 * py + pc) * m_per, m_per), :]

        def copy(k, block, to, src=None):
            return pltpu.make_async_remote_copy(
                src_ref=rows(*block) if src is None else src, dst_ref=rows(*block),
                send_sem=send_sems.at[k], recv_sem=recv_sems.at[k], device_id=to, device_id_type=MESH)

        mine = pltpu.make_async_copy(x_ref, rows(*me), local_sem)
        mine.start()
        first = [copy(0, me, sibling, src=x_ref)]
        first += [copy(1 + j, me, (*chip, c), src=x_ref) for j, chip in enumerate(chips)]
        for cp in first:
            cp.start()
        passed = [copy(4 + j, (*chip, c), sibling) for j, chip in enumerate(chips)]
        for j, chip in enumerate(chips):
            copy(1 + j, (*chip, c), me).wait_recv()
            passed[j].start()
        copy(0, sibling, me).wait_recv()
        for j, chip in enumerate(chips):
            copy(4 + j, (*chip, 1 - c), me).wait_recv()
        for cp in first + passed:
            cp.wait_send()
        mine.wait()

    return pl.pallas_call(
        body, name=name, out_shape=SDS((8 * m_per, n), v.dtype),
        in_specs=[pl.BlockSpec(memory_space=pltpu.VMEM)], out_specs=pl.BlockSpec(memory_space=pltpu.VMEM),
        scratch_shapes=[pltpu.SemaphoreType.DMA((7,)), pltpu.SemaphoreType.DMA((7,)), pltpu.SemaphoreType.DMA])(v)


def _pair_send_half(name, grads):
    n = len(grads)

    def body(*refs):
        ins, outs = refs[:n], refs[n:2 * n]
        send_sems, recv_sems = refs[2 * n:]
        x, y, c = _place()
        cps = []
        for a in range(n):
            cp = pltpu.make_async_remote_copy(ins[a].at[:, 1 - c], outs[a], send_sems.at[a], recv_sems.at[a],
                                              device_id=(x, y, 1 - c), device_id_type=MESH)
            cp.start()
            cps.append(cp)
        for cp in cps:
            cp.wait_recv()
        for cp in cps:
            cp.wait_send()

    any_spec = pl.BlockSpec(memory_space=pl.ANY)
    return pl.pallas_call(
        body, name=name, in_specs=[any_spec] * n, out_specs=[any_spec] * n,
        out_shape=[SDS((g.shape[0], *g.shape[2:]), g.dtype) for g in grads],
        scratch_shapes=[pltpu.SemaphoreType.DMA((n,)), pltpu.SemaphoreType.DMA((n,))])(*grads)


def _pair_gather(name, halves):
    n = len(halves)

    def body(*refs):
        outs = refs[n:2 * n]
        send_sems, recv_sems = refs[2 * n:]
        x, y, c = _place()
        cps = [pltpu.make_async_remote_copy(outs[a].at[c], outs[a].at[c], send_sems.at[a], recv_sems.at[a],
                                            device_id=(x, y, 1 - c), device_id_type=MESH) for a in range(n)]
        for cp in cps:
            cp.start()
        for a in range(n):
            pltpu.make_async_remote_copy(outs[a].at[c], outs[a].at[1 - c], send_sems.at[a], recv_sems.at[a],
                                         device_id=(x, y, 1 - c), device_id_type=MESH).wait_recv()
        for cp in cps:
            cp.wait_send()

    any_spec = pl.BlockSpec(memory_space=pl.ANY)
    return pl.pallas_call(
        body, name=name, in_specs=[any_spec] * n, out_specs=[any_spec] * n,
        out_shape=[SDS(g.shape, g.dtype) for g in halves], input_output_aliases={a: a for a in range(n)},
        scratch_shapes=[pltpu.SemaphoreType.DMA((n,)), pltpu.SemaphoreType.DMA((n,))])(*halves)


def _add_half(name, g, recv, c_idx):
    _, _, R, C = g.shape
    tr = min(R, 256)

    def body(c_ref, g_ref, r_ref, o_ref):
        o_ref[...] = (g_ref[...] + r_ref[...]).astype(BF16)

    return _call(body, name=name, grid=(4, R // tr), nsp=1,
                 in_specs=[pl.BlockSpec((None, None, tr, C), lambda j, r, c_ref: (j, c_ref[0], r, 0)),
                           pl.BlockSpec((None, tr, C), lambda j, r, c_ref: (j, r, 0))],
                 out_specs=pl.BlockSpec((None, tr, C), lambda j, r, c_ref: (j, r, 0)),
                 out_shape=SDS((4, R, C), BF16), sem=("parallel", "parallel"))(c_idx, g, recv)


def _add_shards(name, part, recv, idx):
    _, R, C = part.shape
    tr = min(R, 256)

    def body(idx_ref, p_ref, r_ref, o_ref):
        o_ref[...] = (((p_ref[...].astype(F32) + r_ref[0].astype(F32)) + r_ref[1].astype(F32))
                      + r_ref[2].astype(F32))

    return _call(body, name=name, grid=(R // tr,), nsp=1,
                 in_specs=[pl.BlockSpec((None, tr, C), lambda r, idx_ref: (idx_ref[0], r, 0)),
                           pl.BlockSpec((3, tr, C), lambda r, idx_ref: (0, r, 0))],
                 out_specs=pl.BlockSpec((None, tr, C), lambda r, idx_ref: (idx_ref[1], r, 0)),
                 out_shape=SDS((2, R, C), F32), sem=("parallel",))(idx, part, recv)


def _pair_sum_layer(l, big, c_idx):
    views = []
    for g in big:
        rows = g.shape[-2] if g.ndim == 3 else g.shape[1] * g.shape[2]
        views.append(g.reshape(N_SHARD, 2, rows // 2, g.shape[-1]))
    recv1 = _pair_send_half(f"rs_pair_send_l{l}", views)
    return [_add_half(f"rs_add_half{a}_l{l}", v, r1, c_idx) for a, (v, r1) in enumerate(zip(views, recv1))]


def _finish_reduce_layer(l, big, parts, recv2, idx):
    halves = [_add_shards(f"rs_add_shards{a}_l{l}", pt, r2, idx) for a, (pt, r2) in enumerate(zip(parts, recv2))]
    full = _pair_gather(f"rs_pair_gather_l{l}", halves)
    return [f.reshape(g.shape[1:]) for f, g in zip(full, big)]


def _adamw_math(w, g, m, v):
    m = ADAM_B1 * m + (1.0 - ADAM_B1) * g
    v = ADAM_B2 * v + (1.0 - ADAM_B2) * (g * g)
    m_hat = m / (1.0 - ADAM_B1 ** ADAM_STEP)
    v_hat = v / (1.0 - ADAM_B2 ** ADAM_STEP)
    delta = -ADAM_LR * (m_hat / (jnp.sqrt(v_hat) + ADAM_EPS) + ADAM_WD * w)
    return delta, m, v


def _adamw(name, w, g, m, v):
    shape = w.shape
    C = shape[-1]
    R = int(np.prod(shape[:-1]))
    tr = min(R, 256)

    def body(w_ref, g_ref, m_ref, v_ref, d_ref, nm_ref, nv_ref):
        d, nm, nv = _adamw_math(w_ref[...], g_ref[...], m_ref[...], v_ref[...])
        d_ref[...] = d
        nm_ref[...] = nm
        nv_ref[...] = nv

    blk = pl.BlockSpec((tr, C), lambda i: (i, 0))
    outs = _call(body, name=name, grid=(R // tr,), in_specs=[blk] * 4, out_specs=[blk] * 3,
                 out_shape=[SDS((R, C), F32)] * 3, sem=("parallel",))(*(a.reshape(R, C) for a in (w, g, m, v)))
    return [o.reshape(shape) for o in outs]


def _adamw_small(name, w, g8, m, v):
    R = w.shape[0]

    def body(w_ref, g_ref, m_ref, v_ref, go_ref, d_ref, nm_ref, nv_ref):
        g = g_ref[0]
        for b in range(1, 8):
            g = g + g_ref[b]
        d, nm, nv = _adamw_math(w_ref[...], g, m_ref[...], v_ref[...])
        go_ref[...] = g
        d_ref[...] = d
        nm_ref[...] = nm
        nv_ref[...] = nv

    blk = pl.BlockSpec((R, 128), lambda i: (0, 0))
    return _call(body, name=name, grid=(1,), in_specs=[blk, pl.BlockSpec((8, R, 128), lambda i: (0, 0, 0)), blk, blk],
                 out_specs=[blk] * 4, out_shape=[SDS((R, 128), F32)] * 4, sem=("arbitrary",))(w, g8, m, v)


SMALL_NAMES = ("norm_g", "b_ada", "a_sink", "c_q_norm", "c_k_norm", "d_rel_bias", "final_g")


def _pack(parts, extra_rows=0):
    flat = jnp.concatenate([a.reshape(-1) for a in parts])
    rows = -(-flat.shape[0] // 128)
    rows = -(-rows // 8) * 8 + extra_rows
    return jnp.pad(flat, (0, rows * 128 - flat.shape[0])).reshape(rows, 128)


def _unpack(packed, like):
    flat = packed.reshape(-1)
    out, off = [], 0
    for a in like:
        out.append(flat[off:off + a.size].reshape(a.shape))
        off += a.size
    return out


def _device_step(x, c8, tgt, shards, p, me, c_idx):
    S = x.shape[0]
    L = len(shards)
    tabs = _rope_tables(S)
    first = _Carry("gather", shards[0][:2])
    lw = [(*_own_slot(_run_carry("gather_w_l0", first), shards[0][:2], me), None, None, None)]
    res = []
    for l in range(L):
        x, r, lw[l], nxt = _layer_fwd(l, x, c8, lw[l], p, tabs, shards[l + 1] if l + 1 < L else None, me,
                                      shards[0][2:] if l == 0 else None)
        res.append(r)
        lw.append(nxt)
    dx, dfg, loss = _final_loss("final_loss", x, tgt, p["final_g"][None, :])
    bigs, smalls, parts, arrived = [None] * L, [None] * L, [None] * L, [None] * L
    for l in reversed(range(L)):
        pending = parts[l + 1] if l + 1 < L else None
        dx, bigs[l], smalls[l], arr = _layer_bwd(l, dx, res[l], lw[l], p, tabs, pending)
        if pending is not None:
            arrived[l + 1] = arr
        parts[l] = _pair_sum_layer(l, bigs[l], c_idx)
    arrived[0] = _run_carry("rs_shard_exchange_l0", _Carry("exchange", parts[0]))
    idx = jnp.concatenate([jnp.reshape(me, (1,)).astype(I32), c_idx])
    reduced = [_finish_reduce_layer(l, bigs[l], parts[l], arrived[l], idx) for l in range(L)]
    return loss, dx, reduced, smalls, dfg[0]


def kernel(x, c, norm_g, w_ada, b_ada, w_in, a_sink, c_q_norm, c_k_norm, d_rel_bias, w_gate_merge, w_branch, w_out, final_g, loss_target, m_norm_g, m_w_ada, m_b_ada, m_w_in, m_a_sink, m_c_q_norm, m_c_k_norm, m_d_rel_bias, m_w_gate_merge, m_w_branch, m_w_out, m_final_g, v_norm_g, v_w_ada, v_b_ada, v_w_in, v_a_sink, v_c_q_norm, v_c_k_norm, v_d_rel_bias, v_w_gate_merge, v_w_branch, v_w_out, v_final_g):
    L, D = norm_g.shape
    dq = D // N_SHARD
    p = dict(norm_g=norm_g, b_ada=b_ada, a_sink=a_sink, c_q_norm=c_q_norm, c_k_norm=c_k_norm,
             d_rel_bias=d_rel_bias, final_g=final_g)
    xi, yi, ci = _place()
    c_idx = jnp.reshape(ci, (1,)).astype(I32)
    me = 2 * xi + yi

    shards = [[w_ada[l].astype(BF16), w_in[l].astype(BF16), w_gate_merge[l].astype(BF16),
               w_branch[l].astype(BF16), w_out[l].astype(BF16)] for l in range(L)]
    c8 = jnp.broadcast_to(c, (8, D))
    loss, grad_x, reduced, smalls, dfg = _device_step(x[0], c8, loss_target[0], shards, p, me, c_idx)
    loss = lax.psum(loss[0, 0], ("x", "y", "c"))

    small_parts = [jnp.stack([s[n] for s in smalls]) for n in SMALL_NAMES[:-1]] + [dfg]
    packed = _pack(small_parts + [c[0]])
    rows = packed.shape[0]
    g8 = _gather_small("gather_small", packed).reshape(8, rows, 128)
    small_w = [p[n] for n in SMALL_NAMES]
    small_m = [m_norm_g, m_b_ada, m_a_sink, m_c_q_norm, m_c_k_norm, m_d_rel_bias, m_final_g]
    small_v = [v_norm_g, v_b_ada, v_a_sink, v_c_q_norm, v_c_k_norm, v_d_rel_bias, v_final_g]
    pad_c = [jnp.zeros((D,), F32)]
    sg, sd, sm, sv = _adamw_small("adamw_small", _pack(small_w + pad_c), g8, _pack(small_m + pad_c),
                                  _pack(small_v + pad_c))
    sg, sd, sm, sv = (_unpack(a, small_w) for a in (sg, sd, sm, sv))

    n_small = sum(a.size for a in small_parts)
    flat8 = g8.reshape(8, rows * 128)
    c_all = flat8[:, n_small:n_small + D]
    dada_all = flat8[:, L * D:L * D + L * 3 * D].reshape(8, L, 3 * D)
    dada_mine = lax.dynamic_slice_in_dim(dada_all, (2 * xi + yi) * (3 * dq), 3 * dq, axis=2)
    tma = min(D, 1024)
    g_ada = jnp.stack([
        _matmul(f"mm_gwada_l{l}", (D // tma,), [c_all, dada_mine[:, l]],
                [pl.BlockSpec((8, tma), lambda i: (0, i)), pl.BlockSpec((8, 3 * dq), lambda i: (0, 0))],
                [SDS((D, 3 * dq), F32)], [pl.BlockSpec((tma, 3 * dq), lambda i: (i, 0))],
                ta=True, a_fn=lambda a: a * _sigmoid(a))[0] for l in range(L)])

    g_in, g_gm, g_wb, g_out = (jnp.stack([reduced[l][a] for l in range(L)]) for a in range(4))

    big = {}
    for nm, w, g, m, v in (("w_ada", w_ada, g_ada, m_w_ada, v_w_ada), ("w_in", w_in, g_in, m_w_in, v_w_in),
                           ("w_gate_merge", w_gate_merge, g_gm, m_w_gate_merge, v_w_gate_merge),
                           ("w_branch", w_branch, g_wb, m_w_branch, v_w_branch),
                           ("w_out", w_out, g_out, m_w_out, v_w_out)):
        big[nm] = (g, *_adamw(f"adamw_{nm}", w, g, m, v))

    order = ("norm_g", "w_ada", "b_ada", "w_in", "a_sink", "c_q_norm", "c_k_norm", "d_rel_bias",
             "w_gate_merge", "w_branch", "w_out", "final_g")
    cols = [[], [], [], []]
    for nm in order:
        if nm in big:
            vals = big[nm]
        else:
            k = SMALL_NAMES.index(nm)
            vals = (sg[k], sd[k], sm[k], sv[k])
        for col, val in zip(cols, vals):
            col.append(val)
    return (loss, grad_x[None], *cols[0], *cols[1], *cols[2], *cols[3])
```

```python
import numpy as np
import jax
import jax.numpy as jnp
from jax import lax
from jax.experimental import pallas as pl
from jax.experimental.pallas import tpu as pltpu

F32 = jnp.float32
BF16 = jnp.bfloat16
I32 = jnp.int32
SDS = jax.ShapeDtypeStruct
MESH = pl.DeviceIdType.MESH

HEAD_DIM = 128
GRID_W = 64
EPS = 1e-6
NEG_INF = -1e30
ROPE_THETA = 10000.0
SCALE = HEAD_DIM ** -0.5
LOG2E = 1.4426950408889634
SCALE_LOG2E = SCALE * LOG2E
N_SHARD = 4
BRANCH_W = 512
IN_COLS = 7168
IN_SHARD = IN_COLS // N_SHARD
QCOL = (0, 12, 28, 40)
KCOL = (4, 16, 32, 44)
VCOL = (6, 20, 34, 48)
GCOL = (8, 24, 36, 52)
KV_HEADS = (2, 4, 2, 4)

ADAM_LR = 0.001
ADAM_B1 = 0.9
ADAM_B2 = 0.999
ADAM_EPS = 1e-08
ADAM_WD = 0.01
ADAM_STEP = 10

V7X_VMEM_BYTES = 64 * 1024 * 1024
VMEM_LIMIT = V7X_VMEM_BYTES * 7 // 8

ATT_TILE = {"a": 256, "b": 512, "c": (512, 1024), "c_bwd": 512, "d": 256}
M_INIT = -1e20
MM_ROWS = 1024
ROW_TILE = 512
EW_ROWS = 256


def _band(reach, tile):
    return -(-reach // tile)


def _call(body, *, name, grid, in_specs, out_specs, out_shape, scratch=(), sem=None, nsp=0):
    params = pltpu.CompilerParams(dimension_semantics=sem, vmem_limit_bytes=VMEM_LIMIT)
    if nsp:
        gs = pltpu.PrefetchScalarGridSpec(num_scalar_prefetch=nsp, grid=grid, in_specs=in_specs,
                                          out_specs=out_specs, scratch_shapes=list(scratch))
        return pl.pallas_call(body, grid_spec=gs, out_shape=out_shape, name=name, compiler_params=params)
    return pl.pallas_call(body, grid=grid, in_specs=in_specs, out_specs=out_specs, out_shape=out_shape,
                          scratch_shapes=list(scratch), name=name, compiler_params=params)


def _sigmoid(x):
    return 1.0 / (1.0 + jnp.exp(-x))


def _matmul(name, grid, ins, in_specs, out_shape, out_specs, *, ta=False, tb=False, k_axis=None,
            acc_shape=None, epilogue=None, a_fn=None, k_inner=None):
    n_in = len(ins)
    n_out = len(out_shape)
    nk = grid[k_axis] if k_axis is not None else 1
    dn = (((0 if ta else 1,), (1 if tb else 0,)), ((), ()))

    def body(*refs):
        a = refs[0][...]
        if a_fn is not None:
            a = a_fn(a)
        a = a.astype(BF16)
        if k_inner is None:
            p = lax.dot_general(a, refs[1][...].astype(BF16), dn, preferred_element_type=F32)
        else:
            ck = a.shape[1] // k_inner
            p = None
            for kk in range(k_inner):
                t = lax.dot_general(a[:, kk * ck:(kk + 1) * ck], refs[1][kk].astype(BF16), dn,
                                    preferred_element_type=F32)
                p = t if p is None else p + t
        extra = refs[2:n_in]
        outs = refs[n_in:n_in + n_out]

        def fin(acc):
            vals = epilogue(acc, *extra) if epilogue is not None else (acc,)
            for o_ref, v in zip(outs, vals):
                o_ref[...] = v.astype(o_ref.dtype)

        if k_axis is None:
            fin(p)
        else:
            acc_ref = refs[-1]
            k = pl.program_id(k_axis)

            @pl.when(k == 0)
            def _():
                acc_ref[...] = p

            @pl.when(k > 0)
            def _():
                acc_ref[...] += p

            @pl.when(k == nk - 1)
            def _():
                fin(acc_ref[...])

    sem = tuple("arbitrary" if ax == k_axis else "parallel" for ax in range(len(grid)))
    scratch = [pltpu.VMEM(acc_shape, F32)] if k_axis is not None else []
    return _call(body, name=name, grid=grid, in_specs=in_specs, out_specs=out_specs, out_shape=out_shape,
                 scratch=scratch, sem=sem)(*ins)


def _norm_mod(name, x, g, scale, shift):
    S, D = x.shape
    ts = min(S, EW_ROWS)

    def body(x_ref, g_ref, sc_ref, sh_ref, h_ref):
        xv = x_ref[...]
        r = lax.rsqrt(jnp.mean(xv * xv, axis=-1, keepdims=True) + EPS)
        h_ref[...] = (((xv * r) * g_ref[...]) * (1.0 + sc_ref[...]) + sh_ref[...]).astype(BF16)

    row = pl.BlockSpec((1, D), lambda i: (0, 0))
    blk = pl.BlockSpec((ts, D), lambda i: (i, 0))
    return _call(body, name=name, grid=(S // ts,), in_specs=[blk, row, row, row], out_specs=blk,
                 out_shape=SDS((S, D), BF16), sem=("parallel",))(x, g, scale, shift)


def _norm_mod_bwd(name, x, dh1, dh2, dxo, g, scale):
    S, D = x.shape
    ts = min(S, EW_ROWS)

    def body(x_ref, a_ref, b_ref, dxo_ref, g_ref, sc_ref, dx_ref, dsh_ref, dsc_ref, dg_ref):
        @pl.when(pl.program_id(0) == 0)
        def _():
            dsh_ref[...] = jnp.zeros_like(dsh_ref)
            dsc_ref[...] = jnp.zeros_like(dsc_ref)
            dg_ref[...] = jnp.zeros_like(dg_ref)

        xv = x_ref[...]
        r = lax.rsqrt(jnp.mean(xv * xv, axis=-1, keepdims=True) + EPS)
        xh = xv * r
        dh = a_ref[...] + b_ref[...]
        gv = g_ref[...]
        one_sc = 1.0 + sc_ref[...]
        dsh_ref[...] += jnp.sum(dh, axis=0, keepdims=True)
        dsc_ref[...] += jnp.sum(dh * xh * gv, axis=0, keepdims=True)
        dg_ref[...] += jnp.sum(dh * xh * one_sc, axis=0, keepdims=True)
        dxh = dh * gv * one_sc
        dx = r * (dxh - xh * jnp.mean(dxh * xh, axis=-1, keepdims=True))
        dx_ref[...] = dxo_ref[...] + dx

    row = pl.BlockSpec((1, D), lambda i: (0, 0))
    blk = pl.BlockSpec((ts, D), lambda i: (i, 0))
    return _call(body, name=name, grid=(S // ts,), in_specs=[blk, blk, blk, blk, row, row],
                 out_specs=[blk, row, row, row],
                 out_shape=[SDS((S, D), F32), SDS((1, D), F32), SDS((1, D), F32), SDS((1, D), F32)],
                 sem=("arbitrary",))(x, dh1, dh2, dxo, g, scale)


def _out_bwd_ew(name, dxo, o2, gate):
    S, D = dxo.shape
    ts = min(S, EW_ROWS)

    def body(dxo_ref, o2_ref, gt_ref, do2_ref, dgt_ref):
        @pl.when(pl.program_id(0) == 0)
        def _():
            dgt_ref[...] = jnp.zeros_like(dgt_ref)

        d = dxo_ref[...]
        do2_ref[...] = (d * gt_ref[...]).astype(BF16)
        dgt_ref[...] += jnp.sum(d * o2_ref[...].astype(F32), axis=0, keepdims=True)

    row = pl.BlockSpec((1, D), lambda i: (0, 0))
    blk = pl.BlockSpec((ts, D), lambda i: (i, 0))
    return _call(body, name=name, grid=(S // ts,), in_specs=[blk, blk, row], out_specs=[blk, row],
                 out_shape=[SDS((S, D), BF16), SDS((1, D), F32)], sem=("arbitrary",))(dxo, o2, gate)


def _merge_bwd_ew(name, dz, mg, proj):
    S, D = dz.shape
    ts = min(S, ROW_TILE)
    td = min(D, 512)
    nd = D // td

    def body(dz_ref, mg_ref, pj_ref, dmg_ref, dpj_ref):
        d = dz_ref[...].astype(F32)
        m = mg_ref[...].astype(F32)
        dmg_ref[...] = (d * pj_ref[...].astype(F32) * m * (1.0 - m)).astype(BF16)
        dpj_ref[...] = (d * m).astype(BF16)

    wide = pl.BlockSpec((ts, td), lambda i, j, n: (i, n * nd + j))
    return _call(body, name=name, grid=(S // ts, nd, 4),
                 in_specs=[pl.BlockSpec((ts, td), lambda i, j, n: (i, j)), wide, wide],
                 out_specs=[wide, wide], out_shape=[SDS((S, 4 * D), BF16), SDS((S, 4 * D), BF16)],
                 sem=("parallel", "parallel", "arbitrary"))(dz, mg, proj)


def _final_loss(name, x, tgt, g):
    S, D = x.shape
    ts = min(S, EW_ROWS)

    def body(x_ref, t_ref, g_ref, dx_ref, dg_ref, loss_ref):
        @pl.when(pl.program_id(0) == 0)
        def _():
            dg_ref[...] = jnp.zeros_like(dg_ref)
            loss_ref[...] = jnp.zeros_like(loss_ref)

        xv = x_ref[...]
        r = lax.rsqrt(jnp.mean(xv * xv, axis=-1, keepdims=True) + EPS)
        xh = xv * r
        gv = g_ref[...]
        err = xh * gv - t_ref[...]
        row_loss = jnp.mean(err * err, axis=-1, keepdims=True)
        loss_ref[...] += 0.5 * jnp.sum(row_loss, axis=0, keepdims=True)
        dy = err * (1.0 / D)
        dg_ref[...] += jnp.sum(dy * xh, axis=0, keepdims=True)
        dxh = dy * gv
        dx_ref[...] = r * (dxh - xh * jnp.mean(dxh * xh, axis=-1, keepdims=True))

    row = pl.BlockSpec((1, D), lambda i: (0, 0))
    blk = pl.BlockSpec((ts, D), lambda i: (i, 0))
    return _call(body, name=name, grid=(S // ts,), in_specs=[blk, blk, row],
                 out_specs=[blk, row, pl.BlockSpec((1, 128), lambda i: (0, 0))],
                 out_shape=[SDS((S, D), F32), SDS((1, D), F32), SDS((1, 128), F32)],
                 sem=("arbitrary",))(x, tgt, g)


def _rope_tables(S):
    def tables(pos, dim):
        inv = ROPE_THETA ** (-jnp.arange(0, dim, 2, dtype=F32) / dim)
        ang = pos.astype(F32)[:, None] * inv[None, :]
        ang = jnp.concatenate([ang, ang], axis=-1)
        return jnp.cos(ang), jnp.sin(ang)

    pos = jnp.arange(S, dtype=I32)
    lane = np.arange(HEAD_DIM)
    cos1, sin1 = tables(pos, HEAD_DIM)
    up1 = jnp.asarray((lane >= 64).astype(np.float32))[None, :]
    one_d = (cos1, sin1 * up1, -sin1 * (1.0 - up1))
    cr, sr = tables(pos // GRID_W, HEAD_DIM // 2)
    cc, sc = tables(pos % GRID_W, HEAD_DIM // 2)
    cos2 = jnp.concatenate([cr, cc], axis=-1)
    sin2 = jnp.concatenate([sr, sc], axis=-1)
    up2 = jnp.asarray(((lane % 64) >= 32).astype(np.float32))[None, :]
    axial = (cos2, sin2 * up2, -sin2 * (1.0 - up2))
    return one_d, axial


def _rope_fwd(name, src, c0, nb, tabs, sh, gain=None):
    S = src.shape[0]
    ts = min(S, ROW_TILE)
    has_gain = gain is not None

    assert c0 % nb == 0
    hd = HEAD_DIM

    def body(*refs):
        x_ref, c_ref, sa_ref, sb_ref = refs[:4]
        o_ref = refs[-1]
        cv, sa, sb = c_ref[...], sa_ref[...], sb_ref[...]
        for hh in range(nb):
            lanes = slice(hh * hd, (hh + 1) * hd)
            xv = x_ref[:, lanes].astype(F32)
            if has_gain:
                r = lax.rsqrt(jnp.mean(xv * xv, axis=-1, keepdims=True) + EPS)
                xv = (xv * r) * refs[4][...]
            out = xv * cv + pltpu.roll(xv, sh, 1) * sa + pltpu.roll(xv, hd - sh, 1) * sb
            o_ref[:, lanes] = out.astype(BF16)

    tab = pl.BlockSpec((ts, hd), lambda i: (i, 0))
    in_specs = [pl.BlockSpec((ts, nb * hd), lambda i: (i, c0 // nb)), tab, tab, tab]
    ins = [src, *tabs]
    if has_gain:
        in_specs.append(pl.BlockSpec((1, hd), lambda i: (0, 0)))
        ins.append(gain)
    return _call(body, name=name, grid=(S // ts,), in_specs=in_specs,
                 out_specs=pl.BlockSpec((ts, nb * hd), lambda i: (i, 0)),
                 out_shape=SDS((S, nb * hd), BF16), sem=("parallel",))(*ins)


def _rope_bwd(name, dout, src, c0, nb, tabs, sh, gain=None):
    S = src.shape[0]
    ts = min(S, ROW_TILE)
    has_gain = gain is not None

    assert c0 % nb == 0
    hd = HEAD_DIM

    def body(*refs):
        d_ref, x_ref, c_ref, sa_ref, sb_ref = refs[:5]
        cv, sa, sb = c_ref[...], sa_ref[...], sb_ref[...]
        if has_gain:
            gn_ref, dx_ref, dgn_ref = refs[5:]

            @pl.when(pl.program_id(0) == 0)
            def _():
                dgn_ref[...] = jnp.zeros_like(dgn_ref)
        else:
            dx_ref = refs[5]
        for hh in range(nb):
            lanes = slice(hh * hd, (hh + 1) * hd)
            d = d_ref[:, lanes].astype(F32)
            dxn = d * cv + pltpu.roll(d * sa, hd - sh, 1) + pltpu.roll(d * sb, sh, 1)
            if has_gain:
                xv = x_ref[:, lanes].astype(F32)
                r = lax.rsqrt(jnp.mean(xv * xv, axis=-1, keepdims=True) + EPS)
                xh = xv * r
                dgn_ref[...] += jnp.sum(dxn * xh, axis=0, keepdims=True)
                dxh = dxn * gn_ref[...]
                dx_ref[:, lanes] = (r * (dxh - xh * jnp.mean(dxh * xh, axis=-1, keepdims=True))).astype(BF16)
            else:
                dx_ref[:, lanes] = dxn.astype(BF16)

    tab = pl.BlockSpec((ts, hd), lambda i: (i, 0))
    own = pl.BlockSpec((ts, nb * hd), lambda i: (i, 0))
    in_specs = [own, pl.BlockSpec((ts, nb * hd), lambda i: (i, c0 // nb)), tab, tab, tab]
    ins = [dout, src, *tabs]
    out_specs = [own]
    out_shape = [SDS((S, nb * hd), BF16)]
    if has_gain:
        row = pl.BlockSpec((1, hd), lambda i: (0, 0))
        in_specs.append(row)
        ins.append(gain)
        out_specs.append(row)
        out_shape.append(SDS((1, hd), F32))
    res = _call(body, name=name, grid=(S // ts,), in_specs=in_specs, out_specs=out_specs,
                out_shape=out_shape, sem=("arbitrary",))(*ins)
    return res if has_gain else (res[0], None)


def _offset_grid(T, W):
    d = (np.arange(2 * W + 1) - W)[:, None, None] * T
    return d + np.arange(T)[None, :, None] - np.arange(T)[None, None, :]


def _with_off_tile(tiles):
    xp = np if isinstance(tiles, np.ndarray) else jnp
    off = xp.full((*tiles.shape[:2], 1, *tiles.shape[3:]), NEG_INF, tiles.dtype)
    return xp.concatenate([tiles, off], axis=2)


def _mask_tiles_a(T):
    dk = _offset_grid(T, _band(128, T))
    return _with_off_tile(np.where(np.abs(dk) <= 128, 0.0, NEG_INF).astype(np.float32)[None, None])


def _mask_tiles_b(T):
    dk = _offset_grid(T, _band(1024, T))
    ad = np.abs(dk)
    mult = ((ad <= 64).astype(np.float32) + ((ad <= 256) & (dk % 4 == 0)) + ((ad <= 1024) & (dk % 16 == 0)))
    return _with_off_tile(np.where(mult > 0, np.log(np.maximum(mult, 1.0)) / SCALE, NEG_INF)
                          .astype(np.float32)[None, None])


def _edge_blocks_d(T):
    return -(-4 // (T // GRID_W))


def _mask_tiles_d(S, T):
    rows, nq, rpb = S // GRID_W, S // T, T // GRID_W
    W, E = -(-7 // rpb), _edge_blocks_d(T)
    assert nq >= 2 * E + 1
    out = []
    for i in [*range(E), nq // 2, *range(nq - E, nq)]:
        kp = ((i + np.arange(2 * W + 1) - W) * T)[:, None, None] + np.arange(T)[None, :, None]
        qp = i * T + np.arange(T)[None, None, :]
        qr, qc, kr, kc = qp >> 6, qp & 63, kp >> 6, kp & 63
        rs = np.clip(qr - 4, 0, rows - 8)
        cs = np.clip(qc - 8, 0, GRID_W - 16)
        valid = (kr >= rs) & (kr < rs + 8) & (kc >= cs) & (kc < cs + 16)
        out.append(np.where(valid, 0.0, NEG_INF).astype(np.float32))
    return np.stack(out)[:, None]


def _variant(i, nq, E):
    if E == 0:
        return 0
    return jnp.where(i < E, i, jnp.where(i >= nq - E, i - (nq - 2 * E - 1), E))


_NT = (((1,), (1,)), ((), ()))
_TN = (((0,), (0,)), ((), ()))
_NN = (((1,), (0,)), ((), ()))


class _Mixer:
    def __init__(self, tq, tk, W, G, E=0, hp=1):
        self.tq, self.tk, self.W, self.G, self.E, self.hp = tq, tk, W, G, E, hp


def _attn_fwd(name, mx, q_arr, qc0, k_arr, kc0, v_arr, vc0, u, gc0, sink, bias=None, carry=None):
    S = q_arr.shape[0]
    tq, tk, W, G = mx.tq, mx.tk, mx.W, mx.G
    nq, nk = S // tq, S // tk
    nd = nk if W is None else 2 * W + 1
    has_bias = bias is not None
    nc = 0 if carry is None else carry.n
    hd = HEAD_DIM

    def jmap(i, d):
        return d if W is None else jnp.clip(i + d - W, 0, nk - 1)

    nin = 1 if W is None else nd
    ngd = nd if W is None else 1

    def body(*refs):
        sink_ref, q_ref, g_ref = refs[:3]
        k_refs, v_refs = refs[3:3 + nin], refs[3 + nin:3 + 2 * nin]
        n_in = 3 + 2 * nin + (1 if has_bias else 0)
        bias_ref = refs[n_in - 1] if has_bias else None
        out0 = n_in + 2 * nc
        br_ref, o_ref, lse_ref = refs[out0:out0 + 3]
        if W is None:
            m_s, l_s, acc_s = refs[out0 + 3:out0 + 6]
        i, d = pl.program_id(0), pl.program_id(1)
        if carry is not None:
            carry_refs = (refs[n_in:n_in + nc], refs[n_in + nc:out0], refs[-2], refs[-1])
            pl.when((i == 0) & (d == 0))(lambda: carry.start(*carry_refs))

        def scores(h, dd, tile):
            kv = slice((h // G) * hd, (h // G + 1) * hd)
            s = lax.dot_general(k_refs[dd][:, kv], q_ref[:, h * hd:(h + 1) * hd], _NT,
                                preferred_element_type=F32)
            if has_bias:
                s = s + bias_ref[_variant(i, nq, mx.E), h if per_head else 0, tile]
            return s

        def weighted(h, dd, p):
            kv = slice((h // G) * hd, (h // G + 1) * hd)
            return lax.dot_general(v_refs[dd][:, kv], p.astype(BF16), _TN, preferred_element_type=F32)

        def finish(h, m, l, acc):
            lanes = slice(h * hd, (h + 1) * hd)
            sk = sink_ref[h]
            m = m * SCALE
            mf = jnp.maximum(m, sk)
            a = jnp.exp(m - mf)
            lf = l * a + jnp.exp(sk - mf)
            o = ((acc * a) / lf).T
            gv = g_ref[:, lanes].astype(F32)
            o_ref[:, lanes] = o.astype(BF16)
            br_ref[:, lanes] = (o * (gv * _sigmoid(gv))).astype(BF16)
            lse_ref[h] = mf + jnp.log(lf)

        if W is None:
            @pl.when(d == 0)
            def _():
                m_s[...] = jnp.full_like(m_s, M_INIT)
                l_s[...] = jnp.zeros_like(l_s)
                acc_s[...] = jnp.zeros_like(acc_s)

            for h in range(4):
                s = scores(h, 0, d)
                m_prev = m_s[h]
                m_new = jnp.maximum(m_prev, jnp.max(s, axis=0, keepdims=True))
                alpha = jnp.exp2((m_prev - m_new) * SCALE_LOG2E)
                p = jnp.exp2((s - m_new) * SCALE_LOG2E)
                l_s[h] = alpha * l_s[h] + jnp.sum(p, axis=0, keepdims=True)
                acc_s[h] = alpha * acc_s[h] + weighted(h, 0, p)
                m_s[h] = m_new

            @pl.when(d == ngd - 1)
            def _():
                for h in range(4):
                    finish(h, m_s[h], l_s[h], acc_s[h])
        else:
            tiles = [jnp.where((i + dd - W >= 0) & (i + dd - W < nk), dd, nd) for dd in range(nd)]
            for h in range(4):
                ss = [scores(h, dd, tiles[dd]) for dd in range(nd)]
                top = ss[0]
                for s in ss[1:]:
                    top = jnp.maximum(top, s)
                m = jnp.max(top, axis=0, keepdims=True)
                ps = [jnp.exp2((s - m) * SCALE_LOG2E) for s in ss]
                l = sum(jnp.sum(p, axis=0, keepdims=True) for p in ps)
                acc = sum(weighted(h, dd, ps[dd]) for dd in range(nd))
                finish(h, m, l, acc)

        if carry is not None:
            pl.when((i == nq - 1) & (d == ngd - 1))(lambda: carry.finish(*carry_refs))

    n_kv = 4 // G
    assert qc0 % 4 == 0 and gc0 % 4 == 0 and kc0 % n_kv == 0 and vc0 % n_kv == 0

    def kv_spec(c0, dd):
        if W is None:
            return pl.BlockSpec((tk, n_kv * hd), lambda i, d: (d, c0 // n_kv))
        return pl.BlockSpec((tk, n_kv * hd), lambda i, d: (jnp.clip(i + dd - W, 0, nk - 1), c0 // n_kv))

    per_head = has_bias and bias.shape[1] == 4
    in_specs = [pl.BlockSpec(memory_space=pltpu.SMEM),
                pl.BlockSpec((tq, 4 * hd), lambda i, d: (i, qc0 // 4)),
                pl.BlockSpec((tq, 4 * hd), lambda i, d: (i, gc0 // 4)),
                *[kv_spec(kc0, dd) for dd in range(nin)], *[kv_spec(vc0, dd) for dd in range(nin)]]
    ins = [sink, q_arr, u, *[k_arr] * nin, *[v_arr] * nin]
    if has_bias:
        in_specs.append(pl.BlockSpec(bias.shape, lambda i, d: (0, 0, 0, 0, 0), pipeline_mode=pl.Buffered(1)))
        ins.append(bias)
    own = pl.BlockSpec((tq, 4 * hd), lambda i, d: (i, 0))
    out_specs = [own, own, pl.BlockSpec((4, 1, tq), lambda i, d: (0, 0, i))]
    out_shape = [SDS((S, 4 * hd), BF16), SDS((S, 4 * hd), BF16), SDS((4, 1, S), F32)]
    scratch = []
    if W is None:
        scratch = [pltpu.VMEM((4, 1, tq), F32), pltpu.VMEM((4, 1, tq), F32), pltpu.VMEM((4, hd, tq), F32)]
    sem = ("parallel", "arbitrary")
    if carry is not None:
        ins += carry.ins
        in_specs += carry.in_specs
        out_specs = carry.out_specs + out_specs
        out_shape = carry.out_shape + out_shape
        scratch += carry.scratch
        sem = ("arbitrary",) * 2
    res = _call(body, name=name, grid=(nq, ngd), in_specs=in_specs, out_specs=out_specs, out_shape=out_shape,
                scratch=scratch, sem=sem)(*ins)
    return (*res[nc:], list(res[:nc]))


def _attn_bwd(name, mx, q_arr, qc0, k_arr, kc0, v_arr, vc0, do_all, hb0, lse, delta, bias=None, want_dbias=False,
              carry=None):
    S = q_arr.shape[0]
    tq, tk, W, G = mx.tq, mx.tk, mx.W, mx.G
    nq, nk = S // tq, S // tk
    nd = nq if W is None else 2 * W + 1
    n_kv = 4 // G
    hd = HEAD_DIM
    has_bias = bias is not None
    nc = 0 if carry is None else carry.n
    assert qc0 % G == 0 and hb0 % G == 0 and (not want_dbias or (has_bias and G == 1)) and (W is None or tq == tk)

    def imap(j, d):
        return d if W is None else jnp.clip(j + d - W, 0, nq - 1)

    nin = 1 if W is None else nd
    ngd = nd if W is None else 1

    def body(*refs):
        k_ref, v_ref = refs[:2]
        q_refs, do_refs = refs[2:2 + nin], refs[2 + nin:2 + 2 * nin]
        lse_refs, dl_refs = refs[2 + 2 * nin:2 + 3 * nin], refs[2 + 3 * nin:2 + 4 * nin]
        n_in = 2 + 4 * nin + (1 if has_bias else 0)
        bias_ref = refs[n_in - 1] if has_bias else None
        out0 = n_in + 2 * nc
        dq_ref, dk_ref, dv_ref = refs[out0:out0 + 3]
        n_o = 4 if want_dbias else 3
        db_ref = refs[out0 + 3] if want_dbias else None
        if W is None:
            dk_s, dv_s = refs[out0 + n_o:out0 + n_o + 2]
        kv, j, d = pl.program_id(0), pl.program_id(1), pl.program_id(2)
        if carry is not None:
            carry_refs = (refs[n_in:n_in + nc], refs[n_in + nc:out0], refs[-2], refs[-1])
            pl.when((kv == 0) & (j == 0) & (d == 0))(lambda: carry.start(*carry_refs))

        @pl.when((j == 0) & (d == 0))
        def _():
            dq_ref[...] = jnp.zeros_like(dq_ref)
            if want_dbias:
                db_ref[...] = jnp.zeros_like(db_ref)

        def unit(kh, g, dd, i, tile):
            hh = kh * G + g
            lanes = slice(hh * hd, (hh + 1) * hd)
            k = k_ref[:, kh * hd:(kh + 1) * hd]
            v = v_ref[:, kh * hd:(kh + 1) * hd]
            q = q_refs[dd][:, lanes]
            do = do_refs[dd][:, lanes]
            s = lax.dot_general(k, q, _NT, preferred_element_type=F32)
            if has_bias:
                s = s + bias_ref[_variant(i, nq, mx.E), kh if per_head else 0, tile]
            p = jnp.exp2(s * SCALE_LOG2E - lse_refs[dd][hh] * LOG2E)
            dv = lax.dot_general(p.astype(BF16), do, _NN, preferred_element_type=F32)
            dp = lax.dot_general(v, do, _NT, preferred_element_type=F32)
            ds = p * (dp - dl_refs[dd][hh])
            if want_dbias:
                db_ref[kh, jnp.minimum(tile, nd - 1)] += ds
            dsb = ds.astype(BF16)
            dk = lax.dot_general(dsb, q, _NN, preferred_element_type=F32)
            row0 = pl.multiple_of(i * tq, tq)
            dq_ref[pl.ds(row0, tq), lanes] += lax.dot_general(dsb, k, _TN, preferred_element_type=F32) * SCALE
            return dk, dv

        if W is None:
            @pl.when(d == 0)
            def _():
                dk_s[...] = jnp.zeros_like(dk_s)
                dv_s[...] = jnp.zeros_like(dv_s)

            for kh in range(hp):
                for g in range(G):
                    dk, dv = unit(kh, g, 0, d, None)
                    dk_s[kh] += dk
                    dv_s[kh] += dv

            @pl.when(d == ngd - 1)
            def _():
                for kh in range(hp):
                    dk_ref[:, kh * hd:(kh + 1) * hd] = dk_s[kh] * SCALE
                    dv_ref[:, kh * hd:(kh + 1) * hd] = dv_s[kh]
        else:
            for kh in range(hp):
                parts = []
                for dd in range(nd):
                    i_dd = j + dd - W
                    tile = jnp.where((i_dd >= 0) & (i_dd < nq), 2 * W - dd, nd)
                    parts += [unit(kh, g, dd, jnp.clip(i_dd, 0, nq - 1), tile) for g in range(G)]
                dk_ref[:, kh * hd:(kh + 1) * hd] = sum(pt[0] for pt in parts) * SCALE
                dv_ref[:, kh * hd:(kh + 1) * hd] = sum(pt[1] for pt in parts)

        if carry is not None:
            pl.when((kv == n_kv // hp - 1) & (j == nk - 1) & (d == ngd - 1))(lambda: carry.finish(*carry_refs))

    hp = mx.hp
    hq = hp * G
    assert qc0 % hq == 0 and hb0 % hq == 0 and kc0 % hp == 0 and vc0 % hp == 0 and n_kv % hp == 0
    per_head = has_bias and bias.shape[1] == 4

    def q_spec(shape, col, dd, stat):
        def index(kv, j, d):
            blk = d if W is None else jnp.clip(j + dd - W, 0, nq - 1)
            return (col + kv, 0, blk) if stat else (blk, col + kv)
        return pl.BlockSpec(shape, index)

    in_specs = [pl.BlockSpec((tk, hp * hd), lambda kv, j, d: (j, kc0 // hp + kv)),
                pl.BlockSpec((tk, hp * hd), lambda kv, j, d: (j, vc0 // hp + kv)),
                *[q_spec((tq, hq * hd), qc0 // hq, dd, False) for dd in range(nin)],
                *[q_spec((tq, hq * hd), hb0 // hq, dd, False) for dd in range(nin)],
                *[q_spec((hq, 1, tq), 0, dd, True) for dd in range(nin)],
                *[q_spec((hq, 1, tq), hb0 // hq, dd, True) for dd in range(nin)]]
    ins = [k_arr, v_arr, *[q_arr] * nin, *[do_all] * nin, *[lse] * nin, *[delta] * nin]
    if has_bias:
        in_specs.append(pl.BlockSpec((bias.shape[0], hp if per_head else 1, bias.shape[2], tk, tq),
                                     lambda kv, j, d: (0, kv if per_head else 0, 0, 0, 0)))
        ins.append(bias)
    kv_blk = pl.BlockSpec((tk, hp * hd), lambda kv, j, d: (j, kv))
    out_specs = [pl.BlockSpec((S, hq * hd), lambda kv, j, d: (0, kv)), kv_blk, kv_blk]
    out_shape = [SDS((S, 4 * hd), F32), SDS((S, n_kv * hd), F32), SDS((S, n_kv * hd), F32)]
    if want_dbias:
        out_specs.append(pl.BlockSpec((hp, nd, tk, tq), lambda kv, j, d: (kv, 0, 0, 0)))
        out_shape.append(SDS((4, nd, tk, tq), F32))
    scratch = [pltpu.VMEM((hp, tk, hd), F32), pltpu.VMEM((hp, tk, hd), F32)] if W is None else []
    sem = ("parallel", "arbitrary", "arbitrary")
    if carry is not None:
        ins += carry.ins
        in_specs += carry.in_specs
        out_specs = carry.out_specs + out_specs
        out_shape = carry.out_shape + out_shape
        scratch += carry.scratch
        sem = ("arbitrary",) * 3
    res = _call(body, name=name, grid=(n_kv // hp, nk, ngd), in_specs=in_specs, out_specs=out_specs,
                out_shape=out_shape, scratch=scratch, sem=sem)(*ins)
    main = res[nc:]
    return (*main[:3], main[3] if want_dbias else None, list(res[:nc]))


def _attn_bwd_pre(name, dbr, o_all, u):
    S = dbr.shape[0]
    ts = min(S, ROW_TILE)
    hd = HEAD_DIM

    assert all(g % 4 == 0 for g in GCOL)

    def gcol(n):
        return GCOL[0] // 4 + n * 4 - jnp.where(n >= 2, 1, 0)

    def body(dbr_ref, o_ref, g_ref, do_ref, dg_ref, dl_ref):
        for hh in range(4):
            lanes = slice(hh * hd, (hh + 1) * hd)
            db = dbr_ref[:, lanes].astype(F32)
            o = o_ref[:, lanes].astype(F32)
            gv = g_ref[:, lanes].astype(F32)
            sg = _sigmoid(gv)
            do = db * (gv * sg)
            do_ref[:, lanes] = do.astype(BF16)
            dg_ref[:, lanes] = (db * o * (sg * (1.0 + gv * (1.0 - sg)))).astype(BF16)
            dl_ref[hh] = jnp.sum((do * o).T, axis=0, keepdims=True)

    own = pl.BlockSpec((ts, 4 * hd), lambda i, n: (i, n))
    return _call(body, name=name, grid=(S // ts, 4),
                 in_specs=[own, own, pl.BlockSpec((ts, 4 * hd), lambda i, n: (i, gcol(n)))],
                 out_specs=[own, own, pl.BlockSpec((4, 1, ts), lambda i, n: (n, 0, i))],
                 out_shape=[SDS((S, 16 * hd), BF16), SDS((S, 16 * hd), BF16), SDS((16, 1, S), F32)],
                 sem=("parallel", "parallel"))(dbr, o_all, u)


def _sink_grad(name, sink, lse, delta):
    S = lse.shape[2]
    ts = min(S, 2048)

    def body(sink_ref, lse_ref, dl_ref, out_ref):
        @pl.when(pl.program_id(1) == 0)
        def _():
            out_ref[...] = jnp.zeros_like(out_ref)

        sk = sink_ref[pl.program_id(0)]
        part = jnp.sum(jnp.exp(sk - lse_ref[0]) * dl_ref[0], axis=1, keepdims=True)
        out_ref[0] += -jnp.broadcast_to(part, (1, 128))

    col = pl.BlockSpec((1, 1, ts), lambda h, i: (h, 0, i))
    return _call(body, name=name, grid=(4, S // ts),
                 in_specs=[pl.BlockSpec(memory_space=pltpu.SMEM), col, col],
                 out_specs=pl.BlockSpec((1, 1, 128), lambda h, i: (h, 0, 0)),
                 out_shape=SDS((4, 1, 128), F32), sem=("parallel", "arbitrary"))(sink, lse, delta)


def _bias_maps(T, W):
    rpb = T // GRID_W
    nd = 2 * W + 1
    rmap = np.zeros((nd, rpb, rpb, 15), np.float32)
    for df in range(nd):
        for a in range(rpb):
            for b in range(rpb):
                r = (df - W) * rpb + b - a + 7
                if 0 <= r < 15:
                    rmap[df, a, b, r] = 1.0
    cmap = np.zeros((GRID_W, GRID_W, 31), np.float32)
    for q in range(GRID_W):
        for k in range(GRID_W):
            cmap[q, k, int(np.clip(k - q, -15, 15)) + 15] = 1.0
    return jnp.asarray(rmap), jnp.asarray(cmap)


def _bias_tiles(rel_bias, S, T):
    W = -(-7 // (T // GRID_W))
    rmap, cmap = _bias_maps(T, W)
    t = jnp.einsum("dabr,hrc,qkc->hdbkaq", rmap, rel_bias, cmap, precision=lax.Precision.HIGHEST)
    return _with_off_tile(t.reshape(1, 4, 2 * W + 1, T, T) * (1.0 / SCALE) + jnp.asarray(_mask_tiles_d(S, T)))


def _bias_tiles_t(dtiles, T):
    rpb = T // GRID_W
    W = -(-7 // rpb)
    rmap, cmap = _bias_maps(T, W)
    t = dtiles.reshape(4, 2 * W + 1, rpb, GRID_W, rpb, GRID_W)
    return jnp.einsum("dabr,hdbkaq,qkc->hrc", rmap, t, cmap, precision=lax.Precision.HIGHEST)


def _mixer_cfg(S):
    ta, tb, td = (min(S, ATT_TILE[k]) for k in "abd")
    cq, ck = (min(S, t) for t in ATT_TILE["c"])
    cb = min(S, ATT_TILE["c_bwd"])
    a = _Mixer(ta, ta, _band(128, ta), 2)
    b = _Mixer(tb, tb, _band(1024, tb), 1, hp=2)
    d = _Mixer(td, td, -(-7 // (td // GRID_W)), 1, _edge_blocks_d(td), hp=2)
    return {"a": (a, a, jnp.asarray(_mask_tiles_a(ta))), "b": (b, b, jnp.asarray(_mask_tiles_b(tb))),
            "c": (_Mixer(cq, ck, None, 2), _Mixer(cb, cb, None, 2), None), "d": (d, d, None)}


def _layer_fwd(l, x, c8, lw, p, tabs, next_shards, me, late=None):
    S, D = x.shape
    dq = D // N_SHARD
    ada_sh, win_sh, wgm_sh, wb_sh, wout_sh = lw
    one_d, axial = tabs
    cfg = _mixer_cfg(S)
    tm = min(S, MM_ROWS)

    ada = _matmul(f"ada_l{l}", (N_SHARD,), [c8, ada_sh, p["b_ada"][l][None, :]],
                  [pl.BlockSpec((8, D), lambda j: (0, 0)), pl.BlockSpec((None, D, 3 * dq), lambda j: (j, 0, 0)),
                   pl.BlockSpec((1, 3 * dq), lambda j: (0, j))],
                  [SDS((8, 3 * D), F32)], [pl.BlockSpec((8, 3 * dq), lambda j: (0, j))],
                  epilogue=lambda acc, b_ref: (acc + b_ref[...],), a_fn=lambda a: a * _sigmoid(a))[0][0:1]
    shift, scale, gate = ada[:, :D], ada[:, D:2 * D], ada[:, 2 * D:]
    g_row = p["norm_g"][l][None, :]
    h = _norm_mod(f"norm_mod_l{l}", x, g_row, scale, shift)

    u = _matmul(f"mm_in_l{l}", (S // tm, N_SHARD), [h, win_sh],
                [pl.BlockSpec((tm, D), lambda i, j: (i, 0)), pl.BlockSpec((None, D, IN_SHARD), lambda i, j: (j, 0, 0))],
                [SDS((S, IN_COLS), BF16)], [pl.BlockSpec((tm, IN_SHARD), lambda i, j: (i, j))])[0]

    qa = _rope_fwd(f"rope_qa_l{l}", u, QCOL[0], 4, one_d, 64)
    ka = _rope_fwd(f"rope_ka_l{l}", u, KCOL[0], 2, one_d, 64)
    qb = _rope_fwd(f"rope_qb_l{l}", u, QCOL[1], 4, one_d, 64)
    kb = _rope_fwd(f"rope_kb_l{l}", u, KCOL[1], 4, one_d, 64)
    qc = _rope_fwd(f"rope_qc_l{l}", u, QCOL[2], 4, axial, 32, p["c_q_norm"][l][None, :])
    kc = _rope_fwd(f"rope_kc_l{l}", u, KCOL[2], 2, axial, 32, p["c_k_norm"][l][None, :])

    loads = [[], [], [], []]
    if next_shards is not None:
        s_ada, s_in, s_gm, s_wb, s_out = next_shards
        loads = [[s_wb, s_out], [s_in], [s_gm], [s_ada]]
    if late is not None:
        loads[0] += [late[1], late[2]]
        loads[2] += [late[0]]
    carries = [_Carry("gather", a) if a else None for a in loads]
    no_sink = jnp.full((4,), NEG_INF, F32)
    bias = _bias_tiles(p["d_rel_bias"][l], S, cfg["d"][0].tq)
    br_a, o_a, lse_a, got_a = _attn_fwd(f"attn_a_l{l}", cfg["a"][0], qa, 0, ka, 0, u, VCOL[0], u, GCOL[0],
                                        p["a_sink"][l], cfg["a"][2], carries[0])
    br_b, o_b, lse_b, got_b = _attn_fwd(f"attn_b_l{l}", cfg["b"][0], qb, 0, kb, 0, u, VCOL[1], u, GCOL[1], no_sink,
                                        cfg["b"][2], carries[1])
    br_c, o_c, lse_c, got_c = _attn_fwd(f"attn_c_l{l}", cfg["c"][0], qc, 0, kc, 0, u, VCOL[2], u, GCOL[2], no_sink,
                                        None, carries[2])
    br_d, o_d, lse_d, got_d = _attn_fwd(f"attn_d_l{l}", cfg["d"][0], u, QCOL[3], u, KCOL[3], u, VCOL[3], u, GCOL[3],
                                        no_sink, bias, carries[3])
    next_lw = None
    if next_shards is not None:
        next_lw = _own_slot([got_d[0], got_b[0], got_c[0], got_a[0], got_a[1]], next_shards, me)
    if late is not None:
        wgm_sh, wb_sh, wout_sh = _own_slot([got_c[-1], got_a[-2], got_a[-1]], late, me)
        lw = (ada_sh, win_sh, wgm_sh, wb_sh, wout_sh)
    br = jnp.concatenate([br_a, br_b, br_c, br_d], axis=1)
    o_all = jnp.concatenate([o_a, o_b, o_c, o_d], axis=1)

    def merge_body(h_ref, wg_ref, br_ref, wb_ref, mg_ref, pj_ref, z_ref, acc_ref):
        n = pl.program_id(2)
        mgv = _sigmoid(lax.dot_general(h_ref[...], wg_ref[...], _NN, preferred_element_type=F32))
        pj = lax.dot_general(br_ref[...], wb_ref[...], _NN, preferred_element_type=F32)
        mg_ref[...] = mgv.astype(BF16)
        pj_ref[...] = pj.astype(BF16)

        @pl.when(n == 0)
        def _():
            acc_ref[...] = mgv * pj

        @pl.when(n > 0)
        def _():
            acc_ref[...] += mgv * pj

        @pl.when(n == 3)
        def _():
            z_ref[...] = acc_ref[...].astype(BF16)

    wide = pl.BlockSpec((tm, dq), lambda i, j, n: (i, n * N_SHARD + j))
    mg, proj, z = _call(
        merge_body, name=f"merge_l{l}", grid=(S // tm, N_SHARD, 4),
        in_specs=[pl.BlockSpec((tm, D), lambda i, j, n: (i, 0)),
                  pl.BlockSpec((None, D, dq), lambda i, j, n: (n, 0, j)),
                  pl.BlockSpec((tm, BRANCH_W), lambda i, j, n: (i, n)),
                  pl.BlockSpec((None, None, BRANCH_W, dq), lambda i, j, n: (j, n, 0, 0))],
        out_specs=[wide, wide, pl.BlockSpec((tm, dq), lambda i, j, n: (i, j))],
        out_shape=[SDS((S, 4 * D), BF16), SDS((S, 4 * D), BF16), SDS((S, D), BF16)],
        scratch=[pltpu.VMEM((tm, dq), F32)], sem=("parallel", "parallel", "arbitrary"))(h, wgm_sh, br, wb_sh)

    tn = min(D, 1024)
    x_new, o2 = _matmul(
        f"mm_out_l{l}", (S // tm, D // tn), [z, wout_sh, x, gate],
        [pl.BlockSpec((tm, D), lambda i, j: (i, 0)), pl.BlockSpec((N_SHARD, dq, tn), lambda i, j: (0, 0, j)),
         pl.BlockSpec((tm, tn), lambda i, j: (i, j)), pl.BlockSpec((1, tn), lambda i, j: (0, j))],
        [SDS((S, D), F32), SDS((S, D), BF16)],
        [pl.BlockSpec((tm, tn), lambda i, j: (i, j)), pl.BlockSpec((tm, tn), lambda i, j: (i, j))],
        k_inner=N_SHARD, epilogue=lambda acc, x_ref, g_ref: (x_ref[...] + g_ref[...] * acc, acc))
    res = dict(x=x, h=h, u=u, qa=qa, ka=ka, qb=qb, kb=kb, qc=qc, kc=kc, br=br, o_all=o_all,
               lse=(lse_a, lse_b, lse_c, lse_d), bias=bias, mg=mg, proj=proj, z=z, o2=o2,
               g_row=g_row, scale=scale, gate=gate)
    return x_new, res, (ada_sh, win_sh, wgm_sh, wb_sh, wout_sh), next_lw


def _layer_bwd(l, dxo, r, lw, p, tabs, pending):
    x, h, u = r["x"], r["h"], r["u"]
    S, D = x.shape
    dq = D // N_SHARD
    ada_sh, win_sh, wgm_sh, wb_sh, wout_sh = lw
    one_d, axial = tabs
    cfg = _mixer_cfg(S)
    tm = min(S, MM_ROWS)
    tk = min(S, 1024)
    tn = min(D, 1024)

    do2, dgate = _out_bwd_ew(f"out_bwd_l{l}", dxo, r["o2"], r["gate"])
    dz = _matmul(f"mm_dz_l{l}", (S // tm, N_SHARD), [do2, wout_sh],
                 [pl.BlockSpec((tm, D), lambda i, n: (i, 0)), pl.BlockSpec((None, dq, D), lambda i, n: (n, 0, 0))],
                 [SDS((S, D), BF16)], [pl.BlockSpec((tm, dq), lambda i, n: (i, n))], tb=True)[0]
    tkw = min(S, 2048)
    g_out = _matmul(f"mm_gwout_l{l}", (N_SHARD, D // tn, S // tkw), [r["z"], do2],
                    [pl.BlockSpec((tkw, dq), lambda n, j, k: (k, n)), pl.BlockSpec((tkw, tn), lambda n, j, k: (k, j))],
                    [SDS((N_SHARD, dq, D), F32)], [pl.BlockSpec((None, dq, tn), lambda n, j, k: (n, 0, j))],
                    ta=True, k_axis=2, acc_shape=(dq, tn))[0]

    dmg, dproj = _merge_bwd_ew(f"merge_bwd_l{l}", dz, r["mg"], r["proj"])
    nj = D // tn
    g_gm = _matmul(f"mm_gwgm_l{l}", (4, D // tn, nj, S // tk), [h, dmg],
                   [pl.BlockSpec((tk, tn), lambda n, i, j, k: (k, i)),
                    pl.BlockSpec((tk, tn), lambda n, i, j, k: (k, n * nj + j))],
                   [SDS((4, D, D), F32)], [pl.BlockSpec((None, tn, tn), lambda n, i, j, k: (n, i, j))],
                   ta=True, k_axis=3, acc_shape=(tn, tn))[0]
    dh1 = _matmul(f"mm_dh1_l{l}", (S // tm, nj, 4), [dmg, wgm_sh],
                  [pl.BlockSpec((tm, D), lambda i, j, n: (i, n)),
                   pl.BlockSpec((None, tn, D), lambda i, j, n: (n, j, 0))],
                  [SDS((S, D), F32)], [pl.BlockSpec((tm, tn), lambda i, j, kk: (i, j))],
                  tb=True, k_axis=2, acc_shape=(tm, tn))[0]
    dbr = _matmul(f"mm_dbr_l{l}", (S // tm, 4), [dproj, wb_sh],
                  [pl.BlockSpec((tm, D), lambda i, n: (i, n)),
                   pl.BlockSpec((N_SHARD, None, BRANCH_W, dq), lambda i, n: (0, n, 0, 0))],
                  [SDS((S, 4 * BRANCH_W), BF16)], [pl.BlockSpec((tm, BRANCH_W), lambda i, n: (i, n))],
                  tb=True, k_inner=N_SHARD)[0]
    tkl = min(S, 2048)
    g_wb = _matmul(f"mm_gwb_l{l}", (N_SHARD, 4, S // tkl), [r["br"], dproj],
                   [pl.BlockSpec((tkl, BRANCH_W), lambda j, n, k: (k, n)),
                    pl.BlockSpec((tkl, dq), lambda j, n, k: (k, n * N_SHARD + j))],
                   [SDS((N_SHARD, 4, BRANCH_W, dq), F32)],
                   [pl.BlockSpec((None, None, BRANCH_W, dq), lambda j, n, k: (j, n, 0, 0))],
                   ta=True, k_axis=2, acc_shape=(BRANCH_W, dq))[0]

    do_all, dg_all, delta = _attn_bwd_pre(f"attn_pre_l{l}", dbr, r["o_all"], u)
    lse_a, lse_b, lse_c, lse_d = r["lse"]
    dsink = _sink_grad(f"sink_grad_l{l}", p["a_sink"][l], lse_a, delta)[:, 0, 0]
    carries = [None] * 3
    if pending is not None:
        p_in, p_gm, p_wb, p_out = pending
        carries = [_Carry("exchange", [p_in]), _Carry("exchange", [p_gm]), _Carry("exchange", [p_wb, p_out])]
    dqa, dka, dva, _, _ = _attn_bwd(f"attn_a_bwd_l{l}", cfg["a"][1], r["qa"], 0, r["ka"], 0, u, VCOL[0], do_all, 0,
                                    lse_a, delta, cfg["a"][2])
    dqb, dkb, dvb, _, got_b = _attn_bwd(f"attn_b_bwd_l{l}", cfg["b"][1], r["qb"], 0, r["kb"], 0, u, VCOL[1], do_all, 4,
                                        lse_b, delta, cfg["b"][2], carry=carries[0])
    dqc, dkc, dvc, _, got_c = _attn_bwd(f"attn_c_bwd_l{l}", cfg["c"][1], r["qc"], 0, r["kc"], 0, u, VCOL[2], do_all, 8,
                                        lse_c, delta, carry=carries[1])
    dqd, dkd, dvd, dbias, got_d = _attn_bwd(f"attn_d_bwd_l{l}", cfg["d"][1], u, QCOL[3], u, KCOL[3], u, VCOL[3],
                                            do_all, 12, lse_d, delta, r["bias"], True, carries[2])
    arrived = None if pending is None else [got_b[0], got_c[0], got_d[0], got_d[1]]
    d_rel = _bias_tiles_t(dbias, cfg["d"][1].tq)

    duqa, _ = _rope_bwd(f"rope_qa_bwd_l{l}", dqa, u, QCOL[0], 4, one_d, 64)
    duka, _ = _rope_bwd(f"rope_ka_bwd_l{l}", dka, u, KCOL[0], 2, one_d, 64)
    duqb, _ = _rope_bwd(f"rope_qb_bwd_l{l}", dqb, u, QCOL[1], 4, one_d, 64)
    dukb, _ = _rope_bwd(f"rope_kb_bwd_l{l}", dkb, u, KCOL[1], 4, one_d, 64)
    duqc, dcq = _rope_bwd(f"rope_qc_bwd_l{l}", dqc, u, QCOL[2], 4, axial, 32, p["c_q_norm"][l][None, :])
    dukc, dck = _rope_bwd(f"rope_kc_bwd_l{l}", dkc, u, KCOL[2], 2, axial, 32, p["c_k_norm"][l][None, :])
    bw = BRANCH_W
    du = jnp.concatenate(
        [duqa, duka, dva.astype(BF16), dg_all[:, 0:bw],
         duqb, dukb, dvb.astype(BF16), dg_all[:, bw:2 * bw],
         duqc, dukc, dvc.astype(BF16), dg_all[:, 2 * bw:3 * bw],
         dqd.astype(BF16), dkd.astype(BF16), dvd.astype(BF16), dg_all[:, 3 * bw:]], axis=1)

    tmi = min(D, 1024)
    g_in = _matmul(f"mm_gwin_l{l}", (N_SHARD, D // tmi, S // tk), [h, du],
                   [pl.BlockSpec((tk, tmi), lambda j, i, k: (k, i)), pl.BlockSpec((tk, IN_SHARD), lambda j, i, k: (k, j))],
                   [SDS((N_SHARD, D, IN_SHARD), F32)], [pl.BlockSpec((None, tmi, IN_SHARD), lambda j, i, k: (j, i, 0))],
                   ta=True, k_axis=2, acc_shape=(tmi, IN_SHARD))[0]
    dh2 = _matmul(f"mm_dh2_l{l}", (S // tm, nj, N_SHARD), [du, win_sh],
                  [pl.BlockSpec((tm, IN_SHARD), lambda i, j, k: (i, k)),
                   pl.BlockSpec((None, tn, IN_SHARD), lambda i, j, k: (k, j, 0))],
                  [SDS((S, D), F32)], [pl.BlockSpec((tm, tn), lambda i, j, k: (i, j))],
                  tb=True, k_axis=2, acc_shape=(tm, tn))[0]

    dx_prev, dshift, dscale, dng = _norm_mod_bwd(f"norm_mod_bwd_l{l}", x, dh1, dh2, dxo, r["g_row"], r["scale"])
    d_ada = jnp.concatenate([dshift, dscale, dgate], axis=1)[0]
    big = (g_in, g_gm, g_wb, g_out)
    small = dict(norm_g=dng[0], b_ada=d_ada, a_sink=dsink, c_q_norm=dcq[0], c_k_norm=dck[0], d_rel_bias=d_rel)
    return dx_prev, big, small, arrived


def _place():
    return lax.axis_index("x"), lax.axis_index("y"), lax.axis_index("c")


class _Carry:
    def __init__(self, kind, arrays):
        self.kind, self.n, self.ins = kind, len(arrays), list(arrays)
        any_spec = pl.BlockSpec(memory_space=pl.ANY)
        self.in_specs = [any_spec] * self.n
        self.out_specs = [any_spec] * self.n
        if kind == "gather":
            self.out_shape = [SDS((N_SHARD, *a.shape), a.dtype) for a in arrays]
        else:
            self.out_shape = [SDS((3, *a.shape[1:]), a.dtype) for a in arrays]
        self.scratch = [pltpu.SemaphoreType.DMA((self.n, 3)), pltpu.SemaphoreType.DMA((self.n, 3))]

    def _copies(self, ins, outs, send_sems, recv_sems, arriving):
        x, y, c = _place()
        cps = []
        for a in range(self.n):
            for k, (px, py) in enumerate([(1 - x, y), (x, 1 - y), (1 - x, 1 - y)]):
                if self.kind == "gather":
                    src, dst = ins[a], outs[a].at[2 * px + py if arriving else 2 * x + y]
                else:
                    src, dst = ins[a].at[2 * px + py], outs[a].at[k]
                cps.append(pltpu.make_async_remote_copy(src, dst, send_sems.at[a, k], recv_sems.at[a, k],
                                                        device_id=(px, py, c), device_id_type=MESH))
        return cps

    def start(self, ins, outs, send_sems, recv_sems):
        for cp in self._copies(ins, outs, send_sems, recv_sems, False):
            cp.start()

    def finish(self, ins, outs, send_sems, recv_sems):
        for cp in self._copies(ins, outs, send_sems, recv_sems, True):
            cp.wait_recv()
        for cp in self._copies(ins, outs, send_sems, recv_sems, False):
            cp.wait_send()


def _run_carry(name, carry):
    n = carry.n

    def body(*refs):
        args = (refs[:n], refs[n:2 * n], refs[2 * n], refs[2 * n + 1])
        carry.start(*args)
        carry.finish(*args)

    return pl.pallas_call(body, name=name, in_specs=carry.in_specs, out_specs=carry.out_specs,
                          out_shape=carry.out_shape, scratch_shapes=carry.scratch)(*carry.ins)


def _own_slot(gathered, shards, me):
    return [lax.dynamic_update_index_in_dim(g, s, me, 0) for g, s in zip(gathered, shards)]


def _gather_small(name, v):
    m_per, n = v.shape

    def body(x_ref, out_ref, send_sems, recv_sems, local_sem):
        x, y, c = _place()
        me, sibling = (x, y, c), (x, y, 1 - c)
        chips = [(1 - x, y), (x, 1 - y), (1 - x, 1 - y)]

        def rows(px, py, pc):
            return out_ref.at[pl.ds((4 * px + 2 * py + pc) * m_per, m_per), :]

        def copy(k, block, to, src=None):
            return pltpu.make_async_remote_copy(
                src_ref=rows(*block) if src is None else src, dst_ref=rows(*block),
                send_sem=send_sems.at[k], recv_sem=recv_sems.at[k], device_id=to, device_id_type=MESH)

        mine = pltpu.make_async_copy(x_ref, rows(*me), local_sem)
        mine.start()
        first = [copy(0, me, sibling, src=x_ref)]
        first += [copy(1 + j, me, (*chip, c), src=x_ref) for j, chip in enumerate(chips)]
        for cp in first:
            cp.start()
        passed = [copy(4 + j, (*chip, c), sibling) for j, chip in enumerate(chips)]
        for j, chip in enumerate(chips):
            copy(1 + j, (*chip, c), me).wait_recv()
            passed[j].start()
        copy(0, sibling, me).wait_recv()
        for j, chip in enumerate(chips):
            copy(4 + j, (*chip, 1 - c), me).wait_recv()
        for cp in first + passed:
            cp.wait_send()
        mine.wait()

    return pl.pallas_call(
        body, name=name, out_shape=SDS((8 * m_per, n), v.dtype),
        in_specs=[pl.BlockSpec(memory_space=pltpu.VMEM)], out_specs=pl.BlockSpec(memory_space=pltpu.VMEM),
        scratch_shapes=[pltpu.SemaphoreType.DMA((7,)), pltpu.SemaphoreType.DMA((7,)), pltpu.SemaphoreType.DMA])(v)


def _pair_send_half(name, grads):
    n = len(grads)

    def body(*refs):
        ins, outs = refs[:n], refs[n:2 * n]
        send_sems, recv_sems = refs[2 * n:]
        x, y, c = _place()
        cps = []
        for a in range(n):
            cp = pltpu.make_async_remote_copy(ins[a].at[:, 1 - c], outs[a], send_sems.at[a], recv_sems.at[a],
                                              device_id=(x, y, 1 - c), device_id_type=MESH)
            cp.start()
            cps.append(cp)
        for cp in cps:
            cp.wait_recv()
        for cp in cps:
            cp.wait_send()

    any_spec = pl.BlockSpec(memory_space=pl.ANY)
    return pl.pallas_call(
        body, name=name, in_specs=[any_spec] * n, out_specs=[any_spec] * n,
        out_shape=[SDS((g.shape[0], *g.shape[2:]), g.dtype) for g in grads],
        scratch_shapes=[pltpu.SemaphoreType.DMA((n,)), pltpu.SemaphoreType.DMA((n,))])(*grads)


def _pair_gather(name, halves):
    n = len(halves)

    def body(*refs):
        outs = refs[n:2 * n]
        send_sems, recv_sems = refs[2 * n:]
        x, y, c = _place()
        cps = [pltpu.make_async_remote_copy(outs[a].at[c], outs[a].at[c], send_sems.at[a], recv_sems.at[a],
                                            device_id=(x, y, 1 - c), device_id_type=MESH) for a in range(n)]
        for cp in cps:
            cp.start()
        for a in range(n):
            pltpu.make_async_remote_copy(outs[a].at[c], outs[a].at[1 - c], send_sems.at[a], recv_sems.at[a],
                                         device_id=(x, y, 1 - c), device_id_type=MESH).wait_recv()
        for cp in cps:
            cp.wait_send()

    any_spec = pl.BlockSpec(memory_space=pl.ANY)
    return pl.pallas_call(
        body, name=name, in_specs=[any_spec] * n, out_specs=[any_spec] * n,
        out_shape=[SDS(g.shape, g.dtype) for g in halves], input_output_aliases={a: a for a in range(n)},
        scratch_shapes=[pltpu.SemaphoreType.DMA((n,)), pltpu.SemaphoreType.DMA((n,))])(*halves)


def _add_half(name, g, recv, c_idx):
    _, _, R, C = g.shape
    tr = min(R, 256)

    def body(c_ref, g_ref, r_ref, o_ref):
        o_ref[...] = (g_ref[...] + r_ref[...]).astype(BF16)

    return _call(body, name=name, grid=(4, R // tr), nsp=1,
                 in_specs=[pl.BlockSpec((None, None, tr, C), lambda j, r, c_ref: (j, c_ref[0], r, 0)),
                           pl.BlockSpec((None, tr, C), lambda j, r, c_ref: (j, r, 0))],
                 out_specs=pl.BlockSpec((None, tr, C), lambda j, r, c_ref: (j, r, 0)),
                 out_shape=SDS((4, R, C), BF16), sem=("parallel", "parallel"))(c_idx, g, recv)


def _add_shards(name, part, recv, idx):
    _, R, C = part.shape
    tr = min(R, 256)

    def body(idx_ref, p_ref, r_ref, o_ref):
        o_ref[...] = (((p_ref[...].astype(F32) + r_ref[0].astype(F32)) + r_ref[1].astype(F32))
                      + r_ref[2].astype(F32))

    return _call(body, name=name, grid=(R // tr,), nsp=1,
                 in_specs=[pl.BlockSpec((None, tr, C), lambda r, idx_ref: (idx_ref[0], r, 0)),
                           pl.BlockSpec((3, tr, C), lambda r, idx_ref: (0, r, 0))],
                 out_specs=pl.BlockSpec((None, tr, C), lambda r, idx_ref: (idx_ref[1], r, 0)),
                 out_shape=SDS((2, R, C), F32), sem=("parallel",))(idx, part, recv)


def _pair_sum_layer(l, big, c_idx):
    views = []
    for g in big:
        rows = g.shape[-2] if g.ndim == 3 else g.shape[1] * g.shape[2]
        views.append(g.reshape(N_SHARD, 2, rows // 2, g.shape[-1]))
    recv1 = _pair_send_half(f"rs_pair_send_l{l}", views)
    return [_add_half(f"rs_add_half{a}_l{l}", v, r1, c_idx) for a, (v, r1) in enumerate(zip(views, recv1))]


def _finish_reduce_layer(l, big, parts, recv2, idx):
    halves = [_add_shards(f"rs_add_shards{a}_l{l}", pt, r2, idx) for a, (pt, r2) in enumerate(zip(parts, recv2))]
    full = _pair_gather(f"rs_pair_gather_l{l}", halves)
    return [f.reshape(g.shape[1:]) for f, g in zip(full, big)]


def _adamw_math(w, g, m, v):
    m = ADAM_B1 * m + (1.0 - ADAM_B1) * g
    v = ADAM_B2 * v + (1.0 - ADAM_B2) * (g * g)
    m_hat = m / (1.0 - ADAM_B1 ** ADAM_STEP)
    v_hat = v / (1.0 - ADAM_B2 ** ADAM_STEP)
    delta = -ADAM_LR * (m_hat / (jnp.sqrt(v_hat) + ADAM_EPS) + ADAM_WD * w)
    return delta, m, v


def _adamw(name, w, g, m, v):
    shape = w.shape
    C = shape[-1]
    R = int(np.prod(shape[:-1]))
    tr = min(R, 256)

    def body(w_ref, g_ref, m_ref, v_ref, d_ref, nm_ref, nv_ref):
        d, nm, nv = _adamw_math(w_ref[...], g_ref[...], m_ref[...], v_ref[...])
        d_ref[...] = d
        nm_ref[...] = nm
        nv_ref[...] = nv

    blk = pl.BlockSpec((tr, C), lambda i: (i, 0))
    outs = _call(body, name=name, grid=(R // tr,), in_specs=[blk] * 4, out_specs=[blk] * 3,
                 out_shape=[SDS((R, C), F32)] * 3, sem=("parallel",))(*(a.reshape(R, C) for a in (w, g, m, v)))
    return [o.reshape(shape) for o in outs]


def _adamw_small(name, w, g8, m, v):
    R = w.shape[0]

    def body(w_ref, g_ref, m_ref, v_ref, go_ref, d_ref, nm_ref, nv_ref):
        g = g_ref[0]
        for b in range(1, 8):
            g = g + g_ref[b]
        d, nm, nv = _adamw_math(w_ref[...], g, m_ref[...], v_ref[...])
        go_ref[...] = g
        d_ref[...] = d
        nm_ref[...] = nm
        nv_ref[...] = nv

    blk = pl.BlockSpec((R, 128), lambda i: (0, 0))
    return _call(body, name=name, grid=(1,), in_specs=[blk, pl.BlockSpec((8, R, 128), lambda i: (0, 0, 0)), blk, blk],
                 out_specs=[blk] * 4, out_shape=[SDS((R, 128), F32)] * 4, sem=("arbitrary",))(w, g8, m, v)


SMALL_NAMES = ("norm_g", "b_ada", "a_sink", "c_q_norm", "c_k_norm", "d_rel_bias", "final_g")


def _pack(parts, extra_rows=0):
    flat = jnp.concatenate([a.reshape(-1) for a in parts])
    rows = -(-flat.shape[0] // 128)
    rows = -(-rows // 8) * 8 + extra_rows
    return jnp.pad(flat, (0, rows * 128 - flat.shape[0])).reshape(rows, 128)


def _unpack(packed, like):
    flat = packed.reshape(-1)
    out, off = [], 0
    for a in like:
        out.append(flat[off:off + a.size].reshape(a.shape))
        off += a.size
    return out


def _device_step(x, c8, tgt, shards, p, me, c_idx):
    S = x.shape[0]
    L = len(shards)
    tabs = _rope_tables(S)
    first = _Carry("gather", shards[0][:2])
    lw = [(*_own_slot(_run_carry("gather_w_l0", first), shards[0][:2], me), None, None, None)]
    res = []
    for l in range(L):
        x, r, lw[l], nxt = _layer_fwd(l, x, c8, lw[l], p, tabs, shards[l + 1] if l + 1 < L else None, me,
                                      shards[0][2:] if l == 0 else None)
        res.append(r)
        lw.append(nxt)
    dx, dfg, loss = _final_loss("final_loss", x, tgt, p["final_g"][None, :])
    bigs, smalls, parts, arrived = [None] * L, [None] * L, [None] * L, [None] * L
    for l in reversed(range(L)):
        pending = parts[l + 1] if l + 1 < L else None
        dx, bigs[l], smalls[l], arr = _layer_bwd(l, dx, res[l], lw[l], p, tabs, pending)
        if pending is not None:
            arrived[l + 1] = arr
        parts[l] = _pair_sum_layer(l, bigs[l], c_idx)
    arrived[0] = _run_carry("rs_shard_exchange_l0", _Carry("exchange", parts[0]))
    idx = jnp.concatenate([jnp.reshape(me, (1,)).astype(I32), c_idx])
    reduced = [_finish_reduce_layer(l, bigs[l], parts[l], arrived[l], idx) for l in range(L)]
    return loss, dx, reduced, smalls, dfg[0]


def kernel(x, c, norm_g, w_ada, b_ada, w_in, a_sink, c_q_norm, c_k_norm, d_rel_bias, w_gate_merge, w_branch, w_out, final_g, loss_target, m_norm_g, m_w_ada, m_b_ada, m_w_in, m_a_sink, m_c_q_norm, m_c_k_norm, m_d_rel_bias, m_w_gate_merge, m_w_branch, m_w_out, m_final_g, v_norm_g, v_w_ada, v_b_ada, v_w_in, v_a_sink, v_c_q_norm, v_c_k_norm, v_d_rel_bias, v_w_gate_merge, v_w_branch, v_w_out, v_final_g):
    L, D = norm_g.shape
    dq = D // N_SHARD
    p = dict(norm_g=norm_g, b_ada=b_ada, a_sink=a_sink, c_q_norm=c_q_norm, c_k_norm=c_k_norm,
             d_rel_bias=d_rel_bias, final_g=final_g)
    xi, yi, ci = _place()
    c_idx = jnp.reshape(ci, (1,)).astype(I32)
    me = 2 * xi + yi

    shards = [[w_ada[l].astype(BF16), w_in[l].astype(BF16), w_gate_merge[l].astype(BF16),
               w_branch[l].astype(BF16), w_out[l].astype(BF16)] for l in range(L)]
    c8 = jnp.broadcast_to(c, (8, D))
    loss, grad_x, reduced, smalls, dfg = _device_step(x[0], c8, loss_target[0], shards, p, me, c_idx)
    loss = lax.psum(loss[0, 0], ("x", "y", "c"))

    small_parts = [jnp.stack([s[n] for s in smalls]) for n in SMALL_NAMES[:-1]] + [dfg]
    packed = _pack(small_parts + [c[0]])
    rows = packed.shape[0]
    g8 = _gather_small("gather_small", packed).reshape(8, rows, 128)
    small_w = [p[n] for n in SMALL_NAMES]
    small_m = [m_norm_g, m_b_ada, m_a_sink, m_c_q_norm, m_c_k_norm, m_d_rel_bias, m_final_g]
    small_v = [v_norm_g, v_b_ada, v_a_sink, v_c_q_norm, v_c_k_norm, v_d_rel_bias, v_final_g]
    pad_c = [jnp.zeros((D,), F32)]
    sg, sd, sm, sv = _adamw_small("adamw_small", _pack(small_w + pad_c), g8, _pack(small_m + pad_c),
                                  _pack(small_v + pad_c))
    sg, sd, sm, sv = (_unpack(a, small_w) for a in (sg, sd, sm, sv))

    n_small = sum(a.size for a in small_parts)
    flat8 = g8.reshape(8, rows * 128)
    c_all = flat8[:, n_small:n_small + D]
    dada_all = flat8[:, L * D:L * D + L * 3 * D].reshape(8, L, 3 * D)
    dada_mine = lax.dynamic_slice_in_dim(dada_all, (2 * xi + yi) * (3 * dq), 3 * dq, axis=2)
    tma = min(D, 1024)
    g_ada = jnp.stack([
        _matmul(f"mm_gwada_l{l}", (D // tma,), [c_all, dada_mine[:, l]],
                [pl.BlockSpec((8, tma), lambda i: (0, i)), pl.BlockSpec((8, 3 * dq), lambda i: (0, 0))],
                [SDS((D, 3 * dq), F32)], [pl.BlockSpec((tma, 3 * dq), lambda i: (i, 0))],
                ta=True, a_fn=lambda a: a * _sigmoid(a))[0] for l in range(L)])

    g_in, g_gm, g_wb, g_out = (jnp.stack([reduced[l][a] for l in range(L)]) for a in range(4))

    big = {}
    for nm, w, g, m, v in (("w_ada", w_ada, g_ada, m_w_ada, v_w_ada), ("w_in", w_in, g_in, m_w_in, v_w_in),
                           ("w_gate_merge", w_gate_merge, g_gm, m_w_gate_merge, v_w_gate_merge),
                           ("w_branch", w_branch, g_wb, m_w_branch, v_w_branch),
                           ("w_out", w_out, g_out, m_w_out, v_w_out)):
        big[nm] = (g, *_adamw(f"adamw_{nm}", w, g, m, v))

    order = ("norm_g", "w_ada", "b_ada", "w_in", "a_sink", "c_q_norm", "c_k_norm", "d_rel_bias",
             "w_gate_merge", "w_branch", "w_out", "final_g")
    cols = [[], [], [], []]
    for nm in order:
        if nm in big:
            vals = big[nm]
        else:
            k = SMALL_NAMES.index(nm)
            vals = (sg[k], sd[k], sm[k], sv[k])
        for col, val in zip(cols, vals):
            col.append(val)
    return (loss, grad_x[None], *cols[0], *cols[1], *cols[2], *cols[3])
```

```python
import numpy as np
import jax
import jax.numpy as jnp
from jax import lax
from jax.experimental import pallas as pl
from jax.experimental.pallas import tpu as pltpu

F32 = jnp.float32
BF16 = jnp.bfloat16
I32 = jnp.int32
SDS = jax.ShapeDtypeStruct
MESH = pl.DeviceIdType.MESH

HEAD_DIM = 128
GRID_W = 64
EPS = 1e-6
NEG_INF = -1e30
ROPE_THETA = 10000.0
SCALE = HEAD_DIM ** -0.5
LOG2E = 1.4426950408889634
SCALE_LOG2E = SCALE * LOG2E
N_SHARD = 4
BRANCH_W = 512
IN_COLS = 7168
IN_SHARD = IN_COLS // N_SHARD
QCOL = (0, 12, 28, 40)
KCOL = (4, 16, 32, 44)
VCOL = (6, 20, 34, 48)
GCOL = (8, 24, 36, 52)
KV_HEADS = (2, 4, 2, 4)

ADAM_LR = 0.001
ADAM_B1 = 0.9
ADAM_B2 = 0.999
ADAM_EPS = 1e-08
ADAM_WD = 0.01
ADAM_STEP = 10

V7X_VMEM_BYTES = 64 * 1024 * 1024
VMEM_LIMIT = V7X_VMEM_BYTES * 7 // 8

ATT_TILE = {"a": 256, "b": 512, "c": (512, 1024), "c_bwd": (1024, 512), "d": 256}
M_INIT = -1e20
MM_ROWS = 1024
ROW_TILE = 512
EW_ROWS = 256


def _band(reach, tile):
    return -(-reach // tile)


def _call(body, *, name, grid, in_specs, out_specs, out_shape, scratch=(), sem=None, nsp=0):
    params = pltpu.CompilerParams(dimension_semantics=sem, vmem_limit_bytes=VMEM_LIMIT)
    if nsp:
        gs = pltpu.PrefetchScalarGridSpec(num_scalar_prefetch=nsp, grid=grid, in_specs=in_specs,
                                          out_specs=out_specs, scratch_shapes=list(scratch))
        return pl.pallas_call(body, grid_spec=gs, out_shape=out_shape, name=name, compiler_params=params)
    return pl.pallas_call(body, grid=grid, in_specs=in_specs, out_specs=out_specs, out_shape=out_shape,
                          scratch_shapes=list(scratch), name=name, compiler_params=params)


def _sigmoid(x):
    return 1.0 / (1.0 + jnp.exp(-x))


def _matmul(name, grid, ins, in_specs, out_shape, out_specs, *, ta=False, tb=False, k_axis=None,
            acc_shape=None, epilogue=None, a_fn=None, k_inner=None):
    n_in = len(ins)
    n_out = len(out_shape)
    nk = grid[k_axis] if k_axis is not None else 1
    dn = (((0 if ta else 1,), (1 if tb else 0,)), ((), ()))

    def body(*refs):
        a = refs[0][...]
        if a_fn is not None:
            a = a_fn(a)
        a = a.astype(BF16)
        if k_inner is None:
            p = lax.dot_general(a, refs[1][...].astype(BF16), dn, preferred_element_type=F32)
        else:
            ck = a.shape[1] // k_inner
            p = None
            for kk in range(k_inner):
                t = lax.dot_general(a[:, kk * ck:(kk + 1) * ck], refs[1][kk].astype(BF16), dn,
                                    preferred_element_type=F32)
                p = t if p is None else p + t
        extra = refs[2:n_in]
        outs = refs[n_in:n_in + n_out]

        def fin(acc):
            vals = epilogue(acc, *extra) if epilogue is not None else (acc,)
            for o_ref, v in zip(outs, vals):
                o_ref[...] = v.astype(o_ref.dtype)

        if k_axis is None:
            fin(p)
        else:
            acc_ref = refs[-1]
            k = pl.program_id(k_axis)

            @pl.when(k == 0)
            def _():
                acc_ref[...] = p

            @pl.when(k > 0)
            def _():
                acc_ref[...] += p

            @pl.when(k == nk - 1)
            def _():
                fin(acc_ref[...])

    sem = tuple("arbitrary" if ax == k_axis else "parallel" for ax in range(len(grid)))
    scratch = [pltpu.VMEM(acc_shape, F32)] if k_axis is not None else []
    return _call(body, name=name, grid=grid, in_specs=in_specs, out_specs=out_specs, out_shape=out_shape,
                 scratch=scratch, sem=sem)(*ins)


def _norm_mod(name, x, g, scale, shift):
    S, D = x.shape
    ts = min(S, EW_ROWS)

    def body(x_ref, g_ref, sc_ref, sh_ref, h_ref):
        xv = x_ref[...]
        r = lax.rsqrt(jnp.mean(xv * xv, axis=-1, keepdims=True) + EPS)
        h_ref[...] = (((xv * r) * g_ref[...]) * (1.0 + sc_ref[...]) + sh_ref[...]).astype(BF16)

    row = pl.BlockSpec((1, D), lambda i: (0, 0))
    blk = pl.BlockSpec((ts, D), lambda i: (i, 0))
    return _call(body, name=name, grid=(S // ts,), in_specs=[blk, row, row, row], out_specs=blk,
                 out_shape=SDS((S, D), BF16), sem=("parallel",))(x, g, scale, shift)


def _norm_mod_bwd(name, x, dh1, dh2, dxo, g, scale):
    S, D = x.shape
    ts = min(S, EW_ROWS)

    def body(x_ref, a_ref, b_ref, dxo_ref, g_ref, sc_ref, dx_ref, dsh_ref, dsc_ref, dg_ref):
        @pl.when(pl.program_id(0) == 0)
        def _():
            dsh_ref[...] = jnp.zeros_like(dsh_ref)
            dsc_ref[...] = jnp.zeros_like(dsc_ref)
            dg_ref[...] = jnp.zeros_like(dg_ref)

        xv = x_ref[...]
        r = lax.rsqrt(jnp.mean(xv * xv, axis=-1, keepdims=True) + EPS)
        xh = xv * r
        dh = a_ref[...] + b_ref[...]
        gv = g_ref[...]
        one_sc = 1.0 + sc_ref[...]
        dsh_ref[...] += jnp.sum(dh, axis=0, keepdims=True)
        dsc_ref[...] += jnp.sum(dh * xh * gv, axis=0, keepdims=True)
        dg_ref[...] += jnp.sum(dh * xh * one_sc, axis=0, keepdims=True)
        dxh = dh * gv * one_sc
        dx = r * (dxh - xh * jnp.mean(dxh * xh, axis=-1, keepdims=True))
        dx_ref[...] = dxo_ref[...] + dx

    row = pl.BlockSpec((1, D), lambda i: (0, 0))
    blk = pl.BlockSpec((ts, D), lambda i: (i, 0))
    return _call(body, name=name, grid=(S // ts,), in_specs=[blk, blk, blk, blk, row, row],
                 out_specs=[blk, row, row, row],
                 out_shape=[SDS((S, D), F32), SDS((1, D), F32), SDS((1, D), F32), SDS((1, D), F32)],
                 sem=("arbitrary",))(x, dh1, dh2, dxo, g, scale)


def _out_bwd_ew(name, dxo, o2, gate):
    S, D = dxo.shape
    ts = min(S, EW_ROWS)

    def body(dxo_ref, o2_ref, gt_ref, do2_ref, dgt_ref):
        @pl.when(pl.program_id(0) == 0)
        def _():
            dgt_ref[...] = jnp.zeros_like(dgt_ref)

        d = dxo_ref[...]
        do2_ref[...] = (d * gt_ref[...]).astype(BF16)
        dgt_ref[...] += jnp.sum(d * o2_ref[...].astype(F32), axis=0, keepdims=True)

    row = pl.BlockSpec((1, D), lambda i: (0, 0))
    blk = pl.BlockSpec((ts, D), lambda i: (i, 0))
    return _call(body, name=name, grid=(S // ts,), in_specs=[blk, blk, row], out_specs=[blk, row],
                 out_shape=[SDS((S, D), BF16), SDS((1, D), F32)], sem=("arbitrary",))(dxo, o2, gate)


def _merge_bwd_ew(name, dz, mg, proj):
    S, D = dz.shape
    ts = min(S, ROW_TILE)
    td = min(D, 512)
    nd = D // td

    def body(dz_ref, mg_ref, pj_ref, dmg_ref, dpj_ref):
        d = dz_ref[...].astype(F32)
        m = mg_ref[...].astype(F32)
        dmg_ref[...] = (d * pj_ref[...].astype(F32) * m * (1.0 - m)).astype(BF16)
        dpj_ref[...] = (d * m).astype(BF16)

    wide = pl.BlockSpec((ts, td), lambda i, j, n: (i, n * nd + j))
    return _call(body, name=name, grid=(S // ts, nd, 4),
                 in_specs=[pl.BlockSpec((ts, td), lambda i, j, n: (i, j)), wide, wide],
                 out_specs=[wide, wide], out_shape=[SDS((S, 4 * D), BF16), SDS((S, 4 * D), BF16)],
                 sem=("parallel", "parallel", "arbitrary"))(dz, mg, proj)


def _final_loss(name, x, tgt, g):
    S, D = x.shape
    ts = min(S, EW_ROWS)

    def body(x_ref, t_ref, g_ref, dx_ref, dg_ref, loss_ref):
        @pl.when(pl.program_id(0) == 0)
        def _():
            dg_ref[...] = jnp.zeros_like(dg_ref)
            loss_ref[...] = jnp.zeros_like(loss_ref)

        xv = x_ref[...]
        r = lax.rsqrt(jnp.mean(xv * xv, axis=-1, keepdims=True) + EPS)
        xh = xv * r
        gv = g_ref[...]
        err = xh * gv - t_ref[...]
        row_loss = jnp.mean(err * err, axis=-1, keepdims=True)
        loss_ref[...] += 0.5 * jnp.sum(row_loss, axis=0, keepdims=True)
        dy = err * (1.0 / D)
        dg_ref[...] += jnp.sum(dy * xh, axis=0, keepdims=True)
        dxh = dy * gv
        dx_ref[...] = r * (dxh - xh * jnp.mean(dxh * xh, axis=-1, keepdims=True))

    row = pl.BlockSpec((1, D), lambda i: (0, 0))
    blk = pl.BlockSpec((ts, D), lambda i: (i, 0))
    return _call(body, name=name, grid=(S // ts,), in_specs=[blk, blk, row],
                 out_specs=[blk, row, pl.BlockSpec((1, 128), lambda i: (0, 0))],
                 out_shape=[SDS((S, D), F32), SDS((1, D), F32), SDS((1, 128), F32)],
                 sem=("arbitrary",))(x, tgt, g)


def _rope_tables(S):
    def tables(pos, dim):
        inv = ROPE_THETA ** (-jnp.arange(0, dim, 2, dtype=F32) / dim)
        ang = pos.astype(F32)[:, None] * inv[None, :]
        ang = jnp.concatenate([ang, ang], axis=-1)
        return jnp.cos(ang), jnp.sin(ang)

    pos = jnp.arange(S, dtype=I32)
    lane = np.arange(HEAD_DIM)
    cos1, sin1 = tables(pos, HEAD_DIM)
    up1 = jnp.asarray((lane >= 64).astype(np.float32))[None, :]
    one_d = (cos1, sin1 * up1, -sin1 * (1.0 - up1))
    cr, sr = tables(pos // GRID_W, HEAD_DIM // 2)
    cc, sc = tables(pos % GRID_W, HEAD_DIM // 2)
    cos2 = jnp.concatenate([cr, cc], axis=-1)
    sin2 = jnp.concatenate([sr, sc], axis=-1)
    up2 = jnp.asarray(((lane % 64) >= 32).astype(np.float32))[None, :]
    axial = (cos2, sin2 * up2, -sin2 * (1.0 - up2))
    return one_d, axial


def _rope_fwd(name, src, c0, nb, tabs, sh, gain=None):
    S = src.shape[0]
    ts = min(S, ROW_TILE)
    has_gain = gain is not None

    assert c0 % nb == 0
    hd = HEAD_DIM

    def body(*refs):
        x_ref, c_ref, sa_ref, sb_ref = refs[:4]
        o_ref = refs[-1]
        cv, sa, sb = c_ref[...], sa_ref[...], sb_ref[...]
        for hh in range(nb):
            lanes = slice(hh * hd, (hh + 1) * hd)
            xv = x_ref[:, lanes].astype(F32)
            if has_gain:
                r = lax.rsqrt(jnp.mean(xv * xv, axis=-1, keepdims=True) + EPS)
                xv = (xv * r) * refs[4][...]
            out = xv * cv + pltpu.roll(xv, sh, 1) * sa + pltpu.roll(xv, hd - sh, 1) * sb
            o_ref[:, lanes] = out.astype(BF16)

    tab = pl.BlockSpec((ts, hd), lambda i: (i, 0))
    in_specs = [pl.BlockSpec((ts, nb * hd), lambda i: (i, c0 // nb)), tab, tab, tab]
    ins = [src, *tabs]
    if has_gain:
        in_specs.append(pl.BlockSpec((1, hd), lambda i: (0, 0)))
        ins.append(gain)
    return _call(body, name=name, grid=(S // ts,), in_specs=in_specs,
                 out_specs=pl.BlockSpec((ts, nb * hd), lambda i: (i, 0)),
                 out_shape=SDS((S, nb * hd), BF16), sem=("parallel",))(*ins)


def _rope_bwd(name, dout, src, c0, nb, tabs, sh, gain=None):
    S = src.shape[0]
    ts = min(S, ROW_TILE)
    has_gain = gain is not None

    assert c0 % nb == 0
    hd = HEAD_DIM

    def body(*refs):
        d_ref, x_ref, c_ref, sa_ref, sb_ref = refs[:5]
        cv, sa, sb = c_ref[...], sa_ref[...], sb_ref[...]
        if has_gain:
            gn_ref, dx_ref, dgn_ref = refs[5:]

            @pl.when(pl.program_id(0) == 0)
            def _():
                dgn_ref[...] = jnp.zeros_like(dgn_ref)
        else:
            dx_ref = refs[5]
        for hh in range(nb):
            lanes = slice(hh * hd, (hh + 1) * hd)
            d = d_ref[:, lanes].astype(F32)
            dxn = d * cv + pltpu.roll(d * sa, hd - sh, 1) + pltpu.roll(d * sb, sh, 1)
            if has_gain:
                xv = x_ref[:, lanes].astype(F32)
                r = lax.rsqrt(jnp.mean(xv * xv, axis=-1, keepdims=True) + EPS)
                xh = xv * r
                dgn_ref[...] += jnp.sum(dxn * xh, axis=0, keepdims=True)
                dxh = dxn * gn_ref[...]
                dx_ref[:, lanes] = (r * (dxh - xh * jnp.mean(dxh * xh, axis=-1, keepdims=True))).astype(BF16)
            else:
                dx_ref[:, lanes] = dxn.astype(BF16)

    tab = pl.BlockSpec((ts, hd), lambda i: (i, 0))
    own = pl.BlockSpec((ts, nb * hd), lambda i: (i, 0))
    in_specs = [own, pl.BlockSpec((ts, nb * hd), lambda i: (i, c0 // nb)), tab, tab, tab]
    ins = [dout, src, *tabs]
    out_specs = [own]
    out_shape = [SDS((S, nb * hd), BF16)]
    if has_gain:
        row = pl.BlockSpec((1, hd), lambda i: (0, 0))
        in_specs.append(row)
        ins.append(gain)
        out_specs.append(row)
        out_shape.append(SDS((1, hd), F32))
    res = _call(body, name=name, grid=(S // ts,), in_specs=in_specs, out_specs=out_specs,
                out_shape=out_shape, sem=("arbitrary",))(*ins)
    return res if has_gain else (res[0], None)


def _offset_grid(T, W):
    d = (np.arange(2 * W + 1) - W)[:, None, None] * T
    return d + np.arange(T)[None, :, None] - np.arange(T)[None, None, :]


def _with_off_tile(tiles):
    xp = np if isinstance(tiles, np.ndarray) else jnp
    off = xp.full((*tiles.shape[:2], 1, *tiles.shape[3:]), NEG_INF, tiles.dtype)
    return xp.concatenate([tiles, off], axis=2)


def _mask_tiles_a(T):
    dk = _offset_grid(T, _band(128, T))
    return _with_off_tile(np.where(np.abs(dk) <= 128, 0.0, NEG_INF).astype(np.float32)[None, None])


def _mask_tiles_b(T):
    dk = _offset_grid(T, _band(1024, T))
    ad = np.abs(dk)
    mult = ((ad <= 64).astype(np.float32) + ((ad <= 256) & (dk % 4 == 0)) + ((ad <= 1024) & (dk % 16 == 0)))
    return _with_off_tile(np.where(mult > 0, np.log(np.maximum(mult, 1.0)) / SCALE, NEG_INF)
                          .astype(np.float32)[None, None])


def _edge_blocks_d(T):
    return -(-4 // (T // GRID_W))


def _mask_tiles_d(S, T):
    rows, nq, rpb = S // GRID_W, S // T, T // GRID_W
    W, E = -(-7 // rpb), _edge_blocks_d(T)
    assert nq >= 2 * E + 1
    out = []
    for i in [*range(E), nq // 2, *range(nq - E, nq)]:
        kp = ((i + np.arange(2 * W + 1) - W) * T)[:, None, None] + np.arange(T)[None, :, None]
        qp = i * T + np.arange(T)[None, None, :]
        qr, qc, kr, kc = qp >> 6, qp & 63, kp >> 6, kp & 63
        rs = np.clip(qr - 4, 0, rows - 8)
        cs = np.clip(qc - 8, 0, GRID_W - 16)
        valid = (kr >= rs) & (kr < rs + 8) & (kc >= cs) & (kc < cs + 16)
        out.append(np.where(valid, 0.0, NEG_INF).astype(np.float32))
    return np.stack(out)[:, None]


def _variant(i, nq, E):
    if E == 0:
        return 0
    return jnp.where(i < E, i, jnp.where(i >= nq - E, i - (nq - 2 * E - 1), E))


_NT = (((1,), (1,)), ((), ()))
_TN = (((0,), (0,)), ((), ()))
_NN = (((1,), (0,)), ((), ()))


class _Mixer:
    def __init__(self, tq, tk, W, G, E=0, hp=1):
        self.tq, self.tk, self.W, self.G, self.E, self.hp = tq, tk, W, G, E, hp


def _attn_fwd(name, mx, q_arr, qc0, k_arr, kc0, v_arr, vc0, u, gc0, sink, bias=None, carry=None):
    S = q_arr.shape[0]
    tq, tk, W, G = mx.tq, mx.tk, mx.W, mx.G
    nq, nk = S // tq, S // tk
    nd = nk if W is None else 2 * W + 1
    has_bias = bias is not None
    nc = 0 if carry is None else carry.n
    hd = HEAD_DIM

    def jmap(i, d):
        return d if W is None else jnp.clip(i + d - W, 0, nk - 1)

    nin = 1 if W is None else nd
    ngd = nd if W is None else 1

    def body(*refs):
        sink_ref, q_ref, g_ref = refs[:3]
        k_refs, v_refs = refs[3:3 + nin], refs[3 + nin:3 + 2 * nin]
        n_in = 3 + 2 * nin + (1 if has_bias else 0)
        bias_ref = refs[n_in - 1] if has_bias else None
        out0 = n_in + 2 * nc
        br_ref, o_ref, lse_ref = refs[out0:out0 + 3]
        if W is None:
            m_s, l_s, acc_s = refs[out0 + 3:out0 + 6]
        i, d = pl.program_id(0), pl.program_id(1)
        if carry is not None:
            carry_refs = (refs[n_in:n_in + nc], refs[n_in + nc:out0], refs[-2], refs[-1])
            pl.when((i == 0) & (d == 0))(lambda: carry.start(*carry_refs))

        def scores(h, dd, tile):
            kv = slice((h // G) * hd, (h // G + 1) * hd)
            s = lax.dot_general(k_refs[dd][:, kv], q_ref[:, h * hd:(h + 1) * hd], _NT,
                                preferred_element_type=F32)
            if has_bias:
                s = s + bias_ref[_variant(i, nq, mx.E), h if per_head else 0, tile]
            return s

        def weighted(h, dd, p):
            kv = slice((h // G) * hd, (h // G + 1) * hd)
            return lax.dot_general(v_refs[dd][:, kv], p.astype(BF16), _TN, preferred_element_type=F32)

        def finish(h, m, l, acc):
            lanes = slice(h * hd, (h + 1) * hd)
            sk = sink_ref[h]
            m = m * SCALE
            mf = jnp.maximum(m, sk)
            a = jnp.exp(m - mf)
            lf = l * a + jnp.exp(sk - mf)
            o = ((acc * a) / lf).T
            gv = g_ref[:, lanes].astype(F32)
            o_ref[:, lanes] = o.astype(BF16)
            br_ref[:, lanes] = (o * (gv * _sigmoid(gv))).astype(BF16)
            lse_ref[h] = mf + jnp.log(lf)

        if W is None:
            @pl.when(d == 0)
            def _():
                m_s[...] = jnp.full_like(m_s, M_INIT)
                l_s[...] = jnp.zeros_like(l_s)
                acc_s[...] = jnp.zeros_like(acc_s)

            for h in range(4):
                s = scores(h, 0, d)
                m_prev = m_s[h]
                m_new = jnp.maximum(m_prev, jnp.max(s, axis=0, keepdims=True))
                alpha = jnp.exp2((m_prev - m_new) * SCALE_LOG2E)
                p = jnp.exp2((s - m_new) * SCALE_LOG2E)
                l_s[h] = alpha * l_s[h] + jnp.sum(p, axis=0, keepdims=True)
                acc_s[h] = alpha * acc_s[h] + weighted(h, 0, p)
                m_s[h] = m_new

            @pl.when(d == ngd - 1)
            def _():
                for h in range(4):
                    finish(h, m_s[h], l_s[h], acc_s[h])
        else:
            tiles = [jnp.where((i + dd - W >= 0) & (i + dd - W < nk), dd, nd) for dd in range(nd)]
            for h in range(4):
                ss = [scores(h, dd, tiles[dd]) for dd in range(nd)]
                top = ss[0]
                for s in ss[1:]:
                    top = jnp.maximum(top, s)
                m = jnp.max(top, axis=0, keepdims=True)
                ps = [jnp.exp2((s - m) * SCALE_LOG2E) for s in ss]
                l = sum(jnp.sum(p, axis=0, keepdims=True) for p in ps)
                acc = sum(weighted(h, dd, ps[dd]) for dd in range(nd))
                finish(h, m, l, acc)

        if carry is not None:
            pl.when((i == nq - 1) & (d == ngd - 1))(lambda: carry.finish(*carry_refs))

    n_kv = 4 // G
    assert qc0 % 4 == 0 and gc0 % 4 == 0 and kc0 % n_kv == 0 and vc0 % n_kv == 0

    def kv_spec(c0, dd):
        if W is None:
            return pl.BlockSpec((tk, n_kv * hd), lambda i, d: (d, c0 // n_kv))
        return pl.BlockSpec((tk, n_kv * hd), lambda i, d: (jnp.clip(i + dd - W, 0, nk - 1), c0 // n_kv))

    per_head = has_bias and bias.shape[1] == 4
    in_specs = [pl.BlockSpec(memory_space=pltpu.SMEM),
                pl.BlockSpec((tq, 4 * hd), lambda i, d: (i, qc0 // 4)),
                pl.BlockSpec((tq, 4 * hd), lambda i, d: (i, gc0 // 4)),
                *[kv_spec(kc0, dd) for dd in range(nin)], *[kv_spec(vc0, dd) for dd in range(nin)]]
    ins = [sink, q_arr, u, *[k_arr] * nin, *[v_arr] * nin]
    if has_bias:
        in_specs.append(pl.BlockSpec(bias.shape, lambda i, d: (0, 0, 0, 0, 0), pipeline_mode=pl.Buffered(1)))
        ins.append(bias)
    own = pl.BlockSpec((tq, 4 * hd), lambda i, d: (i, 0))
    out_specs = [own, own, pl.BlockSpec((4, 1, tq), lambda i, d: (0, 0, i))]
    out_shape = [SDS((S, 4 * hd), BF16), SDS((S, 4 * hd), BF16), SDS((4, 1, S), F32)]
    scratch = []
    if W is None:
        scratch = [pltpu.VMEM((4, 1, tq), F32), pltpu.VMEM((4, 1, tq), F32), pltpu.VMEM((4, hd, tq), F32)]
    sem = ("parallel", "arbitrary")
    if carry is not None:
        ins += carry.ins
        in_specs += carry.in_specs
        out_specs = carry.out_specs + out_specs
        out_shape = carry.out_shape + out_shape
        scratch += carry.scratch
        sem = ("arbitrary",) * 2
    res = _call(body, name=name, grid=(nq, ngd), in_specs=in_specs, out_specs=out_specs, out_shape=out_shape,
                scratch=scratch, sem=sem)(*ins)
    return (*res[nc:], list(res[:nc]))


def _attn_bwd(name, mx, q_arr, qc0, k_arr, kc0, v_arr, vc0, do_all, hb0, lse, delta, bias=None, want_dbias=False,
              carry=None):
    S = q_arr.shape[0]
    tq, tk, W, G = mx.tq, mx.tk, mx.W, mx.G
    nq, nk = S // tq, S // tk
    nd = nq if W is None else 2 * W + 1
    n_kv = 4 // G
    hd = HEAD_DIM
    has_bias = bias is not None
    nc = 0 if carry is None else carry.n
    assert qc0 % G == 0 and hb0 % G == 0 and (not want_dbias or (has_bias and G == 1)) and (W is None or tq == tk)

    def imap(j, d):
        return d if W is None else jnp.clip(j + d - W, 0, nq - 1)

    nin = 1 if W is None else nd
    ngd = nd if W is None else 1

    def body(*refs):
        k_ref, v_ref = refs[:2]
        q_refs, do_refs = refs[2:2 + nin], refs[2 + nin:2 + 2 * nin]
        lse_refs, dl_refs = refs[2 + 2 * nin:2 + 3 * nin], refs[2 + 3 * nin:2 + 4 * nin]
        n_in = 2 + 4 * nin + (1 if has_bias else 0)
        bias_ref = refs[n_in - 1] if has_bias else None
        out0 = n_in + 2 * nc
        dq_ref, dk_ref, dv_ref = refs[out0:out0 + 3]
        n_o = 4 if want_dbias else 3
        db_ref = refs[out0 + 3] if want_dbias else None
        if W is None:
            dk_s, dv_s = refs[out0 + n_o:out0 + n_o + 2]
        kv, j, d = pl.program_id(0), pl.program_id(1), pl.program_id(2)
        if carry is not None:
            carry_refs = (refs[n_in:n_in + nc], refs[n_in + nc:out0], refs[-2], refs[-1])
            pl.when((kv == 0) & (j == 0) & (d == 0))(lambda: carry.start(*carry_refs))

        @pl.when((j == 0) & (d == 0))
        def _():
            dq_ref[...] = jnp.zeros_like(dq_ref)
            if want_dbias:
                db_ref[...] = jnp.zeros_like(db_ref)

        def unit(kh, g, dd, i, tile):
            hh = kh * G + g
            lanes = slice(hh * hd, (hh + 1) * hd)
            k = k_ref[:, kh * hd:(kh + 1) * hd]
            v = v_ref[:, kh * hd:(kh + 1) * hd]
            q = q_refs[dd][:, lanes]
            do = do_refs[dd][:, lanes]
            s = lax.dot_general(k, q, _NT, preferred_element_type=F32)
            if has_bias:
                s = s + bias_ref[_variant(i, nq, mx.E), kh if per_head else 0, tile]
            p = jnp.exp2(s * SCALE_LOG2E - lse_refs[dd][hh] * LOG2E)
            dv = lax.dot_general(p.astype(BF16), do, _NN, preferred_element_type=F32)
            dp = lax.dot_general(v, do, _NT, preferred_element_type=F32)
            ds = p * (dp - dl_refs[dd][hh])
            if want_dbias:
                db_ref[kh, jnp.minimum(tile, nd - 1)] += ds
            dsb = ds.astype(BF16)
            dk = lax.dot_general(dsb, q, _NN, preferred_element_type=F32)
            row0 = pl.multiple_of(i * tq, tq)
            dq_ref[pl.ds(row0, tq), lanes] += lax.dot_general(dsb, k, _TN, preferred_element_type=F32) * SCALE
            return dk, dv

        if W is None:
            @pl.when(d == 0)
            def _():
                dk_s[...] = jnp.zeros_like(dk_s)
                dv_s[...] = jnp.zeros_like(dv_s)

            for kh in range(hp):
                for g in range(G):
                    dk, dv = unit(kh, g, 0, d, None)
                    dk_s[kh] += dk
                    dv_s[kh] += dv

            @pl.when(d == ngd - 1)
            def _():
                for kh in range(hp):
                    dk_ref[:, kh * hd:(kh + 1) * hd] = dk_s[kh] * SCALE
                    dv_ref[:, kh * hd:(kh + 1) * hd] = dv_s[kh]
        else:
            for kh in range(hp):
                parts = []
                for dd in range(nd):
                    i_dd = j + dd - W
                    tile = jnp.where((i_dd >= 0) & (i_dd < nq), 2 * W - dd, nd)
                    parts += [unit(kh, g, dd, jnp.clip(i_dd, 0, nq - 1), tile) for g in range(G)]
                dk_ref[:, kh * hd:(kh + 1) * hd] = sum(pt[0] for pt in parts) * SCALE
                dv_ref[:, kh * hd:(kh + 1) * hd] = sum(pt[1] for pt in parts)

        if carry is not None:
            pl.when((kv == n_kv // hp - 1) & (j == nk - 1) & (d == ngd - 1))(lambda: carry.finish(*carry_refs))

    hp = mx.hp
    hq = hp * G
    assert qc0 % hq == 0 and hb0 % hq == 0 and kc0 % hp == 0 and vc0 % hp == 0 and n_kv % hp == 0
    per_head = has_bias and bias.shape[1] == 4

    def q_spec(shape, col, dd, stat, row0=0):
        def index(kv, j, d):
            blk = d if W is None else jnp.clip(j + dd - W, 0, nq - 1)
            return (col + kv, 0, blk) if stat else (row0 + blk, col + kv)
        return pl.BlockSpec(shape, index)

    in_specs = [pl.BlockSpec((tk, hp * hd), lambda kv, j, d: (j, kc0 // hp + kv)),
                pl.BlockSpec((tk, hp * hd), lambda kv, j, d: (j, vc0 // hp + kv)),
                *[q_spec((tq, hq * hd), qc0 // hq, dd, False) for dd in range(nin)],
                *[q_spec((tq, hq * hd), 0, dd, False, (hb0 // 4) * nq) for dd in range(nin)],
                *[q_spec((hq, 1, tq), 0, dd, True) for dd in range(nin)],
                *[q_spec((hq, 1, tq), hb0 // hq, dd, True) for dd in range(nin)]]
    ins = [k_arr, v_arr, *[q_arr] * nin, *[do_all] * nin, *[lse] * nin, *[delta] * nin]
    if has_bias:
        in_specs.append(pl.BlockSpec((bias.shape[0], hp if per_head else 1, bias.shape[2], tk, tq),
                                     lambda kv, j, d: (0, kv if per_head else 0, 0, 0, 0)))
        ins.append(bias)
    kv_blk = pl.BlockSpec((tk, hp * hd), lambda kv, j, d: (j, kv))
    out_specs = [pl.BlockSpec((S, hq * hd), lambda kv, j, d: (0, kv)), kv_blk, kv_blk]
    out_shape = [SDS((S, 4 * hd), F32), SDS((S, n_kv * hd), F32), SDS((S, n_kv * hd), F32)]
    if want_dbias:
        out_specs.append(pl.BlockSpec((hp, nd, tk, tq), lambda kv, j, d: (kv, 0, 0, 0)))
        out_shape.append(SDS((4, nd, tk, tq), F32))
    scratch = [pltpu.VMEM((hp, tk, hd), F32), pltpu.VMEM((hp, tk, hd), F32)] if W is None else []
    sem = ("parallel", "arbitrary", "arbitrary")
    if carry is not None:
        ins += carry.ins
        in_specs += carry.in_specs
        out_specs = carry.out_specs + out_specs
        out_shape = carry.out_shape + out_shape
        scratch += carry.scratch
        sem = ("arbitrary",) * 3
    res = _call(body, name=name, grid=(n_kv // hp, nk, ngd), in_specs=in_specs, out_specs=out_specs,
                out_shape=out_shape, scratch=scratch, sem=sem)(*ins)
    main = res[nc:]
    return (*main[:3], main[3] if want_dbias else None, list(res[:nc]))


def _attn_bwd_pre(name, dbr, o_all, u):
    S = dbr.shape[0]
    ts = min(S, ROW_TILE)
    hd = HEAD_DIM

    assert all(g % 4 == 0 for g in GCOL)

    def gcol(n):
        return GCOL[0] // 4 + n * 4 - jnp.where(n >= 2, 1, 0)

    def body(dbr_ref, o_ref, g_ref, do_ref, dg_ref, dl_ref):
        for hh in range(4):
            lanes = slice(hh * hd, (hh + 1) * hd)
            db = dbr_ref[:, lanes].astype(F32)
            o = o_ref[:, lanes].astype(F32)
            gv = g_ref[:, lanes].astype(F32)
            sg = _sigmoid(gv)
            do = db * (gv * sg)
            do_ref[:, lanes] = do.astype(BF16)
            dg_ref[:, lanes] = (db * o * (sg * (1.0 + gv * (1.0 - sg)))).astype(BF16)
            dl_ref[hh] = jnp.sum((do * o).T, axis=0, keepdims=True)

    own = pl.BlockSpec((ts, 4 * hd), lambda i, n: (i, n))
    stacked = pl.BlockSpec((ts, 4 * hd), lambda i, n: (n * (S // ts) + i, 0))
    return _call(body, name=name, grid=(S // ts, 4),
                 in_specs=[own, own, pl.BlockSpec((ts, 4 * hd), lambda i, n: (i, gcol(n)))],
                 out_specs=[stacked, own, pl.BlockSpec((4, 1, ts), lambda i, n: (n, 0, i))],
                 out_shape=[SDS((4 * S, 4 * hd), BF16), SDS((S, 16 * hd), BF16), SDS((16, 1, S), F32)],
                 sem=("parallel", "parallel"))(dbr, o_all, u)


def _sink_grad(name, sink, lse, delta):
    S = lse.shape[2]
    ts = min(S, 2048)

    def body(sink_ref, lse_ref, dl_ref, out_ref):
        @pl.when(pl.program_id(1) == 0)
        def _():
            out_ref[...] = jnp.zeros_like(out_ref)

        sk = sink_ref[pl.program_id(0)]
        part = jnp.sum(jnp.exp(sk - lse_ref[0]) * dl_ref[0], axis=1, keepdims=True)
        out_ref[0] += -jnp.broadcast_to(part, (1, 128))

    col = pl.BlockSpec((1, 1, ts), lambda h, i: (h, 0, i))
    return _call(body, name=name, grid=(4, S // ts),
                 in_specs=[pl.BlockSpec(memory_space=pltpu.SMEM), col, col],
                 out_specs=pl.BlockSpec((1, 1, 128), lambda h, i: (h, 0, 0)),
                 out_shape=SDS((4, 1, 128), F32), sem=("parallel", "arbitrary"))(sink, lse, delta)


def _bias_maps(T, W):
    rpb = T // GRID_W
    nd = 2 * W + 1
    rmap = np.zeros((nd, rpb, rpb, 15), np.float32)
    for df in range(nd):
        for a in range(rpb):
            for b in range(rpb):
                r = (df - W) * rpb + b - a + 7
                if 0 <= r < 15:
                    rmap[df, a, b, r] = 1.0
    cmap = np.zeros((GRID_W, GRID_W, 31), np.float32)
    for q in range(GRID_W):
        for k in range(GRID_W):
            cmap[q, k, int(np.clip(k - q, -15, 15)) + 15] = 1.0
    return jnp.asarray(rmap), jnp.asarray(cmap)


def _bias_tiles(rel_bias, S, T):
    W = -(-7 // (T // GRID_W))
    rmap, cmap = _bias_maps(T, W)
    t = jnp.einsum("dabr,hrc,qkc->hdbkaq", rmap, rel_bias, cmap, precision=lax.Precision.HIGHEST)
    return _with_off_tile(t.reshape(1, 4, 2 * W + 1, T, T) * (1.0 / SCALE) + jnp.asarray(_mask_tiles_d(S, T)))


def _bias_tiles_t(dtiles, T):
    rpb = T // GRID_W
    W = -(-7 // rpb)
    rmap, cmap = _bias_maps(T, W)
    t = dtiles.reshape(4, 2 * W + 1, rpb, GRID_W, rpb, GRID_W)
    return jnp.einsum("dabr,hdbkaq,qkc->hrc", rmap, t, cmap, precision=lax.Precision.HIGHEST)


def _mixer_cfg(S):
    ta, tb, td = (min(S, ATT_TILE[k]) for k in "abd")
    cq, ck = (min(S, t) for t in ATT_TILE["c"])
    bq, bk = (min(S, t) for t in ATT_TILE["c_bwd"])
    a = _Mixer(ta, ta, _band(128, ta), 2)
    b = _Mixer(tb, tb, _band(1024, tb), 1, hp=2)
    d = _Mixer(td, td, -(-7 // (td // GRID_W)), 1, _edge_blocks_d(td), hp=2)
    return {"a": (a, a, jnp.asarray(_mask_tiles_a(ta))), "b": (b, b, jnp.asarray(_mask_tiles_b(tb))),
            "c": (_Mixer(cq, ck, None, 2), _Mixer(bq, bk, None, 2), None), "d": (d, d, None)}


def _layer_fwd(l, x, c8, lw, p, tabs, next_shards, me, late=None):
    S, D = x.shape
    dq = D // N_SHARD
    ada_sh, win_sh, wgm_sh, wb_sh, wout_sh = lw
    one_d, axial = tabs
    cfg = _mixer_cfg(S)
    tm = min(S, MM_ROWS)

    ada = _matmul(f"ada_l{l}", (N_SHARD,), [c8, ada_sh, p["b_ada"][l][None, :]],
                  [pl.BlockSpec((8, D), lambda j: (0, 0)), pl.BlockSpec((None, D, 3 * dq), lambda j: (j, 0, 0)),
                   pl.BlockSpec((1, 3 * dq), lambda j: (0, j))],
                  [SDS((8, 3 * D), F32)], [pl.BlockSpec((8, 3 * dq), lambda j: (0, j))],
                  epilogue=lambda acc, b_ref: (acc + b_ref[...],), a_fn=lambda a: a * _sigmoid(a))[0][0:1]
    shift, scale, gate = ada[:, :D], ada[:, D:2 * D], ada[:, 2 * D:]
    g_row = p["norm_g"][l][None, :]
    h = _norm_mod(f"norm_mod_l{l}", x, g_row, scale, shift)

    u = _matmul(f"mm_in_l{l}", (S // tm, N_SHARD), [h, win_sh],
                [pl.BlockSpec((tm, D), lambda i, j: (i, 0)), pl.BlockSpec((None, D, IN_SHARD), lambda i, j: (j, 0, 0))],
                [SDS((S, IN_COLS), BF16)], [pl.BlockSpec((tm, IN_SHARD), lambda i, j: (i, j))])[0]

    qa = _rope_fwd(f"rope_qa_l{l}", u, QCOL[0], 4, one_d, 64)
    ka = _rope_fwd(f"rope_ka_l{l}", u, KCOL[0], 2, one_d, 64)
    qb = _rope_fwd(f"rope_qb_l{l}", u, QCOL[1], 4, one_d, 64)
    kb = _rope_fwd(f"rope_kb_l{l}", u, KCOL[1], 4, one_d, 64)
    qc = _rope_fwd(f"rope_qc_l{l}", u, QCOL[2], 4, axial, 32, p["c_q_norm"][l][None, :])
    kc = _rope_fwd(f"rope_kc_l{l}", u, KCOL[2], 2, axial, 32, p["c_k_norm"][l][None, :])

    loads = [[], [], [], []]
    if next_shards is not None:
        s_ada, s_in, s_gm, s_wb, s_out = next_shards
        loads = [[s_wb, s_out], [s_in], [s_gm], [s_ada]]
    if late is not None:
        loads[0] += [late[1], late[2]]
        loads[2] += [late[0]]
    carries = [_Carry("gather", a) if a else None for a in loads]
    no_sink = jnp.full((4,), NEG_INF, F32)
    bias = _bias_tiles(p["d_rel_bias"][l], S, cfg["d"][0].tq)
    hd = HEAD_DIM
    kv_a = jnp.concatenate([ka, u[:, VCOL[0] * hd:(VCOL[0] + 2) * hd]], axis=1)
    kv_b = jnp.concatenate([kb, u[:, VCOL[1] * hd:(VCOL[1] + 4) * hd]], axis=1)
    kv_c = jnp.concatenate([kc, u[:, VCOL[2] * hd:(VCOL[2] + 2) * hd]], axis=1)
    kv_d = u[:, KCOL[3] * hd:(KCOL[3] + 8) * hd]
    qd = u[:, QCOL[3] * hd:(QCOL[3] + 4) * hd]
    br_a, o_a, lse_a, got_a = _attn_fwd(f"attn_a_l{l}", cfg["a"][0], qa, 0, kv_a, 0, kv_a, 2, u, GCOL[0],
                                        p["a_sink"][l], cfg["a"][2], carries[0])
    br_b, o_b, lse_b, got_b = _attn_fwd(f"attn_b_l{l}", cfg["b"][0], qb, 0, kv_b, 0, kv_b, 4, u, GCOL[1], no_sink,
                                        cfg["b"][2], carries[1])
    br_c, o_c, lse_c, got_c = _attn_fwd(f"attn_c_l{l}", cfg["c"][0], qc, 0, kv_c, 0, kv_c, 2, u, GCOL[2], no_sink,
                                        None, carries[2])
    br_d, o_d, lse_d, got_d = _attn_fwd(f"attn_d_l{l}", cfg["d"][0], qd, 0, kv_d, 0, kv_d, 4, u, GCOL[3],
                                        no_sink, bias, carries[3])
    next_lw = None
    if next_shards is not None:
        next_lw = _own_slot([got_d[0], got_b[0], got_c[0], got_a[0], got_a[1]], next_shards, me)
    if late is not None:
        wgm_sh, wb_sh, wout_sh = _own_slot([got_c[-1], got_a[-2], got_a[-1]], late, me)
        lw = (ada_sh, win_sh, wgm_sh, wb_sh, wout_sh)
    br = jnp.concatenate([br_a, br_b, br_c, br_d], axis=1)
    o_all = jnp.concatenate([o_a, o_b, o_c, o_d], axis=1)

    def merge_body(h_ref, wg_ref, br_ref, wb_ref, mg_ref, pj_ref, z_ref, acc_ref):
        n = pl.program_id(2)
        mgv = _sigmoid(lax.dot_general(h_ref[...], wg_ref[...], _NN, preferred_element_type=F32))
        pj = lax.dot_general(br_ref[...], wb_ref[...], _NN, preferred_element_type=F32)
        mg_ref[...] = mgv.astype(BF16)
        pj_ref[...] = pj.astype(BF16)

        @pl.when(n == 0)
        def _():
            acc_ref[...] = mgv * pj

        @pl.when(n > 0)
        def _():
            acc_ref[...] += mgv * pj

        @pl.when(n == 3)
        def _():
            z_ref[...] = acc_ref[...].astype(BF16)

    wide = pl.BlockSpec((tm, dq), lambda i, j, n: (i, n * N_SHARD + j))
    mg, proj, z = _call(
        merge_body, name=f"merge_l{l}", grid=(S // tm, N_SHARD, 4),
        in_specs=[pl.BlockSpec((tm, D), lambda i, j, n: (i, 0)),
                  pl.BlockSpec((None, D, dq), lambda i, j, n: (n, 0, j)),
                  pl.BlockSpec((tm, BRANCH_W), lambda i, j, n: (i, n)),
                  pl.BlockSpec((None, None, BRANCH_W, dq), lambda i, j, n: (j, n, 0, 0))],
        out_specs=[wide, wide, pl.BlockSpec((tm, dq), lambda i, j, n: (i, j))],
        out_shape=[SDS((S, 4 * D), BF16), SDS((S, 4 * D), BF16), SDS((S, D), BF16)],
        scratch=[pltpu.VMEM((tm, dq), F32)], sem=("parallel", "parallel", "arbitrary"))(h, wgm_sh, br, wb_sh)

    tn = min(D, 1024)
    x_new, o2 = _matmul(
        f"mm_out_l{l}", (S // tm, D // tn), [z, wout_sh, x, gate],
        [pl.BlockSpec((tm, D), lambda i, j: (i, 0)), pl.BlockSpec((N_SHARD, dq, tn), lambda i, j: (0, 0, j)),
         pl.BlockSpec((tm, tn), lambda i, j: (i, j)), pl.BlockSpec((1, tn), lambda i, j: (0, j))],
        [SDS((S, D), F32), SDS((S, D), BF16)],
        [pl.BlockSpec((tm, tn), lambda i, j: (i, j)), pl.BlockSpec((tm, tn), lambda i, j: (i, j))],
        k_inner=N_SHARD, epilogue=lambda acc, x_ref, g_ref: (x_ref[...] + g_ref[...] * acc, acc))
    res = dict(x=x, h=h, u=u, qa=qa, qb=qb, qc=qc, qd=qd, kv=(kv_a, kv_b, kv_c, kv_d), br=br, o_all=o_all,
               lse=(lse_a, lse_b, lse_c, lse_d), bias=bias, mg=mg, proj=proj, z=z, o2=o2,
               g_row=g_row, scale=scale, gate=gate)
    return x_new, res, (ada_sh, win_sh, wgm_sh, wb_sh, wout_sh), next_lw


def _layer_bwd(l, dxo, r, lw, p, tabs, pending):
    x, h, u = r["x"], r["h"], r["u"]
    S, D = x.shape
    dq = D // N_SHARD
    ada_sh, win_sh, wgm_sh, wb_sh, wout_sh = lw
    one_d, axial = tabs
    cfg = _mixer_cfg(S)
    tm = min(S, MM_ROWS)
    tk = min(S, 1024)
    tn = min(D, 1024)

    do2, dgate = _out_bwd_ew(f"out_bwd_l{l}", dxo, r["o2"], r["gate"])
    dz = _matmul(f"mm_dz_l{l}", (S // tm, N_SHARD), [do2, wout_sh],
                 [pl.BlockSpec((tm, D), lambda i, n: (i, 0)), pl.BlockSpec((None, dq, D), lambda i, n: (n, 0, 0))],
                 [SDS((S, D), BF16)], [pl.BlockSpec((tm, dq), lambda i, n: (i, n))], tb=True)[0]
    tkw = min(S, 2048)
    g_out = _matmul(f"mm_gwout_l{l}", (N_SHARD, D // tn, S // tkw), [r["z"], do2],
                    [pl.BlockSpec((tkw, dq), lambda n, j, k: (k, n)), pl.BlockSpec((tkw, tn), lambda n, j, k: (k, j))],
                    [SDS((N_SHARD, dq, D), F32)], [pl.BlockSpec((None, dq, tn), lambda n, j, k: (n, 0, j))],
                    ta=True, k_axis=2, acc_shape=(dq, tn))[0]

    dmg, dproj = _merge_bwd_ew(f"merge_bwd_l{l}", dz, r["mg"], r["proj"])
    nj = D // tn
    g_gm = _matmul(f"mm_gwgm_l{l}", (4, D // tn, nj, S // tk), [h, dmg],
                   [pl.BlockSpec((tk, tn), lambda n, i, j, k: (k, i)),
                    pl.BlockSpec((tk, tn), lambda n, i, j, k: (k, n * nj + j))],
                   [SDS((4, D, D), F32)], [pl.BlockSpec((None, tn, tn), lambda n, i, j, k: (n, i, j))],
                   ta=True, k_axis=3, acc_shape=(tn, tn))[0]
    dh1 = _matmul(f"mm_dh1_l{l}", (S // tm, nj, 4), [dmg, wgm_sh],
                  [pl.BlockSpec((tm, D), lambda i, j, n: (i, n)),
                   pl.BlockSpec((None, tn, D), lambda i, j, n: (n, j, 0))],
                  [SDS((S, D), F32)], [pl.BlockSpec((tm, tn), lambda i, j, kk: (i, j))],
                  tb=True, k_axis=2, acc_shape=(tm, tn))[0]
    dbr = _matmul(f"mm_dbr_l{l}", (S // tm, 4), [dproj, wb_sh],
                  [pl.BlockSpec((tm, D), lambda i, n: (i, n)),
                   pl.BlockSpec((N_SHARD, None, BRANCH_W, dq), lambda i, n: (0, n, 0, 0))],
                  [SDS((S, 4 * BRANCH_W), BF16)], [pl.BlockSpec((tm, BRANCH_W), lambda i, n: (i, n))],
                  tb=True, k_inner=N_SHARD)[0]
    tkl = min(S, 2048)
    g_wb = _matmul(f"mm_gwb_l{l}", (N_SHARD, 4, S // tkl), [r["br"], dproj],
                   [pl.BlockSpec((tkl, BRANCH_W), lambda j, n, k: (k, n)),
                    pl.BlockSpec((tkl, dq), lambda j, n, k: (k, n * N_SHARD + j))],
                   [SDS((N_SHARD, 4, BRANCH_W, dq), F32)],
                   [pl.BlockSpec((None, None, BRANCH_W, dq), lambda j, n, k: (j, n, 0, 0))],
                   ta=True, k_axis=2, acc_shape=(BRANCH_W, dq))[0]

    do_all, dg_all, delta = _attn_bwd_pre(f"attn_pre_l{l}", dbr, r["o_all"], u)
    lse_a, lse_b, lse_c, lse_d = r["lse"]
    dsink = _sink_grad(f"sink_grad_l{l}", p["a_sink"][l], lse_a, delta)[:, 0, 0]
    carries = [None] * 3
    if pending is not None:
        p_in, p_gm, p_wb, p_out = pending
        carries = [_Carry("exchange", [p_in]), _Carry("exchange", [p_gm]), _Carry("exchange", [p_wb, p_out])]
    kv_a, kv_b, kv_c, kv_d = r["kv"]
    dqa, dka, dva, _, _ = _attn_bwd(f"attn_a_bwd_l{l}", cfg["a"][1], r["qa"], 0, kv_a, 0, kv_a, 2, do_all, 0,
                                    lse_a, delta, cfg["a"][2])
    dqb, dkb, dvb, _, got_b = _attn_bwd(f"attn_b_bwd_l{l}", cfg["b"][1], r["qb"], 0, kv_b, 0, kv_b, 4, do_all, 4,
                                        lse_b, delta, cfg["b"][2], carry=carries[0])
    dqc, dkc, dvc, _, got_c = _attn_bwd(f"attn_c_bwd_l{l}", cfg["c"][1], r["qc"], 0, kv_c, 0, kv_c, 2, do_all, 8,
                                        lse_c, delta, carry=carries[1])
    dqd, dkd, dvd, dbias, got_d = _attn_bwd(f"attn_d_bwd_l{l}", cfg["d"][1], r["qd"], 0, kv_d, 0, kv_d, 4,
                                            do_all, 12, lse_d, delta, r["bias"], True, carries[2])
    arrived = None if pending is None else [got_b[0], got_c[0], got_d[0], got_d[1]]
    d_rel = _bias_tiles_t(dbias, cfg["d"][1].tq)

    duqa, _ = _rope_bwd(f"rope_qa_bwd_l{l}", dqa, u, QCOL[0], 4, one_d, 64)
    duka, _ = _rope_bwd(f"rope_ka_bwd_l{l}", dka, u, KCOL[0], 2, one_d, 64)
    duqb, _ = _rope_bwd(f"rope_qb_bwd_l{l}", dqb, u, QCOL[1], 4, one_d, 64)
    dukb, _ = _rope_bwd(f"rope_kb_bwd_l{l}", dkb, u, KCOL[1], 4, one_d, 64)
    duqc, dcq = _rope_bwd(f"rope_qc_bwd_l{l}", dqc, u, QCOL[2], 4, axial, 32, p["c_q_norm"][l][None, :])
    dukc, dck = _rope_bwd(f"rope_kc_bwd_l{l}", dkc, u, KCOL[2], 2, axial, 32, p["c_k_norm"][l][None, :])
    bw = BRANCH_W
    du = jnp.concatenate(
        [duqa, duka, dva.astype(BF16), dg_all[:, 0:bw],
         duqb, dukb, dvb.astype(BF16), dg_all[:, bw:2 * bw],
         duqc, dukc, dvc.astype(BF16), dg_all[:, 2 * bw:3 * bw],
         dqd.astype(BF16), dkd.astype(BF16), dvd.astype(BF16), dg_all[:, 3 * bw:]], axis=1)

    tmi = min(D, 1024)
    g_in = _matmul(f"mm_gwin_l{l}", (N_SHARD, D // tmi, S // tk), [h, du],
                   [pl.BlockSpec((tk, tmi), lambda j, i, k: (k, i)), pl.BlockSpec((tk, IN_SHARD), lambda j, i, k: (k, j))],
                   [SDS((N_SHARD, D, IN_SHARD), F32)], [pl.BlockSpec((None, tmi, IN_SHARD), lambda j, i, k: (j, i, 0))],
                   ta=True, k_axis=2, acc_shape=(tmi, IN_SHARD))[0]
    dh2 = _matmul(f"mm_dh2_l{l}", (S // tm, nj, N_SHARD), [du, win_sh],
                  [pl.BlockSpec((tm, IN_SHARD), lambda i, j, k: (i, k)),
                   pl.BlockSpec((None, tn, IN_SHARD), lambda i, j, k: (k, j, 0))],
                  [SDS((S, D), F32)], [pl.BlockSpec((tm, tn), lambda i, j, k: (i, j))],
                  tb=True, k_axis=2, acc_shape=(tm, tn))[0]

    dx_prev, dshift, dscale, dng = _norm_mod_bwd(f"norm_mod_bwd_l{l}", x, dh1, dh2, dxo, r["g_row"], r["scale"])
    d_ada = jnp.concatenate([dshift, dscale, dgate], axis=1)[0]
    big = (g_in, g_gm, g_wb, g_out)
    small = dict(norm_g=dng[0], b_ada=d_ada, a_sink=dsink, c_q_norm=dcq[0], c_k_norm=dck[0], d_rel_bias=d_rel)
    return dx_prev, big, small, arrived


def _place():
    return lax.axis_index("x"), lax.axis_index("y"), lax.axis_index("c")


class _Carry:
    def __init__(self, kind, arrays):
        self.kind, self.n, self.ins = kind, len(arrays), list(arrays)
        any_spec = pl.BlockSpec(memory_space=pl.ANY)
        self.in_specs = [any_spec] * self.n
        self.out_specs = [any_spec] * self.n
        if kind == "gather":
            self.out_shape = [SDS((N_SHARD, *a.shape), a.dtype) for a in arrays]
        else:
            self.out_shape = [SDS((3, *a.shape[1:]), a.dtype) for a in arrays]
        self.scratch = [pltpu.SemaphoreType.DMA((self.n, 3)), pltpu.SemaphoreType.DMA((self.n, 3))]

    def _copies(self, ins, outs, send_sems, recv_sems, arriving):
        x, y, c = _place()
        cps = []
        for a in range(self.n):
            for k, (px, py) in enumerate([(1 - x, y), (x, 1 - y), (1 - x, 1 - y)]):
                if self.kind == "gather":
                    src, dst = ins[a], outs[a].at[2 * px + py if arriving else 2 * x + y]
                else:
                    src, dst = ins[a].at[2 * px + py], outs[a].at[k]
                cps.append(pltpu.make_async_remote_copy(src, dst, send_sems.at[a, k], recv_sems.at[a, k],
                                                        device_id=(px, py, c), device_id_type=MESH))
        return cps

    def start(self, ins, outs, send_sems, recv_sems):
        for cp in self._copies(ins, outs, send_sems, recv_sems, False):
            cp.start()

    def finish(self, ins, outs, send_sems, recv_sems):
        for cp in self._copies(ins, outs, send_sems, recv_sems, True):
            cp.wait_recv()
        for cp in self._copies(ins, outs, send_sems, recv_sems, False):
            cp.wait_send()


def _run_carry(name, carry):
    n = carry.n

    def body(*refs):
        args = (refs[:n], refs[n:2 * n], refs[2 * n], refs[2 * n + 1])
        carry.start(*args)
        carry.finish(*args)

    return pl.pallas_call(body, name=name, in_specs=carry.in_specs, out_specs=carry.out_specs,
                          out_shape=carry.out_shape, scratch_shapes=carry.scratch)(*carry.ins)


def _own_slot(gathered, shards, me):
    return [lax.dynamic_update_index_in_dim(g, s, me, 0) for g, s in zip(gathered, shards)]


def _gather_small(name, v):
    m_per, n = v.shape

    def body(x_ref, out_ref, send_sems, recv_sems, local_sem):
        x, y, c = _place()
        me, sibling = (x, y, c), (x, y, 1 - c)
        chips = [(1 - x, y), (x, 1 - y), (1 - x, 1 - y)]

        def rows(px, py, pc):
            return out_ref.at[pl.ds((4 * px + 2 * py + pc) * m_per, m_per), :]

        def copy(k, block, to, src=None):
            return pltpu.make_async_remote_copy(
                src_ref=rows(*block) if src is None else src, dst_ref=rows(*block),
                send_sem=send_sems.at[k], recv_sem=recv_sems.at[k], device_id=to, device_id_type=MESH)

        mine = pltpu.make_async_copy(x_ref, rows(*me), local_sem)
        mine.start()
        first = [copy(0, me, sibling, src=x_ref)]
        first += [copy(1 + j, me, (*chip, c), src=x_ref) for j, chip in enumerate(chips)]
        for cp in first:
            cp.start()
        passed = [copy(4 + j, (*chip, c), sibling) for j, chip in enumerate(chips)]
        for j, chip in enumerate(chips):
            copy(1 + j, (*chip, c), me).wait_recv()
            passed[j].start()
        copy(0, sibling, me).wait_recv()
        for j, chip in enumerate(chips):
            copy(4 + j, (*chip, 1 - c), me).wait_recv()
        for cp in first + passed:
            cp.wait_send()
        mine.wait()

    return pl.pallas_call(
        body, name=name, out_shape=SDS((8 * m_per, n), v.dtype),
        in_specs=[pl.BlockSpec(memory_space=pltpu.VMEM)], out_specs=pl.BlockSpec(memory_space=pltpu.VMEM),
        scratch_shapes=[pltpu.SemaphoreType.DMA((7,)), pltpu.SemaphoreType.DMA((7,)), pltpu.SemaphoreType.DMA])(v)


def _pair_send_half(name, grads):
    n = len(grads)

    def body(*refs):
        ins, outs = refs[:n], refs[n:2 * n]
        send_sems, recv_sems = refs[2 * n:]
        x, y, c = _place()
        cps = []
        for a in range(n):
            cp = pltpu.make_async_remote_copy(ins[a].at[:, 1 - c], outs[a], send_sems.at[a], recv_sems.at[a],
                                              device_id=(x, y, 1 - c), device_id_type=MESH)
            cp.start()
            cps.append(cp)
        for cp in cps:
            cp.wait_recv()
        for cp in cps:
            cp.wait_send()

    any_spec = pl.BlockSpec(memory_space=pl.ANY)
    return pl.pallas_call(
        body, name=name, in_specs=[any_spec] * n, out_specs=[any_spec] * n,
        out_shape=[SDS((g.shape[0], *g.shape[2:]), g.dtype) for g in grads],
        scratch_shapes=[pltpu.SemaphoreType.DMA((n,)), pltpu.SemaphoreType.DMA((n,))])(*grads)


def _pair_gather(name, halves):
    n = len(halves)

    def body(*refs):
        outs = refs[n:2 * n]
        send_sems, recv_sems = refs[2 * n:]
        x, y, c = _place()
        cps = [pltpu.make_async_remote_copy(outs[a].at[c], outs[a].at[c], send_sems.at[a], recv_sems.at[a],
                                            device_id=(x, y, 1 - c), device_id_type=MESH) for a in range(n)]
        for cp in cps:
            cp.start()
        for a in range(n):
            pltpu.make_async_remote_copy(outs[a].at[c], outs[a].at[1 - c], send_sems.at[a], recv_sems.at[a],
                                         device_id=(x, y, 1 - c), device_id_type=MESH).wait_recv()
        for cp in cps:
            cp.wait_send()

    any_spec = pl.BlockSpec(memory_space=pl.ANY)
    return pl.pallas_call(
        body, name=name, in_specs=[any_spec] * n, out_specs=[any_spec] * n,
        out_shape=[SDS(g.shape, g.dtype) for g in halves], input_output_aliases={a: a for a in range(n)},
        scratch_shapes=[pltpu.SemaphoreType.DMA((n,)), pltpu.SemaphoreType.DMA((n,))])(*halves)


def _add_half(name, g, recv, c_idx):
    _, _, R, C = g.shape
    tr = min(R, 256)

    def body(c_ref, g_ref, r_ref, o_ref):
        o_ref[...] = (g_ref[...] + r_ref[...]).astype(BF16)

    return _call(body, name=name, grid=(4, R // tr), nsp=1,
                 in_specs=[pl.BlockSpec((None, None, tr, C), lambda j, r, c_ref: (j, c_ref[0], r, 0)),
                           pl.BlockSpec((None, tr, C), lambda j, r, c_ref: (j, r, 0))],
                 out_specs=pl.BlockSpec((None, tr, C), lambda j, r, c_ref: (j, r, 0)),
                 out_shape=SDS((4, R, C), BF16), sem=("parallel", "parallel"))(c_idx, g, recv)


def _add_shards(name, part, recv, idx):
    _, R, C = part.shape
    tr = min(R, 256)

    def body(idx_ref, p_ref, r_ref, o_ref):
        o_ref[...] = (((p_ref[...].astype(F32) + r_ref[0].astype(F32)) + r_ref[1].astype(F32))
                      + r_ref[2].astype(F32))

    return _call(body, name=name, grid=(R // tr,), nsp=1,
                 in_specs=[pl.BlockSpec((None, tr, C), lambda r, idx_ref: (idx_ref[0], r, 0)),
                           pl.BlockSpec((3, tr, C), lambda r, idx_ref: (0, r, 0))],
                 out_specs=pl.BlockSpec((None, tr, C), lambda r, idx_ref: (idx_ref[1], r, 0)),
                 out_shape=SDS((2, R, C), F32), sem=("parallel",))(idx, part, recv)


def _pair_sum_layer(l, big, c_idx):
    views = []
    for g in big:
        rows = g.shape[-2] if g.ndim == 3 else g.shape[1] * g.shape[2]
        views.append(g.reshape(N_SHARD, 2, rows // 2, g.shape[-1]))
    recv1 = _pair_send_half(f"rs_pair_send_l{l}", views)
    return [_add_half(f"rs_add_half{a}_l{l}", v, r1, c_idx) for a, (v, r1) in enumerate(zip(views, recv1))]


def _finish_reduce_layer(l, big, parts, recv2, idx):
    halves = [_add_shards(f"rs_add_shards{a}_l{l}", pt, r2, idx) for a, (pt, r2) in enumerate(zip(parts, recv2))]
    full = _pair_gather(f"rs_pair_gather_l{l}", halves)
    return [f.reshape(g.shape[1:]) for f, g in zip(full, big)]


def _adamw_math(w, g, m, v):
    m = ADAM_B1 * m + (1.0 - ADAM_B1) * g
    v = ADAM_B2 * v + (1.0 - ADAM_B2) * (g * g)
    m_hat = m / (1.0 - ADAM_B1 ** ADAM_STEP)
    v_hat = v / (1.0 - ADAM_B2 ** ADAM_STEP)
    delta = -ADAM_LR * (m_hat / (jnp.sqrt(v_hat) + ADAM_EPS) + ADAM_WD * w)
    return delta, m, v


def _adamw(name, w, g, m, v):
    shape = w.shape
    C = shape[-1]
    R = int(np.prod(shape[:-1]))
    tr = min(R, 256)

    def body(w_ref, g_ref, m_ref, v_ref, d_ref, nm_ref, nv_ref):
        d, nm, nv = _adamw_math(w_ref[...], g_ref[...], m_ref[...], v_ref[...])
        d_ref[...] = d
        nm_ref[...] = nm
        nv_ref[...] = nv

    blk = pl.BlockSpec((tr, C), lambda i: (i, 0))
    outs = _call(body, name=name, grid=(R // tr,), in_specs=[blk] * 4, out_specs=[blk] * 3,
                 out_shape=[SDS((R, C), F32)] * 3, sem=("parallel",))(*(a.reshape(R, C) for a in (w, g, m, v)))
    return [o.reshape(shape) for o in outs]


def _adamw_small(name, w, g8, m, v):
    R = w.shape[0]

    def body(w_ref, g_ref, m_ref, v_ref, go_ref, d_ref, nm_ref, nv_ref):
        g = g_ref[0]
        for b in range(1, 8):
            g = g + g_ref[b]
        d, nm, nv = _adamw_math(w_ref[...], g, m_ref[...], v_ref[...])
        go_ref[...] = g
        d_ref[...] = d
        nm_ref[...] = nm
        nv_ref[...] = nv

    blk = pl.BlockSpec((R, 128), lambda i: (0, 0))
    return _call(body, name=name, grid=(1,), in_specs=[blk, pl.BlockSpec((8, R, 128), lambda i: (0, 0, 0)), blk, blk],
                 out_specs=[blk] * 4, out_shape=[SDS((R, 128), F32)] * 4, sem=("arbitrary",))(w, g8, m, v)


SMALL_NAMES = ("norm_g", "b_ada", "a_sink", "c_q_norm", "c_k_norm", "d_rel_bias", "final_g")


def _pack(parts, extra_rows=0):
    flat = jnp.concatenate([a.reshape(-1) for a in parts])
    rows = -(-flat.shape[0] // 128)
    rows = -(-rows // 8) * 8 + extra_rows
    return jnp.pad(flat, (0, rows * 128 - flat.shape[0])).reshape(rows, 128)


def _unpack(packed, like):
    flat = packed.reshape(-1)
    out, off = [], 0
    for a in like:
        out.append(flat[off:off + a.size].reshape(a.shape))
        off += a.size
    return out


def _device_step(x, c8, tgt, shards, p, me, c_idx):
    S = x.shape[0]
    L = len(shards)
    tabs = _rope_tables(S)
    first = _Carry("gather", shards[0][:2])
    lw = [(*_own_slot(_run_carry("gather_w_l0", first), shards[0][:2], me), None, None, None)]
    res = []
    for l in range(L):
        x, r, lw[l], nxt = _layer_fwd(l, x, c8, lw[l], p, tabs, shards[l + 1] if l + 1 < L else None, me,
                                      shards[0][2:] if l == 0 else None)
        res.append(r)
        lw.append(nxt)
    dx, dfg, loss = _final_loss("final_loss", x, tgt, p["final_g"][None, :])
    bigs, smalls, parts, arrived = [None] * L, [None] * L, [None] * L, [None] * L
    for l in reversed(range(L)):
        pending = parts[l + 1] if l + 1 < L else None
        dx, bigs[l], smalls[l], arr = _layer_bwd(l, dx, res[l], lw[l], p, tabs, pending)
        if pending is not None:
            arrived[l + 1] = arr
        parts[l] = _pair_sum_layer(l, bigs[l], c_idx)
    arrived[0] = _run_carry("rs_shard_exchange_l0", _Carry("exchange", parts[0]))
    idx = jnp.concatenate([jnp.reshape(me, (1,)).astype(I32), c_idx])
    reduced = [_finish_reduce_layer(l, bigs[l], parts[l], arrived[l], idx) for l in range(L)]
    return loss, dx, reduced, smalls, dfg[0]


def kernel(x, c, norm_g, w_ada, b_ada, w_in, a_sink, c_q_norm, c_k_norm, d_rel_bias, w_gate_merge, w_branch, w_out, final_g, loss_target, m_norm_g, m_w_ada, m_b_ada, m_w_in, m_a_sink, m_c_q_norm, m_c_k_norm, m_d_rel_bias, m_w_gate_merge, m_w_branch, m_w_out, m_final_g, v_norm_g, v_w_ada, v_b_ada, v_w_in, v_a_sink, v_c_q_norm, v_c_k_norm, v_d_rel_bias, v_w_gate_merge, v_w_branch, v_w_out, v_final_g):
    L, D = norm_g.shape
    dq = D // N_SHARD
    p = dict(norm_g=norm_g, b_ada=b_ada, a_sink=a_sink, c_q_norm=c_q_norm, c_k_norm=c_k_norm,
             d_rel_bias=d_rel_bias, final_g=final_g)
    xi, yi, ci = _place()
    c_idx = jnp.reshape(ci, (1,)).astype(I32)
    me = 2 * xi + yi

    shards = [[w_ada[l].astype(BF16), w_in[l].astype(BF16), w_gate_merge[l].astype(BF16),
               w_branch[l].astype(BF16), w_out[l].astype(BF16)] for l in range(L)]
    c8 = jnp.broadcast_to(c, (8, D))
    loss, grad_x, reduced, smalls, dfg = _device_step(x[0], c8, loss_target[0], shards, p, me, c_idx)
    loss = lax.psum(loss[0, 0], ("x", "y", "c"))

    small_parts = [jnp.stack([s[n] for s in smalls]) for n in SMALL_NAMES[:-1]] + [dfg]
    packed = _pack(small_parts + [c[0]])
    rows = packed.shape[0]
    g8 = _gather_small("gather_small", packed).reshape(8, rows, 128)
    small_w = [p[n] for n in SMALL_NAMES]
    small_m = [m_norm_g, m_b_ada, m_a_sink, m_c_q_norm, m_c_k_norm, m_d_rel_bias, m_final_g]
    small_v = [v_norm_g, v_b_ada, v_a_sink, v_c_q_norm, v_c_k_norm, v_d_rel_bias, v_final_g]
    pad_c = [jnp.zeros((D,), F32)]
    sg, sd, sm, sv = _adamw_small("adamw_small", _pack(small_w + pad_c), g8, _pack(small_m + pad_c),
                                  _pack(small_v + pad_c))
    sg, sd, sm, sv = (_unpack(a, small_w) for a in (sg, sd, sm, sv))

    n_small = sum(a.size for a in small_parts)
    flat8 = g8.reshape(8, rows * 128)
    c_all = flat8[:, n_small:n_small + D]
    dada_all = flat8[:, L * D:L * D + L * 3 * D].reshape(8, L, 3 * D)
    dada_mine = lax.dynamic_slice_in_dim(dada_all, (2 * xi + yi) * (3 * dq), 3 * dq, axis=2)
    tma = min(D, 1024)
    g_ada = jnp.stack([
        _matmul(f"mm_gwada_l{l}", (D // tma,), [c_all, dada_mine[:, l]],
                [pl.BlockSpec((8, tma), lambda i: (0, i)), pl.BlockSpec((8, 3 * dq), lambda i: (0, 0))],
                [SDS((D, 3 * dq), F32)], [pl.BlockSpec((tma, 3 * dq), lambda i: (i, 0))],
                ta=True, a_fn=lambda a: a * _sigmoid(a))[0] for l in range(L)])

    g_in, g_gm, g_wb, g_out = (jnp.stack([reduced[l][a] for l in range(L)]) for a in range(4))

    big = {}
    for nm, w, g, m, v in (("w_ada", w_ada, g_ada, m_w_ada, v_w_ada), ("w_in", w_in, g_in, m_w_in, v_w_in),
                           ("w_gate_merge", w_gate_merge, g_gm, m_w_gate_merge, v_w_gate_merge),
                           ("w_branch", w_branch, g_wb, m_w_branch, v_w_branch),
                           ("w_out", w_out, g_out, m_w_out, v_w_out)):
        big[nm] = (g, *_adamw(f"adamw_{nm}", w, g, m, v))

    order = ("norm_g", "w_ada", "b_ada", "w_in", "a_sink", "c_q_norm", "c_k_norm", "d_rel_bias",
             "w_gate_merge", "w_branch", "w_out", "final_g")
    cols = [[], [], [], []]
    for nm in order:
        if nm in big:
            vals = big[nm]
        else:
            k = SMALL_NAMES.index(nm)
            vals = (sg[k], sd[k], sm[k], sv[k])
        for col, val in zip(cols, vals):
            col.append(val)
    return (loss, grad_x[None], *cols[0], *cols[1], *cols[2], *cols[3])
```

```python
import functools

import numpy as np
import jax
import jax.numpy as jnp
from jax import lax
from jax.experimental import pallas as pl
from jax.experimental.pallas import tpu as pltpu

F32 = jnp.float32
BF16 = jnp.bfloat16
I32 = jnp.int32
SDS = jax.ShapeDtypeStruct
MESH = pl.DeviceIdType.MESH

HEAD_DIM = 128
GRID_W = 64
EPS = 1e-6
NEG_INF = -1e30
ROPE_THETA = 10000.0
SCALE = HEAD_DIM ** -0.5
LOG2E = 1.4426950408889634
SCALE_LOG2E = SCALE * LOG2E
N_SHARD = 4
BRANCH_W = 512
IN_COLS = 7168
IN_SHARD = IN_COLS // N_SHARD
QCOL = (0, 12, 28, 40)
KCOL = (4, 16, 32, 44)
VCOL = (6, 20, 34, 48)
GCOL = (8, 24, 36, 52)
KV_HEADS = (2, 4, 2, 4)

ADAM_LR = 0.001
ADAM_B1 = 0.9
ADAM_B2 = 0.999
ADAM_EPS = 1e-08
ADAM_WD = 0.01
ADAM_STEP = 10

V7X_VMEM_BYTES = 64 * 1024 * 1024
VMEM_LIMIT = V7X_VMEM_BYTES * 7 // 8

ATT_TILE = {"a": 256, "b": 512, "c": (512, 1024), "c_bwd": (1024, 512), "d": 256}
M_INIT = -1e20
MM_ROWS = 1024
ROW_TILE = 512
EW_ROWS = 256


def _band(reach, tile):
    return -(-reach // tile)


def _call(body, *, name, grid, in_specs, out_specs, out_shape, scratch=(), sem=None, nsp=0):
    params = pltpu.CompilerParams(dimension_semantics=sem, vmem_limit_bytes=VMEM_LIMIT)
    if nsp:
        gs = pltpu.PrefetchScalarGridSpec(num_scalar_prefetch=nsp, grid=grid, in_specs=in_specs,
                                          out_specs=out_specs, scratch_shapes=list(scratch))
        return pl.pallas_call(body, grid_spec=gs, out_shape=out_shape, name=name, compiler_params=params)
    return pl.pallas_call(body, grid=grid, in_specs=in_specs, out_specs=out_specs, out_shape=out_shape,
                          scratch_shapes=list(scratch), name=name, compiler_params=params)


def _call_carrying(body, carry, ins, *, name, grid, in_specs, out_specs, out_shape, scratch=(), sem=None):
    if carry is None:
        res = _call(body, name=name, grid=grid, in_specs=in_specs, out_specs=out_specs, out_shape=out_shape,
                    scratch=scratch, sem=sem)(*ins)
        return list(res), []
    n_in, nc = len(in_specs), carry.n

    def wrapped(*refs):
        cr = (refs[n_in:n_in + nc], refs[n_in + nc:n_in + 2 * nc], refs[-2], refs[-1])
        ids = [pl.program_id(ax) for ax in range(len(grid))]
        first = functools.reduce(lambda a, b: a & b, [i == 0 for i in ids])
        last = functools.reduce(lambda a, b: a & b, [i == g - 1 for i, g in zip(ids, grid)])
        pl.when(first)(lambda: carry.start(*cr))
        body(*refs[:n_in], *refs[n_in + 2 * nc:-2])
        pl.when(last)(lambda: carry.finish(*cr))

    res = _call(wrapped, name=name, grid=grid, in_specs=list(in_specs) + carry.in_specs,
                out_specs=carry.out_specs + list(out_specs), out_shape=carry.out_shape + list(out_shape),
                scratch=list(scratch) + carry.scratch, sem=("arbitrary",) * len(grid))(*ins, *carry.ins)
    return list(res[nc:]), list(res[:nc])


def _sigmoid(x):
    return 1.0 / (1.0 + jnp.exp(-x))


def _matmul(name, grid, ins, in_specs, out_shape, out_specs, *, ta=False, tb=False, k_axis=None,
            acc_shape=None, epilogue=None, a_fn=None, k_inner=None, carry=None):
    n_in = len(ins)
    n_out = len(out_shape)
    nk = grid[k_axis] if k_axis is not None else 1
    dn = (((0 if ta else 1,), (1 if tb else 0,)), ((), ()))

    def body(*refs):
        a = refs[0][...]
        if a_fn is not None:
            a = a_fn(a)
        a = a.astype(BF16)
        if k_inner is None:
            p = lax.dot_general(a, refs[1][...].astype(BF16), dn, preferred_element_type=F32)
        else:
            ck = a.shape[1] // k_inner
            p = None
            for kk in range(k_inner):
                t = lax.dot_general(a[:, kk * ck:(kk + 1) * ck], refs[1][kk].astype(BF16), dn,
                                    preferred_element_type=F32)
                p = t if p is None else p + t
        extra = refs[2:n_in]
        outs = refs[n_in:n_in + n_out]

        def fin(acc):
            vals = epilogue(acc, *extra) if epilogue is not None else (acc,)
            for o_ref, v in zip(outs, vals):
                o_ref[...] = v.astype(o_ref.dtype)

        if k_axis is None:
            fin(p)
        else:
            acc_ref = refs[-1]
            k = pl.program_id(k_axis)

            @pl.when(k == 0)
            def _():
                acc_ref[...] = p

            @pl.when(k > 0)
            def _():
                acc_ref[...] += p

            @pl.when(k == nk - 1)
            def _():
                fin(acc_ref[...])

    sem = tuple("arbitrary" if ax == k_axis else "parallel" for ax in range(len(grid)))
    scratch = [pltpu.VMEM(acc_shape, F32)] if k_axis is not None else []
    res, got = _call_carrying(body, carry, ins, name=name, grid=grid, in_specs=in_specs, out_specs=out_specs,
                              out_shape=out_shape, scratch=scratch, sem=sem)
    return res if carry is None else (res, got)


def _norm_mod(name, x, g, scale, shift):
    S, D = x.shape
    ts = min(S, EW_ROWS)

    def body(x_ref, g_ref, sc_ref, sh_ref, h_ref):
        xv = x_ref[...]
        r = lax.rsqrt(jnp.mean(xv * xv, axis=-1, keepdims=True) + EPS)
        h_ref[...] = (((xv * r) * g_ref[...]) * (1.0 + sc_ref[...]) + sh_ref[...]).astype(BF16)

    row = pl.BlockSpec((1, D), lambda i: (0, 0))
    blk = pl.BlockSpec((ts, D), lambda i: (i, 0))
    return _call(body, name=name, grid=(S // ts,), in_specs=[blk, row, row, row], out_specs=blk,
                 out_shape=SDS((S, D), BF16), sem=("parallel",))(x, g, scale, shift)


def _norm_mod_bwd(name, x, dh1, dh2, dxo, g, scale):
    S, D = x.shape
    ts = min(S, EW_ROWS)

    def body(x_ref, a_ref, b_ref, dxo_ref, g_ref, sc_ref, dx_ref, dsh_ref, dsc_ref, dg_ref):
        @pl.when(pl.program_id(0) == 0)
        def _():
            dsh_ref[...] = jnp.zeros_like(dsh_ref)
            dsc_ref[...] = jnp.zeros_like(dsc_ref)
            dg_ref[...] = jnp.zeros_like(dg_ref)

        xv = x_ref[...]
        r = lax.rsqrt(jnp.mean(xv * xv, axis=-1, keepdims=True) + EPS)
        xh = xv * r
        dh = a_ref[...] + b_ref[...]
        gv = g_ref[...]
        one_sc = 1.0 + sc_ref[...]
        dsh_ref[...] += jnp.sum(dh, axis=0, keepdims=True)
        dsc_ref[...] += jnp.sum(dh * xh * gv, axis=0, keepdims=True)
        dg_ref[...] += jnp.sum(dh * xh * one_sc, axis=0, keepdims=True)
        dxh = dh * gv * one_sc
        dx = r * (dxh - xh * jnp.mean(dxh * xh, axis=-1, keepdims=True))
        dx_ref[...] = dxo_ref[...] + dx

    row = pl.BlockSpec((1, D), lambda i: (0, 0))
    blk = pl.BlockSpec((ts, D), lambda i: (i, 0))
    return _call(body, name=name, grid=(S // ts,), in_specs=[blk, blk, blk, blk, row, row],
                 out_specs=[blk, row, row, row],
                 out_shape=[SDS((S, D), F32), SDS((1, D), F32), SDS((1, D), F32), SDS((1, D), F32)],
                 sem=("arbitrary",))(x, dh1, dh2, dxo, g, scale)


def _out_bwd_ew(name, dxo, o2, gate):
    S, D = dxo.shape
    ts = min(S, EW_ROWS)

    def body(dxo_ref, o2_ref, gt_ref, do2_ref, dgt_ref):
        @pl.when(pl.program_id(0) == 0)
        def _():
            dgt_ref[...] = jnp.zeros_like(dgt_ref)

        d = dxo_ref[...]
        do2_ref[...] = (d * gt_ref[...]).astype(BF16)
        dgt_ref[...] += jnp.sum(d * o2_ref[...].astype(F32), axis=0, keepdims=True)

    row = pl.BlockSpec((1, D), lambda i: (0, 0))
    blk = pl.BlockSpec((ts, D), lambda i: (i, 0))
    return _call(body, name=name, grid=(S // ts,), in_specs=[blk, blk, row], out_specs=[blk, row],
                 out_shape=[SDS((S, D), BF16), SDS((1, D), F32)], sem=("arbitrary",))(dxo, o2, gate)


def _merge_bwd_ew(name, dz, mg, proj):
    S, D = dz.shape
    ts = min(S, ROW_TILE)
    td = min(D, 512)
    nd = D // td

    def body(dz_ref, mg_ref, pj_ref, dmg_ref, dpj_ref):
        d = dz_ref[...].astype(F32)
        m = mg_ref[...].astype(F32)
        dmg_ref[...] = (d * pj_ref[...].astype(F32) * m * (1.0 - m)).astype(BF16)
        dpj_ref[...] = (d * m).astype(BF16)

    wide = pl.BlockSpec((ts, td), lambda i, j, n: (i, n * nd + j))
    return _call(body, name=name, grid=(S // ts, nd, 4),
                 in_specs=[pl.BlockSpec((ts, td), lambda i, j, n: (i, j)), wide, wide],
                 out_specs=[wide, wide], out_shape=[SDS((S, 4 * D), BF16), SDS((S, 4 * D), BF16)],
                 sem=("parallel", "parallel", "arbitrary"))(dz, mg, proj)


def _final_loss(name, x, tgt, g):
    S, D = x.shape
    ts = min(S, EW_ROWS)

    def body(x_ref, t_ref, g_ref, dx_ref, dg_ref, loss_ref):
        @pl.when(pl.program_id(0) == 0)
        def _():
            dg_ref[...] = jnp.zeros_like(dg_ref)
            loss_ref[...] = jnp.zeros_like(loss_ref)

        xv = x_ref[...]
        r = lax.rsqrt(jnp.mean(xv * xv, axis=-1, keepdims=True) + EPS)
        xh = xv * r
        gv = g_ref[...]
        err = xh * gv - t_ref[...]
        row_loss = jnp.mean(err * err, axis=-1, keepdims=True)
        loss_ref[...] += 0.5 * jnp.sum(row_loss, axis=0, keepdims=True)
        dy = err * (1.0 / D)
        dg_ref[...] += jnp.sum(dy * xh, axis=0, keepdims=True)
        dxh = dy * gv
        dx_ref[...] = r * (dxh - xh * jnp.mean(dxh * xh, axis=-1, keepdims=True))

    row = pl.BlockSpec((1, D), lambda i: (0, 0))
    blk = pl.BlockSpec((ts, D), lambda i: (i, 0))
    return _call(body, name=name, grid=(S // ts,), in_specs=[blk, blk, row],
                 out_specs=[blk, row, pl.BlockSpec((1, 128), lambda i: (0, 0))],
                 out_shape=[SDS((S, D), F32), SDS((1, D), F32), SDS((1, 128), F32)],
                 sem=("arbitrary",))(x, tgt, g)


def _rope_tables(S):
    def tables(pos, dim):
        inv = ROPE_THETA ** (-jnp.arange(0, dim, 2, dtype=F32) / dim)
        ang = pos.astype(F32)[:, None] * inv[None, :]
        ang = jnp.concatenate([ang, ang], axis=-1)
        return jnp.cos(ang), jnp.sin(ang)

    pos = jnp.arange(S, dtype=I32)
    lane = np.arange(HEAD_DIM)
    cos1, sin1 = tables(pos, HEAD_DIM)
    up1 = jnp.asarray((lane >= 64).astype(np.float32))[None, :]
    one_d = (cos1, sin1 * up1, -sin1 * (1.0 - up1))
    cr, sr = tables(pos // GRID_W, HEAD_DIM // 2)
    cc, sc = tables(pos % GRID_W, HEAD_DIM // 2)
    cos2 = jnp.concatenate([cr, cc], axis=-1)
    sin2 = jnp.concatenate([sr, sc], axis=-1)
    up2 = jnp.asarray(((lane % 64) >= 32).astype(np.float32))[None, :]
    axial = (cos2, sin2 * up2, -sin2 * (1.0 - up2))
    return one_d, axial


def _rope_fwd(name, src, c0, nb, tabs, sh, gain=None):
    S = src.shape[0]
    ts = min(S, ROW_TILE)
    has_gain = gain is not None

    assert c0 % nb == 0
    hd = HEAD_DIM

    def body(*refs):
        x_ref, c_ref, sa_ref, sb_ref = refs[:4]
        o_ref = refs[-1]
        cv, sa, sb = c_ref[...], sa_ref[...], sb_ref[...]
        for hh in range(nb):
            lanes = slice(hh * hd, (hh + 1) * hd)
            xv = x_ref[:, lanes].astype(F32)
            if has_gain:
                r = lax.rsqrt(jnp.mean(xv * xv, axis=-1, keepdims=True) + EPS)
                xv = (xv * r) * refs[4][...]
            out = xv * cv + pltpu.roll(xv, sh, 1) * sa + pltpu.roll(xv, hd - sh, 1) * sb
            o_ref[:, lanes] = out.astype(BF16)

    tab = pl.BlockSpec((ts, hd), lambda i: (i, 0))
    in_specs = [pl.BlockSpec((ts, nb * hd), lambda i: (i, c0 // nb)), tab, tab, tab]
    ins = [src, *tabs]
    if has_gain:
        in_specs.append(pl.BlockSpec((1, hd), lambda i: (0, 0)))
        ins.append(gain)
    return _call(body, name=name, grid=(S // ts,), in_specs=in_specs,
                 out_specs=pl.BlockSpec((ts, nb * hd), lambda i: (i, 0)),
                 out_shape=SDS((S, nb * hd), BF16), sem=("parallel",))(*ins)


def _rope_bwd(name, dout, src, c0, nb, tabs, sh, gain=None):
    S = src.shape[0]
    ts = min(S, ROW_TILE)
    has_gain = gain is not None

    assert c0 % nb == 0
    hd = HEAD_DIM

    def body(*refs):
        d_ref, x_ref, c_ref, sa_ref, sb_ref = refs[:5]
        cv, sa, sb = c_ref[...], sa_ref[...], sb_ref[...]
        if has_gain:
            gn_ref, dx_ref, dgn_ref = refs[5:]

            @pl.when(pl.program_id(0) == 0)
            def _():
                dgn_ref[...] = jnp.zeros_like(dgn_ref)
        else:
            dx_ref = refs[5]
        for hh in range(nb):
            lanes = slice(hh * hd, (hh + 1) * hd)
            d = d_ref[:, lanes].astype(F32)
            dxn = d * cv + pltpu.roll(d * sa, hd - sh, 1) + pltpu.roll(d * sb, sh, 1)
            if has_gain:
                xv = x_ref[:, lanes].astype(F32)
                r = lax.rsqrt(jnp.mean(xv * xv, axis=-1, keepdims=True) + EPS)
                xh = xv * r
                dgn_ref[...] += jnp.sum(dxn * xh, axis=0, keepdims=True)
                dxh = dxn * gn_ref[...]
                dx_ref[:, lanes] = (r * (dxh - xh * jnp.mean(dxh * xh, axis=-1, keepdims=True))).astype(BF16)
            else:
                dx_ref[:, lanes] = dxn.astype(BF16)

    tab = pl.BlockSpec((ts, hd), lambda i: (i, 0))
    own = pl.BlockSpec((ts, nb * hd), lambda i: (i, 0))
    in_specs = [own, pl.BlockSpec((ts, nb * hd), lambda i: (i, c0 // nb)), tab, tab, tab]
    ins = [dout, src, *tabs]
    out_specs = [own]
    out_shape = [SDS((S, nb * hd), BF16)]
    if has_gain:
        row = pl.BlockSpec((1, hd), lambda i: (0, 0))
        in_specs.append(row)
        ins.append(gain)
        out_specs.append(row)
        out_shape.append(SDS((1, hd), F32))
    res = _call(body, name=name, grid=(S // ts,), in_specs=in_specs, out_specs=out_specs,
                out_shape=out_shape, sem=("arbitrary",))(*ins)
    return res if has_gain else (res[0], None)


def _offset_grid(T, W):
    d = (np.arange(2 * W + 1) - W)[:, None, None] * T
    return d + np.arange(T)[None, :, None] - np.arange(T)[None, None, :]


def _with_off_tile(tiles):
    xp = np if isinstance(tiles, np.ndarray) else jnp
    off = xp.full((*tiles.shape[:2], 1, *tiles.shape[3:]), NEG_INF, tiles.dtype)
    return xp.concatenate([tiles, off], axis=2)


def _mask_tiles_a(T):
    dk = _offset_grid(T, _band(128, T))
    return _with_off_tile(np.where(np.abs(dk) <= 128, 0.0, NEG_INF).astype(np.float32)[None, None])


def _mask_tiles_b(T):
    dk = _offset_grid(T, _band(1024, T))
    ad = np.abs(dk)
    mult = ((ad <= 64).astype(np.float32) + ((ad <= 256) & (dk % 4 == 0)) + ((ad <= 1024) & (dk % 16 == 0)))
    return _with_off_tile(np.where(mult > 0, np.log(np.maximum(mult, 1.0)) / SCALE, NEG_INF)
                          .astype(np.float32)[None, None])


def _edge_blocks_d(T):
    return -(-4 // (T // GRID_W))


def _mask_tiles_d(S, T):
    rows, nq, rpb = S // GRID_W, S // T, T // GRID_W
    W, E = -(-7 // rpb), _edge_blocks_d(T)
    assert nq >= 2 * E + 1
    out = []
    for i in [*range(E), nq // 2, *range(nq - E, nq)]:
        kp = ((i + np.arange(2 * W + 1) - W) * T)[:, None, None] + np.arange(T)[None, :, None]
        qp = i * T + np.arange(T)[None, None, :]
        qr, qc, kr, kc = qp >> 6, qp & 63, kp >> 6, kp & 63
        rs = np.clip(qr - 4, 0, rows - 8)
        cs = np.clip(qc - 8, 0, GRID_W - 16)
        valid = (kr >= rs) & (kr < rs + 8) & (kc >= cs) & (kc < cs + 16)
        out.append(np.where(valid, 0.0, NEG_INF).astype(np.float32))
    return np.stack(out)[:, None]


def _variant(i, nq, E):
    if E == 0:
        return 0
    return jnp.where(i < E, i, jnp.where(i >= nq - E, i - (nq - 2 * E - 1), E))


_NT = (((1,), (1,)), ((), ()))
_TN = (((0,), (0,)), ((), ()))
_NN = (((1,), (0,)), ((), ()))


class _Mixer:
    def __init__(self, tq, tk, W, G, E=0, hp=1):
        self.tq, self.tk, self.W, self.G, self.E, self.hp = tq, tk, W, G, E, hp


def _attn_fwd(name, mx, q_arr, qc0, k_arr, kc0, v_arr, vc0, u, gc0, sink, bias=None, carry=None):
    S = q_arr.shape[0]
    tq, tk, W, G = mx.tq, mx.tk, mx.W, mx.G
    nq, nk = S // tq, S // tk
    nd = nk if W is None else 2 * W + 1
    has_bias = bias is not None
    nc = 0 if carry is None else carry.n
    hd = HEAD_DIM

    def jmap(i, d):
        return d if W is None else jnp.clip(i + d - W, 0, nk - 1)

    nin = 1 if W is None else nd
    ngd = nd if W is None else 1

    def body(*refs):
        sink_ref, q_ref, g_ref = refs[:3]
        k_refs, v_refs = refs[3:3 + nin], refs[3 + nin:3 + 2 * nin]
        n_in = 3 + 2 * nin + (1 if has_bias else 0)
        bias_ref = refs[n_in - 1] if has_bias else None
        out0 = n_in + 2 * nc
        br_ref, o_ref, lse_ref = refs[out0:out0 + 3]
        if W is None:
            m_s, l_s, acc_s = refs[out0 + 3:out0 + 6]
        i, d = pl.program_id(0), pl.program_id(1)
        if carry is not None:
            carry_refs = (refs[n_in:n_in + nc], refs[n_in + nc:out0], refs[-2], refs[-1])
            pl.when((i == 0) & (d == 0))(lambda: carry.start(*carry_refs))

        def scores(h, dd, tile):
            kv = slice((h // G) * hd, (h // G + 1) * hd)
            s = lax.dot_general(k_refs[dd][:, kv], q_ref[:, h * hd:(h + 1) * hd], _NT,
                                preferred_element_type=F32)
            if has_bias:
                s = s + bias_ref[_variant(i, nq, mx.E), h if per_head else 0, tile]
            return s

        def weighted(h, dd, p):
            kv = slice((h // G) * hd, (h // G + 1) * hd)
            return lax.dot_general(v_refs[dd][:, kv], p.astype(BF16), _TN, preferred_element_type=F32)

        def finish(h, m, l, acc):
            lanes = slice(h * hd, (h + 1) * hd)
            sk = sink_ref[h]
            m = m * SCALE
            mf = jnp.maximum(m, sk)
            a = jnp.exp(m - mf)
            lf = l * a + jnp.exp(sk - mf)
            o = ((acc * a) / lf).T
            gv = g_ref[:, lanes].astype(F32)
            o_ref[:, lanes] = o.astype(BF16)
            br_ref[:, lanes] = (o * (gv * _sigmoid(gv))).astype(BF16)
            lse_ref[h] = mf + jnp.log(lf)

        if W is None:
            @pl.when(d == 0)
            def _():
                m_s[...] = jnp.full_like(m_s, M_INIT)
                l_s[...] = jnp.zeros_like(l_s)
                acc_s[...] = jnp.zeros_like(acc_s)

            for h in range(4):
                s = scores(h, 0, d)
                m_prev = m_s[h]
                m_new = jnp.maximum(m_prev, jnp.max(s, axis=0, keepdims=True))
                alpha = jnp.exp2((m_prev - m_new) * SCALE_LOG2E)
                p = jnp.exp2((s - m_new) * SCALE_LOG2E)
                l_s[h] = alpha * l_s[h] + jnp.sum(p, axis=0, keepdims=True)
                acc_s[h] = alpha * acc_s[h] + weighted(h, 0, p)
                m_s[h] = m_new

            @pl.when(d == ngd - 1)
            def _():
                for h in range(4):
                    finish(h, m_s[h], l_s[h], acc_s[h])
        else:
            tiles = [jnp.where((i + dd - W >= 0) & (i + dd - W < nk), dd, nd) for dd in range(nd)]
            for h in range(4):
                ss = [scores(h, dd, tiles[dd]) for dd in range(nd)]
                top = ss[0]
                for s in ss[1:]:
                    top = jnp.maximum(top, s)
                m = jnp.max(top, axis=0, keepdims=True)
                ps = [jnp.exp2((s - m) * SCALE_LOG2E) for s in ss]
                l = sum(jnp.sum(p, axis=0, keepdims=True) for p in ps)
                acc = sum(weighted(h, dd, ps[dd]) for dd in range(nd))
                finish(h, m, l, acc)

        if carry is not None:
            pl.when((i == nq - 1) & (d == ngd - 1))(lambda: carry.finish(*carry_refs))

    n_kv = 4 // G
    assert qc0 % 4 == 0 and gc0 % 4 == 0 and kc0 % n_kv == 0 and vc0 % n_kv == 0

    def kv_spec(c0, dd):
        if W is None:
            return pl.BlockSpec((tk, n_kv * hd), lambda i, d: (d, c0 // n_kv))
        return pl.BlockSpec((tk, n_kv * hd), lambda i, d: (jnp.clip(i + dd - W, 0, nk - 1), c0 // n_kv))

    per_head = has_bias and bias.shape[1] == 4
    in_specs = [pl.BlockSpec(memory_space=pltpu.SMEM),
                pl.BlockSpec((tq, 4 * hd), lambda i, d: (i, qc0 // 4)),
                pl.BlockSpec((tq, 4 * hd), lambda i, d: (i, gc0 // 4)),
                *[kv_spec(kc0, dd) for dd in range(nin)], *[kv_spec(vc0, dd) for dd in range(nin)]]
    ins = [sink, q_arr, u, *[k_arr] * nin, *[v_arr] * nin]
    if has_bias:
        in_specs.append(pl.BlockSpec(bias.shape, lambda i, d: (0, 0, 0, 0, 0), pipeline_mode=pl.Buffered(1)))
        ins.append(bias)
    own = pl.BlockSpec((tq, 4 * hd), lambda i, d: (i, 0))
    out_specs = [own, own, pl.BlockSpec((4, 1, tq), lambda i, d: (0, 0, i))]
    out_shape = [SDS((S, 4 * hd), BF16), SDS((S, 4 * hd), BF16), SDS((4, 1, S), F32)]
    scratch = []
    if W is None:
        scratch = [pltpu.VMEM((4, 1, tq), F32), pltpu.VMEM((4, 1, tq), F32), pltpu.VMEM((4, hd, tq), F32)]
    sem = ("parallel", "arbitrary")
    if carry is not None:
        ins += carry.ins
        in_specs += carry.in_specs
        out_specs = carry.out_specs + out_specs
        out_shape = carry.out_shape + out_shape
        scratch += carry.scratch
        sem = ("arbitrary",) * 2
    res = _call(body, name=name, grid=(nq, ngd), in_specs=in_specs, out_specs=out_specs, out_shape=out_shape,
                scratch=scratch, sem=sem)(*ins)
    return (*res[nc:], list(res[:nc]))


def _attn_bwd(name, mx, q_arr, qc0, k_arr, kc0, v_arr, vc0, do_all, hb0, lse, delta, bias=None, want_dbias=False,
              carry=None):
    S = q_arr.shape[0]
    tq, tk, W, G = mx.tq, mx.tk, mx.W, mx.G
    nq, nk = S // tq, S // tk
    nd = nq if W is None else 2 * W + 1
    n_kv = 4 // G
    hd = HEAD_DIM
    has_bias = bias is not None
    nc = 0 if carry is None else carry.n
    assert qc0 % G == 0 and hb0 % G == 0 and (not want_dbias or (has_bias and G == 1)) and (W is None or tq == tk)

    def imap(j, d):
        return d if W is None else jnp.clip(j + d - W, 0, nq - 1)

    nin = 1 if W is None else nd
    ngd = nd if W is None else 1

    def body(*refs):
        k_ref, v_ref = refs[:2]
        q_refs, do_refs = refs[2:2 + nin], refs[2 + nin:2 + 2 * nin]
        lse_refs, dl_refs = refs[2 + 2 * nin:2 + 3 * nin], refs[2 + 3 * nin:2 + 4 * nin]
        n_in = 2 + 4 * nin + (1 if has_bias else 0)
        bias_ref = refs[n_in - 1] if has_bias else None
        out0 = n_in + 2 * nc
        dq_ref, dk_ref, dv_ref = refs[out0:out0 + 3]
        n_o = 4 if want_dbias else 3
        db_ref = refs[out0 + 3] if want_dbias else None
        if W is None:
            dk_s, dv_s = refs[out0 + n_o:out0 + n_o + 2]
        kv, j, d = pl.program_id(0), pl.program_id(1), pl.program_id(2)
        if carry is not None:
            carry_refs = (refs[n_in:n_in + nc], refs[n_in + nc:out0], refs[-2], refs[-1])
            pl.when((kv == 0) & (j == 0) & (d == 0))(lambda: carry.start(*carry_refs))

        @pl.when((j == 0) & (d == 0))
        def _():
            dq_ref[...] = jnp.zeros_like(dq_ref)
            if want_dbias:
                db_ref[...] = jnp.zeros_like(db_ref)

        def unit(kh, g, dd, i, tile):
            hh = kh * G + g
            lanes = slice(hh * hd, (hh + 1) * hd)
            k = k_ref[:, kh * hd:(kh + 1) * hd]
            v = v_ref[:, kh * hd:(kh + 1) * hd]
            q = q_refs[dd][:, lanes]
            do = do_refs[dd][:, lanes]
            s = lax.dot_general(k, q, _NT, preferred_element_type=F32)
            if has_bias:
                s = s + bias_ref[_variant(i, nq, mx.E), kh if per_head else 0, tile]
            p = jnp.exp2(s * SCALE_LOG2E - lse_refs[dd][hh] * LOG2E)
            dv = lax.dot_general(p.astype(BF16), do, _NN, preferred_element_type=F32)
            dp = lax.dot_general(v, do, _NT, preferred_element_type=F32)
            ds = p * (dp - dl_refs[dd][hh])
            if want_dbias:
                db_ref[kh, jnp.minimum(tile, nd - 1)] += ds
            dsb = ds.astype(BF16)
            dk = lax.dot_general(dsb, q, _NN, preferred_element_type=F32)
            row0 = pl.multiple_of(i * tq, tq)
            dq_ref[pl.ds(row0, tq), lanes] += lax.dot_general(dsb, k, _TN, preferred_element_type=F32) * SCALE
            return dk, dv

        if W is None:
            @pl.when(d == 0)
            def _():
                dk_s[...] = jnp.zeros_like(dk_s)
                dv_s[...] = jnp.zeros_like(dv_s)

            for kh in range(hp):
                for g in range(G):
                    dk, dv = unit(kh, g, 0, d, None)
                    dk_s[kh] += dk
                    dv_s[kh] += dv

            @pl.when(d == ngd - 1)
            def _():
                for kh in range(hp):
                    dk_ref[:, kh * hd:(kh + 1) * hd] = dk_s[kh] * SCALE
                    dv_ref[:, kh * hd:(kh + 1) * hd] = dv_s[kh]
        else:
            for kh in range(hp):
                parts = []
                for dd in range(nd):
                    i_dd = j + dd - W
                    tile = jnp.where((i_dd >= 0) & (i_dd < nq), 2 * W - dd, nd)
                    parts += [unit(kh, g, dd, jnp.clip(i_dd, 0, nq - 1), tile) for g in range(G)]
                dk_ref[:, kh * hd:(kh + 1) * hd] = sum(pt[0] for pt in parts) * SCALE
                dv_ref[:, kh * hd:(kh + 1) * hd] = sum(pt[1] for pt in parts)

        if carry is not None:
            pl.when((kv == n_kv // hp - 1) & (j == nk - 1) & (d == ngd - 1))(lambda: carry.finish(*carry_refs))

    hp = mx.hp
    hq = hp * G
    assert qc0 % hq == 0 and hb0 % hq == 0 and kc0 % hp == 0 and vc0 % hp == 0 and n_kv % hp == 0
    per_head = has_bias and bias.shape[1] == 4

    def q_spec(shape, col, dd, stat, row0=0):
        def index(kv, j, d):
            blk = d if W is None else jnp.clip(j + dd - W, 0, nq - 1)
            return (col + kv, 0, blk) if stat else (row0 + blk, col + kv)
        return pl.BlockSpec(shape, index)

    in_specs = [pl.BlockSpec((tk, hp * hd), lambda kv, j, d: (j, kc0 // hp + kv)),
                pl.BlockSpec((tk, hp * hd), lambda kv, j, d: (j, vc0 // hp + kv)),
                *[q_spec((tq, hq * hd), qc0 // hq, dd, False) for dd in range(nin)],
                *[q_spec((tq, hq * hd), 0, dd, False, (hb0 // 4) * nq) for dd in range(nin)],
                *[q_spec((hq, 1, tq), 0, dd, True) for dd in range(nin)],
                *[q_spec((hq, 1, tq), hb0 // hq, dd, True) for dd in range(nin)]]
    ins = [k_arr, v_arr, *[q_arr] * nin, *[do_all] * nin, *[lse] * nin, *[delta] * nin]
    if has_bias:
        in_specs.append(pl.BlockSpec((bias.shape[0], hp if per_head else 1, bias.shape[2], tk, tq),
                                     lambda kv, j, d: (0, kv if per_head else 0, 0, 0, 0)))
        ins.append(bias)
    kv_blk = pl.BlockSpec((tk, hp * hd), lambda kv, j, d: (j, kv))
    out_specs = [pl.BlockSpec((S, hq * hd), lambda kv, j, d: (0, kv)), kv_blk, kv_blk]
    out_shape = [SDS((S, 4 * hd), F32), SDS((S, n_kv * hd), F32), SDS((S, n_kv * hd), F32)]
    if want_dbias:
        out_specs.append(pl.BlockSpec((hp, nd, tk, tq), lambda kv, j, d: (kv, 0, 0, 0)))
        out_shape.append(SDS((4, nd, tk, tq), F32))
    scratch = [pltpu.VMEM((hp, tk, hd), F32), pltpu.VMEM((hp, tk, hd), F32)] if W is None else []
    sem = ("parallel", "arbitrary", "arbitrary")
    if carry is not None:
        ins += carry.ins
        in_specs += carry.in_specs
        out_specs = carry.out_specs + out_specs
        out_shape = carry.out_shape + out_shape
        scratch += carry.scratch
        sem = ("arbitrary",) * 3
    res = _call(body, name=name, grid=(n_kv // hp, nk, ngd), in_specs=in_specs, out_specs=out_specs,
                out_shape=out_shape, scratch=scratch, sem=sem)(*ins)
    main = res[nc:]
    return (*main[:3], main[3] if want_dbias else None, list(res[:nc]))


def _attn_bwd_pre(name, dbr, o_all, u):
    S = dbr.shape[0]
    ts = min(S, ROW_TILE)
    hd = HEAD_DIM

    assert all(g % 4 == 0 for g in GCOL)

    def gcol(n):
        return GCOL[0] // 4 + n * 4 - jnp.where(n >= 2, 1, 0)

    def body(dbr_ref, o_ref, g_ref, do_ref, dg_ref, dl_ref):
        for hh in range(4):
            lanes = slice(hh * hd, (hh + 1) * hd)
            db = dbr_ref[:, lanes].astype(F32)
            o = o_ref[:, lanes].astype(F32)
            gv = g_ref[:, lanes].astype(F32)
            sg = _sigmoid(gv)
            do = db * (gv * sg)
            do_ref[:, lanes] = do.astype(BF16)
            dg_ref[:, lanes] = (db * o * (sg * (1.0 + gv * (1.0 - sg)))).astype(BF16)
            dl_ref[hh] = jnp.sum((do * o).T, axis=0, keepdims=True)

    own = pl.BlockSpec((ts, 4 * hd), lambda i, n: (i, n))
    stacked = pl.BlockSpec((ts, 4 * hd), lambda i, n: (n * (S // ts) + i, 0))
    return _call(body, name=name, grid=(S // ts, 4),
                 in_specs=[own, own, pl.BlockSpec((ts, 4 * hd), lambda i, n: (i, gcol(n)))],
                 out_specs=[stacked, own, pl.BlockSpec((4, 1, ts), lambda i, n: (n, 0, i))],
                 out_shape=[SDS((4 * S, 4 * hd), BF16), SDS((S, 16 * hd), BF16), SDS((16, 1, S), F32)],
                 sem=("parallel", "parallel"))(dbr, o_all, u)


def _sink_grad(name, sink, lse, delta):
    S = lse.shape[2]
    ts = min(S, 2048)

    def body(sink_ref, lse_ref, dl_ref, out_ref):
        @pl.when(pl.program_id(1) == 0)
        def _():
            out_ref[...] = jnp.zeros_like(out_ref)

        sk = sink_ref[pl.program_id(0)]
        part = jnp.sum(jnp.exp(sk - lse_ref[0]) * dl_ref[0], axis=1, keepdims=True)
        out_ref[0] += -jnp.broadcast_to(part, (1, 128))

    col = pl.BlockSpec((1, 1, ts), lambda h, i: (h, 0, i))
    return _call(body, name=name, grid=(4, S // ts),
                 in_specs=[pl.BlockSpec(memory_space=pltpu.SMEM), col, col],
                 out_specs=pl.BlockSpec((1, 1, 128), lambda h, i: (h, 0, 0)),
                 out_shape=SDS((4, 1, 128), F32), sem=("parallel", "arbitrary"))(sink, lse, delta)


def _bias_maps(T, W):
    rpb = T // GRID_W
    nd = 2 * W + 1
    rmap = np.zeros((nd, rpb, rpb, 15), np.float32)
    for df in range(nd):
        for a in range(rpb):
            for b in range(rpb):
                r = (df - W) * rpb + b - a + 7
                if 0 <= r < 15:
                    rmap[df, a, b, r] = 1.0
    cmap = np.zeros((GRID_W, GRID_W, 31), np.float32)
    for q in range(GRID_W):
        for k in range(GRID_W):
            cmap[q, k, int(np.clip(k - q, -15, 15)) + 15] = 1.0
    return jnp.asarray(rmap), jnp.asarray(cmap)


def _bias_tiles(rel_bias, S, T):
    W = -(-7 // (T // GRID_W))
    rmap, cmap = _bias_maps(T, W)
    t = jnp.einsum("dabr,hrc,qkc->hdbkaq", rmap, rel_bias, cmap, precision=lax.Precision.HIGHEST)
    return _with_off_tile(t.reshape(1, 4, 2 * W + 1, T, T) * (1.0 / SCALE) + jnp.asarray(_mask_tiles_d(S, T)))


def _bias_tiles_t(dtiles, T):
    rpb = T // GRID_W
    W = -(-7 // rpb)
    rmap, cmap = _bias_maps(T, W)
    t = dtiles.reshape(4, 2 * W + 1, rpb, GRID_W, rpb, GRID_W)
    return jnp.einsum("dabr,hdbkaq,qkc->hrc", rmap, t, cmap, precision=lax.Precision.HIGHEST)


def _mixer_cfg(S):
    ta, tb, td = (min(S, ATT_TILE[k]) for k in "abd")
    cq, ck = (min(S, t) for t in ATT_TILE["c"])
    bq, bk = (min(S, t) for t in ATT_TILE["c_bwd"])
    a = _Mixer(ta, ta, _band(128, ta), 2)
    b = _Mixer(tb, tb, _band(1024, tb), 1, hp=2)
    d = _Mixer(td, td, -(-7 // (td // GRID_W)), 1, _edge_blocks_d(td), hp=2)
    return {"a": (a, a, jnp.asarray(_mask_tiles_a(ta))), "b": (b, b, jnp.asarray(_mask_tiles_b(tb))),
            "c": (_Mixer(cq, ck, None, 2), _Mixer(bq, bk, None, 2), None), "d": (d, d, None)}


def _layer_fwd(l, x, c8, lw, p, tabs, next_shards, me, late=None):
    S, D = x.shape
    dq = D // N_SHARD
    ada_sh, win_sh, wgm_sh, wb_sh, wout_sh = lw
    one_d, axial = tabs
    cfg = _mixer_cfg(S)
    tm = min(S, MM_ROWS)

    ada = _matmul(f"ada_l{l}", (N_SHARD,), [c8, ada_sh, p["b_ada"][l][None, :]],
                  [pl.BlockSpec((8, D), lambda j: (0, 0)), pl.BlockSpec((None, D, 3 * dq), lambda j: (j, 0, 0)),
                   pl.BlockSpec((1, 3 * dq), lambda j: (0, j))],
                  [SDS((8, 3 * D), F32)], [pl.BlockSpec((8, 3 * dq), lambda j: (0, j))],
                  epilogue=lambda acc, b_ref: (acc + b_ref[...],), a_fn=lambda a: a * _sigmoid(a))[0][0:1]
    shift, scale, gate = ada[:, :D], ada[:, D:2 * D], ada[:, 2 * D:]
    g_row = p["norm_g"][l][None, :]
    h = _norm_mod(f"norm_mod_l{l}", x, g_row, scale, shift)

    loads = {k: [] for k in ("mm_in", "attn_b", "attn_c", "attn_d", "merge")}
    tags = {k: [] for k in loads}
    got = {}

    def ride(host, tag, shard):
        loads[host].append(shard)
        tags[host].append(tag)

    def arrived(host, arrays):
        got.update(zip(tags[host], arrays))

    if late is not None:
        ride("mm_in", "late_gm", late[0])
        ride("attn_b", "late_wb", late[1])
        ride("attn_b", "late_out", late[2])
    if next_shards is not None:
        s_ada, s_in, s_gm, s_wb, s_out = next_shards
        ride("attn_c", "gm", s_gm)
        ride("merge", "in", s_in)
        ride("mm_in" if late is None else "attn_c", "ada", s_ada)
        ride("attn_b" if late is None else "attn_d", "wb", s_wb)
        ride("attn_b" if late is None else "attn_d", "out", s_out)
    carries = {k: _Carry("gather", v) if v else None for k, v in loads.items()}

    res_in = _matmul(f"mm_in_l{l}", (S // tm, N_SHARD), [h, win_sh],
                     [pl.BlockSpec((tm, D), lambda i, j: (i, 0)),
                      pl.BlockSpec((None, D, IN_SHARD), lambda i, j: (j, 0, 0))],
                     [SDS((S, IN_COLS), BF16)], [pl.BlockSpec((tm, IN_SHARD), lambda i, j: (i, j))],
                     carry=carries["mm_in"])
    if carries["mm_in"] is not None:
        arrived("mm_in", res_in[1])
        res_in = res_in[0]
    u = res_in[0]

    qa = _rope_fwd(f"rope_qa_l{l}", u, QCOL[0], 4, one_d, 64)
    ka = _rope_fwd(f"rope_ka_l{l}", u, KCOL[0], 2, one_d, 64)
    qb = _rope_fwd(f"rope_qb_l{l}", u, QCOL[1], 4, one_d, 64)
    kb = _rope_fwd(f"rope_kb_l{l}", u, KCOL[1], 4, one_d, 64)
    qc = _rope_fwd(f"rope_qc_l{l}", u, QCOL[2], 4, axial, 32, p["c_q_norm"][l][None, :])
    kc = _rope_fwd(f"rope_kc_l{l}", u, KCOL[2], 2, axial, 32, p["c_k_norm"][l][None, :])

    no_sink = jnp.full((4,), NEG_INF, F32)
    bias = _bias_tiles(p["d_rel_bias"][l], S, cfg["d"][0].tq)
    hd = HEAD_DIM
    kv_a = jnp.concatenate([ka, u[:, VCOL[0] * hd:(VCOL[0] + 2) * hd]], axis=1)
    kv_b = jnp.concatenate([kb, u[:, VCOL[1] * hd:(VCOL[1] + 4) * hd]], axis=1)
    kv_c = jnp.concatenate([kc, u[:, VCOL[2] * hd:(VCOL[2] + 2) * hd]], axis=1)
    kv_d = u[:, KCOL[3] * hd:(KCOL[3] + 8) * hd]
    qd = u[:, QCOL[3] * hd:(QCOL[3] + 4) * hd]
    br_a, o_a, lse_a, got_a = _attn_fwd(f"attn_a_l{l}", cfg["a"][0], qa, 0, kv_a, 0, kv_a, 2, u, GCOL[0],
                                        p["a_sink"][l], cfg["a"][2])
    br_b, o_b, lse_b, got_b = _attn_fwd(f"attn_b_l{l}", cfg["b"][0], qb, 0, kv_b, 0, kv_b, 4, u, GCOL[1], no_sink,
                                        cfg["b"][2], carries["attn_b"])
    br_c, o_c, lse_c, got_c = _attn_fwd(f"attn_c_l{l}", cfg["c"][0], qc, 0, kv_c, 0, kv_c, 2, u, GCOL[2], no_sink,
                                        None, carries["attn_c"])
    br_d, o_d, lse_d, got_d = _attn_fwd(f"attn_d_l{l}", cfg["d"][0], qd, 0, kv_d, 0, kv_d, 4, u, GCOL[3],
                                        no_sink, bias, carries["attn_d"])
    arrived("attn_b", got_b)
    arrived("attn_c", got_c)
    arrived("attn_d", got_d)
    if late is not None:
        wgm_sh, wb_sh, wout_sh = _own_slot([got["late_gm"], got["late_wb"], got["late_out"]], late, me)
    br = jnp.concatenate([br_a, br_b, br_c, br_d], axis=1)
    o_all = jnp.concatenate([o_a, o_b, o_c, o_d], axis=1)

    def merge_body(h_ref, wg_ref, br_ref, wb_ref, mg_ref, pj_ref, z_ref, acc_ref):
        n = pl.program_id(2)
        mgv = _sigmoid(lax.dot_general(h_ref[...], wg_ref[...], _NN, preferred_element_type=F32))
        pj = lax.dot_general(br_ref[...], wb_ref[...], _NN, preferred_element_type=F32)
        mg_ref[...] = mgv.astype(BF16)
        pj_ref[...] = pj.astype(BF16)

        @pl.when(n == 0)
        def _():
            acc_ref[...] = mgv * pj

        @pl.when(n > 0)
        def _():
            acc_ref[...] += mgv * pj

        @pl.when(n == 3)
        def _():
            z_ref[...] = acc_ref[...].astype(BF16)

    wide = pl.BlockSpec((tm, dq), lambda i, j, n: (i, n * N_SHARD + j))
    (mg, proj, z), got_m = _call_carrying(
        merge_body, carries["merge"], [h, wgm_sh, br, wb_sh], name=f"merge_l{l}", grid=(S // tm, N_SHARD, 4),
        in_specs=[pl.BlockSpec((tm, D), lambda i, j, n: (i, 0)),
                  pl.BlockSpec((None, D, dq), lambda i, j, n: (n, 0, j)),
                  pl.BlockSpec((tm, BRANCH_W), lambda i, j, n: (i, n)),
                  pl.BlockSpec((None, None, BRANCH_W, dq), lambda i, j, n: (j, n, 0, 0))],
        out_specs=[wide, wide, pl.BlockSpec((tm, dq), lambda i, j, n: (i, j))],
        out_shape=[SDS((S, 4 * D), BF16), SDS((S, 4 * D), BF16), SDS((S, D), BF16)],
        scratch=[pltpu.VMEM((tm, dq), F32)], sem=("parallel", "parallel", "arbitrary"))
    arrived("merge", got_m)
    next_lw = None
    if next_shards is not None:
        next_lw = _own_slot([got[t] for t in ("ada", "in", "gm", "wb", "out")], next_shards, me)

    tn = min(D, 1024)
    x_new, o2 = _matmul(
        f"mm_out_l{l}", (S // tm, D // tn), [z, wout_sh, x, gate],
        [pl.BlockSpec((tm, D), lambda i, j: (i, 0)), pl.BlockSpec((N_SHARD, dq, tn), lambda i, j: (0, 0, j)),
         pl.BlockSpec((tm, tn), lambda i, j: (i, j)), pl.BlockSpec((1, tn), lambda i, j: (0, j))],
        [SDS((S, D), F32), SDS((S, D), BF16)],
        [pl.BlockSpec((tm, tn), lambda i, j: (i, j)), pl.BlockSpec((tm, tn), lambda i, j: (i, j))],
        k_inner=N_SHARD, epilogue=lambda acc, x_ref, g_ref: (x_ref[...] + g_ref[...] * acc, acc))
    res = dict(x=x, h=h, u=u, qa=qa, qb=qb, qc=qc, qd=qd, kv=(kv_a, kv_b, kv_c, kv_d), br=br, o_all=o_all,
               lse=(lse_a, lse_b, lse_c, lse_d), bias=bias, mg=mg, proj=proj, z=z, o2=o2,
               g_row=g_row, scale=scale, gate=gate)
    return x_new, res, (ada_sh, win_sh, wgm_sh, wb_sh, wout_sh), next_lw


def _layer_bwd(l, dxo, r, lw, p, tabs, pending):
    x, h, u = r["x"], r["h"], r["u"]
    S, D = x.shape
    dq = D // N_SHARD
    ada_sh, win_sh, wgm_sh, wb_sh, wout_sh = lw
    one_d, axial = tabs
    cfg = _mixer_cfg(S)
    tm = min(S, MM_ROWS)
    tk = min(S, 1024)
    tn = min(D, 1024)

    do2, dgate = _out_bwd_ew(f"out_bwd_l{l}", dxo, r["o2"], r["gate"])
    dz = _matmul(f"mm_dz_l{l}", (S // tm, N_SHARD), [do2, wout_sh],
                 [pl.BlockSpec((tm, D), lambda i, n: (i, 0)), pl.BlockSpec((None, dq, D), lambda i, n: (n, 0, 0))],
                 [SDS((S, D), BF16)], [pl.BlockSpec((tm, dq), lambda i, n: (i, n))], tb=True)[0]
    tkw = min(S, 2048)
    g_out = _matmul(f"mm_gwout_l{l}", (N_SHARD, D // tn, S // tkw), [r["z"], do2],
                    [pl.BlockSpec((tkw, dq), lambda n, j, k: (k, n)), pl.BlockSpec((tkw, tn), lambda n, j, k: (k, j))],
                    [SDS((N_SHARD, dq, D), F32)], [pl.BlockSpec((None, dq, tn), lambda n, j, k: (n, 0, j))],
                    ta=True, k_axis=2, acc_shape=(dq, tn))[0]

    dmg, dproj = _merge_bwd_ew(f"merge_bwd_l{l}", dz, r["mg"], r["proj"])
    nj = D // tn
    g_gm = _matmul(f"mm_gwgm_l{l}", (4, D // tn, nj, S // tk), [h, dmg],
                   [pl.BlockSpec((tk, tn), lambda n, i, j, k: (k, i)),
                    pl.BlockSpec((tk, tn), lambda n, i, j, k: (k, n * nj + j))],
                   [SDS((4, D, D), F32)], [pl.BlockSpec((None, tn, tn), lambda n, i, j, k: (n, i, j))],
                   ta=True, k_axis=3, acc_shape=(tn, tn))[0]
    dh1 = _matmul(f"mm_dh1_l{l}", (S // tm, nj, 4), [dmg, wgm_sh],
                  [pl.BlockSpec((tm, D), lambda i, j, n: (i, n)),
                   pl.BlockSpec((None, tn, D), lambda i, j, n: (n, j, 0))],
                  [SDS((S, D), F32)], [pl.BlockSpec((tm, tn), lambda i, j, kk: (i, j))],
                  tb=True, k_axis=2, acc_shape=(tm, tn))[0]
    dbr = _matmul(f"mm_dbr_l{l}", (S // tm, 4), [dproj, wb_sh],
                  [pl.BlockSpec((tm, D), lambda i, n: (i, n)),
                   pl.BlockSpec((N_SHARD, None, BRANCH_W, dq), lambda i, n: (0, n, 0, 0))],
                  [SDS((S, 4 * BRANCH_W), BF16)], [pl.BlockSpec((tm, BRANCH_W), lambda i, n: (i, n))],
                  tb=True, k_inner=N_SHARD)[0]
    tkl = min(S, 2048)
    g_wb = _matmul(f"mm_gwb_l{l}", (N_SHARD, 4, S // tkl), [r["br"], dproj],
                   [pl.BlockSpec((tkl, BRANCH_W), lambda j, n, k: (k, n)),
                    pl.BlockSpec((tkl, dq), lambda j, n, k: (k, n * N_SHARD + j))],
                   [SDS((N_SHARD, 4, BRANCH_W, dq), F32)],
                   [pl.BlockSpec((None, None, BRANCH_W, dq), lambda j, n, k: (j, n, 0, 0))],
                   ta=True, k_axis=2, acc_shape=(BRANCH_W, dq))[0]

    do_all, dg_all, delta = _attn_bwd_pre(f"attn_pre_l{l}", dbr, r["o_all"], u)
    lse_a, lse_b, lse_c, lse_d = r["lse"]
    dsink = _sink_grad(f"sink_grad_l{l}", p["a_sink"][l], lse_a, delta)[:, 0, 0]
    carries = [None] * 3
    if pending is not None:
        p_in, p_gm, p_wb, p_out = pending
        carries = [_Carry("exchange", [p_in]), _Carry("exchange", [p_gm]), _Carry("exchange", [p_wb, p_out])]
    kv_a, kv_b, kv_c, kv_d = r["kv"]
    dqa, dka, dva, _, _ = _attn_bwd(f"attn_a_bwd_l{l}", cfg["a"][1], r["qa"], 0, kv_a, 0, kv_a, 2, do_all, 0,
                                    lse_a, delta, cfg["a"][2])
    dqb, dkb, dvb, _, got_b = _attn_bwd(f"attn_b_bwd_l{l}", cfg["b"][1], r["qb"], 0, kv_b, 0, kv_b, 4, do_all, 4,
                                        lse_b, delta, cfg["b"][2], carry=carries[0])
    dqc, dkc, dvc, _, got_c = _attn_bwd(f"attn_c_bwd_l{l}", cfg["c"][1], r["qc"], 0, kv_c, 0, kv_c, 2, do_all, 8,
                                        lse_c, delta, carry=carries[1])
    dqd, dkd, dvd, dbias, got_d = _attn_bwd(f"attn_d_bwd_l{l}", cfg["d"][1], r["qd"], 0, kv_d, 0, kv_d, 4,
                                            do_all, 12, lse_d, delta, r["bias"], True, carries[2])
    arrived = None if pending is None else [got_b[0], got_c[0], got_d[0], got_d[1]]
    d_rel = _bias_tiles_t(dbias, cfg["d"][1].tq)

    duqa, _ = _rope_bwd(f"rope_qa_bwd_l{l}", dqa, u, QCOL[0], 4, one_d, 64)
    duka, _ = _rope_bwd(f"rope_ka_bwd_l{l}", dka, u, KCOL[0], 2, one_d, 64)
    duqb, _ = _rope_bwd(f"rope_qb_bwd_l{l}", dqb, u, QCOL[1], 4, one_d, 64)
    dukb, _ = _rope_bwd(f"rope_kb_bwd_l{l}", dkb, u, KCOL[1], 4, one_d, 64)
    duqc, dcq = _rope_bwd(f"rope_qc_bwd_l{l}", dqc, u, QCOL[2], 4, axial, 32, p["c_q_norm"][l][None, :])
    dukc, dck = _rope_bwd(f"rope_kc_bwd_l{l}", dkc, u, KCOL[2], 2, axial, 32, p["c_k_norm"][l][None, :])
    bw = BRANCH_W
    du = jnp.concatenate(
        [duqa, duka, dva.astype(BF16), dg_all[:, 0:bw],
         duqb, dukb, dvb.astype(BF16), dg_all[:, bw:2 * bw],
         duqc, dukc, dvc.astype(BF16), dg_all[:, 2 * bw:3 * bw],
         dqd.astype(BF16), dkd.astype(BF16), dvd.astype(BF16), dg_all[:, 3 * bw:]], axis=1)

    tmi = min(D, 1024)
    g_in = _matmul(f"mm_gwin_l{l}", (N_SHARD, D // tmi, S // tk), [h, du],
                   [pl.BlockSpec((tk, tmi), lambda j, i, k: (k, i)), pl.BlockSpec((tk, IN_SHARD), lambda j, i, k: (k, j))],
                   [SDS((N_SHARD, D, IN_SHARD), F32)], [pl.BlockSpec((None, tmi, IN_SHARD), lambda j, i, k: (j, i, 0))],
                   ta=True, k_axis=2, acc_shape=(tmi, IN_SHARD))[0]
    dh2 = _matmul(f"mm_dh2_l{l}", (S // tm, nj, N_SHARD), [du, win_sh],
                  [pl.BlockSpec((tm, IN_SHARD), lambda i, j, k: (i, k)),
                   pl.BlockSpec((None, tn, IN_SHARD), lambda i, j, k: (k, j, 0))],
                  [SDS((S, D), F32)], [pl.BlockSpec((tm, tn), lambda i, j, k: (i, j))],
                  tb=True, k_axis=2, acc_shape=(tm, tn))[0]

    dx_prev, dshift, dscale, dng = _norm_mod_bwd(f"norm_mod_bwd_l{l}", x, dh1, dh2, dxo, r["g_row"], r["scale"])
    d_ada = jnp.concatenate([dshift, dscale, dgate], axis=1)[0]
    big = (g_in, g_gm, g_wb, g_out)
    small = dict(norm_g=dng[0], b_ada=d_ada, a_sink=dsink, c_q_norm=dcq[0], c_k_norm=dck[0], d_rel_bias=d_rel)
    return dx_prev, big, small, arrived


def _place():
    return lax.axis_index("x"), lax.axis_index("y"), lax.axis_index("c")


class _Carry:
    def __init__(self, kind, arrays):
        self.kind, self.n, self.ins = kind, len(arrays), list(arrays)
        any_spec = pl.BlockSpec(memory_space=pl.ANY)
        self.in_specs = [any_spec] * self.n
        self.out_specs = [any_spec] * self.n
        if kind == "gather":
            self.out_shape = [SDS((N_SHARD, *a.shape), a.dtype) for a in arrays]
        else:
            self.out_shape = [SDS((3, *a.shape[1:]), a.dtype) for a in arrays]
        self.scratch = [pltpu.SemaphoreType.DMA((self.n, 3)), pltpu.SemaphoreType.DMA((self.n, 3))]

    def _copies(self, ins, outs, send_sems, recv_sems, arriving):
        x, y, c = _place()
        cps = []
        for a in range(self.n):
            for k, (px, py) in enumerate([(1 - x, y), (x, 1 - y), (1 - x, 1 - y)]):
                if self.kind == "gather":
                    src, dst = ins[a], outs[a].at[2 * px + py if arriving else 2 * x + y]
                else:
                    src, dst = ins[a].at[2 * px + py], outs[a].at[k]
                cps.append(pltpu.make_async_remote_copy(src, dst, send_sems.at[a, k], recv_sems.at[a, k],
                                                        device_id=(px, py, c), device_id_type=MESH))
        return cps

    def start(self, ins, outs, send_sems, recv_sems):
        for cp in self._copies(ins, outs, send_sems, recv_sems, False):
            cp.start()

    def finish(self, ins, outs, send_sems, recv_sems):
        for cp in self._copies(ins, outs, send_sems, recv_sems, True):
            cp.wait_recv()
        for cp in self._copies(ins, outs, send_sems, recv_sems, False):
            cp.wait_send()


def _run_carry(name, carry):
    n = carry.n

    def body(*refs):
        args = (refs[:n], refs[n:2 * n], refs[2 * n], refs[2 * n + 1])
        carry.start(*args)
        carry.finish(*args)

    return pl.pallas_call(body, name=name, in_specs=carry.in_specs, out_specs=carry.out_specs,
                          out_shape=carry.out_shape, scratch_shapes=carry.scratch)(*carry.ins)


def _own_slot(gathered, shards, me):
    return [lax.dynamic_update_index_in_dim(g, s, me, 0) for g, s in zip(gathered, shards)]


def _gather_small(name, v):
    m_per, n = v.shape

    def body(x_ref, out_ref, send_sems, recv_sems, local_sem):
        x, y, c = _place()
        me, sibling = (x, y, c), (x, y, 1 - c)
        chips = [(1 - x, y), (x, 1 - y), (1 - x, 1 - y)]

        def rows(px, py, pc):
            return out_ref.at[pl.ds((4 * px + 2 * py + pc) * m_per, m_per), :]

        def copy(k, block, to, src=None):
            return pltpu.make_async_remote_copy(
                src_ref=rows(*block) if src is None else src, dst_ref=rows(*block),
                send_sem=send_sems.at[k], recv_sem=recv_sems.at[k], device_id=to, device_id_type=MESH)

        mine = pltpu.make_async_copy(x_ref, rows(*me), local_sem)
        mine.start()
        first = [copy(0, me, sibling, src=x_ref)]
        first += [copy(1 + j, me, (*chip, c), src=x_ref) for j, chip in enumerate(chips)]
        for cp in first:
            cp.start()
        passed = [copy(4 + j, (*chip, c), sibling) for j, chip in enumerate(chips)]
        for j, chip in enumerate(chips):
            copy(1 + j, (*chip, c), me).wait_recv()
            passed[j].start()
        copy(0, sibling, me).wait_recv()
        for j, chip in enumerate(chips):
            copy(4 + j, (*chip, 1 - c), me).wait_recv()
        for cp in first + passed:
            cp.wait_send()
        mine.wait()

    return pl.pallas_call(
        body, name=name, out_shape=SDS((8 * m_per, n), v.dtype),
        in_specs=[pl.BlockSpec(memory_space=pltpu.VMEM)], out_specs=pl.BlockSpec(memory_space=pltpu.VMEM),
        scratch_shapes=[pltpu.SemaphoreType.DMA((7,)), pltpu.SemaphoreType.DMA((7,)), pltpu.SemaphoreType.DMA])(v)


def _pair_send_half(name, grads):
    n = len(grads)

    def body(*refs):
        ins, outs = refs[:n], refs[n:2 * n]
        send_sems, recv_sems = refs[2 * n:]
        x, y, c = _place()
        cps = []
        for a in range(n):
            cp = pltpu.make_async_remote_copy(ins[a].at[:, 1 - c], outs[a], send_sems.at[a], recv_sems.at[a],
                                              device_id=(x, y, 1 - c), device_id_type=MESH)
            cp.start()
            cps.append(cp)
        for cp in cps:
            cp.wait_recv()
        for cp in cps:
            cp.wait_send()

    any_spec = pl.BlockSpec(memory_space=pl.ANY)
    return pl.pallas_call(
        body, name=name, in_specs=[any_spec] * n, out_specs=[any_spec] * n,
        out_shape=[SDS((g.shape[0], *g.shape[2:]), g.dtype) for g in grads],
        scratch_shapes=[pltpu.SemaphoreType.DMA((n,)), pltpu.SemaphoreType.DMA((n,))])(*grads)


def _pair_gather(name, halves):
    n = len(halves)

    def body(*refs):
        outs = refs[n:2 * n]
        send_sems, recv_sems = refs[2 * n:]
        x, y, c = _place()
        cps = [pltpu.make_async_remote_copy(outs[a].at[c], outs[a].at[c], send_sems.at[a], recv_sems.at[a],
                                            device_id=(x, y, 1 - c), device_id_type=MESH) for a in range(n)]
        for cp in cps:
            cp.start()
        for a in range(n):
            pltpu.make_async_remote_copy(outs[a].at[c], outs[a].at[1 - c], send_sems.at[a], recv_sems.at[a],
                                         device_id=(x, y, 1 - c), device_id_type=MESH).wait_recv()
        for cp in cps:
            cp.wait_send()

    any_spec = pl.BlockSpec(memory_space=pl.ANY)
    return pl.pallas_call(
        body, name=name, in_specs=[any_spec] * n, out_specs=[any_spec] * n,
        out_shape=[SDS(g.shape, g.dtype) for g in halves], input_output_aliases={a: a for a in range(n)},
        scratch_shapes=[pltpu.SemaphoreType.DMA((n,)), pltpu.SemaphoreType.DMA((n,))])(*halves)


def _add_half(name, g, recv, c_idx):
    _, _, R, C = g.shape
    tr = min(R, 256)

    def body(c_ref, g_ref, r_ref, o_ref):
        o_ref[...] = (g_ref[...] + r_ref[...]).astype(BF16)

    return _call(body, name=name, grid=(4, R // tr), nsp=1,
                 in_specs=[pl.BlockSpec((None, None, tr, C), lambda j, r, c_ref: (j, c_ref[0], r, 0)),
                           pl.BlockSpec((None, tr, C), lambda j, r, c_ref: (j, r, 0))],
                 out_specs=pl.BlockSpec((None, tr, C), lambda j, r, c_ref: (j, r, 0)),
                 out_shape=SDS((4, R, C), BF16), sem=("parallel", "parallel"))(c_idx, g, recv)


def _add_shards(name, part, recv, idx):
    _, R, C = part.shape
    tr = min(R, 256)

    def body(idx_ref, p_ref, r_ref, o_ref):
        o_ref[...] = (((p_ref[...].astype(F32) + r_ref[0].astype(F32)) + r_ref[1].astype(F32))
                      + r_ref[2].astype(F32))

    return _call(body, name=name, grid=(R // tr,), nsp=1,
                 in_specs=[pl.BlockSpec((None, tr, C), lambda r, idx_ref: (idx_ref[0], r, 0)),
                           pl.BlockSpec((3, tr, C), lambda r, idx_ref: (0, r, 0))],
                 out_specs=pl.BlockSpec((None, tr, C), lambda r, idx_ref: (idx_ref[1], r, 0)),
                 out_shape=SDS((2, R, C), F32), sem=("parallel",))(idx, part, recv)


def _pair_sum_layer(l, big, c_idx):
    views = []
    for g in big:
        rows = g.shape[-2] if g.ndim == 3 else g.shape[1] * g.shape[2]
        views.append(g.reshape(N_SHARD, 2, rows // 2, g.shape[-1]))
    recv1 = _pair_send_half(f"rs_pair_send_l{l}", views)
    return [_add_half(f"rs_add_half{a}_l{l}", v, r1, c_idx) for a, (v, r1) in enumerate(zip(views, recv1))]


def _finish_reduce_layer(l, big, parts, recv2, idx):
    halves = [_add_shards(f"rs_add_shards{a}_l{l}", pt, r2, idx) for a, (pt, r2) in enumerate(zip(parts, recv2))]
    full = _pair_gather(f"rs_pair_gather_l{l}", halves)
    return [f.reshape(g.shape[1:]) for f, g in zip(full, big)]


def _adamw_math(w, g, m, v):
    m = ADAM_B1 * m + (1.0 - ADAM_B1) * g
    v = ADAM_B2 * v + (1.0 - ADAM_B2) * (g * g)
    m_hat = m / (1.0 - ADAM_B1 ** ADAM_STEP)
    v_hat = v / (1.0 - ADAM_B2 ** ADAM_STEP)
    delta = -ADAM_LR * (m_hat / (jnp.sqrt(v_hat) + ADAM_EPS) + ADAM_WD * w)
    return delta, m, v


def _adamw(name, w, g, m, v):
    shape = w.shape
    C = shape[-1]
    R = int(np.prod(shape[:-1]))
    tr = min(R, 256)

    def body(w_ref, g_ref, m_ref, v_ref, d_ref, nm_ref, nv_ref):
        d, nm, nv = _adamw_math(w_ref[...], g_ref[...], m_ref[...], v_ref[...])
        d_ref[...] = d
        nm_ref[...] = nm
        nv_ref[...] = nv

    blk = pl.BlockSpec((tr, C), lambda i: (i, 0))
    outs = _call(body, name=name, grid=(R // tr,), in_specs=[blk] * 4, out_specs=[blk] * 3,
                 out_shape=[SDS((R, C), F32)] * 3, sem=("parallel",))(*(a.reshape(R, C) for a in (w, g, m, v)))
    return [o.reshape(shape) for o in outs]


def _adamw_small(name, w, g8, m, v):
    R = w.shape[0]

    def body(w_ref, g_ref, m_ref, v_ref, go_ref, d_ref, nm_ref, nv_ref):
        g = g_ref[0]
        for b in range(1, 8):
            g = g + g_ref[b]
        d, nm, nv = _adamw_math(w_ref[...], g, m_ref[...], v_ref[...])
        go_ref[...] = g
        d_ref[...] = d
        nm_ref[...] = nm
        nv_ref[...] = nv

    blk = pl.BlockSpec((R, 128), lambda i: (0, 0))
    return _call(body, name=name, grid=(1,), in_specs=[blk, pl.BlockSpec((8, R, 128), lambda i: (0, 0, 0)), blk, blk],
                 out_specs=[blk] * 4, out_shape=[SDS((R, 128), F32)] * 4, sem=("arbitrary",))(w, g8, m, v)


SMALL_NAMES = ("norm_g", "b_ada", "a_sink", "c_q_norm", "c_k_norm", "d_rel_bias", "final_g")


def _pack(parts, extra_rows=0):
    flat = jnp.concatenate([a.reshape(-1) for a in parts])
    rows = -(-flat.shape[0] // 128)
    rows = -(-rows // 8) * 8 + extra_rows
    return jnp.pad(flat, (0, rows * 128 - flat.shape[0])).reshape(rows, 128)


def _unpack(packed, like):
    flat = packed.reshape(-1)
    out, off = [], 0
    for a in like:
        out.append(flat[off:off + a.size].reshape(a.shape))
        off += a.size
    return out


def _device_step(x, c8, tgt, shards, p, me, c_idx):
    S = x.shape[0]
    L = len(shards)
    tabs = _rope_tables(S)
    first = _Carry("gather", shards[0][:2])
    lw = [(*_own_slot(_run_carry("gather_w_l0", first), shards[0][:2], me), None, None, None)]
    res = []
    for l in range(L):
        x, r, lw[l], nxt = _layer_fwd(l, x, c8, lw[l], p, tabs, shards[l + 1] if l + 1 < L else None, me,
                                      shards[0][2:] if l == 0 else None)
        res.append(r)
        lw.append(nxt)
    dx, dfg, loss = _final_loss("final_loss", x, tgt, p["final_g"][None, :])
    bigs, smalls, parts, arrived = [None] * L, [None] * L, [None] * L, [None] * L
    for l in reversed(range(L)):
        pending = parts[l + 1] if l + 1 < L else None
        dx, bigs[l], smalls[l], arr = _layer_bwd(l, dx, res[l], lw[l], p, tabs, pending)
        if pending is not None:
            arrived[l + 1] = arr
        parts[l] = _pair_sum_layer(l, bigs[l], c_idx)
    arrived[0] = _run_carry("rs_shard_exchange_l0", _Carry("exchange", parts[0]))
    idx = jnp.concatenate([jnp.reshape(me, (1,)).astype(I32), c_idx])
    reduced = [_finish_reduce_layer(l, bigs[l], parts[l], arrived[l], idx) for l in range(L)]
    return loss, dx, reduced, smalls, dfg[0]


def kernel(x, c, norm_g, w_ada, b_ada, w_in, a_sink, c_q_norm, c_k_norm, d_rel_bias, w_gate_merge, w_branch, w_out, final_g, loss_target, m_norm_g, m_w_ada, m_b_ada, m_w_in, m_a_sink, m_c_q_norm, m_c_k_norm, m_d_rel_bias, m_w_gate_merge, m_w_branch, m_w_out, m_final_g, v_norm_g, v_w_ada, v_b_ada, v_w_in, v_a_sink, v_c_q_norm, v_c_k_norm, v_d_rel_bias, v_w_gate_merge, v_w_branch, v_w_out, v_final_g):
    L, D = norm_g.shape
    dq = D // N_SHARD
    p = dict(norm_g=norm_g, b_ada=b_ada, a_sink=a_sink, c_q_norm=c_q_norm, c_k_norm=c_k_norm,
             d_rel_bias=d_rel_bias, final_g=final_g)
    xi, yi, ci = _place()
    c_idx = jnp.reshape(ci, (1,)).astype(I32)
    me = 2 * xi + yi

    shards = [[w_ada[l].astype(BF16), w_in[l].astype(BF16), w_gate_merge[l].astype(BF16),
               w_branch[l].astype(BF16), w_out[l].astype(BF16)] for l in range(L)]
    c8 = jnp.broadcast_to(c, (8, D))
    loss, grad_x, reduced, smalls, dfg = _device_step(x[0], c8, loss_target[0], shards, p, me, c_idx)
    loss = lax.psum(loss[0, 0], ("x", "y", "c"))

    small_parts = [jnp.stack([s[n] for s in smalls]) for n in SMALL_NAMES[:-1]] + [dfg]
    packed = _pack(small_parts + [c[0]])
    rows = packed.shape[0]
    g8 = _gather_small("gather_small", packed).reshape(8, rows, 128)
    small_w = [p[n] for n in SMALL_NAMES]
    small_m = [m_norm_g, m_b_ada, m_a_sink, m_c_q_norm, m_c_k_norm, m_d_rel_bias, m_final_g]
    small_v = [v_norm_g, v_b_ada, v_a_sink, v_c_q_norm, v_c_k_norm, v_d_rel_bias, v_final_g]
    pad_c = [jnp.zeros((D,), F32)]
    sg, sd, sm, sv = _adamw_small("adamw_small", _pack(small_w + pad_c), g8, _pack(small_m + pad_c),
                                  _pack(small_v + pad_c))
    sg, sd, sm, sv = (_unpack(a, small_w) for a in (sg, sd, sm, sv))

    n_small = sum(a.size for a in small_parts)
    flat8 = g8.reshape(8, rows * 128)
    c_all = flat8[:, n_small:n_small + D]
    dada_all = flat8[:, L * D:L * D + L * 3 * D].reshape(8, L, 3 * D)
    dada_mine = lax.dynamic_slice_in_dim(dada_all, (2 * xi + yi) * (3 * dq), 3 * dq, axis=2)
    tma = min(D, 1024)
    g_ada = jnp.stack([
        _matmul(f"mm_gwada_l{l}", (D // tma,), [c_all, dada_mine[:, l]],
                [pl.BlockSpec((8, tma), lambda i: (0, i)), pl.BlockSpec((8, 3 * dq), lambda i: (0, 0))],
                [SDS((D, 3 * dq), F32)], [pl.BlockSpec((tma, 3 * dq), lambda i: (i, 0))],
                ta=True, a_fn=lambda a: a * _sigmoid(a))[0] for l in range(L)])

    g_in, g_gm, g_wb, g_out = (jnp.stack([reduced[l][a] for l in range(L)]) for a in range(4))

    big = {}
    for nm, w, g, m, v in (("w_ada", w_ada, g_ada, m_w_ada, v_w_ada), ("w_in", w_in, g_in, m_w_in, v_w_in),
                           ("w_gate_merge", w_gate_merge, g_gm, m_w_gate_merge, v_w_gate_merge),
                           ("w_branch", w_branch, g_wb, m_w_branch, v_w_branch),
                           ("w_out", w_out, g_out, m_w_out, v_w_out)):
        big[nm] = (g, *_adamw(f"adamw_{nm}", w, g, m, v))

    order = ("norm_g", "w_ada", "b_ada", "w_in", "a_sink", "c_q_norm", "c_k_norm", "d_rel_bias",
             "w_gate_merge", "w_branch", "w_out", "final_g")
    cols = [[], [], [], []]
    for nm in order:
        if nm in big:
            vals = big[nm]
        else:
            k = SMALL_NAMES.index(nm)
            vals = (sg[k], sd[k], sm[k], sv[k])
        for col, val in zip(cols, vals):
            col.append(val)
    return (loss, grad_x[None], *cols[0], *cols[1], *cols[2], *cols[3])
```

```python
import functools

import numpy as np
import jax
import jax.numpy as jnp
from jax import lax
from jax.experimental import pallas as pl
from jax.experimental.pallas import tpu as pltpu

F32 = jnp.float32
BF16 = jnp.bfloat16
I32 = jnp.int32
SDS = jax.ShapeDtypeStruct
MESH = pl.DeviceIdType.MESH

HEAD_DIM = 128
GRID_W = 64
EPS = 1e-6
NEG_INF = -1e30
ROPE_THETA = 10000.0
SCALE = HEAD_DIM ** -0.5
LOG2E = 1.4426950408889634
SCALE_LOG2E = SCALE * LOG2E
N_SHARD = 4
BRANCH_W = 512
IN_COLS = 7168
IN_SHARD = IN_COLS // N_SHARD
QCOL = (0, 12, 28, 40)
KCOL = (4, 16, 32, 44)
VCOL = (6, 20, 34, 48)
GCOL = (8, 24, 36, 52)
KV_HEADS = (2, 4, 2, 4)

ADAM_LR = 0.001
ADAM_B1 = 0.9
ADAM_B2 = 0.999
ADAM_EPS = 1e-08
ADAM_WD = 0.01
ADAM_STEP = 10

V7X_VMEM_BYTES = 64 * 1024 * 1024
VMEM_LIMIT = V7X_VMEM_BYTES * 7 // 8

ATT_TILE = {"a": 256, "b": 512, "c": (512, 1024), "c_bwd": (2048, 512), "d": 256}
M_INIT = -1e20
MM_ROWS = 1024
ROW_TILE = 512
EW_ROWS = 256


def _band(reach, tile):
    return -(-reach // tile)


def _call(body, *, name, grid, in_specs, out_specs, out_shape, scratch=(), sem=None, nsp=0):
    params = pltpu.CompilerParams(dimension_semantics=sem, vmem_limit_bytes=VMEM_LIMIT)
    if nsp:
        gs = pltpu.PrefetchScalarGridSpec(num_scalar_prefetch=nsp, grid=grid, in_specs=in_specs,
                                          out_specs=out_specs, scratch_shapes=list(scratch))
        return pl.pallas_call(body, grid_spec=gs, out_shape=out_shape, name=name, compiler_params=params)
    return pl.pallas_call(body, grid=grid, in_specs=in_specs, out_specs=out_specs, out_shape=out_shape,
                          scratch_shapes=list(scratch), name=name, compiler_params=params)


def _call_carrying(body, carry, ins, *, name, grid, in_specs, out_specs, out_shape, scratch=(), sem=None):
    if carry is None:
        res = _call(body, name=name, grid=grid, in_specs=in_specs, out_specs=out_specs, out_shape=out_shape,
                    scratch=scratch, sem=sem)(*ins)
        return list(res), []
    n_in, nc = len(in_specs), carry.n

    def wrapped(*refs):
        cr = (refs[n_in:n_in + nc], refs[n_in + nc:n_in + 2 * nc], refs[-2], refs[-1])
        ids = [pl.program_id(ax) for ax in range(len(grid))]
        first = functools.reduce(lambda a, b: a & b, [i == 0 for i in ids])
        last = functools.reduce(lambda a, b: a & b, [i == g - 1 for i, g in zip(ids, grid)])
        pl.when(first)(lambda: carry.start(*cr))
        body(*refs[:n_in], *refs[n_in + 2 * nc:-2])
        pl.when(last)(lambda: carry.finish(*cr))

    res = _call(wrapped, name=name, grid=grid, in_specs=list(in_specs) + carry.in_specs,
                out_specs=carry.out_specs + list(out_specs), out_shape=carry.out_shape + list(out_shape),
                scratch=list(scratch) + carry.scratch, sem=("arbitrary",) * len(grid))(*ins, *carry.ins)
    return list(res[nc:]), list(res[:nc])


def _sigmoid(x):
    return 1.0 / (1.0 + jnp.exp(-x))


def _matmul(name, grid, ins, in_specs, out_shape, out_specs, *, ta=False, tb=False, k_axis=None,
            acc_shape=None, epilogue=None, a_fn=None, k_inner=None, carry=None):
    n_in = len(ins)
    n_out = len(out_shape)
    nk = grid[k_axis] if k_axis is not None else 1
    dn = (((0 if ta else 1,), (1 if tb else 0,)), ((), ()))

    def body(*refs):
        a = refs[0][...]
        if a_fn is not None:
            a = a_fn(a)
        a = a.astype(BF16)
        if k_inner is None:
            p = lax.dot_general(a, refs[1][...].astype(BF16), dn, preferred_element_type=F32)
        else:
            ck = a.shape[1] // k_inner
            p = None
            for kk in range(k_inner):
                t = lax.dot_general(a[:, kk * ck:(kk + 1) * ck], refs[1][kk].astype(BF16), dn,
                                    preferred_element_type=F32)
                p = t if p is None else p + t
        extra = refs[2:n_in]
        outs = refs[n_in:n_in + n_out]

        def fin(acc):
            vals = epilogue(acc, *extra) if epilogue is not None else (acc,)
            for o_ref, v in zip(outs, vals):
                o_ref[...] = v.astype(o_ref.dtype)

        if k_axis is None:
            fin(p)
        else:
            acc_ref = refs[-1]
            k = pl.program_id(k_axis)

            @pl.when(k == 0)
            def _():
                acc_ref[...] = p

            @pl.when(k > 0)
            def _():
                acc_ref[...] += p

            @pl.when(k == nk - 1)
            def _():
                fin(acc_ref[...])

    sem = tuple("arbitrary" if ax == k_axis else "parallel" for ax in range(len(grid)))
    scratch = [pltpu.VMEM(acc_shape, F32)] if k_axis is not None else []
    res, got = _call_carrying(body, carry, ins, name=name, grid=grid, in_specs=in_specs, out_specs=out_specs,
                              out_shape=out_shape, scratch=scratch, sem=sem)
    return res if carry is None else (res, got)


def _norm_mod(name, x, g, scale, shift):
    S, D = x.shape
    ts = min(S, EW_ROWS)

    def body(x_ref, g_ref, sc_ref, sh_ref, h_ref):
        xv = x_ref[...]
        r = lax.rsqrt(jnp.mean(xv * xv, axis=-1, keepdims=True) + EPS)
        h_ref[...] = (((xv * r) * g_ref[...]) * (1.0 + sc_ref[...]) + sh_ref[...]).astype(BF16)

    row = pl.BlockSpec((1, D), lambda i: (0, 0))
    blk = pl.BlockSpec((ts, D), lambda i: (i, 0))
    return _call(body, name=name, grid=(S // ts,), in_specs=[blk, row, row, row], out_specs=blk,
                 out_shape=SDS((S, D), BF16), sem=("parallel",))(x, g, scale, shift)


def _norm_mod_bwd(name, x, dh1, dh2, dxo, g, scale):
    S, D = x.shape
    ts = min(S, EW_ROWS)

    def body(x_ref, a_ref, b_ref, dxo_ref, g_ref, sc_ref, dx_ref, dsh_ref, dsc_ref, dg_ref):
        @pl.when(pl.program_id(0) == 0)
        def _():
            dsh_ref[...] = jnp.zeros_like(dsh_ref)
            dsc_ref[...] = jnp.zeros_like(dsc_ref)
            dg_ref[...] = jnp.zeros_like(dg_ref)

        xv = x_ref[...]
        r = lax.rsqrt(jnp.mean(xv * xv, axis=-1, keepdims=True) + EPS)
        xh = xv * r
        dh = a_ref[...] + b_ref[...]
        gv = g_ref[...]
        one_sc = 1.0 + sc_ref[...]
        dsh_ref[...] += jnp.sum(dh, axis=0, keepdims=True)
        dsc_ref[...] += jnp.sum(dh * xh * gv, axis=0, keepdims=True)
        dg_ref[...] += jnp.sum(dh * xh * one_sc, axis=0, keepdims=True)
        dxh = dh * gv * one_sc
        dx = r * (dxh - xh * jnp.mean(dxh * xh, axis=-1, keepdims=True))
        dx_ref[...] = dxo_ref[...] + dx

    row = pl.BlockSpec((1, D), lambda i: (0, 0))
    blk = pl.BlockSpec((ts, D), lambda i: (i, 0))
    return _call(body, name=name, grid=(S // ts,), in_specs=[blk, blk, blk, blk, row, row],
                 out_specs=[blk, row, row, row],
                 out_shape=[SDS((S, D), F32), SDS((1, D), F32), SDS((1, D), F32), SDS((1, D), F32)],
                 sem=("arbitrary",))(x, dh1, dh2, dxo, g, scale)


def _out_bwd_ew(name, dxo, o2, gate):
    S, D = dxo.shape
    ts = min(S, EW_ROWS)

    def body(dxo_ref, o2_ref, gt_ref, do2_ref, dgt_ref):
        @pl.when(pl.program_id(0) == 0)
        def _():
            dgt_ref[...] = jnp.zeros_like(dgt_ref)

        d = dxo_ref[...]
        do2_ref[...] = (d * gt_ref[...]).astype(BF16)
        dgt_ref[...] += jnp.sum(d * o2_ref[...].astype(F32), axis=0, keepdims=True)

    row = pl.BlockSpec((1, D), lambda i: (0, 0))
    blk = pl.BlockSpec((ts, D), lambda i: (i, 0))
    return _call(body, name=name, grid=(S // ts,), in_specs=[blk, blk, row], out_specs=[blk, row],
                 out_shape=[SDS((S, D), BF16), SDS((1, D), F32)], sem=("arbitrary",))(dxo, o2, gate)


def _merge_bwd_ew(name, dz, mg, proj):
    S, D = dz.shape
    ts = min(S, ROW_TILE)
    td = min(D, 512)
    nd = D // td

    def body(dz_ref, mg_ref, pj_ref, dmg_ref, dpj_ref):
        d = dz_ref[...].astype(F32)
        m = mg_ref[...].astype(F32)
        dmg_ref[...] = (d * pj_ref[...].astype(F32) * m * (1.0 - m)).astype(BF16)
        dpj_ref[...] = (d * m).astype(BF16)

    wide = pl.BlockSpec((ts, td), lambda i, j, n: (i, n * nd + j))
    return _call(body, name=name, grid=(S // ts, nd, 4),
                 in_specs=[pl.BlockSpec((ts, td), lambda i, j, n: (i, j)), wide, wide],
                 out_specs=[wide, wide], out_shape=[SDS((S, 4 * D), BF16), SDS((S, 4 * D), BF16)],
                 sem=("parallel", "parallel", "arbitrary"))(dz, mg, proj)


def _final_loss(name, x, tgt, g):
    S, D = x.shape
    ts = min(S, EW_ROWS)

    def body(x_ref, t_ref, g_ref, dx_ref, dg_ref, loss_ref):
        @pl.when(pl.program_id(0) == 0)
        def _():
            dg_ref[...] = jnp.zeros_like(dg_ref)
            loss_ref[...] = jnp.zeros_like(loss_ref)

        xv = x_ref[...]
        r = lax.rsqrt(jnp.mean(xv * xv, axis=-1, keepdims=True) + EPS)
        xh = xv * r
        gv = g_ref[...]
        err = xh * gv - t_ref[...]
        row_loss = jnp.mean(err * err, axis=-1, keepdims=True)
        loss_ref[...] += 0.5 * jnp.sum(row_loss, axis=0, keepdims=True)
        dy = err * (1.0 / D)
        dg_ref[...] += jnp.sum(dy * xh, axis=0, keepdims=True)
        dxh = dy * gv
        dx_ref[...] = r * (dxh - xh * jnp.mean(dxh * xh, axis=-1, keepdims=True))

    row = pl.BlockSpec((1, D), lambda i: (0, 0))
    blk = pl.BlockSpec((ts, D), lambda i: (i, 0))
    return _call(body, name=name, grid=(S // ts,), in_specs=[blk, blk, row],
                 out_specs=[blk, row, pl.BlockSpec((1, 128), lambda i: (0, 0))],
                 out_shape=[SDS((S, D), F32), SDS((1, D), F32), SDS((1, 128), F32)],
                 sem=("arbitrary",))(x, tgt, g)


def _rope_tables(S):
    def tables(pos, dim):
        inv = ROPE_THETA ** (-jnp.arange(0, dim, 2, dtype=F32) / dim)
        ang = pos.astype(F32)[:, None] * inv[None, :]
        ang = jnp.concatenate([ang, ang], axis=-1)
        return jnp.cos(ang), jnp.sin(ang)

    pos = jnp.arange(S, dtype=I32)
    lane = np.arange(HEAD_DIM)
    cos1, sin1 = tables(pos, HEAD_DIM)
    up1 = jnp.asarray((lane >= 64).astype(np.float32))[None, :]
    one_d = (cos1, sin1 * up1, -sin1 * (1.0 - up1))
    cr, sr = tables(pos // GRID_W, HEAD_DIM // 2)
    cc, sc = tables(pos % GRID_W, HEAD_DIM // 2)
    cos2 = jnp.concatenate([cr, cc], axis=-1)
    sin2 = jnp.concatenate([sr, sc], axis=-1)
    up2 = jnp.asarray(((lane % 64) >= 32).astype(np.float32))[None, :]
    axial = (cos2, sin2 * up2, -sin2 * (1.0 - up2))
    return one_d, axial


def _rope_fwd(name, src, c0, nb, tabs, sh, gain=None, tail=0):
    S = src.shape[0]
    ts = min(S, ROW_TILE)
    has_gain = gain is not None
    roped, nb = nb, nb + tail
    assert c0 % nb == 0
    hd = HEAD_DIM

    def body(*refs):
        x_ref, c_ref, sa_ref, sb_ref = refs[:4]
        o_ref = refs[-1]
        cv, sa, sb = c_ref[...], sa_ref[...], sb_ref[...]
        for hh in range(nb):
            lanes = slice(hh * hd, (hh + 1) * hd)
            if hh >= roped:
                o_ref[:, lanes] = x_ref[:, lanes]
                continue
            xv = x_ref[:, lanes].astype(F32)
            if has_gain:
                r = lax.rsqrt(jnp.mean(xv * xv, axis=-1, keepdims=True) + EPS)
                xv = (xv * r) * refs[4][...]
            out = xv * cv + pltpu.roll(xv, sh, 1) * sa + pltpu.roll(xv, hd - sh, 1) * sb
            o_ref[:, lanes] = out.astype(BF16)

    tab = pl.BlockSpec((ts, hd), lambda i: (i, 0))
    in_specs = [pl.BlockSpec((ts, nb * hd), lambda i: (i, c0 // nb)), tab, tab, tab]
    ins = [src, *tabs]
    if has_gain:
        in_specs.append(pl.BlockSpec((1, hd), lambda i: (0, 0)))
        ins.append(gain)
    return _call(body, name=name, grid=(S // ts,), in_specs=in_specs,
                 out_specs=pl.BlockSpec((ts, nb * hd), lambda i: (i, 0)),
                 out_shape=SDS((S, nb * hd), BF16), sem=("parallel",))(*ins)


def _rope_bwd(name, dout, src, c0, nb, tabs, sh, gain=None):
    S = src.shape[0]
    ts = min(S, ROW_TILE)
    has_gain = gain is not None

    assert c0 % nb == 0
    hd = HEAD_DIM

    def body(*refs):
        d_ref, x_ref, c_ref, sa_ref, sb_ref = refs[:5]
        cv, sa, sb = c_ref[...], sa_ref[...], sb_ref[...]
        if has_gain:
            gn_ref, dx_ref, dgn_ref = refs[5:]

            @pl.when(pl.program_id(0) == 0)
            def _():
                dgn_ref[...] = jnp.zeros_like(dgn_ref)
        else:
            dx_ref = refs[5]
        for hh in range(nb):
            lanes = slice(hh * hd, (hh + 1) * hd)
            d = d_ref[:, lanes].astype(F32)
            dxn = d * cv + pltpu.roll(d * sa, hd - sh, 1) + pltpu.roll(d * sb, sh, 1)
            if has_gain:
                xv = x_ref[:, lanes].astype(F32)
                r = lax.rsqrt(jnp.mean(xv * xv, axis=-1, keepdims=True) + EPS)
                xh = xv * r
                dgn_ref[...] += jnp.sum(dxn * xh, axis=0, keepdims=True)
                dxh = dxn * gn_ref[...]
                dx_ref[:, lanes] = (r * (dxh - xh * jnp.mean(dxh * xh, axis=-1, keepdims=True))).astype(BF16)
            else:
                dx_ref[:, lanes] = dxn.astype(BF16)

    tab = pl.BlockSpec((ts, hd), lambda i: (i, 0))
    own = pl.BlockSpec((ts, nb * hd), lambda i: (i, 0))
    in_specs = [own, pl.BlockSpec((ts, nb * hd), lambda i: (i, c0 // nb)), tab, tab, tab]
    ins = [dout, src, *tabs]
    out_specs = [own]
    out_shape = [SDS((S, nb * hd), BF16)]
    if has_gain:
        row = pl.BlockSpec((1, hd), lambda i: (0, 0))
        in_specs.append(row)
        ins.append(gain)
        out_specs.append(row)
        out_shape.append(SDS((1, hd), F32))
    res = _call(body, name=name, grid=(S // ts,), in_specs=in_specs, out_specs=out_specs,
                out_shape=out_shape, sem=("arbitrary",))(*ins)
    return res if has_gain else (res[0], None)


def _offset_grid(T, W):
    d = (np.arange(2 * W + 1) - W)[:, None, None] * T
    return d + np.arange(T)[None, :, None] - np.arange(T)[None, None, :]


def _with_off_tile(tiles):
    xp = np if isinstance(tiles, np.ndarray) else jnp
    off = xp.full((*tiles.shape[:2], 1, *tiles.shape[3:]), NEG_INF, tiles.dtype)
    return xp.concatenate([tiles, off], axis=2)


def _mask_tiles_a(T):
    dk = _offset_grid(T, _band(128, T))
    return _with_off_tile(np.where(np.abs(dk) <= 128, 0.0, NEG_INF).astype(np.float32)[None, None])


def _mask_tiles_b(T):
    dk = _offset_grid(T, _band(1024, T))
    ad = np.abs(dk)
    mult = ((ad <= 64).astype(np.float32) + ((ad <= 256) & (dk % 4 == 0)) + ((ad <= 1024) & (dk % 16 == 0)))
    return _with_off_tile(np.where(mult > 0, np.log(np.maximum(mult, 1.0)) / SCALE, NEG_INF)
                          .astype(np.float32)[None, None])


def _edge_blocks_d(T):
    return -(-4 // (T // GRID_W))


def _mask_tiles_d(S, T):
    rows, nq, rpb = S // GRID_W, S // T, T // GRID_W
    W, E = -(-7 // rpb), _edge_blocks_d(T)
    assert nq >= 2 * E + 1
    out = []
    for i in [*range(E), nq // 2, *range(nq - E, nq)]:
        kp = ((i + np.arange(2 * W + 1) - W) * T)[:, None, None] + np.arange(T)[None, :, None]
        qp = i * T + np.arange(T)[None, None, :]
        qr, qc, kr, kc = qp >> 6, qp & 63, kp >> 6, kp & 63
        rs = np.clip(qr - 4, 0, rows - 8)
        cs = np.clip(qc - 8, 0, GRID_W - 16)
        valid = (kr >= rs) & (kr < rs + 8) & (kc >= cs) & (kc < cs + 16)
        out.append(np.where(valid, 0.0, NEG_INF).astype(np.float32))
    return np.stack(out)[:, None]


def _variant(i, nq, E):
    if E == 0:
        return 0
    return jnp.where(i < E, i, jnp.where(i >= nq - E, i - (nq - 2 * E - 1), E))


_NT = (((1,), (1,)), ((), ()))
_TN = (((0,), (0,)), ((), ()))
_NN = (((1,), (0,)), ((), ()))


class _Mixer:
    def __init__(self, tq, tk, W, G, E=0, hp=1):
        self.tq, self.tk, self.W, self.G, self.E, self.hp = tq, tk, W, G, E, hp


def _attn_fwd(name, mx, q_arr, qc0, k_arr, kc0, v_arr, vc0, u, gc0, sink, bias=None, carry=None):
    S = q_arr.shape[0]
    tq, tk, W, G = mx.tq, mx.tk, mx.W, mx.G
    nq, nk = S // tq, S // tk
    nd = nk if W is None else 2 * W + 1
    has_bias = bias is not None
    nc = 0 if carry is None else carry.n
    hd = HEAD_DIM

    def jmap(i, d):
        return d if W is None else jnp.clip(i + d - W, 0, nk - 1)

    nin = 1 if W is None else nd
    ngd = nd if W is None else 1

    def body(*refs):
        sink_ref, q_ref, g_ref = refs[:3]
        k_refs, v_refs = refs[3:3 + nin], refs[3 + nin:3 + 2 * nin]
        n_in = 3 + 2 * nin + (1 if has_bias else 0)
        bias_ref = refs[n_in - 1] if has_bias else None
        out0 = n_in + 2 * nc
        br_ref, o_ref, lse_ref = refs[out0:out0 + 3]
        if W is None:
            m_s, l_s, acc_s = refs[out0 + 3:out0 + 6]
        i, d = pl.program_id(0), pl.program_id(1)
        if carry is not None:
            carry_refs = (refs[n_in:n_in + nc], refs[n_in + nc:out0], refs[-2], refs[-1])
            pl.when((i == 0) & (d == 0))(lambda: carry.start(*carry_refs))

        def scores(h, dd, tile):
            kv = slice((h // G) * hd, (h // G + 1) * hd)
            s = lax.dot_general(k_refs[dd][:, kv], q_ref[:, h * hd:(h + 1) * hd], _NT,
                                preferred_element_type=F32)
            if has_bias:
                s = s + bias_ref[_variant(i, nq, mx.E), h if per_head else 0, tile]
            return s

        def weighted(h, dd, p):
            kv = slice((h // G) * hd, (h // G + 1) * hd)
            return lax.dot_general(v_refs[dd][:, kv], p.astype(BF16), _TN, preferred_element_type=F32)

        def finish(h, m, l, acc):
            lanes = slice(h * hd, (h + 1) * hd)
            sk = sink_ref[h]
            m = m * SCALE
            mf = jnp.maximum(m, sk)
            a = jnp.exp(m - mf)
            lf = l * a + jnp.exp(sk - mf)
            o = ((acc * a) / lf).T
            gv = g_ref[:, lanes].astype(F32)
            o_ref[:, lanes] = o.astype(BF16)
            br_ref[:, lanes] = (o * (gv * _sigmoid(gv))).astype(BF16)
            lse_ref[h] = mf + jnp.log(lf)

        if W is None:
            @pl.when(d == 0)
            def _():
                m_s[...] = jnp.full_like(m_s, M_INIT)
                l_s[...] = jnp.zeros_like(l_s)
                acc_s[...] = jnp.zeros_like(acc_s)

            for h in range(4):
                s = scores(h, 0, d)
                m_prev = m_s[h]
                m_new = jnp.maximum(m_prev, jnp.max(s, axis=0, keepdims=True))
                alpha = jnp.exp2((m_prev - m_new) * SCALE_LOG2E)
                p = jnp.exp2((s - m_new) * SCALE_LOG2E)
                l_s[h] = alpha * l_s[h] + jnp.sum(p, axis=0, keepdims=True)
                acc_s[h] = alpha * acc_s[h] + weighted(h, 0, p)
                m_s[h] = m_new

            @pl.when(d == ngd - 1)
            def _():
                for h in range(4):
                    finish(h, m_s[h], l_s[h], acc_s[h])
        else:
            tiles = [jnp.where((i + dd - W >= 0) & (i + dd - W < nk), dd, nd) for dd in range(nd)]
            for h in range(4):
                ss = [scores(h, dd, tiles[dd]) for dd in range(nd)]
                top = ss[0]
                for s in ss[1:]:
                    top = jnp.maximum(top, s)
                m = jnp.max(top, axis=0, keepdims=True)
                ps = [jnp.exp2((s - m) * SCALE_LOG2E) for s in ss]
                l = sum(jnp.sum(p, axis=0, keepdims=True) for p in ps)
                acc = sum(weighted(h, dd, ps[dd]) for dd in range(nd))
                finish(h, m, l, acc)

        if carry is not None:
            pl.when((i == nq - 1) & (d == ngd - 1))(lambda: carry.finish(*carry_refs))

    n_kv = 4 // G
    assert qc0 % 4 == 0 and gc0 % 4 == 0 and kc0 % n_kv == 0 and vc0 % n_kv == 0

    def kv_spec(c0, dd):
        if W is None:
            return pl.BlockSpec((tk, n_kv * hd), lambda i, d: (d, c0 // n_kv))
        return pl.BlockSpec((tk, n_kv * hd), lambda i, d: (jnp.clip(i + dd - W, 0, nk - 1), c0 // n_kv))

    per_head = has_bias and bias.shape[1] == 4
    in_specs = [pl.BlockSpec(memory_space=pltpu.SMEM),
                pl.BlockSpec((tq, 4 * hd), lambda i, d: (i, qc0 // 4)),
                pl.BlockSpec((tq, 4 * hd), lambda i, d: (i, gc0 // 4)),
                *[kv_spec(kc0, dd) for dd in range(nin)], *[kv_spec(vc0, dd) for dd in range(nin)]]
    ins = [sink, q_arr, u, *[k_arr] * nin, *[v_arr] * nin]
    if has_bias:
        in_specs.append(pl.BlockSpec(bias.shape, lambda i, d: (0, 0, 0, 0, 0), pipeline_mode=pl.Buffered(1)))
        ins.append(bias)
    own = pl.BlockSpec((tq, 4 * hd), lambda i, d: (i, 0))
    out_specs = [own, own, pl.BlockSpec((4, 1, tq), lambda i, d: (0, 0, i))]
    out_shape = [SDS((S, 4 * hd), BF16), SDS((S, 4 * hd), BF16), SDS((4, 1, S), F32)]
    scratch = []
    if W is None:
        scratch = [pltpu.VMEM((4, 1, tq), F32), pltpu.VMEM((4, 1, tq), F32), pltpu.VMEM((4, hd, tq), F32)]
    sem = ("parallel", "arbitrary")
    if carry is not None:
        ins += carry.ins
        in_specs += carry.in_specs
        out_specs = carry.out_specs + out_specs
        out_shape = carry.out_shape + out_shape
        scratch += carry.scratch
        sem = ("arbitrary",) * 2
    res = _call(body, name=name, grid=(nq, ngd), in_specs=in_specs, out_specs=out_specs, out_shape=out_shape,
                scratch=scratch, sem=sem)(*ins)
    return (*res[nc:], list(res[:nc]))


def _attn_bwd(name, mx, q_arr, qc0, k_arr, kc0, v_arr, vc0, do_all, hb0, lse, delta, bias=None, want_dbias=False,
              carry=None):
    S = q_arr.shape[0]
    tq, tk, W, G = mx.tq, mx.tk, mx.W, mx.G
    nq, nk = S // tq, S // tk
    nd = nq if W is None else 2 * W + 1
    n_kv = 4 // G
    hd = HEAD_DIM
    has_bias = bias is not None
    nc = 0 if carry is None else carry.n
    assert qc0 % G == 0 and hb0 % G == 0 and (not want_dbias or (has_bias and G == 1)) and (W is None or tq == tk)

    def imap(j, d):
        return d if W is None else jnp.clip(j + d - W, 0, nq - 1)

    nin = 1 if W is None else nd
    ngd = nd if W is None else 1

    def body(*refs):
        k_ref, v_ref = refs[:2]
        q_refs, do_refs = refs[2:2 + nin], refs[2 + nin:2 + 2 * nin]
        lse_refs, dl_refs = refs[2 + 2 * nin:2 + 3 * nin], refs[2 + 3 * nin:2 + 4 * nin]
        n_in = 2 + 4 * nin + (1 if has_bias else 0)
        bias_ref = refs[n_in - 1] if has_bias else None
        out0 = n_in + 2 * nc
        dq_ref, dk_ref, dv_ref = refs[out0:out0 + 3]
        n_o = 4 if want_dbias else 3
        db_ref = refs[out0 + 3] if want_dbias else None
        if W is None:
            dk_s, dv_s = refs[out0 + n_o:out0 + n_o + 2]
        kv, j, d = pl.program_id(0), pl.program_id(1), pl.program_id(2)
        if carry is not None:
            carry_refs = (refs[n_in:n_in + nc], refs[n_in + nc:out0], refs[-2], refs[-1])
            pl.when((kv == 0) & (j == 0) & (d == 0))(lambda: carry.start(*carry_refs))

        @pl.when((j == 0) & (d == 0))
        def _():
            dq_ref[...] = jnp.zeros_like(dq_ref)
            if want_dbias:
                db_ref[...] = jnp.zeros_like(db_ref)

        def unit(kh, g, dd, i, tile):
            hh = kh * G + g
            lanes = slice(hh * hd, (hh + 1) * hd)
            k = k_ref[:, kh * hd:(kh + 1) * hd]
            v = v_ref[:, kh * hd:(kh + 1) * hd]
            q = q_refs[dd][:, lanes]
            do = do_refs[dd][:, lanes]
            s = lax.dot_general(k, q, _NT, preferred_element_type=F32)
            if has_bias:
                s = s + bias_ref[_variant(i, nq, mx.E), kh if per_head else 0, tile]
            p = jnp.exp2(s * SCALE_LOG2E - lse_refs[dd][hh] * LOG2E)
            dv = lax.dot_general(p.astype(BF16), do, _NN, preferred_element_type=F32)
            dp = lax.dot_general(v, do, _NT, preferred_element_type=F32)
            ds = p * (dp - dl_refs[dd][hh])
            if want_dbias:
                db_ref[kh, jnp.minimum(tile, nd - 1)] += ds
            dsb = ds.astype(BF16)
            dk = lax.dot_general(dsb, q, _NN, preferred_element_type=F32)
            row0 = pl.multiple_of(i * tq, tq)
            dq_ref[pl.ds(row0, tq), lanes] += lax.dot_general(dsb, k, _TN, preferred_element_type=F32) * SCALE
            return dk, dv

        if W is None:
            @pl.when(d == 0)
            def _():
                dk_s[...] = jnp.zeros_like(dk_s)
                dv_s[...] = jnp.zeros_like(dv_s)

            for kh in range(hp):
                for g in range(G):
                    dk, dv = unit(kh, g, 0, d, None)
                    dk_s[kh] += dk
                    dv_s[kh] += dv

            @pl.when(d == ngd - 1)
            def _():
                for kh in range(hp):
                    dk_ref[:, kh * hd:(kh + 1) * hd] = dk_s[kh] * SCALE
                    dv_ref[:, kh * hd:(kh + 1) * hd] = dv_s[kh]
        else:
            for kh in range(hp):
                parts = []
                for dd in range(nd):
                    i_dd = j + dd - W
                    tile = jnp.where((i_dd >= 0) & (i_dd < nq), 2 * W - dd, nd)
                    parts += [unit(kh, g, dd, jnp.clip(i_dd, 0, nq - 1), tile) for g in range(G)]
                dk_ref[:, kh * hd:(kh + 1) * hd] = sum(pt[0] for pt in parts) * SCALE
                dv_ref[:, kh * hd:(kh + 1) * hd] = sum(pt[1] for pt in parts)

        if carry is not None:
            pl.when((kv == n_kv // hp - 1) & (j == nk - 1) & (d == ngd - 1))(lambda: carry.finish(*carry_refs))

    hp = mx.hp
    hq = hp * G
    assert qc0 % hq == 0 and hb0 % hq == 0 and kc0 % hp == 0 and vc0 % hp == 0 and n_kv % hp == 0
    per_head = has_bias and bias.shape[1] == 4

    def q_spec(shape, col, dd, stat, row0=0):
        def index(kv, j, d):
            blk = d if W is None else jnp.clip(j + dd - W, 0, nq - 1)
            return (col + kv, 0, blk) if stat else (row0 + blk, col + kv)
        return pl.BlockSpec(shape, index)

    in_specs = [pl.BlockSpec((tk, hp * hd), lambda kv, j, d: (j, kc0 // hp + kv)),
                pl.BlockSpec((tk, hp * hd), lambda kv, j, d: (j, vc0 // hp + kv)),
                *[q_spec((tq, hq * hd), qc0 // hq, dd, False) for dd in range(nin)],
                *[q_spec((tq, hq * hd), 0, dd, False, (hb0 // 4) * nq) for dd in range(nin)],
                *[q_spec((hq, 1, tq), 0, dd, True) for dd in range(nin)],
                *[q_spec((hq, 1, tq), hb0 // hq, dd, True) for dd in range(nin)]]
    ins = [k_arr, v_arr, *[q_arr] * nin, *[do_all] * nin, *[lse] * nin, *[delta] * nin]
    if has_bias:
        in_specs.append(pl.BlockSpec((bias.shape[0], hp if per_head else 1, bias.shape[2], tk, tq),
                                     lambda kv, j, d: (0, kv if per_head else 0, 0, 0, 0)))
        ins.append(bias)
    kv_blk = pl.BlockSpec((tk, hp * hd), lambda kv, j, d: (j, kv))
    out_specs = [pl.BlockSpec((S, hq * hd), lambda kv, j, d: (0, kv)), kv_blk, kv_blk]
    out_shape = [SDS((S, 4 * hd), F32), SDS((S, n_kv * hd), F32), SDS((S, n_kv * hd), F32)]
    if want_dbias:
        out_specs.append(pl.BlockSpec((hp, nd, tk, tq), lambda kv, j, d: (kv, 0, 0, 0)))
        out_shape.append(SDS((4, nd, tk, tq), F32))
    scratch = [pltpu.VMEM((hp, tk, hd), F32), pltpu.VMEM((hp, tk, hd), F32)] if W is None else []
    sem = ("parallel", "arbitrary", "arbitrary")
    if carry is not None:
        ins += carry.ins
        in_specs += carry.in_specs
        out_specs = carry.out_specs + out_specs
        out_shape = carry.out_shape + out_shape
        scratch += carry.scratch
        sem = ("arbitrary",) * 3
    res = _call(body, name=name, grid=(n_kv // hp, nk, ngd), in_specs=in_specs, out_specs=out_specs,
                out_shape=out_shape, scratch=scratch, sem=sem)(*ins)
    main = res[nc:]
    return (*main[:3], main[3] if want_dbias else None, list(res[:nc]))


def _attn_bwd_pre(name, dbr, o_all, u):
    S = dbr.shape[0]
    ts = min(S, ROW_TILE)
    hd = HEAD_DIM

    assert all(g % 4 == 0 for g in GCOL)

    def gcol(n):
        return GCOL[0] // 4 + n * 4 - jnp.where(n >= 2, 1, 0)

    def body(dbr_ref, o_ref, g_ref, do_ref, dg_ref, dl_ref):
        for hh in range(4):
            lanes = slice(hh * hd, (hh + 1) * hd)
            db = dbr_ref[:, lanes].astype(F32)
            o = o_ref[:, lanes].astype(F32)
            gv = g_ref[:, lanes].astype(F32)
            sg = _sigmoid(gv)
            do = db * (gv * sg)
            do_ref[:, lanes] = do.astype(BF16)
            dg_ref[:, lanes] = (db * o * (sg * (1.0 + gv * (1.0 - sg)))).astype(BF16)
            dl_ref[hh] = jnp.sum((do * o).T, axis=0, keepdims=True)

    own = pl.BlockSpec((ts, 4 * hd), lambda i, n: (i, n))
    stacked = pl.BlockSpec((ts, 4 * hd), lambda i, n: (n * (S // ts) + i, 0))
    return _call(body, name=name, grid=(S // ts, 4),
                 in_specs=[own, own, pl.BlockSpec((ts, 4 * hd), lambda i, n: (i, gcol(n)))],
                 out_specs=[stacked, own, pl.BlockSpec((4, 1, ts), lambda i, n: (n, 0, i))],
                 out_shape=[SDS((4 * S, 4 * hd), BF16), SDS((S, 16 * hd), BF16), SDS((16, 1, S), F32)],
                 sem=("parallel", "parallel"))(dbr, o_all, u)


def _sink_grad(name, sink, lse, delta):
    S = lse.shape[2]
    ts = min(S, 2048)

    def body(sink_ref, lse_ref, dl_ref, out_ref):
        @pl.when(pl.program_id(1) == 0)
        def _():
            out_ref[...] = jnp.zeros_like(out_ref)

        sk = sink_ref[pl.program_id(0)]
        part = jnp.sum(jnp.exp(sk - lse_ref[0]) * dl_ref[0], axis=1, keepdims=True)
        out_ref[0] += -jnp.broadcast_to(part, (1, 128))

    col = pl.BlockSpec((1, 1, ts), lambda h, i: (h, 0, i))
    return _call(body, name=name, grid=(4, S // ts),
                 in_specs=[pl.BlockSpec(memory_space=pltpu.SMEM), col, col],
                 out_specs=pl.BlockSpec((1, 1, 128), lambda h, i: (h, 0, 0)),
                 out_shape=SDS((4, 1, 128), F32), sem=("parallel", "arbitrary"))(sink, lse, delta)


def _bias_maps(T, W):
    rpb = T // GRID_W
    nd = 2 * W + 1
    rmap = np.zeros((nd, rpb, rpb, 15), np.float32)
    for df in range(nd):
        for a in range(rpb):
            for b in range(rpb):
                r = (df - W) * rpb + b - a + 7
                if 0 <= r < 15:
                    rmap[df, a, b, r] = 1.0
    cmap = np.zeros((GRID_W, GRID_W, 31), np.float32)
    for q in range(GRID_W):
        for k in range(GRID_W):
            cmap[q, k, int(np.clip(k - q, -15, 15)) + 15] = 1.0
    return jnp.asarray(rmap), jnp.asarray(cmap)


def _bias_tiles(rel_bias, S, T):
    W = -(-7 // (T // GRID_W))
    rmap, cmap = _bias_maps(T, W)
    t = jnp.einsum("dabr,hrc,qkc->hdbkaq", rmap, rel_bias, cmap, precision=lax.Precision.HIGHEST)
    return _with_off_tile(t.reshape(1, 4, 2 * W + 1, T, T) * (1.0 / SCALE) + jnp.asarray(_mask_tiles_d(S, T)))


def _bias_tiles_t(dtiles, T):
    rpb = T // GRID_W
    W = -(-7 // rpb)
    rmap, cmap = _bias_maps(T, W)
    t = dtiles.reshape(4, 2 * W + 1, rpb, GRID_W, rpb, GRID_W)
    return jnp.einsum("dabr,hdbkaq,qkc->hrc", rmap, t, cmap, precision=lax.Precision.HIGHEST)


def _mixer_cfg(S):
    ta, tb, td = (min(S, ATT_TILE[k]) for k in "abd")
    cq, ck = (min(S, t) for t in ATT_TILE["c"])
    bq, bk = (min(S, t) for t in ATT_TILE["c_bwd"])
    a = _Mixer(ta, ta, _band(128, ta), 2)
    b = _Mixer(tb, tb, _band(1024, tb), 1, hp=2)
    d = _Mixer(td, td, -(-7 // (td // GRID_W)), 1, _edge_blocks_d(td), hp=2)
    return {"a": (a, a, jnp.asarray(_mask_tiles_a(ta))), "b": (b, b, jnp.asarray(_mask_tiles_b(tb))),
            "c": (_Mixer(cq, ck, None, 2), _Mixer(bq, bk, None, 2), None), "d": (d, d, None)}


def _layer_fwd(l, x, c8, lw, p, tabs, next_shards, me, late=None):
    S, D = x.shape
    dq = D // N_SHARD
    ada_sh, win_sh, wgm_sh, wb_sh, wout_sh = lw
    one_d, axial = tabs
    cfg = _mixer_cfg(S)
    tm = min(S, MM_ROWS)

    ada = _matmul(f"ada_l{l}", (N_SHARD,), [c8, ada_sh, p["b_ada"][l][None, :]],
                  [pl.BlockSpec((8, D), lambda j: (0, 0)), pl.BlockSpec((None, D, 3 * dq), lambda j: (j, 0, 0)),
                   pl.BlockSpec((1, 3 * dq), lambda j: (0, j))],
                  [SDS((8, 3 * D), F32)], [pl.BlockSpec((8, 3 * dq), lambda j: (0, j))],
                  epilogue=lambda acc, b_ref: (acc + b_ref[...],), a_fn=lambda a: a * _sigmoid(a))[0][0:1]
    shift, scale, gate = ada[:, :D], ada[:, D:2 * D], ada[:, 2 * D:]
    g_row = p["norm_g"][l][None, :]
    h = _norm_mod(f"norm_mod_l{l}", x, g_row, scale, shift)

    loads = {k: [] for k in ("mm_in", "attn_b", "attn_c", "attn_d", "merge")}
    tags = {k: [] for k in loads}
    got = {}

    def ride(host, tag, shard):
        loads[host].append(shard)
        tags[host].append(tag)

    def arrived(host, arrays):
        got.update(zip(tags[host], arrays))

    if late is not None:
        ride("mm_in", "late_gm", late[0])
        ride("attn_b", "late_wb", late[1])
        ride("attn_b", "late_out", late[2])
    if next_shards is not None:
        s_ada, s_in, s_gm, s_wb, s_out = next_shards
        ride("attn_c", "gm", s_gm)
        ride("merge", "in", s_in)
        ride("mm_in" if late is None else "attn_c", "ada", s_ada)
        ride("attn_b" if late is None else "attn_d", "wb", s_wb)
        ride("attn_b" if late is None else "attn_d", "out", s_out)
    carries = {k: _Carry("gather", v) if v else None for k, v in loads.items()}

    res_in = _matmul(f"mm_in_l{l}", (S // tm, N_SHARD), [h, win_sh],
                     [pl.BlockSpec((tm, D), lambda i, j: (i, 0)),
                      pl.BlockSpec((None, D, IN_SHARD), lambda i, j: (j, 0, 0))],
                     [SDS((S, IN_COLS), BF16)], [pl.BlockSpec((tm, IN_SHARD), lambda i, j: (i, j))],
                     carry=carries["mm_in"])
    if carries["mm_in"] is not None:
        arrived("mm_in", res_in[1])
        res_in = res_in[0]
    u = res_in[0]

    qa = _rope_fwd(f"rope_qa_l{l}", u, QCOL[0], 4, one_d, 64)
    kv_a = _rope_fwd(f"rope_ka_l{l}", u, KCOL[0], 2, one_d, 64, tail=2)
    qb = _rope_fwd(f"rope_qb_l{l}", u, QCOL[1], 4, one_d, 64)
    kv_b = _rope_fwd(f"rope_kb_l{l}", u, KCOL[1], 4, one_d, 64, tail=4)
    qc = _rope_fwd(f"rope_qc_l{l}", u, QCOL[2], 4, axial, 32, p["c_q_norm"][l][None, :])
    kv_c = _rope_fwd(f"rope_kc_l{l}", u, KCOL[2], 2, axial, 32, p["c_k_norm"][l][None, :], tail=2)

    no_sink = jnp.full((4,), NEG_INF, F32)
    bias = _bias_tiles(p["d_rel_bias"][l], S, cfg["d"][0].tq)
    hd = HEAD_DIM
    kv_d = u[:, KCOL[3] * hd:(KCOL[3] + 8) * hd]
    qd = u[:, QCOL[3] * hd:(QCOL[3] + 4) * hd]
    br_a, o_a, lse_a, got_a = _attn_fwd(f"attn_a_l{l}", cfg["a"][0], qa, 0, kv_a, 0, kv_a, 2, u, GCOL[0],
                                        p["a_sink"][l], cfg["a"][2])
    br_b, o_b, lse_b, got_b = _attn_fwd(f"attn_b_l{l}", cfg["b"][0], qb, 0, kv_b, 0, kv_b, 4, u, GCOL[1], no_sink,
                                        cfg["b"][2], carries["attn_b"])
    br_c, o_c, lse_c, got_c = _attn_fwd(f"attn_c_l{l}", cfg["c"][0], qc, 0, kv_c, 0, kv_c, 2, u, GCOL[2], no_sink,
                                        None, carries["attn_c"])
    br_d, o_d, lse_d, got_d = _attn_fwd(f"attn_d_l{l}", cfg["d"][0], qd, 0, kv_d, 0, kv_d, 4, u, GCOL[3],
                                        no_sink, bias, carries["attn_d"])
    arrived("attn_b", got_b)
    arrived("attn_c", got_c)
    arrived("attn_d", got_d)
    if late is not None:
        wgm_sh, wb_sh, wout_sh = _own_slot([got["late_gm"], got["late_wb"], got["late_out"]], late, me)
    br = jnp.concatenate([br_a, br_b, br_c, br_d], axis=1)
    o_all = jnp.concatenate([o_a, o_b, o_c, o_d], axis=1)

    def merge_body(h_ref, wg_ref, br_ref, wb_ref, mg_ref, pj_ref, z_ref, acc_ref):
        n = pl.program_id(2)
        mgv = _sigmoid(lax.dot_general(h_ref[...], wg_ref[...], _NN, preferred_element_type=F32))
        pj = lax.dot_general(br_ref[...], wb_ref[...], _NN, preferred_element_type=F32)
        mg_ref[...] = mgv.astype(BF16)
        pj_ref[...] = pj.astype(BF16)

        @pl.when(n == 0)
        def _():
            acc_ref[...] = mgv * pj

        @pl.when(n > 0)
        def _():
            acc_ref[...] += mgv * pj

        @pl.when(n == 3)
        def _():
            z_ref[...] = acc_ref[...].astype(BF16)

    wide = pl.BlockSpec((tm, dq), lambda i, j, n: (i, n * N_SHARD + j))
    (mg, proj, z), got_m = _call_carrying(
        merge_body, carries["merge"], [h, wgm_sh, br, wb_sh], name=f"merge_l{l}", grid=(S // tm, N_SHARD, 4),
        in_specs=[pl.BlockSpec((tm, D), lambda i, j, n: (i, 0)),
                  pl.BlockSpec((None, D, dq), lambda i, j, n: (n, 0, j)),
                  pl.BlockSpec((tm, BRANCH_W), lambda i, j, n: (i, n)),
                  pl.BlockSpec((None, None, BRANCH_W, dq), lambda i, j, n: (j, n, 0, 0))],
        out_specs=[wide, wide, pl.BlockSpec((tm, dq), lambda i, j, n: (i, j))],
        out_shape=[SDS((S, 4 * D), BF16), SDS((S, 4 * D), BF16), SDS((S, D), BF16)],
        scratch=[pltpu.VMEM((tm, dq), F32)], sem=("parallel", "parallel", "arbitrary"))
    arrived("merge", got_m)
    next_lw = None
    if next_shards is not None:
        next_lw = _own_slot([got[t] for t in ("ada", "in", "gm", "wb", "out")], next_shards, me)

    tn = min(D, 1024)
    x_new, o2 = _matmul(
        f"mm_out_l{l}", (S // tm, D // tn), [z, wout_sh, x, gate],
        [pl.BlockSpec((tm, D), lambda i, j: (i, 0)), pl.BlockSpec((N_SHARD, dq, tn), lambda i, j: (0, 0, j)),
         pl.BlockSpec((tm, tn), lambda i, j: (i, j)), pl.BlockSpec((1, tn), lambda i, j: (0, j))],
        [SDS((S, D), F32), SDS((S, D), BF16)],
        [pl.BlockSpec((tm, tn), lambda i, j: (i, j)), pl.BlockSpec((tm, tn), lambda i, j: (i, j))],
        k_inner=N_SHARD, epilogue=lambda acc, x_ref, g_ref: (x_ref[...] + g_ref[...] * acc, acc))
    res = dict(x=x, h=h, u=u, qa=qa, qb=qb, qc=qc, qd=qd, kv=(kv_a, kv_b, kv_c, kv_d), br=br, o_all=o_all,
               lse=(lse_a, lse_b, lse_c, lse_d), bias=bias, mg=mg, proj=proj, z=z, o2=o2,
               g_row=g_row, scale=scale, gate=gate)
    return x_new, res, (ada_sh, win_sh, wgm_sh, wb_sh, wout_sh), next_lw


def _layer_bwd(l, dxo, r, lw, p, tabs, pending, c_idx):
    x, h, u = r["x"], r["h"], r["u"]
    S, D = x.shape
    dq = D // N_SHARD
    ada_sh, win_sh, wgm_sh, wb_sh, wout_sh = lw
    one_d, axial = tabs
    cfg = _mixer_cfg(S)
    tm = min(S, MM_ROWS)
    tk = min(S, 1024)
    tn = min(D, 1024)

    do2, dgate = _out_bwd_ew(f"out_bwd_l{l}", dxo, r["o2"], r["gate"])
    dz = _matmul(f"mm_dz_l{l}", (S // tm, N_SHARD), [do2, wout_sh],
                 [pl.BlockSpec((tm, D), lambda i, n: (i, 0)), pl.BlockSpec((None, dq, D), lambda i, n: (n, 0, 0))],
                 [SDS((S, D), BF16)], [pl.BlockSpec((tm, dq), lambda i, n: (i, n))], tb=True)[0]
    tkw = min(S, 2048)
    g_out = _matmul(f"mm_gwout_l{l}", (N_SHARD, D // tn, S // tkw), [r["z"], do2],
                    [pl.BlockSpec((tkw, dq), lambda n, j, k: (k, n)), pl.BlockSpec((tkw, tn), lambda n, j, k: (k, j))],
                    [SDS((N_SHARD, dq, D), F32)], [pl.BlockSpec((None, dq, tn), lambda n, j, k: (n, 0, j))],
                    ta=True, k_axis=2, acc_shape=(dq, tn))[0]

    dmg, dproj = _merge_bwd_ew(f"merge_bwd_l{l}", dz, r["mg"], r["proj"])
    nj = D // tn
    g_gm = _matmul(f"mm_gwgm_l{l}", (4, D // tn, nj, S // tk), [h, dmg],
                   [pl.BlockSpec((tk, tn), lambda n, i, j, k: (k, i)),
                    pl.BlockSpec((tk, tn), lambda n, i, j, k: (k, n * nj + j))],
                   [SDS((4, D, D), F32)], [pl.BlockSpec((None, tn, tn), lambda n, i, j, k: (n, i, j))],
                   ta=True, k_axis=3, acc_shape=(tn, tn))[0]
    dh1 = _matmul(f"mm_dh1_l{l}", (S // tm, nj, 4), [dmg, wgm_sh],
                  [pl.BlockSpec((tm, D), lambda i, j, n: (i, n)),
                   pl.BlockSpec((None, tn, D), lambda i, j, n: (n, j, 0))],
                  [SDS((S, D), F32)], [pl.BlockSpec((tm, tn), lambda i, j, kk: (i, j))],
                  tb=True, k_axis=2, acc_shape=(tm, tn))[0]
    dbr = _matmul(f"mm_dbr_l{l}", (S // tm, 4), [dproj, wb_sh],
                  [pl.BlockSpec((tm, D), lambda i, n: (i, n)),
                   pl.BlockSpec((N_SHARD, None, BRANCH_W, dq), lambda i, n: (0, n, 0, 0))],
                  [SDS((S, 4 * BRANCH_W), BF16)], [pl.BlockSpec((tm, BRANCH_W), lambda i, n: (i, n))],
                  tb=True, k_inner=N_SHARD)[0]
    tkl = min(S, 2048)
    g_wb = _matmul(f"mm_gwb_l{l}", (N_SHARD, 4, S // tkl), [r["br"], dproj],
                   [pl.BlockSpec((tkl, BRANCH_W), lambda j, n, k: (k, n)),
                    pl.BlockSpec((tkl, dq), lambda j, n, k: (k, n * N_SHARD + j))],
                   [SDS((N_SHARD, 4, BRANCH_W, dq), F32)],
                   [pl.BlockSpec((None, None, BRANCH_W, dq), lambda j, n, k: (j, n, 0, 0))],
                   ta=True, k_axis=2, acc_shape=(BRANCH_W, dq))[0]

    do_all, dg_all, delta = _attn_bwd_pre(f"attn_pre_l{l}", dbr, r["o_all"], u)
    lse_a, lse_b, lse_c, lse_d = r["lse"]
    dsink = _sink_grad(f"sink_grad_l{l}", p["a_sink"][l], lse_a, delta)[:, 0, 0]
    carries = [None] * 3
    if pending is not None:
        p_in, p_gm, p_wb, p_out = pending
        carries = [_Carry("exchange", [p_in]), _Carry("exchange", [p_gm]), _Carry("exchange", [p_wb, p_out])]
    kv_a, kv_b, kv_c, kv_d = r["kv"]
    v_gm, v_wb, v_out = (_half_view(g) for g in (g_gm, g_wb, g_out))
    dqa, dka, dva, _, from_sib = _attn_bwd(f"attn_a_bwd_l{l}", cfg["a"][1], r["qa"], 0, kv_a, 0, kv_a, 2, do_all, 0,
                                           lse_a, delta, cfg["a"][2], carry=_Carry("pair", [v_gm, v_wb, v_out]))
    dqb, dkb, dvb, _, got_b = _attn_bwd(f"attn_b_bwd_l{l}", cfg["b"][1], r["qb"], 0, kv_b, 0, kv_b, 4, do_all, 4,
                                        lse_b, delta, cfg["b"][2], carry=carries[0])
    dqc, dkc, dvc, _, got_c = _attn_bwd(f"attn_c_bwd_l{l}", cfg["c"][1], r["qc"], 0, kv_c, 0, kv_c, 2, do_all, 8,
                                        lse_c, delta, carry=carries[1])
    dqd, dkd, dvd, dbias, got_d = _attn_bwd(f"attn_d_bwd_l{l}", cfg["d"][1], r["qd"], 0, kv_d, 0, kv_d, 4,
                                            do_all, 12, lse_d, delta, r["bias"], True, carries[2])
    arrived = None if pending is None else [got_b[0], got_c[0], got_d[0], got_d[1]]
    d_rel = _bias_tiles_t(dbias, cfg["d"][1].tq)

    duqa, _ = _rope_bwd(f"rope_qa_bwd_l{l}", dqa, u, QCOL[0], 4, one_d, 64)
    duka, _ = _rope_bwd(f"rope_ka_bwd_l{l}", dka, u, KCOL[0], 2, one_d, 64)
    duqb, _ = _rope_bwd(f"rope_qb_bwd_l{l}", dqb, u, QCOL[1], 4, one_d, 64)
    dukb, _ = _rope_bwd(f"rope_kb_bwd_l{l}", dkb, u, KCOL[1], 4, one_d, 64)
    duqc, dcq = _rope_bwd(f"rope_qc_bwd_l{l}", dqc, u, QCOL[2], 4, axial, 32, p["c_q_norm"][l][None, :])
    dukc, dck = _rope_bwd(f"rope_kc_bwd_l{l}", dkc, u, KCOL[2], 2, axial, 32, p["c_k_norm"][l][None, :])
    bw = BRANCH_W
    du = jnp.concatenate(
        [duqa, duka, dva.astype(BF16), dg_all[:, 0:bw],
         duqb, dukb, dvb.astype(BF16), dg_all[:, bw:2 * bw],
         duqc, dukc, dvc.astype(BF16), dg_all[:, 2 * bw:3 * bw],
         dqd.astype(BF16), dkd.astype(BF16), dvd.astype(BF16), dg_all[:, 3 * bw:]], axis=1)

    tmi = min(D, 1024)
    g_in = _matmul(f"mm_gwin_l{l}", (N_SHARD, D // tmi, S // tk), [h, du],
                   [pl.BlockSpec((tk, tmi), lambda j, i, k: (k, i)), pl.BlockSpec((tk, IN_SHARD), lambda j, i, k: (k, j))],
                   [SDS((N_SHARD, D, IN_SHARD), F32)], [pl.BlockSpec((None, tmi, IN_SHARD), lambda j, i, k: (j, i, 0))],
                   ta=True, k_axis=2, acc_shape=(tmi, IN_SHARD))[0]
    dh2 = _matmul(f"mm_dh2_l{l}", (S // tm, nj, N_SHARD), [du, win_sh],
                  [pl.BlockSpec((tm, IN_SHARD), lambda i, j, k: (i, k)),
                   pl.BlockSpec((None, tn, IN_SHARD), lambda i, j, k: (k, j, 0))],
                  [SDS((S, D), F32)], [pl.BlockSpec((tm, tn), lambda i, j, k: (i, j))],
                  tb=True, k_axis=2, acc_shape=(tm, tn))[0]

    dx_prev, dshift, dscale, dng = _norm_mod_bwd(f"norm_mod_bwd_l{l}", x, dh1, dh2, dxo, r["g_row"], r["scale"])
    d_ada = jnp.concatenate([dshift, dscale, dgate], axis=1)[0]
    big = (g_in, g_gm, g_wb, g_out)
    small = dict(norm_g=dng[0], b_ada=d_ada, a_sink=dsink, c_q_norm=dcq[0], c_k_norm=dck[0], d_rel_bias=d_rel)
    v_in = _half_view(g_in)
    views = [v_in, v_gm, v_wb, v_out]
    recv1 = list(_pair_send_half(f"rs_pair_send_l{l}", [v_in])) + list(from_sib)
    parts = [_add_half(f"rs_add_half{a}_l{l}", v, r1, c_idx) for a, (v, r1) in enumerate(zip(views, recv1))]
    return dx_prev, big, small, arrived, parts


def _place():
    return lax.axis_index("x"), lax.axis_index("y"), lax.axis_index("c")


class _Carry:
    def __init__(self, kind, arrays):
        self.kind, self.n, self.ins = kind, len(arrays), list(arrays)
        any_spec = pl.BlockSpec(memory_space=pl.ANY)
        self.in_specs = [any_spec] * self.n
        self.out_specs = [any_spec] * self.n
        if kind == "gather":
            self.out_shape = [SDS((N_SHARD, *a.shape), a.dtype) for a in arrays]
        elif kind == "pair":
            self.out_shape = [SDS((a.shape[0], *a.shape[2:]), a.dtype) for a in arrays]
        else:
            self.out_shape = [SDS((3, *a.shape[1:]), a.dtype) for a in arrays]
        self.scratch = [pltpu.SemaphoreType.DMA((self.n, 3)), pltpu.SemaphoreType.DMA((self.n, 3))]

    def _copies(self, ins, outs, send_sems, recv_sems, arriving):
        x, y, c = _place()
        if self.kind == "pair":
            return [pltpu.make_async_remote_copy(ins[a].at[:, 1 - c], outs[a], send_sems.at[a, 0], recv_sems.at[a, 0],
                                                 device_id=(x, y, 1 - c), device_id_type=MESH)
                    for a in range(self.n)]
        cps = []
        for a in range(self.n):
            for k, (px, py) in enumerate([(1 - x, y), (x, 1 - y), (1 - x, 1 - y)]):
                if self.kind == "gather":
                    src, dst = ins[a], outs[a].at[2 * px + py if arriving else 2 * x + y]
                else:
                    src, dst = ins[a].at[2 * px + py], outs[a].at[k]
                cps.append(pltpu.make_async_remote_copy(src, dst, send_sems.at[a, k], recv_sems.at[a, k],
                                                        device_id=(px, py, c), device_id_type=MESH))
        return cps

    def start(self, ins, outs, send_sems, recv_sems):
        for cp in self._copies(ins, outs, send_sems, recv_sems, False):
            cp.start()

    def finish(self, ins, outs, send_sems, recv_sems):
        for cp in self._copies(ins, outs, send_sems, recv_sems, True):
            cp.wait_recv()
        for cp in self._copies(ins, outs, send_sems, recv_sems, False):
            cp.wait_send()


def _run_carry(name, carry):
    n = carry.n

    def body(*refs):
        args = (refs[:n], refs[n:2 * n], refs[2 * n], refs[2 * n + 1])
        carry.start(*args)
        carry.finish(*args)

    return pl.pallas_call(body, name=name, in_specs=carry.in_specs, out_specs=carry.out_specs,
                          out_shape=carry.out_shape, scratch_shapes=carry.scratch)(*carry.ins)


def _own_slot(gathered, shards, me):
    return [lax.dynamic_update_index_in_dim(g, s, me, 0) for g, s in zip(gathered, shards)]


def _gather_small(name, v):
    m_per, n = v.shape

    def body(x_ref, out_ref, send_sems, recv_sems, local_sem):
        x, y, c = _place()
        me, sibling = (x, y, c), (x, y, 1 - c)
        chips = [(1 - x, y), (x, 1 - y), (1 - x, 1 - y)]

        def rows(px, py, pc):
            return out_ref.at[pl.ds((4 * px + 2 * py + pc) * m_per, m_per), :]

        def copy(k, block, to, src=None):
            return pltpu.make_async_remote_copy(
                src_ref=rows(*block) if src is None else src, dst_ref=rows(*block),
                send_sem=send_sems.at[k], recv_sem=recv_sems.at[k], device_id=to, device_id_type=MESH)

        mine = pltpu.make_async_copy(x_ref, rows(*me), local_sem)
        mine.start()
        first = [copy(0, me, sibling, src=x_ref)]
        first += [copy(1 + j, me, (*chip, c), src=x_ref) for j, chip in enumerate(chips)]
        for cp in first:
            cp.start()
        passed = [copy(4 + j, (*chip, c), sibling) for j, chip in enumerate(chips)]
        for j, chip in enumerate(chips):
            copy(1 + j, (*chip, c), me).wait_recv()
            passed[j].start()
        copy(0, sibling, me).wait_recv()
        for j, chip in enumerate(chips):
            copy(4 + j, (*chip, 1 - c), me).wait_recv()
        for cp in first + passed:
            cp.wait_send()
        mine.wait()

    return pl.pallas_call(
        body, name=name, out_shape=SDS((8 * m_per, n), v.dtype),
        in_specs=[pl.BlockSpec(memory_space=pltpu.VMEM)], out_specs=pl.BlockSpec(memory_space=pltpu.VMEM),
        scratch_shapes=[pltpu.SemaphoreType.DMA((7,)), pltpu.SemaphoreType.DMA((7,)), pltpu.SemaphoreType.DMA])(v)


def _pair_send_half(name, grads):
    n = len(grads)

    def body(*refs):
        ins, outs = refs[:n], refs[n:2 * n]
        send_sems, recv_sems = refs[2 * n:]
        x, y, c = _place()
        cps = []
        for a in range(n):
            cp = pltpu.make_async_remote_copy(ins[a].at[:, 1 - c], outs[a], send_sems.at[a], recv_sems.at[a],
                                              device_id=(x, y, 1 - c), device_id_type=MESH)
            cp.start()
            cps.append(cp)
        for cp in cps:
            cp.wait_recv()
        for cp in cps:
            cp.wait_send()

    any_spec = pl.BlockSpec(memory_space=pl.ANY)
    return pl.pallas_call(
        body, name=name, in_specs=[any_spec] * n, out_specs=[any_spec] * n,
        out_shape=[SDS((g.shape[0], *g.shape[2:]), g.dtype) for g in grads],
        scratch_shapes=[pltpu.SemaphoreType.DMA((n,)), pltpu.SemaphoreType.DMA((n,))])(*grads)


def _pair_gather(name, halves):
    n = len(halves)

    def body(*refs):
        outs = refs[n:2 * n]
        send_sems, recv_sems = refs[2 * n:]
        x, y, c = _place()
        cps = [pltpu.make_async_remote_copy(outs[a].at[c], outs[a].at[c], send_sems.at[a], recv_sems.at[a],
                                            device_id=(x, y, 1 - c), device_id_type=MESH) for a in range(n)]
        for cp in cps:
            cp.start()
        for a in range(n):
            pltpu.make_async_remote_copy(outs[a].at[c], outs[a].at[1 - c], send_sems.at[a], recv_sems.at[a],
                                         device_id=(x, y, 1 - c), device_id_type=MESH).wait_recv()
        for cp in cps:
            cp.wait_send()

    any_spec = pl.BlockSpec(memory_space=pl.ANY)
    return pl.pallas_call(
        body, name=name, in_specs=[any_spec] * n, out_specs=[any_spec] * n,
        out_shape=[SDS(g.shape, g.dtype) for g in halves], input_output_aliases={a: a for a in range(n)},
        scratch_shapes=[pltpu.SemaphoreType.DMA((n,)), pltpu.SemaphoreType.DMA((n,))])(*halves)


def _add_half(name, g, recv, c_idx):
    _, _, R, C = g.shape
    tr = min(R, 256)

    def body(c_ref, g_ref, r_ref, o_ref):
        o_ref[...] = (g_ref[...] + r_ref[...]).astype(BF16)

    return _call(body, name=name, grid=(4, R // tr), nsp=1,
                 in_specs=[pl.BlockSpec((None, None, tr, C), lambda j, r, c_ref: (j, c_ref[0], r, 0)),
                           pl.BlockSpec((None, tr, C), lambda j, r, c_ref: (j, r, 0))],
                 out_specs=pl.BlockSpec((None, tr, C), lambda j, r, c_ref: (j, r, 0)),
                 out_shape=SDS((4, R, C), BF16), sem=("parallel", "parallel"))(c_idx, g, recv)


def _add_shards(name, part, recv, idx):
    _, R, C = part.shape
    tr = min(R, 256)

    def body(idx_ref, p_ref, r_ref, o_ref):
        o_ref[...] = (((p_ref[...].astype(F32) + r_ref[0].astype(F32)) + r_ref[1].astype(F32))
                      + r_ref[2].astype(F32))

    return _call(body, name=name, grid=(R // tr,), nsp=1,
                 in_specs=[pl.BlockSpec((None, tr, C), lambda r, idx_ref: (idx_ref[0], r, 0)),
                           pl.BlockSpec((3, tr, C), lambda r, idx_ref: (0, r, 0))],
                 out_specs=pl.BlockSpec((None, tr, C), lambda r, idx_ref: (idx_ref[1], r, 0)),
                 out_shape=SDS((2, R, C), F32), sem=("parallel",))(idx, part, recv)


def _half_view(g):
    rows = g.shape[-2] if g.ndim == 3 else g.shape[1] * g.shape[2]
    return g.reshape(N_SHARD, 2, rows // 2, g.shape[-1])


def _finish_reduce_layer(l, big, parts, recv2, idx):
    halves = [_add_shards(f"rs_add_shards{a}_l{l}", pt, r2, idx) for a, (pt, r2) in enumerate(zip(parts, recv2))]
    full = _pair_gather(f"rs_pair_gather_l{l}", halves)
    return [f.reshape(g.shape[1:]) for f, g in zip(full, big)]


def _adamw_math(w, g, m, v):
    m = ADAM_B1 * m + (1.0 - ADAM_B1) * g
    v = ADAM_B2 * v + (1.0 - ADAM_B2) * (g * g)
    m_hat = m / (1.0 - ADAM_B1 ** ADAM_STEP)
    v_hat = v / (1.0 - ADAM_B2 ** ADAM_STEP)
    delta = -ADAM_LR * (m_hat / (jnp.sqrt(v_hat) + ADAM_EPS) + ADAM_WD * w)
    return delta, m, v


def _adamw(name, w, g, m, v):
    shape = w.shape
    C = shape[-1]
    R = int(np.prod(shape[:-1]))
    tr = min(R, 256)

    def body(w_ref, g_ref, m_ref, v_ref, d_ref, nm_ref, nv_ref):
        d, nm, nv = _adamw_math(w_ref[...], g_ref[...], m_ref[...], v_ref[...])
        d_ref[...] = d
        nm_ref[...] = nm
        nv_ref[...] = nv

    blk = pl.BlockSpec((tr, C), lambda i: (i, 0))
    outs = _call(body, name=name, grid=(R // tr,), in_specs=[blk] * 4, out_specs=[blk] * 3,
                 out_shape=[SDS((R, C), F32)] * 3, sem=("parallel",))(*(a.reshape(R, C) for a in (w, g, m, v)))
    return [o.reshape(shape) for o in outs]


def _adamw_small(name, w, g8, m, v):
    R = w.shape[0]

    def body(w_ref, g_ref, m_ref, v_ref, go_ref, d_ref, nm_ref, nv_ref):
        g = g_ref[0]
        for b in range(1, 8):
            g = g + g_ref[b]
        d, nm, nv = _adamw_math(w_ref[...], g, m_ref[...], v_ref[...])
        go_ref[...] = g
        d_ref[...] = d
        nm_ref[...] = nm
        nv_ref[...] = nv

    blk = pl.BlockSpec((R, 128), lambda i: (0, 0))
    return _call(body, name=name, grid=(1,), in_specs=[blk, pl.BlockSpec((8, R, 128), lambda i: (0, 0, 0)), blk, blk],
                 out_specs=[blk] * 4, out_shape=[SDS((R, 128), F32)] * 4, sem=("arbitrary",))(w, g8, m, v)


SMALL_NAMES = ("norm_g", "b_ada", "a_sink", "c_q_norm", "c_k_norm", "d_rel_bias", "final_g")


def _pack(parts, extra_rows=0):
    flat = jnp.concatenate([a.reshape(-1) for a in parts])
    rows = -(-flat.shape[0] // 128)
    rows = -(-rows // 8) * 8 + extra_rows
    return jnp.pad(flat, (0, rows * 128 - flat.shape[0])).reshape(rows, 128)


def _unpack(packed, like):
    flat = packed.reshape(-1)
    out, off = [], 0
    for a in like:
        out.append(flat[off:off + a.size].reshape(a.shape))
        off += a.size
    return out


def _device_step(x, c8, tgt, shards, p, me, c_idx):
    S = x.shape[0]
    L = len(shards)
    tabs = _rope_tables(S)
    first = _Carry("gather", shards[0][:2])
    lw = [(*_own_slot(_run_carry("gather_w_l0", first), shards[0][:2], me), None, None, None)]
    res = []
    for l in range(L):
        x, r, lw[l], nxt = _layer_fwd(l, x, c8, lw[l], p, tabs, shards[l + 1] if l + 1 < L else None, me,
                                      shards[0][2:] if l == 0 else None)
        res.append(r)
        lw.append(nxt)
    dx, dfg, loss = _final_loss("final_loss", x, tgt, p["final_g"][None, :])
    bigs, smalls, parts, arrived = [None] * L, [None] * L, [None] * L, [None] * L
    for l in reversed(range(L)):
        pending = parts[l + 1] if l + 1 < L else None
        dx, bigs[l], smalls[l], arr, parts[l] = _layer_bwd(l, dx, res[l], lw[l], p, tabs, pending, c_idx)
        if pending is not None:
            arrived[l + 1] = arr
    arrived[0] = _run_carry("rs_shard_exchange_l0", _Carry("exchange", parts[0]))
    idx = jnp.concatenate([jnp.reshape(me, (1,)).astype(I32), c_idx])
    reduced = [_finish_reduce_layer(l, bigs[l], parts[l], arrived[l], idx) for l in range(L)]
    return loss, dx, reduced, smalls, dfg[0]


def kernel(x, c, norm_g, w_ada, b_ada, w_in, a_sink, c_q_norm, c_k_norm, d_rel_bias, w_gate_merge, w_branch, w_out, final_g, loss_target, m_norm_g, m_w_ada, m_b_ada, m_w_in, m_a_sink, m_c_q_norm, m_c_k_norm, m_d_rel_bias, m_w_gate_merge, m_w_branch, m_w_out, m_final_g, v_norm_g, v_w_ada, v_b_ada, v_w_in, v_a_sink, v_c_q_norm, v_c_k_norm, v_d_rel_bias, v_w_gate_merge, v_w_branch, v_w_out, v_final_g):
    L, D = norm_g.shape
    dq = D // N_SHARD
    p = dict(norm_g=norm_g, b_ada=b_ada, a_sink=a_sink, c_q_norm=c_q_norm, c_k_norm=c_k_norm,
             d_rel_bias=d_rel_bias, final_g=final_g)
    xi, yi, ci = _place()
    c_idx = jnp.reshape(ci, (1,)).astype(I32)
    me = 2 * xi + yi

    shards = [[w_ada[l].astype(BF16), w_in[l].astype(BF16), w_gate_merge[l].astype(BF16),
               w_branch[l].astype(BF16), w_out[l].astype(BF16)] for l in range(L)]
    c8 = jnp.broadcast_to(c, (8, D))
    loss, grad_x, reduced, smalls, dfg = _device_step(x[0], c8, loss_target[0], shards, p, me, c_idx)
    loss = lax.psum(loss[0, 0], ("x", "y", "c"))

    small_parts = [jnp.stack([s[n] for s in smalls]) for n in SMALL_NAMES[:-1]] + [dfg]
    packed = _pack(small_parts + [c[0]])
    rows = packed.shape[0]
    g8 = _gather_small("gather_small", packed).reshape(8, rows, 128)
    small_w = [p[n] for n in SMALL_NAMES]
    small_m = [m_norm_g, m_b_ada, m_a_sink, m_c_q_norm, m_c_k_norm, m_d_rel_bias, m_final_g]
    small_v = [v_norm_g, v_b_ada, v_a_sink, v_c_q_norm, v_c_k_norm, v_d_rel_bias, v_final_g]
    pad_c = [jnp.zeros((D,), F32)]
    sg, sd, sm, sv = _adamw_small("adamw_small", _pack(small_w + pad_c), g8, _pack(small_m + pad_c),
                                  _pack(small_v + pad_c))
    sg, sd, sm, sv = (_unpack(a, small_w) for a in (sg, sd, sm, sv))

    n_small = sum(a.size for a in small_parts)
    flat8 = g8.reshape(8, rows * 128)
    c_all = flat8[:, n_small:n_small + D]
    dada_all = flat8[:, L * D:L * D + L * 3 * D].reshape(8, L, 3 * D)
    dada_mine = lax.dynamic_slice_in_dim(dada_all, (2 * xi + yi) * (3 * dq), 3 * dq, axis=2)
    tma = min(D, 1024)
    g_ada = jnp.stack([
        _matmul(f"mm_gwada_l{l}", (D // tma,), [c_all, dada_mine[:, l]],
                [pl.BlockSpec((8, tma), lambda i: (0, i)), pl.BlockSpec((8, 3 * dq), lambda i: (0, 0))],
                [SDS((D, 3 * dq), F32)], [pl.BlockSpec((tma, 3 * dq), lambda i: (i, 0))],
                ta=True, a_fn=lambda a: a * _sigmoid(a))[0] for l in range(L)])

    g_in, g_gm, g_wb, g_out = (jnp.stack([reduced[l][a] for l in range(L)]) for a in range(4))

    big = {}
    for nm, w, g, m, v in (("w_ada", w_ada, g_ada, m_w_ada, v_w_ada), ("w_in", w_in, g_in, m_w_in, v_w_in),
                           ("w_gate_merge", w_gate_merge, g_gm, m_w_gate_merge, v_w_gate_merge),
                           ("w_branch", w_branch, g_wb, m_w_branch, v_w_branch),
                           ("w_out", w_out, g_out, m_w_out, v_w_out)):
        big[nm] = (g, *_adamw(f"adamw_{nm}", w, g, m, v))

    order = ("norm_g", "w_ada", "b_ada", "w_in", "a_sink", "c_q_norm", "c_k_norm", "d_rel_bias",
             "w_gate_merge", "w_branch", "w_out", "final_g")
    cols = [[], [], [], []]
    for nm in order:
        if nm in big:
            vals = big[nm]
        else:
            k = SMALL_NAMES.index(nm)
            vals = (sg[k], sd[k], sm[k], sv[k])
        for col, val in zip(cols, vals):
            col.append(val)
    return (loss, grad_x[None], *cols[0], *cols[1], *cols[2], *cols[3])
```

```python
import functools

import numpy as np
import jax
import jax.numpy as jnp
from jax import lax
from jax.experimental import pallas as pl
from jax.experimental.pallas import tpu as pltpu

F32 = jnp.float32
BF16 = jnp.bfloat16
I32 = jnp.int32
SDS = jax.ShapeDtypeStruct
MESH = pl.DeviceIdType.MESH

HEAD_DIM = 128
GRID_W = 64
EPS = 1e-6
NEG_INF = -1e30
ROPE_THETA = 10000.0
SCALE = HEAD_DIM ** -0.5
LOG2E = 1.4426950408889634
SCALE_LOG2E = SCALE * LOG2E
N_SHARD = 4
BRANCH_W = 512
IN_COLS = 7168
IN_SHARD = IN_COLS // N_SHARD
QCOL = (0, 12, 28, 40)
KCOL = (4, 16, 32, 44)
VCOL = (6, 20, 34, 48)
GCOL = (8, 24, 36, 52)
KV_HEADS = (2, 4, 2, 4)

ADAM_LR = 0.001
ADAM_B1 = 0.9
ADAM_B2 = 0.999
ADAM_EPS = 1e-08
ADAM_WD = 0.01
ADAM_STEP = 10

V7X_VMEM_BYTES = 64 * 1024 * 1024
VMEM_LIMIT = V7X_VMEM_BYTES * 7 // 8

ATT_TILE = {"a": 256, "b": 512, "c": (512, 1024), "c_bwd": (2048, 512), "d": 256}
M_INIT = -1e20
MM_ROWS = 1024
ROW_TILE = 512
EW_ROWS = 256


def _band(reach, tile):
    return -(-reach // tile)


def _call(body, *, name, grid, in_specs, out_specs, out_shape, scratch=(), sem=None, nsp=0):
    params = pltpu.CompilerParams(dimension_semantics=sem, vmem_limit_bytes=VMEM_LIMIT)
    if nsp:
        gs = pltpu.PrefetchScalarGridSpec(num_scalar_prefetch=nsp, grid=grid, in_specs=in_specs,
                                          out_specs=out_specs, scratch_shapes=list(scratch))
        return pl.pallas_call(body, grid_spec=gs, out_shape=out_shape, name=name, compiler_params=params)
    return pl.pallas_call(body, grid=grid, in_specs=in_specs, out_specs=out_specs, out_shape=out_shape,
                          scratch_shapes=list(scratch), name=name, compiler_params=params)


def _call_carrying(body, carry, ins, *, name, grid, in_specs, out_specs, out_shape, scratch=(), sem=None):
    if carry is None:
        res = _call(body, name=name, grid=grid, in_specs=in_specs, out_specs=out_specs, out_shape=out_shape,
                    scratch=scratch, sem=sem)(*ins)
        return list(res), []
    n_in, nc = len(in_specs), carry.n

    def wrapped(*refs):
        cr = (refs[n_in:n_in + nc], refs[n_in + nc:n_in + 2 * nc], refs[-2], refs[-1])
        ids = [pl.program_id(ax) for ax in range(len(grid))]
        first = functools.reduce(lambda a, b: a & b, [i == 0 for i in ids])
        last = functools.reduce(lambda a, b: a & b, [i == g - 1 for i, g in zip(ids, grid)])
        pl.when(first)(lambda: carry.start(*cr))
        body(*refs[:n_in], *refs[n_in + 2 * nc:-2])
        pl.when(last)(lambda: carry.finish(*cr))

    res = _call(wrapped, name=name, grid=grid, in_specs=list(in_specs) + carry.in_specs,
                out_specs=carry.out_specs + list(out_specs), out_shape=carry.out_shape + list(out_shape),
                scratch=list(scratch) + carry.scratch, sem=("arbitrary",) * len(grid))(*ins, *carry.ins)
    return list(res[nc:]), list(res[:nc])


def _sigmoid(x):
    return 1.0 / (1.0 + jnp.exp(-x))


def _matmul(name, grid, ins, in_specs, out_shape, out_specs, *, ta=False, tb=False, k_axis=None,
            acc_shape=None, epilogue=None, a_fn=None, k_inner=None, carry=None):
    n_in = len(ins)
    n_out = len(out_shape)
    nk = grid[k_axis] if k_axis is not None else 1
    dn = (((0 if ta else 1,), (1 if tb else 0,)), ((), ()))

    def body(*refs):
        a = refs[0][...]
        if a_fn is not None:
            a = a_fn(a)
        a = a.astype(BF16)
        if k_inner is None:
            p = lax.dot_general(a, refs[1][...].astype(BF16), dn, preferred_element_type=F32)
        else:
            ck = a.shape[1] // k_inner
            p = None
            for kk in range(k_inner):
                t = lax.dot_general(a[:, kk * ck:(kk + 1) * ck], refs[1][kk].astype(BF16), dn,
                                    preferred_element_type=F32)
                p = t if p is None else p + t
        extra = refs[2:n_in]
        outs = refs[n_in:n_in + n_out]

        def fin(acc):
            vals = epilogue(acc, *extra) if epilogue is not None else (acc,)
            for o_ref, v in zip(outs, vals):
                o_ref[...] = v.astype(o_ref.dtype)

        if k_axis is None:
            fin(p)
        else:
            acc_ref = refs[-1]
            k = pl.program_id(k_axis)

            @pl.when(k == 0)
            def _():
                acc_ref[...] = p

            @pl.when(k > 0)
            def _():
                acc_ref[...] += p

            @pl.when(k == nk - 1)
            def _():
                fin(acc_ref[...])

    sem = tuple("arbitrary" if ax == k_axis else "parallel" for ax in range(len(grid)))
    scratch = [pltpu.VMEM(acc_shape, F32)] if k_axis is not None else []
    res, got = _call_carrying(body, carry, ins, name=name, grid=grid, in_specs=in_specs, out_specs=out_specs,
                              out_shape=out_shape, scratch=scratch, sem=sem)
    return res if carry is None else (res, got)


def _norm_mod(name, x, g, scale, shift):
    S, D = x.shape
    ts = min(S, EW_ROWS)

    def body(x_ref, g_ref, sc_ref, sh_ref, h_ref):
        xv = x_ref[...]
        r = lax.rsqrt(jnp.mean(xv * xv, axis=-1, keepdims=True) + EPS)
        h_ref[...] = (((xv * r) * g_ref[...]) * (1.0 + sc_ref[...]) + sh_ref[...]).astype(BF16)

    row = pl.BlockSpec((1, D), lambda i: (0, 0))
    blk = pl.BlockSpec((ts, D), lambda i: (i, 0))
    return _call(body, name=name, grid=(S // ts,), in_specs=[blk, row, row, row], out_specs=blk,
                 out_shape=SDS((S, D), BF16), sem=("parallel",))(x, g, scale, shift)


def _norm_mod_bwd(name, x, dh1, dh2, dxo, g, scale, carry=None):
    S, D = x.shape
    ts = min(S, EW_ROWS)

    def body(x_ref, a_ref, b_ref, dxo_ref, g_ref, sc_ref, dx_ref, dsh_ref, dsc_ref, dg_ref):
        @pl.when(pl.program_id(0) == 0)
        def _():
            dsh_ref[...] = jnp.zeros_like(dsh_ref)
            dsc_ref[...] = jnp.zeros_like(dsc_ref)
            dg_ref[...] = jnp.zeros_like(dg_ref)

        xv = x_ref[...]
        r = lax.rsqrt(jnp.mean(xv * xv, axis=-1, keepdims=True) + EPS)
        xh = xv * r
        dh = a_ref[...] + b_ref[...]
        gv = g_ref[...]
        one_sc = 1.0 + sc_ref[...]
        dsh_ref[...] += jnp.sum(dh, axis=0, keepdims=True)
        dsc_ref[...] += jnp.sum(dh * xh * gv, axis=0, keepdims=True)
        dg_ref[...] += jnp.sum(dh * xh * one_sc, axis=0, keepdims=True)
        dxh = dh * gv * one_sc
        dx = r * (dxh - xh * jnp.mean(dxh * xh, axis=-1, keepdims=True))
        dx_ref[...] = dxo_ref[...] + dx

    row = pl.BlockSpec((1, D), lambda i: (0, 0))
    blk = pl.BlockSpec((ts, D), lambda i: (i, 0))
    return _call_carrying(body, carry, [x, dh1, dh2, dxo, g, scale], name=name, grid=(S // ts,),
                          in_specs=[blk, blk, blk, blk, row, row], out_specs=[blk, row, row, row],
                          out_shape=[SDS((S, D), F32), SDS((1, D), F32), SDS((1, D), F32), SDS((1, D), F32)],
                          sem=("arbitrary",))


def _out_bwd_ew(name, dxo, o2, gate):
    S, D = dxo.shape
    ts = min(S, EW_ROWS)

    def body(dxo_ref, o2_ref, gt_ref, do2_ref, dgt_ref):
        @pl.when(pl.program_id(0) == 0)
        def _():
            dgt_ref[...] = jnp.zeros_like(dgt_ref)

        d = dxo_ref[...]
        do2_ref[...] = (d * gt_ref[...]).astype(BF16)
        dgt_ref[...] += jnp.sum(d * o2_ref[...].astype(F32), axis=0, keepdims=True)

    row = pl.BlockSpec((1, D), lambda i: (0, 0))
    blk = pl.BlockSpec((ts, D), lambda i: (i, 0))
    return _call(body, name=name, grid=(S // ts,), in_specs=[blk, blk, row], out_specs=[blk, row],
                 out_shape=[SDS((S, D), BF16), SDS((1, D), F32)], sem=("arbitrary",))(dxo, o2, gate)


def _merge_bwd_ew(name, dz, mg, proj):
    S, D = dz.shape
    ts = min(S, ROW_TILE)
    td = min(D, 512)
    nd = D // td

    def body(dz_ref, mg_ref, pj_ref, dmg_ref, dpj_ref):
        d = dz_ref[...].astype(F32)
        m = mg_ref[...].astype(F32)
        dmg_ref[...] = (d * pj_ref[...].astype(F32) * m * (1.0 - m)).astype(BF16)
        dpj_ref[...] = (d * m).astype(BF16)

    wide = pl.BlockSpec((ts, td), lambda i, j, n: (i, n * nd + j))
    return _call(body, name=name, grid=(S // ts, nd, 4),
                 in_specs=[pl.BlockSpec((ts, td), lambda i, j, n: (i, j)), wide, wide],
                 out_specs=[wide, wide], out_shape=[SDS((S, 4 * D), BF16), SDS((S, 4 * D), BF16)],
                 sem=("parallel", "parallel", "arbitrary"))(dz, mg, proj)


def _final_loss(name, x, tgt, g):
    S, D = x.shape
    ts = min(S, EW_ROWS)

    def body(x_ref, t_ref, g_ref, dx_ref, dg_ref, loss_ref):
        @pl.when(pl.program_id(0) == 0)
        def _():
            dg_ref[...] = jnp.zeros_like(dg_ref)
            loss_ref[...] = jnp.zeros_like(loss_ref)

        xv = x_ref[...]
        r = lax.rsqrt(jnp.mean(xv * xv, axis=-1, keepdims=True) + EPS)
        xh = xv * r
        gv = g_ref[...]
        err = xh * gv - t_ref[...]
        row_loss = jnp.mean(err * err, axis=-1, keepdims=True)
        loss_ref[...] += 0.5 * jnp.sum(row_loss, axis=0, keepdims=True)
        dy = err * (1.0 / D)
        dg_ref[...] += jnp.sum(dy * xh, axis=0, keepdims=True)
        dxh = dy * gv
        dx_ref[...] = r * (dxh - xh * jnp.mean(dxh * xh, axis=-1, keepdims=True))

    row = pl.BlockSpec((1, D), lambda i: (0, 0))
    blk = pl.BlockSpec((ts, D), lambda i: (i, 0))
    return _call(body, name=name, grid=(S // ts,), in_specs=[blk, blk, row],
                 out_specs=[blk, row, pl.BlockSpec((1, 128), lambda i: (0, 0))],
                 out_shape=[SDS((S, D), F32), SDS((1, D), F32), SDS((1, 128), F32)],
                 sem=("arbitrary",))(x, tgt, g)


def _rope_tables(S):
    def tables(pos, dim):
        inv = ROPE_THETA ** (-jnp.arange(0, dim, 2, dtype=F32) / dim)
        ang = pos.astype(F32)[:, None] * inv[None, :]
        ang = jnp.concatenate([ang, ang], axis=-1)
        return jnp.cos(ang), jnp.sin(ang)

    pos = jnp.arange(S, dtype=I32)
    lane = np.arange(HEAD_DIM)
    cos1, sin1 = tables(pos, HEAD_DIM)
    up1 = jnp.asarray((lane >= 64).astype(np.float32))[None, :]
    one_d = (cos1, sin1 * up1, -sin1 * (1.0 - up1))
    cr, sr = tables(pos // GRID_W, HEAD_DIM // 2)
    cc, sc = tables(pos % GRID_W, HEAD_DIM // 2)
    cos2 = jnp.concatenate([cr, cc], axis=-1)
    sin2 = jnp.concatenate([sr, sc], axis=-1)
    up2 = jnp.asarray(((lane % 64) >= 32).astype(np.float32))[None, :]
    axial = (cos2, sin2 * up2, -sin2 * (1.0 - up2))
    return one_d, axial


def _rope_fwd(name, src, c0, nb, tabs, sh, gain=None, tail=0):
    S = src.shape[0]
    ts = min(S, ROW_TILE)
    has_gain = gain is not None
    roped, nb = nb, nb + tail
    assert c0 % nb == 0
    hd = HEAD_DIM

    def body(*refs):
        x_ref, c_ref, sa_ref, sb_ref = refs[:4]
        o_ref = refs[-1]
        cv, sa, sb = c_ref[...], sa_ref[...], sb_ref[...]
        for hh in range(nb):
            lanes = slice(hh * hd, (hh + 1) * hd)
            if hh >= roped:
                o_ref[:, lanes] = x_ref[:, lanes]
                continue
            xv = x_ref[:, lanes].astype(F32)
            if has_gain:
                r = lax.rsqrt(jnp.mean(xv * xv, axis=-1, keepdims=True) + EPS)
                xv = (xv * r) * refs[4][...]
            out = xv * cv + pltpu.roll(xv, sh, 1) * sa + pltpu.roll(xv, hd - sh, 1) * sb
            o_ref[:, lanes] = out.astype(BF16)

    tab = pl.BlockSpec((ts, hd), lambda i: (i, 0))
    in_specs = [pl.BlockSpec((ts, nb * hd), lambda i: (i, c0 // nb)), tab, tab, tab]
    ins = [src, *tabs]
    if has_gain:
        in_specs.append(pl.BlockSpec((1, hd), lambda i: (0, 0)))
        ins.append(gain)
    return _call(body, name=name, grid=(S // ts,), in_specs=in_specs,
                 out_specs=pl.BlockSpec((ts, nb * hd), lambda i: (i, 0)),
                 out_shape=SDS((S, nb * hd), BF16), sem=("parallel",))(*ins)


def _rope_bwd(name, dout, src, c0, nb, tabs, sh, gain=None):
    S = src.shape[0]
    ts = min(S, ROW_TILE)
    has_gain = gain is not None

    assert c0 % nb == 0
    hd = HEAD_DIM

    def body(*refs):
        d_ref, x_ref, c_ref, sa_ref, sb_ref = refs[:5]
        cv, sa, sb = c_ref[...], sa_ref[...], sb_ref[...]
        if has_gain:
            gn_ref, dx_ref, dgn_ref = refs[5:]

            @pl.when(pl.program_id(0) == 0)
            def _():
                dgn_ref[...] = jnp.zeros_like(dgn_ref)
        else:
            dx_ref = refs[5]
        for hh in range(nb):
            lanes = slice(hh * hd, (hh + 1) * hd)
            d = d_ref[:, lanes].astype(F32)
            dxn = d * cv + pltpu.roll(d * sa, hd - sh, 1) + pltpu.roll(d * sb, sh, 1)
            if has_gain:
                xv = x_ref[:, lanes].astype(F32)
                r = lax.rsqrt(jnp.mean(xv * xv, axis=-1, keepdims=True) + EPS)
                xh = xv * r
                dgn_ref[...] += jnp.sum(dxn * xh, axis=0, keepdims=True)
                dxh = dxn * gn_ref[...]
                dx_ref[:, lanes] = (r * (dxh - xh * jnp.mean(dxh * xh, axis=-1, keepdims=True))).astype(BF16)
            else:
                dx_ref[:, lanes] = dxn.astype(BF16)

    tab = pl.BlockSpec((ts, hd), lambda i: (i, 0))
    own = pl.BlockSpec((ts, nb * hd), lambda i: (i, 0))
    in_specs = [own, pl.BlockSpec((ts, nb * hd), lambda i: (i, c0 // nb)), tab, tab, tab]
    ins = [dout, src, *tabs]
    out_specs = [own]
    out_shape = [SDS((S, nb * hd), BF16)]
    if has_gain:
        row = pl.BlockSpec((1, hd), lambda i: (0, 0))
        in_specs.append(row)
        ins.append(gain)
        out_specs.append(row)
        out_shape.append(SDS((1, hd), F32))
    res = _call(body, name=name, grid=(S // ts,), in_specs=in_specs, out_specs=out_specs,
                out_shape=out_shape, sem=("arbitrary",))(*ins)
    return res if has_gain else (res[0], None)


def _offset_grid(T, W):
    d = (np.arange(2 * W + 1) - W)[:, None, None] * T
    return d + np.arange(T)[None, :, None] - np.arange(T)[None, None, :]


def _with_off_tile(tiles):
    xp = np if isinstance(tiles, np.ndarray) else jnp
    off = xp.full((*tiles.shape[:2], 1, *tiles.shape[3:]), NEG_INF, tiles.dtype)
    return xp.concatenate([tiles, off], axis=2)


def _mask_tiles_a(T):
    dk = _offset_grid(T, _band(128, T))
    return _with_off_tile(np.where(np.abs(dk) <= 128, 0.0, NEG_INF).astype(np.float32)[None, None])


def _mask_tiles_b(T):
    dk = _offset_grid(T, _band(1024, T))
    ad = np.abs(dk)
    mult = ((ad <= 64).astype(np.float32) + ((ad <= 256) & (dk % 4 == 0)) + ((ad <= 1024) & (dk % 16 == 0)))
    return _with_off_tile(np.where(mult > 0, np.log(np.maximum(mult, 1.0)) / SCALE, NEG_INF)
                          .astype(np.float32)[None, None])


def _edge_blocks_d(T):
    return -(-4 // (T // GRID_W))


def _mask_tiles_d(S, T):
    rows, nq, rpb = S // GRID_W, S // T, T // GRID_W
    W, E = -(-7 // rpb), _edge_blocks_d(T)
    assert nq >= 2 * E + 1
    out = []
    for i in [*range(E), nq // 2, *range(nq - E, nq)]:
        kp = ((i + np.arange(2 * W + 1) - W) * T)[:, None, None] + np.arange(T)[None, :, None]
        qp = i * T + np.arange(T)[None, None, :]
        qr, qc, kr, kc = qp >> 6, qp & 63, kp >> 6, kp & 63
        rs = np.clip(qr - 4, 0, rows - 8)
        cs = np.clip(qc - 8, 0, GRID_W - 16)
        valid = (kr >= rs) & (kr < rs + 8) & (kc >= cs) & (kc < cs + 16)
        out.append(np.where(valid, 0.0, NEG_INF).astype(np.float32))
    return np.stack(out)[:, None]


def _variant(i, nq, E):
    if E == 0:
        return 0
    return jnp.where(i < E, i, jnp.where(i >= nq - E, i - (nq - 2 * E - 1), E))


_NT = (((1,), (1,)), ((), ()))
_TN = (((0,), (0,)), ((), ()))
_NN = (((1,), (0,)), ((), ()))


class _Mixer:
    def __init__(self, tq, tk, W, G, E=0, hp=1):
        self.tq, self.tk, self.W, self.G, self.E, self.hp = tq, tk, W, G, E, hp


def _attn_fwd(name, mx, q_arr, qc0, k_arr, kc0, v_arr, vc0, u, gc0, sink, bias=None, carry=None):
    S = q_arr.shape[0]
    tq, tk, W, G = mx.tq, mx.tk, mx.W, mx.G
    nq, nk = S // tq, S // tk
    nd = nk if W is None else 2 * W + 1
    has_bias = bias is not None
    nc = 0 if carry is None else carry.n
    hd = HEAD_DIM

    def jmap(i, d):
        return d if W is None else jnp.clip(i + d - W, 0, nk - 1)

    nin = 1 if W is None else nd
    ngd = nd if W is None else 1

    def body(*refs):
        sink_ref, q_ref, g_ref = refs[:3]
        k_refs, v_refs = refs[3:3 + nin], refs[3 + nin:3 + 2 * nin]
        n_in = 3 + 2 * nin + (1 if has_bias else 0)
        bias_ref = refs[n_in - 1] if has_bias else None
        out0 = n_in + 2 * nc
        br_ref, o_ref, lse_ref = refs[out0:out0 + 3]
        if W is None:
            m_s, l_s, acc_s = refs[out0 + 3:out0 + 6]
        i, d = pl.program_id(0), pl.program_id(1)
        if carry is not None:
            carry_refs = (refs[n_in:n_in + nc], refs[n_in + nc:out0], refs[-2], refs[-1])
            pl.when((i == 0) & (d == 0))(lambda: carry.start(*carry_refs))

        def scores(h, dd, tile):
            kv = slice((h // G) * hd, (h // G + 1) * hd)
            s = lax.dot_general(k_refs[dd][:, kv], q_ref[:, h * hd:(h + 1) * hd], _NT,
                                preferred_element_type=F32)
            if has_bias:
                s = s + bias_ref[_variant(i, nq, mx.E), h if per_head else 0, tile]
            return s

        def weighted(h, dd, p):
            kv = slice((h // G) * hd, (h // G + 1) * hd)
            return lax.dot_general(v_refs[dd][:, kv], p.astype(BF16), _TN, preferred_element_type=F32)

        def finish(h, m, l, acc):
            lanes = slice(h * hd, (h + 1) * hd)
            sk = sink_ref[h]
            m = m * SCALE
            mf = jnp.maximum(m, sk)
            a = jnp.exp(m - mf)
            lf = l * a + jnp.exp(sk - mf)
            o = ((acc * a) / lf).T
            gv = g_ref[:, lanes].astype(F32)
            o_ref[:, lanes] = o.astype(BF16)
            br_ref[:, lanes] = (o * (gv * _sigmoid(gv))).astype(BF16)
            lse_ref[h] = mf + jnp.log(lf)

        if W is None:
            @pl.when(d == 0)
            def _():
                m_s[...] = jnp.full_like(m_s, M_INIT)
                l_s[...] = jnp.zeros_like(l_s)
                acc_s[...] = jnp.zeros_like(acc_s)

            for h in range(4):
                s = scores(h, 0, d)
                m_prev = m_s[h]
                m_new = jnp.maximum(m_prev, jnp.max(s, axis=0, keepdims=True))
                alpha = jnp.exp2((m_prev - m_new) * SCALE_LOG2E)
                p = jnp.exp2((s - m_new) * SCALE_LOG2E)
                l_s[h] = alpha * l_s[h] + jnp.sum(p, axis=0, keepdims=True)
                acc_s[h] = alpha * acc_s[h] + weighted(h, 0, p)
                m_s[h] = m_new

            @pl.when(d == ngd - 1)
            def _():
                for h in range(4):
                    finish(h, m_s[h], l_s[h], acc_s[h])
        else:
            tiles = [jnp.where((i + dd - W >= 0) & (i + dd - W < nk), dd, nd) for dd in range(nd)]
            for h in range(4):
                ss = [scores(h, dd, tiles[dd]) for dd in range(nd)]
                top = ss[0]
                for s in ss[1:]:
                    top = jnp.maximum(top, s)
                m = jnp.max(top, axis=0, keepdims=True)
                ps = [jnp.exp2((s - m) * SCALE_LOG2E) for s in ss]
                l = sum(jnp.sum(p, axis=0, keepdims=True) for p in ps)
                acc = sum(weighted(h, dd, ps[dd]) for dd in range(nd))
                finish(h, m, l, acc)

        if carry is not None:
            pl.when((i == nq - 1) & (d == ngd - 1))(lambda: carry.finish(*carry_refs))

    n_kv = 4 // G
    assert qc0 % 4 == 0 and gc0 % 4 == 0 and kc0 % n_kv == 0 and vc0 % n_kv == 0

    def kv_spec(c0, dd):
        if W is None:
            return pl.BlockSpec((tk, n_kv * hd), lambda i, d: (d, c0 // n_kv))
        return pl.BlockSpec((tk, n_kv * hd), lambda i, d: (jnp.clip(i + dd - W, 0, nk - 1), c0 // n_kv))

    per_head = has_bias and bias.shape[1] == 4
    in_specs = [pl.BlockSpec(memory_space=pltpu.SMEM),
                pl.BlockSpec((tq, 4 * hd), lambda i, d: (i, qc0 // 4)),
                pl.BlockSpec((tq, 4 * hd), lambda i, d: (i, gc0 // 4)),
                *[kv_spec(kc0, dd) for dd in range(nin)], *[kv_spec(vc0, dd) for dd in range(nin)]]
    ins = [sink, q_arr, u, *[k_arr] * nin, *[v_arr] * nin]
    if has_bias:
        in_specs.append(pl.BlockSpec(bias.shape, lambda i, d: (0, 0, 0, 0, 0), pipeline_mode=pl.Buffered(1)))
        ins.append(bias)
    own = pl.BlockSpec((tq, 4 * hd), lambda i, d: (i, 0))
    out_specs = [own, own, pl.BlockSpec((4, 1, tq), lambda i, d: (0, 0, i))]
    out_shape = [SDS((S, 4 * hd), BF16), SDS((S, 4 * hd), BF16), SDS((4, 1, S), F32)]
    scratch = []
    if W is None:
        scratch = [pltpu.VMEM((4, 1, tq), F32), pltpu.VMEM((4, 1, tq), F32), pltpu.VMEM((4, hd, tq), F32)]
    sem = ("parallel", "arbitrary")
    if carry is not None:
        ins += carry.ins
        in_specs += carry.in_specs
        out_specs = carry.out_specs + out_specs
        out_shape = carry.out_shape + out_shape
        scratch += carry.scratch
        sem = ("arbitrary",) * 2
    res = _call(body, name=name, grid=(nq, ngd), in_specs=in_specs, out_specs=out_specs, out_shape=out_shape,
                scratch=scratch, sem=sem)(*ins)
    return (*res[nc:], list(res[:nc]))


def _attn_bwd(name, mx, q_arr, qc0, k_arr, kc0, v_arr, vc0, do_all, hb0, lse, delta, bias=None, want_dbias=False,
              carry=None):
    S = q_arr.shape[0]
    tq, tk, W, G = mx.tq, mx.tk, mx.W, mx.G
    nq, nk = S // tq, S // tk
    nd = nq if W is None else 2 * W + 1
    n_kv = 4 // G
    hd = HEAD_DIM
    has_bias = bias is not None
    nc = 0 if carry is None else carry.n
    assert qc0 % G == 0 and hb0 % G == 0 and (not want_dbias or (has_bias and G == 1)) and (W is None or tq == tk)

    def imap(j, d):
        return d if W is None else jnp.clip(j + d - W, 0, nq - 1)

    nin = 1 if W is None else nd
    ngd = nd if W is None else 1

    def body(*refs):
        k_ref, v_ref = refs[:2]
        q_refs, do_refs = refs[2:2 + nin], refs[2 + nin:2 + 2 * nin]
        lse_refs, dl_refs = refs[2 + 2 * nin:2 + 3 * nin], refs[2 + 3 * nin:2 + 4 * nin]
        n_in = 2 + 4 * nin + (1 if has_bias else 0)
        bias_ref = refs[n_in - 1] if has_bias else None
        out0 = n_in + 2 * nc
        dq_ref, dk_ref, dv_ref = refs[out0:out0 + 3]
        n_o = 4 if want_dbias else 3
        db_ref = refs[out0 + 3] if want_dbias else None
        if W is None:
            dk_s, dv_s = refs[out0 + n_o:out0 + n_o + 2]
        kv, j, d = pl.program_id(0), pl.program_id(1), pl.program_id(2)
        if carry is not None:
            carry_refs = (refs[n_in:n_in + nc], refs[n_in + nc:out0], refs[-2], refs[-1])
            pl.when((kv == 0) & (j == 0) & (d == 0))(lambda: carry.start(*carry_refs))

        @pl.when((j == 0) & (d == 0))
        def _():
            dq_ref[...] = jnp.zeros_like(dq_ref)
            if want_dbias:
                db_ref[...] = jnp.zeros_like(db_ref)

        def unit(kh, g, dd, i, tile):
            hh = kh * G + g
            lanes = slice(hh * hd, (hh + 1) * hd)
            k = k_ref[:, kh * hd:(kh + 1) * hd]
            v = v_ref[:, kh * hd:(kh + 1) * hd]
            q = q_refs[dd][:, lanes]
            do = do_refs[dd][:, lanes]
            s = lax.dot_general(k, q, _NT, preferred_element_type=F32)
            if has_bias:
                s = s + bias_ref[_variant(i, nq, mx.E), kh if per_head else 0, tile]
            p = jnp.exp2(s * SCALE_LOG2E - lse_refs[dd][hh] * LOG2E)
            dv = lax.dot_general(p.astype(BF16), do, _NN, preferred_element_type=F32)
            dp = lax.dot_general(v, do, _NT, preferred_element_type=F32)
            ds = p * (dp - dl_refs[dd][hh])
            if want_dbias:
                db_ref[kh, jnp.minimum(tile, nd - 1)] += ds
            dsb = ds.astype(BF16)
            dk = lax.dot_general(dsb, q, _NN, preferred_element_type=F32)
            row0 = pl.multiple_of(i * tq, tq)
            dq_ref[pl.ds(row0, tq), lanes] += lax.dot_general(dsb, k, _TN, preferred_element_type=F32) * SCALE
            return dk, dv

        if W is None:
            @pl.when(d == 0)
            def _():
                dk_s[...] = jnp.zeros_like(dk_s)
                dv_s[...] = jnp.zeros_like(dv_s)

            for kh in range(hp):
                for g in range(G):
                    dk, dv = unit(kh, g, 0, d, None)
                    dk_s[kh] += dk
                    dv_s[kh] += dv

            @pl.when(d == ngd - 1)
            def _():
                for kh in range(hp):
                    dk_ref[:, kh * hd:(kh + 1) * hd] = dk_s[kh] * SCALE
                    dv_ref[:, kh * hd:(kh + 1) * hd] = dv_s[kh]
        else:
            for kh in range(hp):
                parts = []
                for dd in range(nd):
                    i_dd = j + dd - W
                    tile = jnp.where((i_dd >= 0) & (i_dd < nq), 2 * W - dd, nd)
                    parts += [unit(kh, g, dd, jnp.clip(i_dd, 0, nq - 1), tile) for g in range(G)]
                dk_ref[:, kh * hd:(kh + 1) * hd] = sum(pt[0] for pt in parts) * SCALE
                dv_ref[:, kh * hd:(kh + 1) * hd] = sum(pt[1] for pt in parts)

        if carry is not None:
            pl.when((kv == n_kv // hp - 1) & (j == nk - 1) & (d == ngd - 1))(lambda: carry.finish(*carry_refs))

    hp = mx.hp
    hq = hp * G
    assert qc0 % hq == 0 and hb0 % hq == 0 and kc0 % hp == 0 and vc0 % hp == 0 and n_kv % hp == 0
    per_head = has_bias and bias.shape[1] == 4

    def q_spec(shape, col, dd, stat, row0=0):
        def index(kv, j, d):
            blk = d if W is None else jnp.clip(j + dd - W, 0, nq - 1)
            return (col + kv, 0, blk) if stat else (row0 + blk, col + kv)
        return pl.BlockSpec(shape, index)

    in_specs = [pl.BlockSpec((tk, hp * hd), lambda kv, j, d: (j, kc0 // hp + kv)),
                pl.BlockSpec((tk, hp * hd), lambda kv, j, d: (j, vc0 // hp + kv)),
                *[q_spec((tq, hq * hd), qc0 // hq, dd, False) for dd in range(nin)],
                *[q_spec((tq, hq * hd), 0, dd, False, (hb0 // 4) * nq) for dd in range(nin)],
                *[q_spec((hq, 1, tq), 0, dd, True) for dd in range(nin)],
                *[q_spec((hq, 1, tq), hb0 // hq, dd, True) for dd in range(nin)]]
    ins = [k_arr, v_arr, *[q_arr] * nin, *[do_all] * nin, *[lse] * nin, *[delta] * nin]
    if has_bias:
        in_specs.append(pl.BlockSpec((bias.shape[0], hp if per_head else 1, bias.shape[2], tk, tq),
                                     lambda kv, j, d: (0, kv if per_head else 0, 0, 0, 0)))
        ins.append(bias)
    kv_blk = pl.BlockSpec((tk, hp * hd), lambda kv, j, d: (j, kv))
    out_specs = [pl.BlockSpec((S, hq * hd), lambda kv, j, d: (0, kv)), kv_blk, kv_blk]
    out_shape = [SDS((S, 4 * hd), F32), SDS((S, n_kv * hd), F32), SDS((S, n_kv * hd), F32)]
    if want_dbias:
        out_specs.append(pl.BlockSpec((hp, nd, tk, tq), lambda kv, j, d: (kv, 0, 0, 0)))
        out_shape.append(SDS((4, nd, tk, tq), F32))
    scratch = [pltpu.VMEM((hp, tk, hd), F32), pltpu.VMEM((hp, tk, hd), F32)] if W is None else []
    sem = ("parallel", "arbitrary", "arbitrary")
    if carry is not None:
        ins += carry.ins
        in_specs += carry.in_specs
        out_specs = carry.out_specs + out_specs
        out_shape = carry.out_shape + out_shape
        scratch += carry.scratch
        sem = ("arbitrary",) * 3
    res = _call(body, name=name, grid=(n_kv // hp, nk, ngd), in_specs=in_specs, out_specs=out_specs,
                out_shape=out_shape, scratch=scratch, sem=sem)(*ins)
    main = res[nc:]
    return (*main[:3], main[3] if want_dbias else None, list(res[:nc]))


def _attn_bwd_pre(name, dbr, o_all, u):
    S = dbr.shape[0]
    ts = min(S, ROW_TILE)
    hd = HEAD_DIM

    assert all(g % 4 == 0 for g in GCOL)

    def gcol(n):
        return GCOL[0] // 4 + n * 4 - jnp.where(n >= 2, 1, 0)

    def body(dbr_ref, o_ref, g_ref, do_ref, dg_ref, dl_ref):
        for hh in range(4):
            lanes = slice(hh * hd, (hh + 1) * hd)
            db = dbr_ref[:, lanes].astype(F32)
            o = o_ref[:, lanes].astype(F32)
            gv = g_ref[:, lanes].astype(F32)
            sg = _sigmoid(gv)
            do = db * (gv * sg)
            do_ref[:, lanes] = do.astype(BF16)
            dg_ref[:, lanes] = (db * o * (sg * (1.0 + gv * (1.0 - sg)))).astype(BF16)
            dl_ref[hh] = jnp.sum((do * o).T, axis=0, keepdims=True)

    own = pl.BlockSpec((ts, 4 * hd), lambda i, n: (i, n))
    stacked = pl.BlockSpec((ts, 4 * hd), lambda i, n: (n * (S // ts) + i, 0))
    return _call(body, name=name, grid=(S // ts, 4),
                 in_specs=[own, own, pl.BlockSpec((ts, 4 * hd), lambda i, n: (i, gcol(n)))],
                 out_specs=[stacked, own, pl.BlockSpec((4, 1, ts), lambda i, n: (n, 0, i))],
                 out_shape=[SDS((4 * S, 4 * hd), BF16), SDS((S, 16 * hd), BF16), SDS((16, 1, S), F32)],
                 sem=("parallel", "parallel"))(dbr, o_all, u)


def _sink_grad(name, sink, lse, delta):
    S = lse.shape[2]
    ts = min(S, 2048)

    def body(sink_ref, lse_ref, dl_ref, out_ref):
        @pl.when(pl.program_id(1) == 0)
        def _():
            out_ref[...] = jnp.zeros_like(out_ref)

        sk = sink_ref[pl.program_id(0)]
        part = jnp.sum(jnp.exp(sk - lse_ref[0]) * dl_ref[0], axis=1, keepdims=True)
        out_ref[0] += -jnp.broadcast_to(part, (1, 128))

    col = pl.BlockSpec((1, 1, ts), lambda h, i: (h, 0, i))
    return _call(body, name=name, grid=(4, S // ts),
                 in_specs=[pl.BlockSpec(memory_space=pltpu.SMEM), col, col],
                 out_specs=pl.BlockSpec((1, 1, 128), lambda h, i: (h, 0, 0)),
                 out_shape=SDS((4, 1, 128), F32), sem=("parallel", "arbitrary"))(sink, lse, delta)


def _bias_maps(T, W):
    rpb = T // GRID_W
    nd = 2 * W + 1
    rmap = np.zeros((nd, rpb, rpb, 15), np.float32)
    for df in range(nd):
        for a in range(rpb):
            for b in range(rpb):
                r = (df - W) * rpb + b - a + 7
                if 0 <= r < 15:
                    rmap[df, a, b, r] = 1.0
    cmap = np.zeros((GRID_W, GRID_W, 31), np.float32)
    for q in range(GRID_W):
        for k in range(GRID_W):
            cmap[q, k, int(np.clip(k - q, -15, 15)) + 15] = 1.0
    return jnp.asarray(rmap), jnp.asarray(cmap)


def _bias_tiles(rel_bias, S, T):
    W = -(-7 // (T // GRID_W))
    rmap, cmap = _bias_maps(T, W)
    t = jnp.einsum("dabr,hrc,qkc->hdbkaq", rmap, rel_bias, cmap, precision=lax.Precision.HIGHEST)
    return _with_off_tile(t.reshape(1, 4, 2 * W + 1, T, T) * (1.0 / SCALE) + jnp.asarray(_mask_tiles_d(S, T)))


def _bias_tiles_t(dtiles, T):
    rpb = T // GRID_W
    W = -(-7 // rpb)
    rmap, cmap = _bias_maps(T, W)
    t = dtiles.reshape(4, 2 * W + 1, rpb, GRID_W, rpb, GRID_W)
    return jnp.einsum("dabr,hdbkaq,qkc->hrc", rmap, t, cmap, precision=lax.Precision.HIGHEST)


def _mixer_cfg(S):
    ta, tb, td = (min(S, ATT_TILE[k]) for k in "abd")
    cq, ck = (min(S, t) for t in ATT_TILE["c"])
    bq, bk = (min(S, t) for t in ATT_TILE["c_bwd"])
    a = _Mixer(ta, ta, _band(128, ta), 2)
    b = _Mixer(tb, tb, _band(1024, tb), 1, hp=2)
    d = _Mixer(td, td, -(-7 // (td // GRID_W)), 1, _edge_blocks_d(td), hp=2)
    return {"a": (a, a, jnp.asarray(_mask_tiles_a(ta))), "b": (b, b, jnp.asarray(_mask_tiles_b(tb))),
            "c": (_Mixer(cq, ck, None, 2), _Mixer(bq, bk, None, 2), None), "d": (d, d, None)}


def _layer_fwd(l, x, c8, lw, p, tabs, next_shards, me, late=None):
    S, D = x.shape
    dq = D // N_SHARD
    ada_sh, win_sh, wgm_sh, wb_sh, wout_sh = lw
    one_d, axial = tabs
    cfg = _mixer_cfg(S)
    tm = min(S, MM_ROWS)

    ada = _matmul(f"ada_l{l}", (N_SHARD,), [c8, ada_sh, p["b_ada"][l][None, :]],
                  [pl.BlockSpec((8, D), lambda j: (0, 0)), pl.BlockSpec((None, D, 3 * dq), lambda j: (j, 0, 0)),
                   pl.BlockSpec((1, 3 * dq), lambda j: (0, j))],
                  [SDS((8, 3 * D), F32)], [pl.BlockSpec((8, 3 * dq), lambda j: (0, j))],
                  epilogue=lambda acc, b_ref: (acc + b_ref[...],), a_fn=lambda a: a * _sigmoid(a))[0][0:1]
    shift, scale, gate = ada[:, :D], ada[:, D:2 * D], ada[:, 2 * D:]
    g_row = p["norm_g"][l][None, :]
    h = _norm_mod(f"norm_mod_l{l}", x, g_row, scale, shift)

    loads = {k: [] for k in ("mm_in", "attn_b", "attn_c", "attn_d", "merge")}
    tags = {k: [] for k in loads}
    got = {}

    def ride(host, tag, shard):
        loads[host].append(shard)
        tags[host].append(tag)

    def arrived(host, arrays):
        got.update(zip(tags[host], arrays))

    if late is not None:
        ride("mm_in", "late_gm", late[0])
        ride("attn_b", "late_wb", late[1])
        ride("attn_b", "late_out", late[2])
    if next_shards is not None:
        s_ada, s_in, s_gm, s_wb, s_out = next_shards
        ride("attn_c", "gm", s_gm)
        ride("merge", "in", s_in)
        ride("mm_in" if late is None else "attn_c", "ada", s_ada)
        ride("attn_b" if late is None else "attn_d", "wb", s_wb)
        ride("attn_b" if late is None else "attn_d", "out", s_out)
    carries = {k: _Carry("gather", v) if v else None for k, v in loads.items()}

    res_in = _matmul(f"mm_in_l{l}", (S // tm, N_SHARD), [h, win_sh],
                     [pl.BlockSpec((tm, D), lambda i, j: (i, 0)),
                      pl.BlockSpec((None, D, IN_SHARD), lambda i, j: (j, 0, 0))],
                     [SDS((S, IN_COLS), BF16)], [pl.BlockSpec((tm, IN_SHARD), lambda i, j: (i, j))],
                     carry=carries["mm_in"])
    if carries["mm_in"] is not None:
        arrived("mm_in", res_in[1])
        res_in = res_in[0]
    u = res_in[0]

    qa = _rope_fwd(f"rope_qa_l{l}", u, QCOL[0], 4, one_d, 64)
    kv_a = _rope_fwd(f"rope_ka_l{l}", u, KCOL[0], 2, one_d, 64, tail=2)
    qb = _rope_fwd(f"rope_qb_l{l}", u, QCOL[1], 4, one_d, 64)
    kv_b = _rope_fwd(f"rope_kb_l{l}", u, KCOL[1], 4, one_d, 64, tail=4)
    qc = _rope_fwd(f"rope_qc_l{l}", u, QCOL[2], 4, axial, 32, p["c_q_norm"][l][None, :])
    kv_c = _rope_fwd(f"rope_kc_l{l}", u, KCOL[2], 2, axial, 32, p["c_k_norm"][l][None, :], tail=2)

    no_sink = jnp.full((4,), NEG_INF, F32)
    bias = _bias_tiles(p["d_rel_bias"][l], S, cfg["d"][0].tq)
    hd = HEAD_DIM
    kv_d = u[:, KCOL[3] * hd:(KCOL[3] + 8) * hd]
    qd = u[:, QCOL[3] * hd:(QCOL[3] + 4) * hd]
    br_a, o_a, lse_a, got_a = _attn_fwd(f"attn_a_l{l}", cfg["a"][0], qa, 0, kv_a, 0, kv_a, 2, u, GCOL[0],
                                        p["a_sink"][l], cfg["a"][2])
    br_b, o_b, lse_b, got_b = _attn_fwd(f"attn_b_l{l}", cfg["b"][0], qb, 0, kv_b, 0, kv_b, 4, u, GCOL[1], no_sink,
                                        cfg["b"][2], carries["attn_b"])
    br_c, o_c, lse_c, got_c = _attn_fwd(f"attn_c_l{l}", cfg["c"][0], qc, 0, kv_c, 0, kv_c, 2, u, GCOL[2], no_sink,
                                        None, carries["attn_c"])
    br_d, o_d, lse_d, got_d = _attn_fwd(f"attn_d_l{l}", cfg["d"][0], qd, 0, kv_d, 0, kv_d, 4, u, GCOL[3],
                                        no_sink, bias, carries["attn_d"])
    arrived("attn_b", got_b)
    arrived("attn_c", got_c)
    arrived("attn_d", got_d)
    if late is not None:
        wgm_sh, wb_sh, wout_sh = _own_slot([got["late_gm"], got["late_wb"], got["late_out"]], late, me)
    br = jnp.concatenate([br_a, br_b, br_c, br_d], axis=1)
    o_all = jnp.concatenate([o_a, o_b, o_c, o_d], axis=1)

    def merge_body(h_ref, wg_ref, br_ref, wb_ref, mg_ref, pj_ref, z_ref, acc_ref):
        n = pl.program_id(2)
        mgv = _sigmoid(lax.dot_general(h_ref[...], wg_ref[...], _NN, preferred_element_type=F32))
        pj = lax.dot_general(br_ref[...], wb_ref[...], _NN, preferred_element_type=F32)
        mg_ref[...] = mgv.astype(BF16)
        pj_ref[...] = pj.astype(BF16)

        @pl.when(n == 0)
        def _():
            acc_ref[...] = mgv * pj

        @pl.when(n > 0)
        def _():
            acc_ref[...] += mgv * pj

        @pl.when(n == 3)
        def _():
            z_ref[...] = acc_ref[...].astype(BF16)

    wide = pl.BlockSpec((tm, dq), lambda i, j, n: (i, n * N_SHARD + j))
    (mg, proj, z), got_m = _call_carrying(
        merge_body, carries["merge"], [h, wgm_sh, br, wb_sh], name=f"merge_l{l}", grid=(S // tm, N_SHARD, 4),
        in_specs=[pl.BlockSpec((tm, D), lambda i, j, n: (i, 0)),
                  pl.BlockSpec((None, D, dq), lambda i, j, n: (n, 0, j)),
                  pl.BlockSpec((tm, BRANCH_W), lambda i, j, n: (i, n)),
                  pl.BlockSpec((None, None, BRANCH_W, dq), lambda i, j, n: (j, n, 0, 0))],
        out_specs=[wide, wide, pl.BlockSpec((tm, dq), lambda i, j, n: (i, j))],
        out_shape=[SDS((S, 4 * D), BF16), SDS((S, 4 * D), BF16), SDS((S, D), BF16)],
        scratch=[pltpu.VMEM((tm, dq), F32)], sem=("parallel", "parallel", "arbitrary"))
    arrived("merge", got_m)
    next_lw = None
    if next_shards is not None:
        next_lw = _own_slot([got[t] for t in ("ada", "in", "gm", "wb", "out")], next_shards, me)

    tn = min(D, 1024)
    x_new, o2 = _matmul(
        f"mm_out_l{l}", (S // tm, D // tn), [z, wout_sh, x, gate],
        [pl.BlockSpec((tm, D), lambda i, j: (i, 0)), pl.BlockSpec((N_SHARD, dq, tn), lambda i, j: (0, 0, j)),
         pl.BlockSpec((tm, tn), lambda i, j: (i, j)), pl.BlockSpec((1, tn), lambda i, j: (0, j))],
        [SDS((S, D), F32), SDS((S, D), BF16)],
        [pl.BlockSpec((tm, tn), lambda i, j: (i, j)), pl.BlockSpec((tm, tn), lambda i, j: (i, j))],
        k_inner=N_SHARD, epilogue=lambda acc, x_ref, g_ref: (x_ref[...] + g_ref[...] * acc, acc))
    res = dict(x=x, h=h, u=u, qa=qa, qb=qb, qc=qc, qd=qd, kv=(kv_a, kv_b, kv_c, kv_d), br=br, o_all=o_all,
               lse=(lse_a, lse_b, lse_c, lse_d), bias=bias, mg=mg, proj=proj, z=z, o2=o2,
               g_row=g_row, scale=scale, gate=gate)
    return x_new, res, (ada_sh, win_sh, wgm_sh, wb_sh, wout_sh), next_lw


def _layer_bwd(l, dxo, r, lw, p, tabs, pending, c_idx):
    x, h, u = r["x"], r["h"], r["u"]
    S, D = x.shape
    dq = D // N_SHARD
    ada_sh, win_sh, wgm_sh, wb_sh, wout_sh = lw
    one_d, axial = tabs
    cfg = _mixer_cfg(S)
    tm = min(S, MM_ROWS)
    tk = min(S, 1024)
    tn = min(D, 1024)

    do2, dgate = _out_bwd_ew(f"out_bwd_l{l}", dxo, r["o2"], r["gate"])
    dz = _matmul(f"mm_dz_l{l}", (S // tm, N_SHARD), [do2, wout_sh],
                 [pl.BlockSpec((tm, D), lambda i, n: (i, 0)), pl.BlockSpec((None, dq, D), lambda i, n: (n, 0, 0))],
                 [SDS((S, D), BF16)], [pl.BlockSpec((tm, dq), lambda i, n: (i, n))], tb=True)[0]
    tkw = min(S, 2048)
    g_out = _matmul(f"mm_gwout_l{l}", (N_SHARD, D // tn, S // tkw), [r["z"], do2],
                    [pl.BlockSpec((tkw, dq), lambda n, j, k: (k, n)), pl.BlockSpec((tkw, tn), lambda n, j, k: (k, j))],
                    [SDS((N_SHARD, dq, D), F32)], [pl.BlockSpec((None, dq, tn), lambda n, j, k: (n, 0, j))],
                    ta=True, k_axis=2, acc_shape=(dq, tn))[0]

    dmg, dproj = _merge_bwd_ew(f"merge_bwd_l{l}", dz, r["mg"], r["proj"])
    nj = D // tn
    g_gm = _matmul(f"mm_gwgm_l{l}", (4, D // tn, nj, S // tk), [h, dmg],
                   [pl.BlockSpec((tk, tn), lambda n, i, j, k: (k, i)),
                    pl.BlockSpec((tk, tn), lambda n, i, j, k: (k, n * nj + j))],
                   [SDS((4, D, D), F32)], [pl.BlockSpec((None, tn, tn), lambda n, i, j, k: (n, i, j))],
                   ta=True, k_axis=3, acc_shape=(tn, tn))[0]
    dh1 = _matmul(f"mm_dh1_l{l}", (S // tm, nj, 4), [dmg, wgm_sh],
                  [pl.BlockSpec((tm, D), lambda i, j, n: (i, n)),
                   pl.BlockSpec((None, tn, D), lambda i, j, n: (n, j, 0))],
                  [SDS((S, D), F32)], [pl.BlockSpec((tm, tn), lambda i, j, kk: (i, j))],
                  tb=True, k_axis=2, acc_shape=(tm, tn))[0]
    dbr = _matmul(f"mm_dbr_l{l}", (S // tm, 4), [dproj, wb_sh],
                  [pl.BlockSpec((tm, D), lambda i, n: (i, n)),
                   pl.BlockSpec((N_SHARD, None, BRANCH_W, dq), lambda i, n: (0, n, 0, 0))],
                  [SDS((S, 4 * BRANCH_W), BF16)], [pl.BlockSpec((tm, BRANCH_W), lambda i, n: (i, n))],
                  tb=True, k_inner=N_SHARD)[0]
    tkl = min(S, 2048)
    g_wb = _matmul(f"mm_gwb_l{l}", (N_SHARD, 4, S // tkl), [r["br"], dproj],
                   [pl.BlockSpec((tkl, BRANCH_W), lambda j, n, k: (k, n)),
                    pl.BlockSpec((tkl, dq), lambda j, n, k: (k, n * N_SHARD + j))],
                   [SDS((N_SHARD, 4, BRANCH_W, dq), F32)],
                   [pl.BlockSpec((None, None, BRANCH_W, dq), lambda j, n, k: (j, n, 0, 0))],
                   ta=True, k_axis=2, acc_shape=(BRANCH_W, dq))[0]

    do_all, dg_all, delta = _attn_bwd_pre(f"attn_pre_l{l}", dbr, r["o_all"], u)
    lse_a, lse_b, lse_c, lse_d = r["lse"]
    dsink = _sink_grad(f"sink_grad_l{l}", p["a_sink"][l], lse_a, delta)[:, 0, 0]
    carries = [None] * 3
    if pending is not None:
        p_in, p_gm, p_wb, p_out = pending
        carries = [_Carry("exchange", [p_in]), _Carry("exchange", [p_gm]), _Carry("exchange", [p_wb, p_out])]
    kv_a, kv_b, kv_c, kv_d = r["kv"]
    v_gm, v_wb, v_out = (_half_view(g) for g in (g_gm, g_wb, g_out))
    dqa, dka, dva, _, from_sib = _attn_bwd(f"attn_a_bwd_l{l}", cfg["a"][1], r["qa"], 0, kv_a, 0, kv_a, 2, do_all, 0,
                                           lse_a, delta, cfg["a"][2], carry=_Carry("pair", [v_gm, v_wb, v_out]))
    dqb, dkb, dvb, _, got_b = _attn_bwd(f"attn_b_bwd_l{l}", cfg["b"][1], r["qb"], 0, kv_b, 0, kv_b, 4, do_all, 4,
                                        lse_b, delta, cfg["b"][2], carry=carries[0])
    dqc, dkc, dvc, _, got_c = _attn_bwd(f"attn_c_bwd_l{l}", cfg["c"][1], r["qc"], 0, kv_c, 0, kv_c, 2, do_all, 8,
                                        lse_c, delta, carry=carries[1])
    dqd, dkd, dvd, dbias, got_d = _attn_bwd(f"attn_d_bwd_l{l}", cfg["d"][1], r["qd"], 0, kv_d, 0, kv_d, 4,
                                            do_all, 12, lse_d, delta, r["bias"], True, carries[2])
    arrived = None if pending is None else [got_b[0], got_c[0], got_d[0], got_d[1]]
    d_rel = _bias_tiles_t(dbias, cfg["d"][1].tq)

    duqa, _ = _rope_bwd(f"rope_qa_bwd_l{l}", dqa, u, QCOL[0], 4, one_d, 64)
    duka, _ = _rope_bwd(f"rope_ka_bwd_l{l}", dka, u, KCOL[0], 2, one_d, 64)
    duqb, _ = _rope_bwd(f"rope_qb_bwd_l{l}", dqb, u, QCOL[1], 4, one_d, 64)
    dukb, _ = _rope_bwd(f"rope_kb_bwd_l{l}", dkb, u, KCOL[1], 4, one_d, 64)
    duqc, dcq = _rope_bwd(f"rope_qc_bwd_l{l}", dqc, u, QCOL[2], 4, axial, 32, p["c_q_norm"][l][None, :])
    dukc, dck = _rope_bwd(f"rope_kc_bwd_l{l}", dkc, u, KCOL[2], 2, axial, 32, p["c_k_norm"][l][None, :])
    bw = BRANCH_W
    du = jnp.concatenate(
        [duqa, duka, dva.astype(BF16), dg_all[:, 0:bw],
         duqb, dukb, dvb.astype(BF16), dg_all[:, bw:2 * bw],
         duqc, dukc, dvc.astype(BF16), dg_all[:, 2 * bw:3 * bw],
         dqd.astype(BF16), dkd.astype(BF16), dvd.astype(BF16), dg_all[:, 3 * bw:]], axis=1)

    tmi = min(D, 1024)
    g_in = _matmul(f"mm_gwin_l{l}", (N_SHARD, D // tmi, S // tk), [h, du],
                   [pl.BlockSpec((tk, tmi), lambda j, i, k: (k, i)), pl.BlockSpec((tk, IN_SHARD), lambda j, i, k: (k, j))],
                   [SDS((N_SHARD, D, IN_SHARD), F32)], [pl.BlockSpec((None, tmi, IN_SHARD), lambda j, i, k: (j, i, 0))],
                   ta=True, k_axis=2, acc_shape=(tmi, IN_SHARD))[0]
    dh2 = _matmul(f"mm_dh2_l{l}", (S // tm, nj, N_SHARD), [du, win_sh],
                  [pl.BlockSpec((tm, IN_SHARD), lambda i, j, k: (i, k)),
                   pl.BlockSpec((None, tn, IN_SHARD), lambda i, j, k: (k, j, 0))],
                  [SDS((S, D), F32)], [pl.BlockSpec((tm, tn), lambda i, j, k: (i, j))],
                  tb=True, k_axis=2, acc_shape=(tm, tn))[0]

    v_in = _half_view(g_in)
    (dx_prev, dshift, dscale, dng), in_from_sib = _norm_mod_bwd(
        f"norm_mod_bwd_l{l}", x, dh1, dh2, dxo, r["g_row"], r["scale"], _Carry("pair", [v_in]))
    d_ada = jnp.concatenate([dshift, dscale, dgate], axis=1)[0]
    big = (g_in, g_gm, g_wb, g_out)
    small = dict(norm_g=dng[0], b_ada=d_ada, a_sink=dsink, c_q_norm=dcq[0], c_k_norm=dck[0], d_rel_bias=d_rel)
    views = [v_in, v_gm, v_wb, v_out]
    recv1 = list(in_from_sib) + list(from_sib)
    parts = [_add_half(f"rs_add_half{a}_l{l}", v, r1, c_idx) for a, (v, r1) in enumerate(zip(views, recv1))]
    return dx_prev, big, small, arrived, parts


def _place():
    return lax.axis_index("x"), lax.axis_index("y"), lax.axis_index("c")


class _Carry:
    def __init__(self, kind, arrays):
        self.kind, self.n, self.ins = kind, len(arrays), list(arrays)
        any_spec = pl.BlockSpec(memory_space=pl.ANY)
        self.in_specs = [any_spec] * self.n
        self.out_specs = [any_spec] * self.n
        if kind == "gather":
            self.out_shape = [SDS((N_SHARD, *a.shape), a.dtype) for a in arrays]
        elif kind == "pair":
            self.out_shape = [SDS((a.shape[0], *a.shape[2:]), a.dtype) for a in arrays]
        else:
            self.out_shape = [SDS((3, *a.shape[1:]), a.dtype) for a in arrays]
        self.scratch = [pltpu.SemaphoreType.DMA((self.n, 3)), pltpu.SemaphoreType.DMA((self.n, 3))]

    def _copies(self, ins, outs, send_sems, recv_sems, arriving):
        x, y, c = _place()
        if self.kind == "pair":
            return [pltpu.make_async_remote_copy(ins[a].at[:, 1 - c], outs[a], send_sems.at[a, 0], recv_sems.at[a, 0],
                                                 device_id=(x, y, 1 - c), device_id_type=MESH)
                    for a in range(self.n)]
        cps = []
        for a in range(self.n):
            for k, (px, py) in enumerate([(1 - x, y), (x, 1 - y), (1 - x, 1 - y)]):
                if self.kind == "gather":
                    src, dst = ins[a], outs[a].at[2 * px + py if arriving else 2 * x + y]
                else:
                    src, dst = ins[a].at[2 * px + py], outs[a].at[k]
                cps.append(pltpu.make_async_remote_copy(src, dst, send_sems.at[a, k], recv_sems.at[a, k],
                                                        device_id=(px, py, c), device_id_type=MESH))
        return cps

    def start(self, ins, outs, send_sems, recv_sems):
        for cp in self._copies(ins, outs, send_sems, recv_sems, False):
            cp.start()

    def finish(self, ins, outs, send_sems, recv_sems):
        for cp in self._copies(ins, outs, send_sems, recv_sems, True):
            cp.wait_recv()
        for cp in self._copies(ins, outs, send_sems, recv_sems, False):
            cp.wait_send()


def _run_carry(name, carry):
    n = carry.n

    def body(*refs):
        args = (refs[:n], refs[n:2 * n], refs[2 * n], refs[2 * n + 1])
        carry.start(*args)
        carry.finish(*args)

    return pl.pallas_call(body, name=name, in_specs=carry.in_specs, out_specs=carry.out_specs,
                          out_shape=carry.out_shape, scratch_shapes=carry.scratch)(*carry.ins)


def _own_slot(gathered, shards, me):
    return [lax.dynamic_update_index_in_dim(g, s, me, 0) for g, s in zip(gathered, shards)]


def _gather_small(name, v):
    m_per, n = v.shape

    def body(x_ref, out_ref, send_sems, recv_sems, local_sem):
        x, y, c = _place()
        me, sibling = (x, y, c), (x, y, 1 - c)
        chips = [(1 - x, y), (x, 1 - y), (1 - x, 1 - y)]

        def rows(px, py, pc):
            return out_ref.at[pl.ds((4 * px + 2 * py + pc) * m_per, m_per), :]

        def copy(k, block, to, src=None):
            return pltpu.make_async_remote_copy(
                src_ref=rows(*block) if src is None else src, dst_ref=rows(*block),
                send_sem=send_sems.at[k], recv_sem=recv_sems.at[k], device_id=to, device_id_type=MESH)

        mine = pltpu.make_async_copy(x_ref, rows(*me), local_sem)
        mine.start()
        first = [copy(0, me, sibling, src=x_ref)]
        first += [copy(1 + j, me, (*chip, c), src=x_ref) for j, chip in enumerate(chips)]
        for cp in first:
            cp.start()
        passed = [copy(4 + j, (*chip, c), sibling) for j, chip in enumerate(chips)]
        for j, chip in enumerate(chips):
            copy(1 + j, (*chip, c), me).wait_recv()
            passed[j].start()
        copy(0, sibling, me).wait_recv()
        for j, chip in enumerate(chips):
            copy(4 + j, (*chip, 1 - c), me).wait_recv()
        for cp in first + passed:
            cp.wait_send()
        mine.wait()

    return pl.pallas_call(
        body, name=name, out_shape=SDS((8 * m_per, n), v.dtype),
        in_specs=[pl.BlockSpec(memory_space=pltpu.VMEM)], out_specs=pl.BlockSpec(memory_space=pltpu.VMEM),
        scratch_shapes=[pltpu.SemaphoreType.DMA((7,)), pltpu.SemaphoreType.DMA((7,)), pltpu.SemaphoreType.DMA])(v)


def _pair_gather(name, halves):
    n = len(halves)

    def body(*refs):
        outs = refs[n:2 * n]
        send_sems, recv_sems = refs[2 * n:]
        x, y, c = _place()
        cps = [pltpu.make_async_remote_copy(outs[a].at[c], outs[a].at[c], send_sems.at[a], recv_sems.at[a],
                                            device_id=(x, y, 1 - c), device_id_type=MESH) for a in range(n)]
        for cp in cps:
            cp.start()
        for a in range(n):
            pltpu.make_async_remote_copy(outs[a].at[c], outs[a].at[1 - c], send_sems.at[a], recv_sems.at[a],
                                         device_id=(x, y, 1 - c), device_id_type=MESH).wait_recv()
        for cp in cps:
            cp.wait_send()

    any_spec = pl.BlockSpec(memory_space=pl.ANY)
    return pl.pallas_call(
        body, name=name, in_specs=[any_spec] * n, out_specs=[any_spec] * n,
        out_shape=[SDS(g.shape, g.dtype) for g in halves], input_output_aliases={a: a for a in range(n)},
        scratch_shapes=[pltpu.SemaphoreType.DMA((n,)), pltpu.SemaphoreType.DMA((n,))])(*halves)


def _add_half(name, g, recv, c_idx):
    _, _, R, C = g.shape
    tr = min(R, 256)

    def body(c_ref, g_ref, r_ref, o_ref):
        o_ref[...] = (g_ref[...] + r_ref[...]).astype(BF16)

    return _call(body, name=name, grid=(4, R // tr), nsp=1,
                 in_specs=[pl.BlockSpec((None, None, tr, C), lambda j, r, c_ref: (j, c_ref[0], r, 0)),
                           pl.BlockSpec((None, tr, C), lambda j, r, c_ref: (j, r, 0))],
                 out_specs=pl.BlockSpec((None, tr, C), lambda j, r, c_ref: (j, r, 0)),
                 out_shape=SDS((4, R, C), BF16), sem=("parallel", "parallel"))(c_idx, g, recv)


def _add_shards(name, part, recv, idx):
    _, R, C = part.shape
    tr = min(R, 256)

    def body(idx_ref, p_ref, r_ref, o_ref):
        o_ref[...] = (((p_ref[...].astype(F32) + r_ref[0].astype(F32)) + r_ref[1].astype(F32))
                      + r_ref[2].astype(F32))

    return _call(body, name=name, grid=(R // tr,), nsp=1,
                 in_specs=[pl.BlockSpec((None, tr, C), lambda r, idx_ref: (idx_ref[0], r, 0)),
                           pl.BlockSpec((3, tr, C), lambda r, idx_ref: (0, r, 0))],
                 out_specs=pl.BlockSpec((None, tr, C), lambda r, idx_ref: (idx_ref[1], r, 0)),
                 out_shape=SDS((2, R, C), F32), sem=("parallel",))(idx, part, recv)


def _half_view(g):
    rows = g.shape[-2] if g.ndim == 3 else g.shape[1] * g.shape[2]
    return g.reshape(N_SHARD, 2, rows // 2, g.shape[-1])


def _finish_reduce_layer(l, big, parts, recv2, idx):
    halves = [_add_shards(f"rs_add_shards{a}_l{l}", pt, r2, idx) for a, (pt, r2) in enumerate(zip(parts, recv2))]
    full = _pair_gather(f"rs_pair_gather_l{l}", halves)
    return [f.reshape(g.shape[1:]) for f, g in zip(full, big)]


def _adamw_math(w, g, m, v):
    m = ADAM_B1 * m + (1.0 - ADAM_B1) * g
    v = ADAM_B2 * v + (1.0 - ADAM_B2) * (g * g)
    m_hat = m / (1.0 - ADAM_B1 ** ADAM_STEP)
    v_hat = v / (1.0 - ADAM_B2 ** ADAM_STEP)
    delta = -ADAM_LR * (m_hat / (jnp.sqrt(v_hat) + ADAM_EPS) + ADAM_WD * w)
    return delta, m, v


def _adamw(name, w, g, m, v):
    shape = w.shape
    C = shape[-1]
    R = int(np.prod(shape[:-1]))
    tr = min(R, 256)

    def body(w_ref, g_ref, m_ref, v_ref, d_ref, nm_ref, nv_ref):
        d, nm, nv = _adamw_math(w_ref[...], g_ref[...], m_ref[...], v_ref[...])
        d_ref[...] = d
        nm_ref[...] = nm
        nv_ref[...] = nv

    blk = pl.BlockSpec((tr, C), lambda i: (i, 0))
    outs = _call(body, name=name, grid=(R // tr,), in_specs=[blk] * 4, out_specs=[blk] * 3,
                 out_shape=[SDS((R, C), F32)] * 3, sem=("parallel",))(*(a.reshape(R, C) for a in (w, g, m, v)))
    return [o.reshape(shape) for o in outs]


def _adamw_small(name, w, g8, m, v):
    R = w.shape[0]

    def body(w_ref, g_ref, m_ref, v_ref, go_ref, d_ref, nm_ref, nv_ref):
        g = g_ref[0]
        for b in range(1, 8):
            g = g + g_ref[b]
        d, nm, nv = _adamw_math(w_ref[...], g, m_ref[...], v_ref[...])
        go_ref[...] = g
        d_ref[...] = d
        nm_ref[...] = nm
        nv_ref[...] = nv

    blk = pl.BlockSpec((R, 128), lambda i: (0, 0))
    return _call(body, name=name, grid=(1,), in_specs=[blk, pl.BlockSpec((8, R, 128), lambda i: (0, 0, 0)), blk, blk],
                 out_specs=[blk] * 4, out_shape=[SDS((R, 128), F32)] * 4, sem=("arbitrary",))(w, g8, m, v)


SMALL_NAMES = ("norm_g", "b_ada", "a_sink", "c_q_norm", "c_k_norm", "d_rel_bias", "final_g")


def _pack(parts, extra_rows=0):
    flat = jnp.concatenate([a.reshape(-1) for a in parts])
    rows = -(-flat.shape[0] // 128)
    rows = -(-rows // 8) * 8 + extra_rows
    return jnp.pad(flat, (0, rows * 128 - flat.shape[0])).reshape(rows, 128)


def _unpack(packed, like):
    flat = packed.reshape(-1)
    out, off = [], 0
    for a in like:
        out.append(flat[off:off + a.size].reshape(a.shape))
        off += a.size
    return out


def _device_step(x, c8, tgt, shards, p, me, c_idx):
    S = x.shape[0]
    L = len(shards)
    tabs = _rope_tables(S)
    first = _Carry("gather", shards[0][:2])
    lw = [(*_own_slot(_run_carry("gather_w_l0", first), shards[0][:2], me), None, None, None)]
    res = []
    for l in range(L):
        x, r, lw[l], nxt = _layer_fwd(l, x, c8, lw[l], p, tabs, shards[l + 1] if l + 1 < L else None, me,
                                      shards[0][2:] if l == 0 else None)
        res.append(r)
        lw.append(nxt)
    dx, dfg, loss = _final_loss("final_loss", x, tgt, p["final_g"][None, :])
    bigs, smalls, parts, arrived = [None] * L, [None] * L, [None] * L, [None] * L
    for l in reversed(range(L)):
        pending = parts[l + 1] if l + 1 < L else None
        dx, bigs[l], smalls[l], arr, parts[l] = _layer_bwd(l, dx, res[l], lw[l], p, tabs, pending, c_idx)
        if pending is not None:
            arrived[l + 1] = arr
    arrived[0] = _run_carry("rs_shard_exchange_l0", _Carry("exchange", parts[0]))
    idx = jnp.concatenate([jnp.reshape(me, (1,)).astype(I32), c_idx])
    reduced = [_finish_reduce_layer(l, bigs[l], parts[l], arrived[l], idx) for l in range(L)]
    return loss, dx, reduced, smalls, dfg[0]


def kernel(x, c, norm_g, w_ada, b_ada, w_in, a_sink, c_q_norm, c_k_norm, d_rel_bias, w_gate_merge, w_branch, w_out, final_g, loss_target, m_norm_g, m_w_ada, m_b_ada, m_w_in, m_a_sink, m_c_q_norm, m_c_k_norm, m_d_rel_bias, m_w_gate_merge, m_w_branch, m_w_out, m_final_g, v_norm_g, v_w_ada, v_b_ada, v_w_in, v_a_sink, v_c_q_norm, v_c_k_norm, v_d_rel_bias, v_w_gate_merge, v_w_branch, v_w_out, v_final_g):
    L, D = norm_g.shape
    dq = D // N_SHARD
    p = dict(norm_g=norm_g, b_ada=b_ada, a_sink=a_sink, c_q_norm=c_q_norm, c_k_norm=c_k_norm,
             d_rel_bias=d_rel_bias, final_g=final_g)
    xi, yi, ci = _place()
    c_idx = jnp.reshape(ci, (1,)).astype(I32)
    me = 2 * xi + yi

    shards = [[w_ada[l].astype(BF16), w_in[l].astype(BF16), w_gate_merge[l].astype(BF16),
               w_branch[l].astype(BF16), w_out[l].astype(BF16)] for l in range(L)]
    c8 = jnp.broadcast_to(c, (8, D))
    loss, grad_x, reduced, smalls, dfg = _device_step(x[0], c8, loss_target[0], shards, p, me, c_idx)
    loss = lax.psum(loss[0, 0], ("x", "y", "c"))

    small_parts = [jnp.stack([s[n] for s in smalls]) for n in SMALL_NAMES[:-1]] + [dfg]
    packed = _pack(small_parts + [c[0]])
    rows = packed.shape[0]
    g8 = _gather_small("gather_small", packed).reshape(8, rows, 128)
    small_w = [p[n] for n in SMALL_NAMES]
    small_m = [m_norm_g, m_b_ada, m_a_sink, m_c_q_norm, m_c_k_norm, m_d_rel_bias, m_final_g]
    small_v = [v_norm_g, v_b_ada, v_a_sink, v_c_q_norm, v_c_k_norm, v_d_rel_bias, v_final_g]
    pad_c = [jnp.zeros((D,), F32)]
    sg, sd, sm, sv = _adamw_small("adamw_small", _pack(small_w + pad_c), g8, _pack(small_m + pad_c),
                                  _pack(small_v + pad_c))
    sg, sd, sm, sv = (_unpack(a, small_w) for a in (sg, sd, sm, sv))

    n_small = sum(a.size for a in small_parts)
    flat8 = g8.reshape(8, rows * 128)
    c_all = flat8[:, n_small:n_small + D]
    dada_all = flat8[:, L * D:L * D + L * 3 * D].reshape(8, L, 3 * D)
    dada_mine = lax.dynamic_slice_in_dim(dada_all, (2 * xi + yi) * (3 * dq), 3 * dq, axis=2)
    tma = min(D, 1024)
    g_ada = jnp.stack([
        _matmul(f"mm_gwada_l{l}", (D // tma,), [c_all, dada_mine[:, l]],
                [pl.BlockSpec((8, tma), lambda i: (0, i)), pl.BlockSpec((8, 3 * dq), lambda i: (0, 0))],
                [SDS((D, 3 * dq), F32)], [pl.BlockSpec((tma, 3 * dq), lambda i: (i, 0))],
                ta=True, a_fn=lambda a: a * _sigmoid(a))[0] for l in range(L)])

    g_in, g_gm, g_wb, g_out = (jnp.stack([reduced[l][a] for l in range(L)]) for a in range(4))

    big = {}
    for nm, w, g, m, v in (("w_ada", w_ada, g_ada, m_w_ada, v_w_ada), ("w_in", w_in, g_in, m_w_in, v_w_in),
                           ("w_gate_merge", w_gate_merge, g_gm, m_w_gate_merge, v_w_gate_merge),
                           ("w_branch", w_branch, g_wb, m_w_branch, v_w_branch),
                           ("w_out", w_out, g_out, m_w_out, v_w_out)):
        big[nm] = (g, *_adamw(f"adamw_{nm}", w, g, m, v))

    order = ("norm_g", "w_ada", "b_ada", "w_in", "a_sink", "c_q_norm", "c_k_norm", "d_rel_bias",
             "w_gate_merge", "w_branch", "w_out", "final_g")
    cols = [[], [], [], []]
    for nm in order:
        if nm in big:
            vals = big[nm]
        else:
            k = SMALL_NAMES.index(nm)
            vals = (sg[k], sd[k], sm[k], sv[k])
        for col, val in zip(cols, vals):
            col.append(val)
    return (loss, grad_x[None], *cols[0], *cols[1], *cols[2], *cols[3])
```
